```python
import jax, jax.numpy as jnp
from jax import lax
import numpy as np

D_MODEL = 2048
BATCH = 2
SEQ = 4096
DEPTH = 1

RW_HEADS = 16
RW_HEAD = 64
RW_DIM = RW_HEADS * RW_HEAD
DECAY_LORA = 96
ICLR_LORA = 96
GATE_LORA = 256
GN_EPS = 64e-5
ATT_HEADS = 16
ATT_KV_HEADS = 4
ATT_GROUP = ATT_HEADS // ATT_KV_HEADS
ATT_HEAD = 64
ATT_Q_DIM = ATT_HEADS * ATT_HEAD
ATT_KV_DIM = ATT_KV_HEADS * ATT_HEAD
WINDOW = 128
BLOCK = 128
D_FF = 5632
CONV_W = 3
RMS_EPS = 1e-6
NEG_BIG = -1e30

RW_SHIFT_COLS = 3 * RW_DIM + DECAY_LORA + ICLR_LORA + GATE_LORA
ATT_COLS = ATT_Q_DIM + 2 * ATT_KV_DIM
GATE_COLS = 2 * D_MODEL
IN_COLS = RW_SHIFT_COLS + ATT_COLS + GATE_COLS

kernel_name = "hybrid_rwkv7_swa_gated_convffn"


def rms_norm(x, g, eps=RMS_EPS):
    xf = x.astype(jnp.float32)
    y = xf * lax.rsqrt(jnp.mean(xf * xf, axis=-1, keepdims=True) + eps)
    return (y * g.astype(jnp.float32)).astype(x.dtype)


def token_shift(z, mu):
    z_prev = jnp.pad(z, ((0, 0), (1, 0), (0, 0)))[:, :-1]
    return z + (z_prev - z) * mu


def rwkv7_mix(zs, w0, w2, a0, a2, g2, k_k, k_a, r_k, gn_w, gn_b):
    B, S, _ = zs.shape
    f32 = jnp.float32
    o1 = RW_DIM
    o2 = 2 * RW_DIM
    o3 = 3 * RW_DIM
    o4 = o3 + DECAY_LORA
    o5 = o4 + ICLR_LORA
    r, k, v, wd, ad, gd = jnp.split(zs, [o1, o2, o3, o4, o5], axis=-1)
    w = -jax.nn.softplus(-(w0 + jnp.tanh(wd) @ w2)) - 0.5
    decay = jnp.exp(-jnp.exp(w.astype(f32)))
    a = jax.nn.sigmoid(a0 + ad @ a2)
    g = jax.nn.sigmoid(gd) @ g2
    hd = (B, S, RW_HEADS, RW_HEAD)
    kk = (k * k_k).reshape(hd).astype(f32)
    kk = kk / jnp.maximum(jnp.sqrt(jnp.sum(kk * kk, axis=-1, keepdims=True)), 1e-12)
    k = k * (1.0 + (a - 1.0) * k_a)
    rh = r.reshape(hd).astype(f32)
    kh = k.reshape(hd).astype(f32)
    vh = v.reshape(hd).astype(f32)
    wh = decay.reshape(hd)
    ah = a.reshape(hd).astype(f32)
    tm = lambda t: jnp.moveaxis(t, 1, 0)
    xs = (tm(rh), tm(kh), tm(vh), tm(wh), tm(-kk), tm(kk * ah))

    def step(state, inp):
        r_t, k_t, v_t, w_t, ka_t, kb_t = inp
        sa = jnp.einsum('bhvk,bhk->bhv', state, ka_t)
        state = (state * w_t[:, :, None, :] + sa[..., None] * kb_t[:, :, None, :]
                 + v_t[..., None] * k_t[:, :, None, :])
        y = jnp.einsum('bhvk,bhk->bhv', state, r_t)
        return state, y

    state0 = jnp.zeros((B, RW_HEADS, RW_HEAD, RW_HEAD), f32)
    _, y = lax.scan(step, state0, xs)
    y = jnp.moveaxis(y, 0, 1)
    mu = jnp.mean(y, axis=-1, keepdims=True)
    var = jnp.mean(jnp.square(y - mu), axis=-1, keepdims=True)
    y = ((y - mu) * lax.rsqrt(var + GN_EPS)).reshape(B, S, RW_DIM)
    y = y * gn_w.astype(f32) + gn_b.astype(f32)
    bonus = jnp.sum(rh * kh * r_k.astype(f32), axis=-1, keepdims=True) * vh
    out = (y + bonus.reshape(B, S, RW_DIM)) * g.astype(f32)
    return out.astype(zs.dtype)


def swa_attention(q, k, v, q_gain, k_gain, sinks):
    B, S, _ = q.shape
    nb = S // BLOCK
    q = rms_norm(q.reshape(B, S, ATT_HEADS, ATT_HEAD), q_gain)
    k = rms_norm(k.reshape(B, S, ATT_KV_HEADS, ATT_HEAD), k_gain)
    v = v.reshape(B, S, ATT_KV_HEADS, ATT_HEAD)
    qb = q.reshape(B, nb, BLOCK, ATT_KV_HEADS, ATT_GROUP, ATT_HEAD)

    def band(t):
        tp = jnp.pad(t, ((0, 0), (BLOCK, 0), (0, 0), (0, 0)))[:, :S]
        prev = tp.reshape(B, nb, BLOCK, ATT_KV_HEADS, ATT_HEAD)
        cur = t.reshape(B, nb, BLOCK, ATT_KV_HEADS, ATT_HEAD)
        return jnp.concatenate([prev, cur], axis=2)

    kb, vb = band(k), band(v)
    scores = jnp.einsum('bnqhgd,bnkhd->bnhgqk', qb, kb).astype(jnp.float32) * (ATT_HEAD ** -0.5)
    qi = jnp.arange(BLOCK)[:, None]
    kj = jnp.arange(2 * BLOCK)[None, :]
    dist = (BLOCK + qi - kj).astype(jnp.float32)
    key_pos = jnp.arange(nb)[:, None] * BLOCK - BLOCK + jnp.arange(2 * BLOCK)[None, :]
    valid = ((dist >= 0) & (dist < WINDOW))[None] & (key_pos[:, None, :] >= 0)
    slopes = jnp.exp2(-8.0 * jnp.arange(1, ATT_HEADS + 1, dtype=jnp.float32) / ATT_HEADS)
    alibi = -slopes.reshape(ATT_KV_HEADS, ATT_GROUP)[:, :, None, None] * dist
    scores = jnp.where(valid[None, :, None, None], scores + alibi[None, None], NEG_BIG)
    sink = jnp.broadcast_to(sinks.astype(jnp.float32).reshape(1, 1, ATT_KV_HEADS, ATT_GROUP, 1, 1),
                            scores.shape[:-1] + (1,))
    p = jax.nn.softmax(jnp.concatenate([scores, sink], axis=-1), axis=-1)[..., :-1]
    o = jnp.einsum('bnhgqk,bnkhd->bnqhgd', p.astype(vb.dtype), vb)
    return o.reshape(B, S, ATT_Q_DIM)


def conv_ffn(h, w_up, conv_w, conv_b, w_down):
    u = h @ w_up
    S = u.shape[1]
    up = jnp.pad(u, ((0, 0), (CONV_W - 1, 0), (0, 0)))
    c = up[:, 0:S] * conv_w[0] + up[:, 1:S + 1] * conv_w[1] + up[:, 2:S + 2] * conv_w[2] + conv_b
    val, gate = jnp.split(c, 2, axis=-1)
    return (jax.nn.silu(gate) * val) @ w_down


def setup_inputs(seed: int = 0) -> dict:
    key = jax.random.key(seed)
    ks = jax.random.split(key, 26)
    f32 = jnp.float32
    L = DEPTH
    nrm = lambda k, shape, s: jax.random.normal(k, shape, f32) * s
    return {
        "x": nrm(ks[0], (BATCH, SEQ, D_MODEL), 1.0),
        "norm1_g": 1.0 + nrm(ks[1], (L, D_MODEL), 0.02),
        "w_in": nrm(ks[2], (L, D_MODEL, IN_COLS), D_MODEL ** -0.5),
        "rw_mu": jax.random.uniform(ks[3], (L, RW_SHIFT_COLS), f32),
        "rw_w0": jax.random.uniform(ks[4], (L, RW_DIM), f32, -6.0, -1.0),
        "rw_w2": nrm(ks[5], (L, DECAY_LORA, RW_DIM), 0.1 * DECAY_LORA ** -0.5),
        "rw_a0": nrm(ks[6], (L, RW_DIM), 0.5),
        "rw_a2": nrm(ks[7], (L, ICLR_LORA, RW_DIM), ICLR_LORA ** -0.5),
        "rw_g2": nrm(ks[8], (L, GATE_LORA, RW_DIM), GATE_LORA ** -0.5),
        "rw_k_k": 0.85 + nrm(ks[9], (L, RW_DIM), 0.02),
        "rw_k_a": 1.0 + nrm(ks[10], (L, RW_DIM), 0.02),
        "rw_r_k": nrm(ks[11], (L, RW_HEADS, RW_HEAD), 0.1),
        "rw_gn_w": 1.0 + nrm(ks[12], (L, RW_DIM), 0.02),
        "rw_gn_b": nrm(ks[13], (L, RW_DIM), 0.02),
        "q_norm_g": 1.0 + nrm(ks[14], (L, ATT_HEAD), 0.02),
        "k_norm_g": 1.0 + nrm(ks[15], (L, ATT_HEAD), 0.02),
        "attn_sinks": nrm(ks[16], (L, ATT_HEADS), 0.5),
        "w_branch": nrm(ks[17], (L, RW_DIM + ATT_Q_DIM, D_MODEL), RW_DIM ** -0.5),
        "w_out": nrm(ks[18], (L, D_MODEL, D_MODEL), D_MODEL ** -0.5),
        "norm2_g": 1.0 + nrm(ks[19], (L, D_MODEL), 0.02),
        "w_up": nrm(ks[20], (L, D_MODEL, 2 * D_FF), D_MODEL ** -0.5),
        "conv_w": nrm(ks[21], (L, CONV_W, 2 * D_FF), CONV_W ** -0.5),
        "conv_b": nrm(ks[22], (L, 2 * D_FF), 0.02),
        "w_down": nrm(ks[23], (L, D_FF, D_MODEL), D_FF ** -0.5),
    }


def reference(x, norm1_g, w_in, rw_mu, rw_w0, rw_w2, rw_a0, rw_a2, rw_g2, rw_k_k, rw_k_a,
              rw_r_k, rw_gn_w, rw_gn_b, q_norm_g, k_norm_g, attn_sinks, w_branch, w_out,
              norm2_g, w_up, conv_w, conv_b, w_down):
    for l in range(DEPTH):
        h = rms_norm(x, norm1_g[l])
        z = h @ w_in[l]
        zs = token_shift(z[..., :RW_SHIFT_COLS], rw_mu[l])
        za = z[..., RW_SHIFT_COLS:RW_SHIFT_COLS + ATT_COLS]
        zg = z[..., RW_SHIFT_COLS + ATT_COLS:]
        o_rw = rwkv7_mix(zs, rw_w0[l], rw_w2[l], rw_a0[l], rw_a2[l], rw_g2[l], rw_k_k[l],
                         rw_k_a[l], rw_r_k[l], rw_gn_w[l], rw_gn_b[l])
        q = za[..., :ATT_Q_DIM]
        k = za[..., ATT_Q_DIM:ATT_Q_DIM + ATT_KV_DIM]
        v = za[..., ATT_Q_DIM + ATT_KV_DIM:]
        o_att = swa_attention(q, k, v, q_norm_g[l], k_norm_g[l], attn_sinks[l])
        p_rw = o_rw @ w_branch[l, :RW_DIM]
        p_att = o_att @ w_branch[l, RW_DIM:]
        g_rw = jax.nn.sigmoid(zg[..., :D_MODEL])
        g_att = jax.nn.sigmoid(zg[..., D_MODEL:])
        x = x + (g_rw * p_rw + g_att * p_att) @ w_out[l]
        x = x + conv_ffn(rms_norm(x, norm2_g[l]), w_up[l], conv_w[l], conv_b[l], w_down[l])
    return x
```

```python
import functools

import jax
import jax.numpy as jnp
import numpy as np
from jax import lax
from jax.experimental import pallas as pl
from jax.experimental.pallas import tpu as pltpu

F32 = jnp.float32
BF16 = jnp.bfloat16

LANES = 128
BF16_SUBLANES = 16
VMEM_LIMIT = 56 * 1024 * 1024

D_MODEL = 2048
RW_HEADS = 16
RW_HEAD = 64
RW_DIM = RW_HEADS * RW_HEAD
DECAY_LORA = 96
ICLR_LORA = 96
GATE_LORA = 256
GN_EPS = 64e-5
ATT_HEADS = 16
ATT_KV_HEADS = 4
ATT_GROUP = ATT_HEADS // ATT_KV_HEADS
ATT_HEAD = 64
ATT_Q_DIM = ATT_HEADS * ATT_HEAD
ATT_KV_DIM = ATT_KV_HEADS * ATT_HEAD
WINDOW = 128
BLOCK = 128
D_FF = 5632
RMS_EPS = 1e-6
NEG_BIG = -1e30

C_R = 0
C_K = RW_DIM
C_V = 2 * RW_DIM
C_Q = 3 * RW_DIM
C_GRW = C_Q + ATT_Q_DIM
C_GATT = C_GRW + D_MODEL
C_LORA = C_GATT + D_MODEL
LORA_W = 128
LORA_A = 128
LORA_COLS = LORA_W + LORA_A + GATE_LORA
C_KA = C_LORA + LORA_COLS
C_VA = C_KA + ATT_KV_DIM
Z_COLS = C_VA + ATT_KV_DIM

CHUNK = 64
PAIR = 2 * RW_HEAD
N_PAIRS = RW_DIM // PAIR


def _cparams(sem):
    return pltpu.CompilerParams(dimension_semantics=sem, vmem_limit_bytes=VMEM_LIMIT)


def _mm(a, b):
    return jnp.dot(a.astype(BF16), b.astype(BF16), preferred_element_type=F32)


def _mm_nt(a, b):
    return lax.dot_general(a.astype(BF16), b.astype(BF16), (((1,), (1,)), ((), ())),
                           preferred_element_type=F32)


def _mm_tn(a, b):
    return lax.dot_general(a.astype(BF16), b.astype(BF16), (((0,), (0,)), ((), ())),
                           preferred_element_type=F32)


def _mm_split(m01, x):
    hi = x.astype(BF16)
    lo = (x - hi.astype(F32)).astype(BF16)
    return (jnp.dot(m01, hi, preferred_element_type=F32)
            + jnp.dot(m01, lo, preferred_element_type=F32))


def _mm_split_r(x, m01):
    hi = x.astype(BF16)
    lo = (x - hi.astype(F32)).astype(BF16)
    return (jnp.dot(hi, m01, preferred_element_type=F32)
            + jnp.dot(lo, m01, preferred_element_type=F32))


def _head_ones(n, head, scale):
    r = lax.broadcasted_iota(jnp.int32, (n, n), 0) // head
    c = lax.broadcasted_iota(jnp.int32, (n, n), 1) // head
    return jnp.where(r == c, scale, 0.0).astype(BF16)


def _in_proj_kernel(x_ref, g_ref, w_ref, z_ref, h_ref):
    @pl.when(pl.program_id(1) == 0)
    def _():
        xf = x_ref[...]
        y = xf * lax.rsqrt(jnp.mean(xf * xf, axis=-1, keepdims=True) + RMS_EPS)
        h_ref[...] = (y * g_ref[...]).astype(BF16)

    z_ref[...] = jnp.dot(h_ref[...], w_ref[...], preferred_element_type=F32).astype(BF16)


def _in_proj(x2, g1, w_in_p, tm=1024, tn=1024):
    m, d = x2.shape
    n = w_in_p.shape[1]
    return pl.pallas_call(
        _in_proj_kernel,
        grid=(m // tm, n // tn),
        in_specs=[pl.BlockSpec((tm, d), lambda i, j: (i, 0)),
                  pl.BlockSpec((1, d), lambda i, j: (0, 0)),
                  pl.BlockSpec((d, tn), lambda i, j: (0, j))],
        out_specs=pl.BlockSpec((tm, tn), lambda i, j: (i, j)),
        out_shape=jax.ShapeDtypeStruct((m, n), BF16),
        scratch_shapes=[pltpu.VMEM((tm, d), BF16)],
        compiler_params=_cparams(("parallel", "arbitrary")),
        name="in_proj",
    )(x2, g1, w_in_p)


def _rwkv_prep_kernel(zr_ref, zk_ref, zv_ref, zl_ref, pr_ref, pk_ref, pv_ref, pli_ref,
                      mur_ref, muk_ref, muv_ref, mul_ref,
                      w0_ref, a0_ref, kk_ref, ka_ref, rk_ref, w2_ref, a2_ref, g2_ref,
                      at_ref, rt_ref, bt_ref, kt_ref, bg_ref, kg_ref, v_ref, g_ref, bonus_ref,
                      gT_ref, gm_ref, *, tm, tiles_per_seq):
    first = (pl.program_id(0) % tiles_per_seq) == 0
    keep = jnp.where(first, 0.0, 1.0)

    def shifted(z_ref, p_ref, mu_ref):
        z = z_ref[...].astype(F32)
        prev_last = p_ref[BF16_SUBLANES - 1:BF16_SUBLANES, :].astype(F32) * keep
        row = lax.broadcasted_iota(jnp.int32, z.shape, 0)
        zprev = jnp.where(row == 0, prev_last, pltpu.roll(z, 1, 0))
        return z + (zprev - z) * mu_ref[...]

    r = shifted(zr_ref, pr_ref, mur_ref)
    k = shifted(zk_ref, pk_ref, muk_ref)
    v = shifted(zv_ref, pv_ref, muv_ref)
    lo = shifted(zl_ref, pli_ref, mul_ref)
    wd = lo[:, :LORA_W]
    ad = lo[:, LORA_W:LORA_W + LORA_A]
    gd = lo[:, LORA_W + LORA_A:]

    wpre = w0_ref[...] + _mm(jnp.tanh(wd), w2_ref[...])
    xneg = -wpre
    softplus = jnp.maximum(xneg, 0.0) + jnp.log1p(jnp.exp(-jnp.abs(xneg)))
    w = -softplus - 0.5
    lw = -jnp.exp(w)
    a = jax.nn.sigmoid(a0_ref[...] + _mm(ad, a2_ref[...]))
    g = _mm(jax.nn.sigmoid(gd), g2_ref[...])

    ones_h = _head_ones(PAIR, RW_HEAD, 1.0)
    kk = k * kk_ref[...]
    kk = kk / jnp.maximum(jnp.sqrt(_mm_split_r(kk * kk, ones_h)), 1e-12)
    k2 = k * (1.0 + (a - 1.0) * ka_ref[...])
    bonus = _mm_split_r(r * k2 * rk_ref[...], ones_h) * v
    ka_vec = -kk
    kb_vec = kk * a

    ti = lax.broadcasted_iota(jnp.int32, (tm, tm), 0)
    si = lax.broadcasted_iota(jnp.int32, (tm, tm), 1)
    same = (ti // CHUNK) == (si // CHUNK)
    tri = jnp.where(same & (si <= ti), 1.0, 0.0).astype(BF16)
    mid = jnp.where(same & ((si % CHUNK) < CHUNK // 2), 1.0, 0.0).astype(BF16)
    full = jnp.where(same, 1.0, 0.0).astype(BF16)
    c = _mm_split(tri, lw)
    cm = _mm_split(mid, lw)
    cT = _mm_split(full, lw)

    e_in = jnp.exp(c - cm)
    e_out = jnp.exp(cm - c)
    e_end = jnp.exp(cT - c)
    at_ref[...] = (ka_vec * jnp.exp(c - lw - cm)).astype(BF16)
    rt_ref[...] = (r * e_in).astype(BF16)
    bt_ref[...] = (kb_vec * e_out).astype(BF16)
    kt_ref[...] = (k2 * e_out).astype(BF16)
    bg_ref[...] = (kb_vec * e_end).astype(BF16)
    kg_ref[...] = (k2 * e_end).astype(BF16)
    v_ref[...] = v.astype(BF16)
    g_ref[...] = g.astype(BF16)
    bonus_ref[...] = bonus.astype(BF16)

    nc = tm // CHUNK
    cj = lax.broadcasted_iota(jnp.int32, (nc, tm), 0)
    sj = lax.broadcasted_iota(jnp.int32, (nc, tm), 1)
    in_chunk = (sj // CHUNK) == cj
    sel_full = jnp.where(in_chunk, 1.0, 0.0).astype(BF16)
    sel_mid = jnp.where(in_chunk & ((sj % CHUNK) < CHUNK // 2), 1.0, 0.0).astype(BF16)
    g_end = jnp.exp(_mm_split(sel_full, lw))
    g_mid = jnp.exp(_mm_split(sel_mid, lw))
    for ch in range(nc):
        gT_ref[ch] = g_end[ch:ch + 1, :]
        gm_ref[ch] = g_mid[ch:ch + 1, :]


def _rwkv_prep(z, seq, mu_r, mu_k, mu_v, mu_l, w0, a0, k_k, k_a, r_k, w2p, a2p, g2, tm=256):
    m = z.shape[0]
    nt = m // tm
    pb = tm // BF16_SUBLANES
    lb = LORA_COLS // PAIR

    def cur(col0):
        return pl.BlockSpec((tm, PAIR), lambda i, p, c0=col0 // PAIR: (i, c0 + p))

    def prev(col0):
        return pl.BlockSpec((BF16_SUBLANES, PAIR),
                            lambda i, p, c0=col0 // PAIR: (jnp.maximum(i * pb - 1, 0), c0 + p))

    def vec(_=None):
        return pl.BlockSpec((1, PAIR), lambda i, p: (0, p))

    in_specs = [cur(C_R), cur(C_K), cur(C_V),
                pl.BlockSpec((tm, LORA_COLS), lambda i, p: (i, C_LORA // LORA_COLS)),
                prev(C_R), prev(C_K), prev(C_V),
                pl.BlockSpec((BF16_SUBLANES, LORA_COLS),
                             lambda i, p: (jnp.maximum(i * pb - 1, 0), C_LORA // LORA_COLS)),
                vec(), vec(), vec(),
                pl.BlockSpec((1, LORA_COLS), lambda i, p: (0, 0)),
                vec(), vec(), vec(), vec(), vec(),
                pl.BlockSpec((LORA_W, PAIR), lambda i, p: (0, p)),
                pl.BlockSpec((LORA_A, PAIR), lambda i, p: (0, p)),
                pl.BlockSpec((GATE_LORA, PAIR), lambda i, p: (0, p))]
    big = jax.ShapeDtypeStruct((m, RW_DIM), BF16)
    per_chunk = jax.ShapeDtypeStruct((m // CHUNK, 1, RW_DIM), F32)
    big_spec = pl.BlockSpec((tm, PAIR), lambda i, p: (i, p))
    pc_spec = pl.BlockSpec((tm // CHUNK, 1, PAIR), lambda i, p: (i, 0, p))
    del lb
    return pl.pallas_call(
        functools.partial(_rwkv_prep_kernel, tm=tm, tiles_per_seq=seq // tm),
        grid=(nt, N_PAIRS),
        in_specs=in_specs,
        out_specs=[big_spec] * 9 + [pc_spec] * 2,
        out_shape=[big] * 9 + [per_chunk] * 2,
        compiler_params=_cparams(("parallel", "parallel")),
        name="rwkv_prep",
    )(z, z, z, z, z, z, z, z, mu_r, mu_k, mu_v, mu_l, w0, a0, k_k, k_a, r_k, w2p, a2p, g2)


def _rwkv_chunk_kernel(at_ref, rt_ref, bt_ref, kt_ref, bg_ref, kg_ref, v_ref, g_ref, bonus_ref,
                       gT_ref, gm_ref, gnw_ref, gnb_ref, o_ref, s_ref, *, chunks):
    @pl.when(pl.program_id(2) == 0)
    def _():
        s_ref[...] = jnp.zeros_like(s_ref)

    T = CHUNK
    lane = lax.broadcasted_iota(jnp.int32, (T, PAIR), 1)
    head0 = lane < RW_HEAD
    ri = lax.broadcasted_iota(jnp.int32, (2 * T, 2 * T), 0)
    ci = lax.broadcasted_iota(jnp.int32, (2 * T, 2 * T), 1)
    same = (ri // T) == (ci // T)
    strict = same & ((ri % T) > (ci % T))
    incl = same & ((ri % T) >= (ci % T))
    eye = jnp.where(ri == ci, 1.0, 0.0)
    mean_h = _head_ones(PAIR, RW_HEAD, 1.0 / RW_HEAD)

    def stack(ref, c):
        x = ref[c * T:(c + 1) * T, :]
        zero = jnp.zeros_like(x)
        return jnp.concatenate([jnp.where(head0, x, zero), jnp.where(head0, zero, x)], axis=0)

    pre = []
    for c in range(chunks):
        La, Lr = stack(at_ref, c), stack(rt_ref, c)
        Rb, Rk = stack(bt_ref, c), stack(kt_ref, c)
        Rbg, Rkg = stack(bg_ref, c), stack(kg_ref, c)
        Vs = stack(v_ref, c)
        gm_row = gm_ref[c]
        N = jnp.where(strict, _mm_nt(La, Rb), 0.0)
        Aak = jnp.where(strict, _mm_nt(La, Rk), 0.0)
        Arb = jnp.where(incl, _mm_nt(Lr, Rb), 0.0)
        Ark = jnp.where(incl, _mm_nt(Lr, Rk), 0.0)
        W = eye + N
        P = N
        span = 2
        while span < T:
            P = _mm(P, P)
            W = W + _mm(W, P)
            span *= 2
        AkV = _mm(Aak, Vs)
        X = _mm(W, jnp.concatenate([La, AkV.astype(BF16)], axis=1))
        Wa = X[:, :PAIR]
        WAkV = X[:, PAIR:]
        Z = _mm(Arb, X) + jnp.concatenate([Lr.astype(F32), _mm(Ark, Vs)], axis=1)
        Q = Z[:, :PAIR] * gm_row
        Y0 = Z[:, PAIR:]
        Mbd = _mm_tn(Rbg, Wa) * gm_row
        NcT = _mm_tn(WAkV, Rbg) + _mm_tn(Vs, Rkg)
        pre.append((Q, Y0, Mbd, NcT))

    S = s_ref[...]
    for c in range(chunks):
        Q, Y0, Mbd, NcT = pre[c]
        Ys = _mm_nt(Q, S) + Y0
        S = S * gT_ref[c] + _mm_nt(S, Mbd) + NcT
        y = Ys[:T, :] + Ys[T:, :]
        mu = _mm_split_r(y, mean_h)
        d = y - mu
        var = _mm_split_r(d * d, mean_h)
        yn = d * lax.rsqrt(var + GN_EPS) * gnw_ref[...] + gnb_ref[...]
        rows = slice(c * T, (c + 1) * T)
        out = (yn + bonus_ref[rows, :].astype(F32)) * g_ref[rows, :].astype(F32)
        o_ref[rows, :] = out.astype(BF16)
    s_ref[...] = S


def _rwkv_chunk(ops, gT, gm, gn_w, gn_b, batch, seq, tb=256):
    m = ops[0].shape[0]
    nb = seq // tb
    chunks = tb // CHUNK
    big_spec = pl.BlockSpec((tb, PAIR), lambda b, p, j: (b * nb + j, p))
    pc_spec = pl.BlockSpec((chunks, 1, PAIR), lambda b, p, j: (b * nb + j, 0, p))
    vec = pl.BlockSpec((1, PAIR), lambda b, p, j: (0, p))
    return pl.pallas_call(
        functools.partial(_rwkv_chunk_kernel, chunks=chunks),
        grid=(batch, N_PAIRS, nb),
        in_specs=[big_spec] * 9 + [pc_spec] * 2 + [vec, vec],
        out_specs=big_spec,
        out_shape=jax.ShapeDtypeStruct((m, RW_DIM), BF16),
        scratch_shapes=[pltpu.VMEM((PAIR, PAIR), F32)],
        compiler_params=_cparams(("parallel", "parallel", "arbitrary")),
        name="rwkv_chunk",
    )(*ops, gT, gm, gn_w, gn_b)


def _swa_kernel(sink_ref, q_ref, kc_ref, kp_ref, vc_ref, vp_ref, qg_ref, kg_ref, o_ref, *,
                blocks_per_seq, slopes):
    first = (pl.program_id(0) % blocks_per_seq) == 0
    mean_h = _head_ones(LANES, ATT_HEAD, 1.0 / ATT_HEAD)

    def head_rms(x, gain):
        parts = []
        for b in range(x.shape[1] // LANES):
            xb = x[:, b * LANES:(b + 1) * LANES]
            ms = _mm_split_r(xb * xb, mean_h)
            parts.append(xb * lax.rsqrt(ms + RMS_EPS))
        return jnp.concatenate(parts, axis=1) * gain

    q = head_rms(q_ref[...].astype(F32), qg_ref[...]).astype(BF16)
    kcat = jnp.concatenate([kp_ref[...], kc_ref[...]], axis=0).astype(F32)
    kcat = head_rms(kcat, kg_ref[...]).astype(BF16)
    vcat = jnp.concatenate([vp_ref[...], vc_ref[...]], axis=0)

    qi = lax.broadcasted_iota(jnp.int32, (BLOCK, 2 * BLOCK), 0)
    kj = lax.broadcasted_iota(jnp.int32, (BLOCK, 2 * BLOCK), 1)
    dist_i = BLOCK + qi - kj
    first_key = jnp.where(first, BLOCK, 0)
    valid = (dist_i >= 0) & (dist_i < WINDOW) & (kj >= first_key)
    dist = dist_i.astype(F32)

    outs = []
    for h in range(ATT_HEADS):
        j = h // ATT_GROUP
        qh = q[:, h * ATT_HEAD:(h + 1) * ATT_HEAD]
        kh = kcat[:, j * ATT_HEAD:(j + 1) * ATT_HEAD]
        vh = vcat[:, j * ATT_HEAD:(j + 1) * ATT_HEAD]
        s = _mm_nt(qh, kh) * (ATT_HEAD ** -0.5)
        s = jnp.where(valid, s + (-slopes[h]) * dist, NEG_BIG)
        sink = sink_ref[h]
        mx = jnp.maximum(jnp.max(s, axis=-1, keepdims=True), sink)
        p = jnp.exp(s - mx)
        denom = jnp.sum(p, axis=-1, keepdims=True) + jnp.exp(sink - mx)
        p = p / denom
        outs.append(_mm(p, vh))
    o_ref[...] = jnp.concatenate(outs, axis=1).astype(BF16)


def _swa(z, sinks, q_gain_t, k_gain_t, seq):
    m = z.shape[0]
    nblk = m // BLOCK
    bps = seq // BLOCK
    slopes = tuple(float(s) for s in
                   np.exp2(-8.0 * np.arange(1, ATT_HEADS + 1, dtype=np.float32) / ATT_HEADS).astype(np.float32))
    kv_cur = lambda c0: pl.BlockSpec((BLOCK, ATT_KV_DIM), lambda n, c=c0 // ATT_KV_DIM: (n, c))
    kv_prev = lambda c0: pl.BlockSpec((BLOCK, ATT_KV_DIM),
                                      lambda n, c=c0 // ATT_KV_DIM: (jnp.maximum(n - 1, 0), c))
    return pl.pallas_call(
        functools.partial(_swa_kernel, blocks_per_seq=bps, slopes=slopes),
        grid=(nblk,),
        in_specs=[pl.BlockSpec(memory_space=pltpu.SMEM),
                  pl.BlockSpec((BLOCK, ATT_Q_DIM), lambda n: (n, C_Q // ATT_Q_DIM)),
                  kv_cur(C_KA), kv_prev(C_KA), kv_cur(C_VA), kv_prev(C_VA),
                  pl.BlockSpec((1, ATT_Q_DIM), lambda n: (0, 0)),
                  pl.BlockSpec((1, ATT_KV_DIM), lambda n: (0, 0))],
        out_specs=pl.BlockSpec((BLOCK, ATT_Q_DIM), lambda n: (n, 0)),
        out_shape=jax.ShapeDtypeStruct((m, ATT_Q_DIM), BF16),
        compiler_params=_cparams(("parallel",)),
        name="swa",
    )(sinks, z, z, z, z, z, q_gain_t, k_gain_t)


def _branch_mix_kernel(orw_ref, oatt_ref, wb1_ref, wb2_ref, zg1_ref, zg2_ref, m_ref):
    p_rw = jnp.dot(orw_ref[...], wb1_ref[...], preferred_element_type=F32)
    p_att = jnp.dot(oatt_ref[...], wb2_ref[...], preferred_element_type=F32)
    g_rw = jax.nn.sigmoid(zg1_ref[...].astype(F32))
    g_att = jax.nn.sigmoid(zg2_ref[...].astype(F32))
    m_ref[...] = (g_rw * p_rw + g_att * p_att).astype(BF16)


def _branch_mix(o_rw, o_att, wb1, wb2, z, tm=1024, tn=1024):
    m = o_rw.shape[0]
    return pl.pallas_call(
        _branch_mix_kernel,
        grid=(m // tm, D_MODEL // tn),
        in_specs=[pl.BlockSpec((tm, RW_DIM), lambda i, j: (i, 0)),
                  pl.BlockSpec((tm, ATT_Q_DIM), lambda i, j: (i, 0)),
                  pl.BlockSpec((RW_DIM, tn), lambda i, j: (0, j)),
                  pl.BlockSpec((ATT_Q_DIM, tn), lambda i, j: (0, j)),
                  pl.BlockSpec((tm, tn), lambda i, j: (i, C_GRW // tn + j)),
                  pl.BlockSpec((tm, tn), lambda i, j: (i, C_GATT // tn + j))],
        out_specs=pl.BlockSpec((tm, tn), lambda i, j: (i, j)),
        out_shape=jax.ShapeDtypeStruct((m, D_MODEL), BF16),
        compiler_params=_cparams(("parallel", "arbitrary")),
        name="branch_mix",
    )(o_rw, o_att, wb1, wb2, z, z)


def _out_proj_kernel(m_ref, w_ref, x_ref, g_ref, x1_ref, h2_ref):
    x1 = x_ref[...] + jnp.dot(m_ref[...], w_ref[...], preferred_element_type=F32)
    x1_ref[...] = x1
    y = x1 * lax.rsqrt(jnp.mean(x1 * x1, axis=-1, keepdims=True) + RMS_EPS)
    h2_ref[...] = (y * g_ref[...]).astype(BF16)


def _out_proj(mix, w_out, x2, g2, tm=256):
    m = x2.shape[0]
    return pl.pallas_call(
        _out_proj_kernel,
        grid=(m // tm,),
        in_specs=[pl.BlockSpec((tm, D_MODEL), lambda i: (i, 0)),
                  pl.BlockSpec((D_MODEL, D_MODEL), lambda i: (0, 0)),
                  pl.BlockSpec((tm, D_MODEL), lambda i: (i, 0)),
                  pl.BlockSpec((1, D_MODEL), lambda i: (0, 0))],
        out_specs=[pl.BlockSpec((tm, D_MODEL), lambda i: (i, 0)),
                   pl.BlockSpec((tm, D_MODEL), lambda i: (i, 0))],
        out_shape=[jax.ShapeDtypeStruct((m, D_MODEL), F32),
                   jax.ShapeDtypeStruct((m, D_MODEL), BF16)],
        compiler_params=_cparams(("parallel",)),
        name="out_proj",
    )(mix, w_out, x2, g2)


def _ffn_up_kernel(h_ref, wv_ref, wg_ref, cwv_ref, cwg_ref, cbv_ref, cbg_ref, a_ref,
                   uv_ref, ug_ref, *, tm, tiles_per_seq):
    halo = 8
    first = (pl.program_id(1) % tiles_per_seq) == 0

    def conv(u_ref, w_ref, cw_ref, cb_ref):
        @pl.when(first)
        def _():
            u_ref[0:halo, :] = jnp.zeros((halo, u_ref.shape[1]), F32)

        u_ref[halo:halo + tm, :] = jnp.dot(h_ref[...], w_ref[...], preferred_element_type=F32)
        c = (u_ref[halo - 2:halo - 2 + tm, :] * cw_ref[0:1, :]
             + u_ref[halo - 1:halo - 1 + tm, :] * cw_ref[1:2, :]
             + u_ref[halo:halo + tm, :] * cw_ref[2:3, :] + cb_ref[...])
        u_ref[0:halo, :] = u_ref[tm:tm + halo, :]
        return c

    val = conv(uv_ref, wv_ref, cwv_ref, cbv_ref)
    gate = conv(ug_ref, wg_ref, cwg_ref, cbg_ref)
    a_ref[...] = (gate * jax.nn.sigmoid(gate) * val).astype(BF16)


def _ffn_up(h2, w_up, conv_w, conv_b, seq, tm=1024, tn=512):
    m = h2.shape[0]
    nj = D_FF // tn
    return pl.pallas_call(
        functools.partial(_ffn_up_kernel, tm=tm, tiles_per_seq=seq // tm),
        grid=(nj, m // tm),
        in_specs=[pl.BlockSpec((tm, D_MODEL), lambda j, i: (i, 0)),
                  pl.BlockSpec((D_MODEL, tn), lambda j, i: (0, j)),
                  pl.BlockSpec((D_MODEL, tn), lambda j, i: (0, nj + j)),
                  pl.BlockSpec((3, tn), lambda j, i: (0, j)),
                  pl.BlockSpec((3, tn), lambda j, i: (0, nj + j)),
                  pl.BlockSpec((1, tn), lambda j, i: (0, j)),
                  pl.BlockSpec((1, tn), lambda j, i: (0, nj + j))],
        out_specs=pl.BlockSpec((tm, tn), lambda j, i: (i, j)),
        out_shape=jax.ShapeDtypeStruct((m, D_FF), BF16),
        scratch_shapes=[pltpu.VMEM((tm + 8, tn), F32), pltpu.VMEM((tm + 8, tn), F32)],
        compiler_params=_cparams(("parallel", "arbitrary")),
        name="ffn_up",
    )(h2, w_up, w_up, conv_w, conv_w, conv_b, conv_b)


def _ffn_down_kernel(a_ref, w_ref, x1_ref, o_ref):
    o_ref[...] = x1_ref[...] + jnp.dot(a_ref[...], w_ref[...], preferred_element_type=F32)


def _ffn_down(act, w_down, x1, tm=512, tn=1024):
    m = act.shape[0]
    return pl.pallas_call(
        _ffn_down_kernel,
        grid=(m // tm, D_MODEL // tn),
        in_specs=[pl.BlockSpec((tm, D_FF), lambda i, j: (i, 0)),
                  pl.BlockSpec((D_FF, tn), lambda i, j: (0, j)),
                  pl.BlockSpec((tm, tn), lambda i, j: (i, j))],
        out_specs=pl.BlockSpec((tm, tn), lambda i, j: (i, j)),
        out_shape=jax.ShapeDtypeStruct((m, D_MODEL), F32),
        compiler_params=_cparams(("parallel", "arbitrary")),
        name="ffn_down",
    )(act, w_down, x1)


def _pad_cols(w, n):
    return jnp.pad(w, ((0, 0), (0, n - w.shape[1])))


def _pad_rows(w, n):
    return jnp.pad(w, ((0, n - w.shape[0]), (0, 0)))


def _layer(x2, batch, seq, norm1_g, w_in, rw_mu, rw_w0, rw_w2, rw_a0, rw_a2, rw_g2, rw_k_k, rw_k_a,
           rw_r_k, rw_gn_w, rw_gn_b, q_norm_g, k_norm_g, attn_sinks, w_branch, w_out,
           norm2_g, w_up, conv_w, conv_b, w_down):
    row = lambda v: v.reshape(1, -1).astype(F32)
    o3 = 3 * RW_DIM
    o4 = o3 + DECAY_LORA
    o5 = o4 + ICLR_LORA
    o6 = o5 + GATE_LORA
    oq = o6 + ATT_Q_DIM
    ok = oq + ATT_KV_DIM
    ov = ok + ATT_KV_DIM
    w_in_p = jnp.concatenate([
        w_in[:, :o3], w_in[:, o6:oq], w_in[:, ov:],
        _pad_cols(w_in[:, o3:o4], LORA_W), _pad_cols(w_in[:, o4:o5], LORA_A), w_in[:, o5:o6],
        w_in[:, oq:ok], w_in[:, ok:ov]], axis=1).astype(BF16)
    mu = rw_mu.reshape(1, -1)
    mu_l = jnp.concatenate([_pad_cols(mu[:, o3:o4], LORA_W), _pad_cols(mu[:, o4:o5], LORA_A),
                            mu[:, o5:o6]], axis=1)
    w2p = _pad_rows(rw_w2, LORA_W).astype(BF16)
    a2p = _pad_rows(rw_a2, LORA_A).astype(BF16)

    z = _in_proj(x2, row(norm1_g), w_in_p)
    prep = _rwkv_prep(z, seq, mu[:, :RW_DIM], mu[:, RW_DIM:2 * RW_DIM], mu[:, 2 * RW_DIM:o3], mu_l,
                      row(rw_w0), row(rw_a0), row(rw_k_k), row(rw_k_a), row(rw_r_k),
                      w2p, a2p, rw_g2.astype(BF16))
    o_rw = _rwkv_chunk(prep[:9], prep[9], prep[10], row(rw_gn_w), row(rw_gn_b), batch, seq)
    o_att = _swa(z, attn_sinks.astype(F32), jnp.tile(row(q_norm_g), (1, ATT_HEADS)),
                 jnp.tile(row(k_norm_g), (1, ATT_KV_HEADS)), seq)
    mix = _branch_mix(o_rw, o_att, w_branch[:RW_DIM].astype(BF16), w_branch[RW_DIM:].astype(BF16), z)
    x1, h2 = _out_proj(mix, w_out.astype(BF16), x2, row(norm2_g))
    act = _ffn_up(h2, w_up.astype(BF16), conv_w.astype(F32), row(conv_b), seq)
    return _ffn_down(act, w_down.astype(BF16), x1)


def kernel(x, norm1_g, w_in, rw_mu, rw_w0, rw_w2, rw_a0, rw_a2, rw_g2, rw_k_k, rw_k_a, rw_r_k,
           rw_gn_w, rw_gn_b, q_norm_g, k_norm_g, attn_sinks, w_branch, w_out, norm2_g, w_up,
           conv_w, conv_b, w_down):
    batch, seq, d = x.shape
    x2 = x.reshape(batch * seq, d)
    params = (norm1_g, w_in, rw_mu, rw_w0, rw_w2, rw_a0, rw_a2, rw_g2, rw_k_k, rw_k_a, rw_r_k,
              rw_gn_w, rw_gn_b, q_norm_g, k_norm_g, attn_sinks, w_branch, w_out, norm2_g, w_up,
              conv_w, conv_b, w_down)
    for layer in range(norm1_g.shape[0]):
        x2 = _layer(x2, batch, seq, *(p[layer] for p in params))
    return x2.reshape(batch, seq, d)
```

```python
import functools

import jax
import jax.numpy as jnp
import numpy as np
from jax import lax
from jax.experimental import pallas as pl
from jax.experimental.pallas import tpu as pltpu

F32 = jnp.float32
BF16 = jnp.bfloat16

LANES = 128
BF16_SUBLANES = 16
VMEM_LIMIT = 56 * 1024 * 1024

D_MODEL = 2048
RW_HEADS = 16
RW_HEAD = 64
RW_DIM = RW_HEADS * RW_HEAD
DECAY_LORA = 96
ICLR_LORA = 96
GATE_LORA = 256
GN_EPS = 64e-5
ATT_HEADS = 16
ATT_KV_HEADS = 4
ATT_GROUP = ATT_HEADS // ATT_KV_HEADS
ATT_HEAD = 64
ATT_Q_DIM = ATT_HEADS * ATT_HEAD
ATT_KV_DIM = ATT_KV_HEADS * ATT_HEAD
WINDOW = 128
BLOCK = 128
D_FF = 5632
RMS_EPS = 1e-6
NEG_BIG = -1e30

C_R = 0
C_K = RW_DIM
C_V = 2 * RW_DIM
C_Q = 3 * RW_DIM
C_GRW = C_Q + ATT_Q_DIM
C_GATT = C_GRW + D_MODEL
C_LORA = C_GATT + D_MODEL
LORA_W = 128
LORA_A = 128
LORA_COLS = LORA_W + LORA_A + GATE_LORA
C_KA = C_LORA + LORA_COLS
C_VA = C_KA + ATT_KV_DIM
Z_COLS = C_VA + ATT_KV_DIM

CHUNK = 64
PAIR = 2 * RW_HEAD
N_PAIRS = RW_DIM // PAIR


def _cparams(sem):
    return pltpu.CompilerParams(dimension_semantics=sem, vmem_limit_bytes=VMEM_LIMIT)


def _mm(a, b):
    return jnp.dot(a.astype(BF16), b.astype(BF16), preferred_element_type=F32)


def _mm_nt(a, b):
    return lax.dot_general(a.astype(BF16), b.astype(BF16), (((1,), (1,)), ((), ())),
                           preferred_element_type=F32)


def _mm_tn(a, b):
    return lax.dot_general(a.astype(BF16), b.astype(BF16), (((0,), (0,)), ((), ())),
                           preferred_element_type=F32)


def _mm_split(m01, x):
    hi = x.astype(BF16)
    lo = (x - hi.astype(F32)).astype(BF16)
    return (jnp.dot(m01, hi, preferred_element_type=F32)
            + jnp.dot(m01, lo, preferred_element_type=F32))


def _mm_split_r(x, m01):
    hi = x.astype(BF16)
    lo = (x - hi.astype(F32)).astype(BF16)
    return (jnp.dot(hi, m01, preferred_element_type=F32)
            + jnp.dot(lo, m01, preferred_element_type=F32))


def _head_ones(n, head, scale):
    r = lax.broadcasted_iota(jnp.int32, (n, n), 0) // head
    c = lax.broadcasted_iota(jnp.int32, (n, n), 1) // head
    return jnp.where(r == c, scale, 0.0).astype(BF16)


def _in_proj_kernel(x_ref, g_ref, w_ref, z_ref, h_ref):
    @pl.when(pl.program_id(1) == 0)
    def _():
        xf = x_ref[...]
        y = xf * lax.rsqrt(jnp.mean(xf * xf, axis=-1, keepdims=True) + RMS_EPS)
        h_ref[...] = (y * g_ref[...]).astype(BF16)

    z_ref[...] = jnp.dot(h_ref[...], w_ref[...], preferred_element_type=F32).astype(BF16)


def _in_proj(x2, g1, w_in_p, tm=1024, tn=1024):
    m, d = x2.shape
    n = w_in_p.shape[1]
    return pl.pallas_call(
        _in_proj_kernel,
        grid=(m // tm, n // tn),
        in_specs=[pl.BlockSpec((tm, d), lambda i, j: (i, 0)),
                  pl.BlockSpec((1, d), lambda i, j: (0, 0)),
                  pl.BlockSpec((d, tn), lambda i, j: (0, j))],
        out_specs=pl.BlockSpec((tm, tn), lambda i, j: (i, j)),
        out_shape=jax.ShapeDtypeStruct((m, n), BF16),
        scratch_shapes=[pltpu.VMEM((tm, d), BF16)],
        compiler_params=_cparams(("parallel", "arbitrary")),
        name="in_proj",
    )(x2, g1, w_in_p)


def _rwkv_prep_kernel(zr_ref, zk_ref, zv_ref, zl_ref, pr_ref, pk_ref, pv_ref, pli_ref,
                      mur_ref, muk_ref, muv_ref, mul_ref,
                      w0_ref, a0_ref, kk_ref, ka_ref, rk_ref, w2_ref, a2_ref, g2_ref,
                      at_ref, rt_ref, bt_ref, kt_ref, bg_ref, kg_ref, v_ref, g_ref, bonus_ref,
                      gT_ref, gm_ref, *, tm, tiles_per_seq):
    first = (pl.program_id(0) % tiles_per_seq) == 0
    keep = jnp.where(first, 0.0, 1.0)

    def shifted(z_ref, p_ref, mu_ref):
        z = z_ref[...].astype(F32)
        prev_last = p_ref[BF16_SUBLANES - 1:BF16_SUBLANES, :].astype(F32) * keep
        row = lax.broadcasted_iota(jnp.int32, z.shape, 0)
        zprev = jnp.where(row == 0, prev_last, pltpu.roll(z, 1, 0))
        return z + (zprev - z) * mu_ref[...]

    r = shifted(zr_ref, pr_ref, mur_ref)
    k = shifted(zk_ref, pk_ref, muk_ref)
    v = shifted(zv_ref, pv_ref, muv_ref)
    lo = shifted(zl_ref, pli_ref, mul_ref)
    wd = lo[:, :LORA_W]
    ad = lo[:, LORA_W:LORA_W + LORA_A]
    gd = lo[:, LORA_W + LORA_A:]

    wpre = w0_ref[...] + _mm(jnp.tanh(wd), w2_ref[...])
    xneg = -wpre
    softplus = jnp.maximum(xneg, 0.0) + jnp.log1p(jnp.exp(-jnp.abs(xneg)))
    w = -softplus - 0.5
    lw = -jnp.exp(w)
    a = jax.nn.sigmoid(a0_ref[...] + _mm(ad, a2_ref[...]))
    g = _mm(jax.nn.sigmoid(gd), g2_ref[...])

    ones_h = _head_ones(PAIR, RW_HEAD, 1.0)
    kk = k * kk_ref[...]
    kk = kk / jnp.maximum(jnp.sqrt(_mm_split_r(kk * kk, ones_h)), 1e-12)
    k2 = k * (1.0 + (a - 1.0) * ka_ref[...])
    bonus = _mm_split_r(r * k2 * rk_ref[...], ones_h) * v
    ka_vec = -kk
    kb_vec = kk * a

    ti = lax.broadcasted_iota(jnp.int32, (tm, tm), 0)
    si = lax.broadcasted_iota(jnp.int32, (tm, tm), 1)
    same = (ti // CHUNK) == (si // CHUNK)
    tri = jnp.where(same & (si <= ti), 1.0, 0.0).astype(BF16)
    mid = jnp.where(same & ((si % CHUNK) < CHUNK // 2), 1.0, 0.0).astype(BF16)
    full = jnp.where(same, 1.0, 0.0).astype(BF16)
    c = _mm_split(tri, lw)
    cm = _mm_split(mid, lw)
    cT = _mm_split(full, lw)

    e_in = jnp.exp(c - cm)
    e_out = jnp.exp(cm - c)
    e_end = jnp.exp(cT - c)
    at_ref[...] = (ka_vec * jnp.exp(c - lw - cm)).astype(BF16)
    rt_ref[...] = (r * e_in).astype(BF16)
    bt_ref[...] = (kb_vec * e_out).astype(BF16)
    kt_ref[...] = (k2 * e_out).astype(BF16)
    bg_ref[...] = (kb_vec * e_end).astype(BF16)
    kg_ref[...] = (k2 * e_end).astype(BF16)
    v_ref[...] = v.astype(BF16)
    g_ref[...] = g.astype(BF16)
    bonus_ref[...] = bonus.astype(BF16)

    nc = tm // CHUNK
    cj = lax.broadcasted_iota(jnp.int32, (nc, tm), 0)
    sj = lax.broadcasted_iota(jnp.int32, (nc, tm), 1)
    in_chunk = (sj // CHUNK) == cj
    sel_full = jnp.where(in_chunk, 1.0, 0.0).astype(BF16)
    sel_mid = jnp.where(in_chunk & ((sj % CHUNK) < CHUNK // 2), 1.0, 0.0).astype(BF16)
    g_end = jnp.exp(_mm_split(sel_full, lw))
    g_mid = jnp.exp(_mm_split(sel_mid, lw))
    for ch in range(nc):
        gT_ref[ch] = g_end[ch:ch + 1, :]
        gm_ref[ch] = g_mid[ch:ch + 1, :]


def _rwkv_prep(z, seq, mu_r, mu_k, mu_v, mu_l, w0, a0, k_k, k_a, r_k, w2p, a2p, g2, tm=256):
    m = z.shape[0]
    nt = m // tm
    pb = tm // BF16_SUBLANES
    lb = LORA_COLS // PAIR

    def cur(col0):
        return pl.BlockSpec((tm, PAIR), lambda i, p, c0=col0 // PAIR: (i, c0 + p))

    def prev(col0):
        return pl.BlockSpec((BF16_SUBLANES, PAIR),
                            lambda i, p, c0=col0 // PAIR: (jnp.maximum(i * pb - 1, 0), c0 + p))

    def vec(_=None):
        return pl.BlockSpec((1, PAIR), lambda i, p: (0, p))

    in_specs = [cur(C_R), cur(C_K), cur(C_V),
                pl.BlockSpec((tm, LORA_COLS), lambda i, p: (i, C_LORA // LORA_COLS)),
                prev(C_R), prev(C_K), prev(C_V),
                pl.BlockSpec((BF16_SUBLANES, LORA_COLS),
                             lambda i, p: (jnp.maximum(i * pb - 1, 0), C_LORA // LORA_COLS)),
                vec(), vec(), vec(),
                pl.BlockSpec((1, LORA_COLS), lambda i, p: (0, 0)),
                vec(), vec(), vec(), vec(), vec(),
                pl.BlockSpec((LORA_W, PAIR), lambda i, p: (0, p)),
                pl.BlockSpec((LORA_A, PAIR), lambda i, p: (0, p)),
                pl.BlockSpec((GATE_LORA, PAIR), lambda i, p: (0, p))]
    big = jax.ShapeDtypeStruct((m, RW_DIM), BF16)
    per_chunk = jax.ShapeDtypeStruct((m // CHUNK, 1, RW_DIM), F32)
    big_spec = pl.BlockSpec((tm, PAIR), lambda i, p: (i, p))
    pc_spec = pl.BlockSpec((tm // CHUNK, 1, PAIR), lambda i, p: (i, 0, p))
    del lb
    return pl.pallas_call(
        functools.partial(_rwkv_prep_kernel, tm=tm, tiles_per_seq=seq // tm),
        grid=(nt, N_PAIRS),
        in_specs=in_specs,
        out_specs=[big_spec] * 9 + [pc_spec] * 2,
        out_shape=[big] * 9 + [per_chunk] * 2,
        compiler_params=_cparams(("parallel", "parallel")),
        name="rwkv_prep",
    )(z, z, z, z, z, z, z, z, mu_r, mu_k, mu_v, mu_l, w0, a0, k_k, k_a, r_k, w2p, a2p, g2)


def _rwkv_chunk_kernel(at_ref, rt_ref, bt_ref, kt_ref, bg_ref, kg_ref, v_ref, g_ref, bonus_ref,
                       gT_ref, gm_ref, gnw_ref, gnb_ref, o_ref, s_ref, *, chunks, pairs):
    @pl.when(pl.program_id(2) == 0)
    def _():
        s_ref[...] = jnp.zeros_like(s_ref)

    T = CHUNK
    lane = lax.broadcasted_iota(jnp.int32, (T, PAIR), 1)
    head0 = lane < RW_HEAD
    ri = lax.broadcasted_iota(jnp.int32, (2 * T, 2 * T), 0)
    ci = lax.broadcasted_iota(jnp.int32, (2 * T, 2 * T), 1)
    same = (ri // T) == (ci // T)
    strict = same & ((ri % T) > (ci % T))
    incl = same & ((ri % T) >= (ci % T))
    eye = jnp.where(ri == ci, 1.0, 0.0)
    mean_h = _head_ones(PAIR, RW_HEAD, 1.0 / RW_HEAD)

    items = [(p, c) for c in range(chunks) for p in range(pairs)]
    rows = lambda c: slice(c * T, (c + 1) * T)
    cols = lambda p: slice(p * PAIR, (p + 1) * PAIR)

    def stack(ref):
        out = []
        for p, c in items:
            x = ref[rows(c), cols(p)]
            zero = jnp.zeros_like(x)
            out.append(jnp.concatenate([jnp.where(head0, x, zero), jnp.where(head0, zero, x)], axis=0))
        return out

    each = lambda f, *ls: [f(*xs) for xs in zip(*ls)]
    La, Lr, Rb, Rk = stack(at_ref), stack(rt_ref), stack(bt_ref), stack(kt_ref)
    Rbg, Rkg, Vs = stack(bg_ref), stack(kg_ref), stack(v_ref)
    gm_row = [gm_ref[c][:, cols(p)] for p, c in items]

    N = each(lambda a, b: jnp.where(strict, _mm_nt(a, b), 0.0), La, Rb)
    Aak = each(lambda a, b: jnp.where(strict, _mm_nt(a, b), 0.0), La, Rk)
    Arb = each(lambda a, b: jnp.where(incl, _mm_nt(a, b), 0.0), Lr, Rb)
    Ark = each(lambda a, b: jnp.where(incl, _mm_nt(a, b), 0.0), Lr, Rk)
    W = each(lambda n: eye + n, N)
    P = N
    span = 2
    while span < T:
        P = each(lambda p_: _mm(p_, p_), P)
        W = each(lambda w, p_: w + _mm(w, p_), W, P)
        span *= 2
    AkV = each(_mm, Aak, Vs)
    X = each(lambda w, la, akv: _mm(w, jnp.concatenate([la, akv.astype(BF16)], axis=1)), W, La, AkV)
    Z = each(lambda arb, x, lr, ark, vs:
             _mm(arb, x) + jnp.concatenate([lr.astype(F32), _mm(ark, vs)], axis=1), Arb, X, Lr, Ark, Vs)
    Q = each(lambda z, gm: z[:, :PAIR] * gm, Z, gm_row)
    Y0 = [z[:, PAIR:] for z in Z]
    Mbd = each(lambda rbg, x, gm: _mm_tn(rbg, x[:, :PAIR]) * gm, Rbg, X, gm_row)
    NcT = each(lambda x, rbg, vs, rkg: _mm_tn(x[:, PAIR:], rbg) + _mm_tn(vs, rkg), X, Rbg, Vs, Rkg)

    S = [s_ref[p] for p in range(pairs)]
    for c in range(chunks):
        for p in range(pairs):
            i = c * pairs + p
            Ys = _mm_nt(Q[i], S[p]) + Y0[i]
            S[p] = S[p] * gT_ref[c][:, cols(p)] + _mm_nt(S[p], Mbd[i]) + NcT[i]
            y = Ys[:T, :] + Ys[T:, :]
            mu = _mm_split_r(y, mean_h)
            d = y - mu
            var = _mm_split_r(d * d, mean_h)
            yn = d * lax.rsqrt(var + GN_EPS) * gnw_ref[:, cols(p)] + gnb_ref[:, cols(p)]
            out = (yn + bonus_ref[rows(c), cols(p)].astype(F32)) * g_ref[rows(c), cols(p)].astype(F32)
            o_ref[rows(c), cols(p)] = out.astype(BF16)
    for p in range(pairs):
        s_ref[p] = S[p]


def _rwkv_chunk(ops, gT, gm, gn_w, gn_b, batch, seq, tb=128, pairs=8):
    m = ops[0].shape[0]
    nb = seq // tb
    chunks = tb // CHUNK
    width = pairs * PAIR
    big_spec = pl.BlockSpec((tb, width), lambda b, p, j: (b * nb + j, p))
    pc_spec = pl.BlockSpec((chunks, 1, width), lambda b, p, j: (b * nb + j, 0, p))
    vec = pl.BlockSpec((1, width), lambda b, p, j: (0, p))
    return pl.pallas_call(
        functools.partial(_rwkv_chunk_kernel, chunks=chunks, pairs=pairs),
        grid=(batch, N_PAIRS // pairs, nb),
        in_specs=[big_spec] * 9 + [pc_spec] * 2 + [vec, vec],
        out_specs=big_spec,
        out_shape=jax.ShapeDtypeStruct((m, RW_DIM), BF16),
        scratch_shapes=[pltpu.VMEM((pairs, PAIR, PAIR), F32)],
        compiler_params=_cparams(("parallel", "parallel", "arbitrary")),
        name="rwkv_chunk",
    )(*ops, gT, gm, gn_w, gn_b)


def _swa_kernel(sink_ref, q_ref, kc_ref, kp_ref, vc_ref, vp_ref, qg_ref, kg_ref, o_ref, *,
                blocks_per_seq, slopes):
    first = (pl.program_id(0) % blocks_per_seq) == 0
    mean_h = _head_ones(LANES, ATT_HEAD, 1.0 / ATT_HEAD)

    def head_rms(x, gain):
        parts = []
        for b in range(x.shape[1] // LANES):
            xb = x[:, b * LANES:(b + 1) * LANES]
            ms = _mm_split_r(xb * xb, mean_h)
            parts.append(xb * lax.rsqrt(ms + RMS_EPS))
        return jnp.concatenate(parts, axis=1) * gain

    q = head_rms(q_ref[...].astype(F32), qg_ref[...]).astype(BF16)
    kcat = jnp.concatenate([kp_ref[...], kc_ref[...]], axis=0).astype(F32)
    kcat = head_rms(kcat, kg_ref[...]).astype(BF16)
    vcat = jnp.concatenate([vp_ref[...], vc_ref[...]], axis=0)

    qi = lax.broadcasted_iota(jnp.int32, (BLOCK, 2 * BLOCK), 0)
    kj = lax.broadcasted_iota(jnp.int32, (BLOCK, 2 * BLOCK), 1)
    dist_i = BLOCK + qi - kj
    first_key = jnp.where(first, BLOCK, 0)
    valid = (dist_i >= 0) & (dist_i < WINDOW) & (kj >= first_key)
    dist = dist_i.astype(F32)

    outs = []
    for h in range(ATT_HEADS):
        j = h // ATT_GROUP
        qh = q[:, h * ATT_HEAD:(h + 1) * ATT_HEAD]
        kh = kcat[:, j * ATT_HEAD:(j + 1) * ATT_HEAD]
        vh = vcat[:, j * ATT_HEAD:(j + 1) * ATT_HEAD]
        s = _mm_nt(qh, kh) * (ATT_HEAD ** -0.5)
        s = jnp.where(valid, s + (-slopes[h]) * dist, NEG_BIG)
        sink = sink_ref[h]
        mx = jnp.maximum(jnp.max(s, axis=-1, keepdims=True), sink)
        p = jnp.exp(s - mx)
        denom = jnp.sum(p, axis=-1, keepdims=True) + jnp.exp(sink - mx)
        p = p / denom
        outs.append(_mm(p, vh))
    o_ref[...] = jnp.concatenate(outs, axis=1).astype(BF16)


def _swa(z, sinks, q_gain_t, k_gain_t, seq):
    m = z.shape[0]
    nblk = m // BLOCK
    bps = seq // BLOCK
    slopes = tuple(float(s) for s in
                   np.exp2(-8.0 * np.arange(1, ATT_HEADS + 1, dtype=np.float32) / ATT_HEADS).astype(np.float32))
    kv_cur = lambda c0: pl.BlockSpec((BLOCK, ATT_KV_DIM), lambda n, c=c0 // ATT_KV_DIM: (n, c))
    kv_prev = lambda c0: pl.BlockSpec((BLOCK, ATT_KV_DIM),
                                      lambda n, c=c0 // ATT_KV_DIM: (jnp.maximum(n - 1, 0), c))
    return pl.pallas_call(
        functools.partial(_swa_kernel, blocks_per_seq=bps, slopes=slopes),
        grid=(nblk,),
        in_specs=[pl.BlockSpec(memory_space=pltpu.SMEM),
                  pl.BlockSpec((BLOCK, ATT_Q_DIM), lambda n: (n, C_Q // ATT_Q_DIM)),
                  kv_cur(C_KA), kv_prev(C_KA), kv_cur(C_VA), kv_prev(C_VA),
                  pl.BlockSpec((1, ATT_Q_DIM), lambda n: (0, 0)),
                  pl.BlockSpec((1, ATT_KV_DIM), lambda n: (0, 0))],
        out_specs=pl.BlockSpec((BLOCK, ATT_Q_DIM), lambda n: (n, 0)),
        out_shape=jax.ShapeDtypeStruct((m, ATT_Q_DIM), BF16),
        compiler_params=_cparams(("parallel",)),
        name="swa",
    )(sinks, z, z, z, z, z, q_gain_t, k_gain_t)


def _branch_mix_kernel(orw_ref, oatt_ref, wb1_ref, wb2_ref, zg1_ref, zg2_ref, m_ref):
    p_rw = jnp.dot(orw_ref[...], wb1_ref[...], preferred_element_type=F32)
    p_att = jnp.dot(oatt_ref[...], wb2_ref[...], preferred_element_type=F32)
    g_rw = jax.nn.sigmoid(zg1_ref[...].astype(F32))
    g_att = jax.nn.sigmoid(zg2_ref[...].astype(F32))
    m_ref[...] = (g_rw * p_rw + g_att * p_att).astype(BF16)


def _branch_mix(o_rw, o_att, wb1, wb2, z, tm=1024, tn=1024):
    m = o_rw.shape[0]
    return pl.pallas_call(
        _branch_mix_kernel,
        grid=(m // tm, D_MODEL // tn),
        in_specs=[pl.BlockSpec((tm, RW_DIM), lambda i, j: (i, 0)),
                  pl.BlockSpec((tm, ATT_Q_DIM), lambda i, j: (i, 0)),
                  pl.BlockSpec((RW_DIM, tn), lambda i, j: (0, j)),
                  pl.BlockSpec((ATT_Q_DIM, tn), lambda i, j: (0, j)),
                  pl.BlockSpec((tm, tn), lambda i, j: (i, C_GRW // tn + j)),
                  pl.BlockSpec((tm, tn), lambda i, j: (i, C_GATT // tn + j))],
        out_specs=pl.BlockSpec((tm, tn), lambda i, j: (i, j)),
        out_shape=jax.ShapeDtypeStruct((m, D_MODEL), BF16),
        compiler_params=_cparams(("parallel", "arbitrary")),
        name="branch_mix",
    )(o_rw, o_att, wb1, wb2, z, z)


def _out_proj_kernel(m_ref, w_ref, x_ref, g_ref, x1_ref, h2_ref):
    x1 = x_ref[...] + jnp.dot(m_ref[...], w_ref[...], preferred_element_type=F32)
    x1_ref[...] = x1
    y = x1 * lax.rsqrt(jnp.mean(x1 * x1, axis=-1, keepdims=True) + RMS_EPS)
    h2_ref[...] = (y * g_ref[...]).astype(BF16)


def _out_proj(mix, w_out, x2, g2, tm=256):
    m = x2.shape[0]
    return pl.pallas_call(
        _out_proj_kernel,
        grid=(m // tm,),
        in_specs=[pl.BlockSpec((tm, D_MODEL), lambda i: (i, 0)),
                  pl.BlockSpec((D_MODEL, D_MODEL), lambda i: (0, 0)),
                  pl.BlockSpec((tm, D_MODEL), lambda i: (i, 0)),
                  pl.BlockSpec((1, D_MODEL), lambda i: (0, 0))],
        out_specs=[pl.BlockSpec((tm, D_MODEL), lambda i: (i, 0)),
                   pl.BlockSpec((tm, D_MODEL), lambda i: (i, 0))],
        out_shape=[jax.ShapeDtypeStruct((m, D_MODEL), F32),
                   jax.ShapeDtypeStruct((m, D_MODEL), BF16)],
        compiler_params=_cparams(("parallel",)),
        name="out_proj",
    )(mix, w_out, x2, g2)


def _ffn_up_kernel(h_ref, wv_ref, wg_ref, cwv_ref, cwg_ref, cbv_ref, cbg_ref, a_ref,
                   uv_ref, ug_ref, *, tm, tiles_per_seq):
    halo = 8
    first = (pl.program_id(1) % tiles_per_seq) == 0

    def conv(u_ref, w_ref, cw_ref, cb_ref):
        @pl.when(first)
        def _():
            u_ref[0:halo, :] = jnp.zeros((halo, u_ref.shape[1]), F32)

        u_ref[halo:halo + tm, :] = jnp.dot(h_ref[...], w_ref[...], preferred_element_type=F32)
        c = (u_ref[halo - 2:halo - 2 + tm, :] * cw_ref[0:1, :]
             + u_ref[halo - 1:halo - 1 + tm, :] * cw_ref[1:2, :]
             + u_ref[halo:halo + tm, :] * cw_ref[2:3, :] + cb_ref[...])
        u_ref[0:halo, :] = u_ref[tm:tm + halo, :]
        return c

    val = conv(uv_ref, wv_ref, cwv_ref, cbv_ref)
    gate = conv(ug_ref, wg_ref, cwg_ref, cbg_ref)
    a_ref[...] = (gate * jax.nn.sigmoid(gate) * val).astype(BF16)


def _ffn_up(h2, w_up, conv_w, conv_b, seq, tm=1024, tn=512):
    m = h2.shape[0]
    nj = D_FF // tn
    return pl.pallas_call(
        functools.partial(_ffn_up_kernel, tm=tm, tiles_per_seq=seq // tm),
        grid=(nj, m // tm),
        in_specs=[pl.BlockSpec((tm, D_MODEL), lambda j, i: (i, 0)),
                  pl.BlockSpec((D_MODEL, tn), lambda j, i: (0, j)),
                  pl.BlockSpec((D_MODEL, tn), lambda j, i: (0, nj + j)),
                  pl.BlockSpec((3, tn), lambda j, i: (0, j)),
                  pl.BlockSpec((3, tn), lambda j, i: (0, nj + j)),
                  pl.BlockSpec((1, tn), lambda j, i: (0, j)),
                  pl.BlockSpec((1, tn), lambda j, i: (0, nj + j))],
        out_specs=pl.BlockSpec((tm, tn), lambda j, i: (i, j)),
        out_shape=jax.ShapeDtypeStruct((m, D_FF), BF16),
        scratch_shapes=[pltpu.VMEM((tm + 8, tn), F32), pltpu.VMEM((tm + 8, tn), F32)],
        compiler_params=_cparams(("parallel", "arbitrary")),
        name="ffn_up",
    )(h2, w_up, w_up, conv_w, conv_w, conv_b, conv_b)


def _ffn_down_kernel(a_ref, w_ref, x1_ref, o_ref):
    o_ref[...] = x1_ref[...] + jnp.dot(a_ref[...], w_ref[...], preferred_element_type=F32)


def _ffn_down(act, w_down, x1, tm=512, tn=1024):
    m = act.shape[0]
    return pl.pallas_call(
        _ffn_down_kernel,
        grid=(m // tm, D_MODEL // tn),
        in_specs=[pl.BlockSpec((tm, D_FF), lambda i, j: (i, 0)),
                  pl.BlockSpec((D_FF, tn), lambda i, j: (0, j)),
                  pl.BlockSpec((tm, tn), lambda i, j: (i, j))],
        out_specs=pl.BlockSpec((tm, tn), lambda i, j: (i, j)),
        out_shape=jax.ShapeDtypeStruct((m, D_MODEL), F32),
        compiler_params=_cparams(("parallel", "arbitrary")),
        name="ffn_down",
    )(act, w_down, x1)


def _pad_cols(w, n):
    return jnp.pad(w, ((0, 0), (0, n - w.shape[1])))


def _pad_rows(w, n):
    return jnp.pad(w, ((0, n - w.shape[0]), (0, 0)))


def _layer(x2, batch, seq, norm1_g, w_in, rw_mu, rw_w0, rw_w2, rw_a0, rw_a2, rw_g2, rw_k_k, rw_k_a,
           rw_r_k, rw_gn_w, rw_gn_b, q_norm_g, k_norm_g, attn_sinks, w_branch, w_out,
           norm2_g, w_up, conv_w, conv_b, w_down):
    row = lambda v: v.reshape(1, -1).astype(F32)
    o3 = 3 * RW_DIM
    o4 = o3 + DECAY_LORA
    o5 = o4 + ICLR_LORA
    o6 = o5 + GATE_LORA
    oq = o6 + ATT_Q_DIM
    ok = oq + ATT_KV_DIM
    ov = ok + ATT_KV_DIM
    w_in_p = jnp.concatenate([
        w_in[:, :o3], w_in[:, o6:oq], w_in[:, ov:],
        _pad_cols(w_in[:, o3:o4], LORA_W), _pad_cols(w_in[:, o4:o5], LORA_A), w_in[:, o5:o6],
        w_in[:, oq:ok], w_in[:, ok:ov]], axis=1).astype(BF16)
    mu = rw_mu.reshape(1, -1)
    mu_l = jnp.concatenate([_pad_cols(mu[:, o3:o4], LORA_W), _pad_cols(mu[:, o4:o5], LORA_A),
                            mu[:, o5:o6]], axis=1)
    w2p = _pad_rows(rw_w2, LORA_W).astype(BF16)
    a2p = _pad_rows(rw_a2, LORA_A).astype(BF16)

    z = _in_proj(x2, row(norm1_g), w_in_p)
    prep = _rwkv_prep(z, seq, mu[:, :RW_DIM], mu[:, RW_DIM:2 * RW_DIM], mu[:, 2 * RW_DIM:o3], mu_l,
                      row(rw_w0), row(rw_a0), row(rw_k_k), row(rw_k_a), row(rw_r_k),
                      w2p, a2p, rw_g2.astype(BF16))
    o_rw = _rwkv_chunk(prep[:9], prep[9], prep[10], row(rw_gn_w), row(rw_gn_b), batch, seq)
    o_att = _swa(z, attn_sinks.astype(F32), jnp.tile(row(q_norm_g), (1, ATT_HEADS)),
                 jnp.tile(row(k_norm_g), (1, ATT_KV_HEADS)), seq)
    mix = _branch_mix(o_rw, o_att, w_branch[:RW_DIM].astype(BF16), w_branch[RW_DIM:].astype(BF16), z)
    x1, h2 = _out_proj(mix, w_out.astype(BF16), x2, row(norm2_g))
    act = _ffn_up(h2, w_up.astype(BF16), conv_w.astype(F32), row(conv_b), seq)
    return _ffn_down(act, w_down.astype(BF16), x1)


def kernel(x, norm1_g, w_in, rw_mu, rw_w0, rw_w2, rw_a0, rw_a2, rw_g2, rw_k_k, rw_k_a, rw_r_k,
           rw_gn_w, rw_gn_b, q_norm_g, k_norm_g, attn_sinks, w_branch, w_out, norm2_g, w_up,
           conv_w, conv_b, w_down):
    batch, seq, d = x.shape
    x2 = x.reshape(batch * seq, d)
    params = (norm1_g, w_in, rw_mu, rw_w0, rw_w2, rw_a0, rw_a2, rw_g2, rw_k_k, rw_k_a, rw_r_k,
              rw_gn_w, rw_gn_b, q_norm_g, k_norm_g, attn_sinks, w_branch, w_out, norm2_g, w_up,
              conv_w, conv_b, w_down)
    for layer in range(norm1_g.shape[0]):
        x2 = _layer(x2, batch, seq, *(p[layer] for p in params))
    return x2.reshape(batch, seq, d)
```

```python
import functools

import jax
import jax.numpy as jnp
import numpy as np
from jax import lax
from jax.experimental import pallas as pl
from jax.experimental.pallas import tpu as pltpu

F32 = jnp.float32
BF16 = jnp.bfloat16

LANES = 128
BF16_SUBLANES = 16
VMEM_LIMIT = 56 * 1024 * 1024

D_MODEL = 2048
RW_HEADS = 16
RW_HEAD = 64
RW_DIM = RW_HEADS * RW_HEAD
DECAY_LORA = 96
ICLR_LORA = 96
GATE_LORA = 256
GN_EPS = 64e-5
ATT_HEADS = 16
ATT_KV_HEADS = 4
ATT_GROUP = ATT_HEADS // ATT_KV_HEADS
ATT_HEAD = 64
ATT_Q_DIM = ATT_HEADS * ATT_HEAD
ATT_KV_DIM = ATT_KV_HEADS * ATT_HEAD
WINDOW = 128
BLOCK = 128
D_FF = 5632
RMS_EPS = 1e-6
NEG_BIG = -1e30

C_R = 0
C_K = RW_DIM
C_V = 2 * RW_DIM
C_Q = 3 * RW_DIM
C_GRW = C_Q + ATT_Q_DIM
C_GATT = C_GRW + D_MODEL
C_LORA = C_GATT + D_MODEL
LORA_W = 128
LORA_A = 128
LORA_COLS = LORA_W + LORA_A + GATE_LORA
C_KA = C_LORA + LORA_COLS
C_VA = C_KA + ATT_KV_DIM
Z_COLS = C_VA + ATT_KV_DIM

CHUNK = 64
PAIR = 2 * RW_HEAD
N_PAIRS = RW_DIM // PAIR


def _cparams(sem):
    return pltpu.CompilerParams(dimension_semantics=sem, vmem_limit_bytes=VMEM_LIMIT)


def _mm(a, b):
    return jnp.dot(a.astype(BF16), b.astype(BF16), preferred_element_type=F32)


def _mm_nt(a, b):
    return lax.dot_general(a.astype(BF16), b.astype(BF16), (((1,), (1,)), ((), ())),
                           preferred_element_type=F32)


def _mm_tn(a, b):
    return lax.dot_general(a.astype(BF16), b.astype(BF16), (((0,), (0,)), ((), ())),
                           preferred_element_type=F32)


def _mm_split(m01, x):
    hi = x.astype(BF16)
    lo = (x - hi.astype(F32)).astype(BF16)
    return (jnp.dot(m01, hi, preferred_element_type=F32)
            + jnp.dot(m01, lo, preferred_element_type=F32))


def _mm_split_r(x, m01):
    hi = x.astype(BF16)
    lo = (x - hi.astype(F32)).astype(BF16)
    return (jnp.dot(hi, m01, preferred_element_type=F32)
            + jnp.dot(lo, m01, preferred_element_type=F32))


def _head_ones(n, head, scale):
    r = lax.broadcasted_iota(jnp.int32, (n, n), 0) // head
    c = lax.broadcasted_iota(jnp.int32, (n, n), 1) // head
    return jnp.where(r == c, scale, 0.0).astype(BF16)


def _in_proj_kernel(x_ref, g_ref, w_ref, z_ref, h_ref):
    @pl.when(pl.program_id(1) == 0)
    def _():
        xf = x_ref[...]
        y = xf * lax.rsqrt(jnp.mean(xf * xf, axis=-1, keepdims=True) + RMS_EPS)
        h_ref[...] = (y * g_ref[...]).astype(BF16)

    z_ref[...] = jnp.dot(h_ref[...], w_ref[...], preferred_element_type=F32).astype(BF16)


def _in_proj(x2, g1, w_in_p, tm=1024, tn=1024):
    m, d = x2.shape
    n = w_in_p.shape[1]
    return pl.pallas_call(
        _in_proj_kernel,
        grid=(m // tm, n // tn),
        in_specs=[pl.BlockSpec((tm, d), lambda i, j: (i, 0)),
                  pl.BlockSpec((1, d), lambda i, j: (0, 0)),
                  pl.BlockSpec((d, tn), lambda i, j: (0, j))],
        out_specs=pl.BlockSpec((tm, tn), lambda i, j: (i, j)),
        out_shape=jax.ShapeDtypeStruct((m, n), BF16),
        scratch_shapes=[pltpu.VMEM((tm, d), BF16)],
        compiler_params=_cparams(("parallel", "arbitrary")),
        name="in_proj",
    )(x2, g1, w_in_p)


def _rwkv_prep_kernel(zr_ref, zk_ref, zv_ref, zl_ref, pr_ref, pk_ref, pv_ref, pli_ref,
                      mur_ref, muk_ref, muv_ref, mul_ref,
                      w0_ref, a0_ref, kk_ref, ka_ref, rk_ref, w2_ref, a2_ref, g2_ref,
                      at_ref, rt_ref, bt_ref, kt_ref, bg_ref, kg_ref, v_ref, g_ref, bonus_ref,
                      gT_ref, gm_ref, *, tm, tiles_per_seq):
    first = (pl.program_id(0) % tiles_per_seq) == 0
    keep = jnp.where(first, 0.0, 1.0)
    nc = tm // CHUNK
    last = slice(BF16_SUBLANES - 1, BF16_SUBLANES)

    def shifted(z, prev_last, mu):
        z = z.astype(F32)
        row = lax.broadcasted_iota(jnp.int32, z.shape, 0)
        zprev = jnp.where(row == 0, prev_last.astype(F32) * keep, pltpu.roll(z, 1, 0))
        return z + (zprev - z) * mu

    lo = shifted(zl_ref[...], pli_ref[last, :], mul_ref[...])
    tanh_wd = jnp.tanh(lo[:, :LORA_W]).astype(BF16)
    ad = lo[:, LORA_W:LORA_W + LORA_A].astype(BF16)
    sig_gd = jax.nn.sigmoid(lo[:, LORA_W + LORA_A:]).astype(BF16)

    ti = lax.broadcasted_iota(jnp.int32, (tm, tm), 0)
    si = lax.broadcasted_iota(jnp.int32, (tm, tm), 1)
    tri = jnp.where(((ti // CHUNK) == (si // CHUNK)) & (si <= ti), 1.0, 0.0).astype(BF16)
    ones_h = _head_ones(PAIR, RW_HEAD, 1.0)

    for p in range(N_PAIRS):
        cs = slice(p * PAIR, (p + 1) * PAIR)
        r = shifted(zr_ref[:, cs], pr_ref[last, cs], mur_ref[:, cs])
        k = shifted(zk_ref[:, cs], pk_ref[last, cs], muk_ref[:, cs])
        v = shifted(zv_ref[:, cs], pv_ref[last, cs], muv_ref[:, cs])

        wpre = w0_ref[:, cs] + jnp.dot(tanh_wd, w2_ref[:, cs], preferred_element_type=F32)
        xneg = -wpre
        softplus = jnp.maximum(xneg, 0.0) + jnp.log1p(jnp.exp(-jnp.abs(xneg)))
        w = -softplus - 0.5
        lw = -jnp.exp(w)
        a = jax.nn.sigmoid(a0_ref[:, cs] + jnp.dot(ad, a2_ref[:, cs], preferred_element_type=F32))
        g = jnp.dot(sig_gd, g2_ref[:, cs], preferred_element_type=F32)

        kk = k * kk_ref[:, cs]
        kk = kk / jnp.maximum(jnp.sqrt(_mm(kk * kk, ones_h)), 1e-12)
        k2 = k * (1.0 + (a - 1.0) * ka_ref[:, cs])
        bonus = _mm(r * k2 * rk_ref[:, cs], ones_h) * v
        ka_vec = -kk
        kb_vec = kk * a

        c = _mm_split(tri, lw)
        c3 = c.reshape(nc, CHUNK, PAIR)
        c_mid = c3[:, CHUNK // 2 - 1:CHUNK // 2, :]
        c_end = c3[:, CHUNK - 1:CHUNK, :]
        cm = jnp.broadcast_to(c_mid, c3.shape).reshape(tm, PAIR)
        cT = jnp.broadcast_to(c_end, c3.shape).reshape(tm, PAIR)

        e_in = jnp.exp(c - cm)
        e_out = jnp.exp(cm - c)
        e_end = jnp.exp(cT - c)
        at_ref[:, cs] = (ka_vec * jnp.exp(c - lw - cm)).astype(BF16)
        rt_ref[:, cs] = (r * e_in).astype(BF16)
        bt_ref[:, cs] = (kb_vec * e_out).astype(BF16)
        kt_ref[:, cs] = (k2 * e_out).astype(BF16)
        bg_ref[:, cs] = (kb_vec * e_end).astype(BF16)
        kg_ref[:, cs] = (k2 * e_end).astype(BF16)
        v_ref[:, cs] = v.astype(BF16)
        g_ref[:, cs] = g.astype(BF16)
        bonus_ref[:, cs] = bonus.astype(BF16)
        gT_ref[:, :, cs] = jnp.exp(c_end)
        gm_ref[:, :, cs] = jnp.exp(c_mid)


def _rwkv_prep(z, seq, mu_r, mu_k, mu_v, mu_l, w0, a0, k_k, k_a, r_k, w2p, a2p, g2, tm=256):
    m = z.shape[0]
    pb = tm // BF16_SUBLANES

    def cur(col0, width):
        return pl.BlockSpec((tm, width), lambda i, c0=col0 // width: (i, c0))

    def prev(col0, width):
        return pl.BlockSpec((BF16_SUBLANES, width),
                            lambda i, c0=col0 // width: (jnp.maximum(i * pb - 1, 0), c0))

    def whole(rows, width):
        return pl.BlockSpec((rows, width), lambda i: (0, 0))

    in_specs = [cur(C_R, RW_DIM), cur(C_K, RW_DIM), cur(C_V, RW_DIM), cur(C_LORA, LORA_COLS),
                prev(C_R, RW_DIM), prev(C_K, RW_DIM), prev(C_V, RW_DIM), prev(C_LORA, LORA_COLS),
                whole(1, RW_DIM), whole(1, RW_DIM), whole(1, RW_DIM), whole(1, LORA_COLS),
                whole(1, RW_DIM), whole(1, RW_DIM), whole(1, RW_DIM), whole(1, RW_DIM), whole(1, RW_DIM),
                whole(LORA_W, RW_DIM), whole(LORA_A, RW_DIM), whole(GATE_LORA, RW_DIM)]
    big = jax.ShapeDtypeStruct((m, RW_DIM), BF16)
    per_chunk = jax.ShapeDtypeStruct((m // CHUNK, 1, RW_DIM), F32)
    big_spec = pl.BlockSpec((tm, RW_DIM), lambda i: (i, 0))
    pc_spec = pl.BlockSpec((tm // CHUNK, 1, RW_DIM), lambda i: (i, 0, 0))
    return pl.pallas_call(
        functools.partial(_rwkv_prep_kernel, tm=tm, tiles_per_seq=seq // tm),
        grid=(m // tm,),
        in_specs=in_specs,
        out_specs=[big_spec] * 9 + [pc_spec] * 2,
        out_shape=[big] * 9 + [per_chunk] * 2,
        compiler_params=_cparams(("parallel",)),
        name="rwkv_prep",
    )(z, z, z, z, z, z, z, z, mu_r, mu_k, mu_v, mu_l, w0, a0, k_k, k_a, r_k, w2p, a2p, g2)


def _rwkv_chunk_kernel(at_ref, rt_ref, bt_ref, kt_ref, bg_ref, kg_ref, v_ref, g_ref, bonus_ref,
                       gT_ref, gm_ref, gnw_ref, gnb_ref, o_ref, s_ref, *, chunks, pairs):
    @pl.when(pl.program_id(2) == 0)
    def _():
        s_ref[...] = jnp.zeros_like(s_ref)

    T = CHUNK
    lane = lax.broadcasted_iota(jnp.int32, (T, PAIR), 1)
    head0 = lane < RW_HEAD
    ri = lax.broadcasted_iota(jnp.int32, (2 * T, 2 * T), 0)
    ci = lax.broadcasted_iota(jnp.int32, (2 * T, 2 * T), 1)
    same = (ri // T) == (ci // T)
    strict = same & ((ri % T) > (ci % T))
    incl = same & ((ri % T) >= (ci % T))
    eye = jnp.where(ri == ci, 1.0, 0.0)
    own = (ri // T) == (ci // RW_HEAD)

    items = [(p, c) for c in range(chunks) for p in range(pairs)]
    rows = lambda c: slice(c * T, (c + 1) * T)
    cols = lambda p: slice(p * PAIR, (p + 1) * PAIR)

    def stack(ref):
        out = []
        for p, c in items:
            x = ref[rows(c), cols(p)]
            zero = jnp.zeros_like(x)
            out.append(jnp.concatenate([jnp.where(head0, x, zero), jnp.where(head0, zero, x)], axis=0))
        return out

    each = lambda f, *ls: [f(*xs) for xs in zip(*ls)]
    La, Lr, Rb, Rk = stack(at_ref), stack(rt_ref), stack(bt_ref), stack(kt_ref)
    Rbg, Rkg, Vs = stack(bg_ref), stack(kg_ref), stack(v_ref)
    gm_row = [gm_ref[c][:, cols(p)] for p, c in items]

    AA = each(lambda la, lr, rb, rk: _mm_nt(jnp.concatenate([la, lr], axis=0),
                                            jnp.concatenate([rb, rk], axis=0)), La, Lr, Rb, Rk)
    N = [jnp.where(strict, aa[:2 * T, :2 * T], 0.0) for aa in AA]
    Aak = [jnp.where(strict, aa[:2 * T, 2 * T:], 0.0) for aa in AA]
    Arb = [jnp.where(incl, aa[2 * T:, :2 * T], 0.0) for aa in AA]
    Ark = [jnp.where(incl, aa[2 * T:, 2 * T:], 0.0) for aa in AA]
    W = each(lambda n: eye + n, N)
    P = N
    span = 2
    while span < T:
        P = each(lambda p_: _mm(p_, p_), P)
        W = each(lambda w, p_: w + _mm(w, p_), W, P)
        span *= 2
    AkV = each(_mm, Aak, Vs)
    X = each(lambda w, la, akv: _mm(w, jnp.concatenate([la, akv.astype(BF16)], axis=1)).astype(BF16),
             W, La, AkV)
    Z = each(lambda arb, ark, x, vs:
             _mm(jnp.concatenate([arb.astype(BF16), ark.astype(BF16)], axis=1),
                 jnp.concatenate([x, jnp.concatenate([jnp.zeros_like(vs), vs], axis=1)], axis=0)),
             Arb, Ark, X, Vs)
    Q = each(lambda z, lr, gm: (z[:, :PAIR] + lr.astype(F32)) * gm, Z, Lr, gm_row)
    Y0 = [z[:, PAIR:] for z in Z]
    Mbd = each(lambda rbg, x, gm: _mm_tn(rbg, x[:, :PAIR]) * gm, Rbg, X, gm_row)
    NcT = each(lambda x, vs, rbg, rkg: _mm_tn(jnp.concatenate([x[:, PAIR:], vs], axis=0),
                                              jnp.concatenate([rbg, rkg], axis=0)), X, Vs, Rbg, Rkg)

    S = [s_ref[p] for p in range(pairs)]
    for c in range(chunks):
        for p in range(pairs):
            i = c * pairs + p
            Ys = _mm_nt(Q[i], S[p]) + Y0[i]
            S[p] = S[p] * gT_ref[c][:, cols(p)] + _mm_nt(S[p], Mbd[i]) + NcT[i]
            mu = jnp.sum(Ys, axis=-1, keepdims=True) * (1.0 / RW_HEAD)
            d = jnp.where(own, Ys - mu, 0.0)
            var = jnp.sum(d * d, axis=-1, keepdims=True) * (1.0 / RW_HEAD)
            dn = d * lax.rsqrt(var + GN_EPS)
            yn = (dn[:T, :] + dn[T:, :]) * gnw_ref[:, cols(p)] + gnb_ref[:, cols(p)]
            out = (yn + bonus_ref[rows(c), cols(p)].astype(F32)) * g_ref[rows(c), cols(p)].astype(F32)
            o_ref[rows(c), cols(p)] = out.astype(BF16)
    for p in range(pairs):
        s_ref[p] = S[p]


def _rwkv_chunk(ops, gT, gm, gn_w, gn_b, batch, seq, tb=128, pairs=8):
    m = ops[0].shape[0]
    nb = seq // tb
    chunks = tb // CHUNK
    width = pairs * PAIR
    big_spec = pl.BlockSpec((tb, width), lambda b, p, j: (b * nb + j, p))
    pc_spec = pl.BlockSpec((chunks, 1, width), lambda b, p, j: (b * nb + j, 0, p))
    vec = pl.BlockSpec((1, width), lambda b, p, j: (0, p))
    return pl.pallas_call(
        functools.partial(_rwkv_chunk_kernel, chunks=chunks, pairs=pairs),
        grid=(batch, N_PAIRS // pairs, nb),
        in_specs=[big_spec] * 9 + [pc_spec] * 2 + [vec, vec],
        out_specs=big_spec,
        out_shape=jax.ShapeDtypeStruct((m, RW_DIM), BF16),
        scratch_shapes=[pltpu.VMEM((pairs, PAIR, PAIR), F32)],
        compiler_params=_cparams(("parallel", "parallel", "arbitrary")),
        name="rwkv_chunk",
    )(*ops, gT, gm, gn_w, gn_b)


def _swa_kernel(sink_ref, q_ref, kc_ref, kp_ref, vc_ref, vp_ref, qg_ref, kg_ref, o_ref, *,
                blocks_per_seq, slopes):
    first = (pl.program_id(0) % blocks_per_seq) == 0
    mean_h = _head_ones(LANES, ATT_HEAD, 1.0 / ATT_HEAD)

    def head_rms(x, gain):
        parts = []
        for b in range(x.shape[1] // LANES):
            xb = x[:, b * LANES:(b + 1) * LANES]
            ms = _mm_split_r(xb * xb, mean_h)
            parts.append(xb * lax.rsqrt(ms + RMS_EPS))
        return jnp.concatenate(parts, axis=1) * gain

    q = head_rms(q_ref[...].astype(F32), qg_ref[...]).astype(BF16)
    kcat = jnp.concatenate([kp_ref[...], kc_ref[...]], axis=0).astype(F32)
    kcat = head_rms(kcat, kg_ref[...]).astype(BF16)
    vcat = jnp.concatenate([vp_ref[...], vc_ref[...]], axis=0)

    qi = lax.broadcasted_iota(jnp.int32, (BLOCK, 2 * BLOCK), 0)
    kj = lax.broadcasted_iota(jnp.int32, (BLOCK, 2 * BLOCK), 1)
    dist_i = BLOCK + qi - kj
    first_key = jnp.where(first, BLOCK, 0)
    valid = (dist_i >= 0) & (dist_i < WINDOW) & (kj >= first_key)
    dist = dist_i.astype(F32)

    outs = []
    for h in range(ATT_HEADS):
        j = h // ATT_GROUP
        qh = q[:, h * ATT_HEAD:(h + 1) * ATT_HEAD]
        kh = kcat[:, j * ATT_HEAD:(j + 1) * ATT_HEAD]
        vh = vcat[:, j * ATT_HEAD:(j + 1) * ATT_HEAD]
        s = _mm_nt(qh, kh) * (ATT_HEAD ** -0.5)
        s = jnp.where(valid, s + (-slopes[h]) * dist, NEG_BIG)
        sink = sink_ref[h]
        mx = jnp.maximum(jnp.max(s, axis=-1, keepdims=True), sink)
        p = jnp.exp(s - mx)
        denom = jnp.sum(p, axis=-1, keepdims=True) + jnp.exp(sink - mx)
        p = p / denom
        outs.append(_mm(p, vh))
    o_ref[...] = jnp.concatenate(outs, axis=1).astype(BF16)


def _swa(z, sinks, q_gain_t, k_gain_t, seq):
    m = z.shape[0]
    nblk = m // BLOCK
    bps = seq // BLOCK
    slopes = tuple(float(s) for s in
                   np.exp2(-8.0 * np.arange(1, ATT_HEADS + 1, dtype=np.float32) / ATT_HEADS).astype(np.float32))
    kv_cur = lambda c0: pl.BlockSpec((BLOCK, ATT_KV_DIM), lambda n, c=c0 // ATT_KV_DIM: (n, c))
    kv_prev = lambda c0: pl.BlockSpec((BLOCK, ATT_KV_DIM),
                                      lambda n, c=c0 // ATT_KV_DIM: (jnp.maximum(n - 1, 0), c))
    return pl.pallas_call(
        functools.partial(_swa_kernel, blocks_per_seq=bps, slopes=slopes),
        grid=(nblk,),
        in_specs=[pl.BlockSpec(memory_space=pltpu.SMEM),
                  pl.BlockSpec((BLOCK, ATT_Q_DIM), lambda n: (n, C_Q // ATT_Q_DIM)),
                  kv_cur(C_KA), kv_prev(C_KA), kv_cur(C_VA), kv_prev(C_VA),
                  pl.BlockSpec((1, ATT_Q_DIM), lambda n: (0, 0)),
                  pl.BlockSpec((1, ATT_KV_DIM), lambda n: (0, 0))],
        out_specs=pl.BlockSpec((BLOCK, ATT_Q_DIM), lambda n: (n, 0)),
        out_shape=jax.ShapeDtypeStruct((m, ATT_Q_DIM), BF16),
        compiler_params=_cparams(("parallel",)),
        name="swa",
    )(sinks, z, z, z, z, z, q_gain_t, k_gain_t)


def _branch_mix_kernel(orw_ref, oatt_ref, wb1_ref, wb2_ref, zg1_ref, zg2_ref, m_ref):
    p_rw = jnp.dot(orw_ref[...], wb1_ref[...], preferred_element_type=F32)
    p_att = jnp.dot(oatt_ref[...], wb2_ref[...], preferred_element_type=F32)
    g_rw = jax.nn.sigmoid(zg1_ref[...].astype(F32))
    g_att = jax.nn.sigmoid(zg2_ref[...].astype(F32))
    m_ref[...] = (g_rw * p_rw + g_att * p_att).astype(BF16)


def _branch_mix(o_rw, o_att, wb1, wb2, z, tm=1024, tn=1024):
    m = o_rw.shape[0]
    return pl.pallas_call(
        _branch_mix_kernel,
        grid=(m // tm, D_MODEL // tn),
        in_specs=[pl.BlockSpec((tm, RW_DIM), lambda i, j: (i, 0)),
                  pl.BlockSpec((tm, ATT_Q_DIM), lambda i, j: (i, 0)),
                  pl.BlockSpec((RW_DIM, tn), lambda i, j: (0, j)),
                  pl.BlockSpec((ATT_Q_DIM, tn), lambda i, j: (0, j)),
                  pl.BlockSpec((tm, tn), lambda i, j: (i, C_GRW // tn + j)),
                  pl.BlockSpec((tm, tn), lambda i, j: (i, C_GATT // tn + j))],
        out_specs=pl.BlockSpec((tm, tn), lambda i, j: (i, j)),
        out_shape=jax.ShapeDtypeStruct((m, D_MODEL), BF16),
        compiler_params=_cparams(("parallel", "arbitrary")),
        name="branch_mix",
    )(o_rw, o_att, wb1, wb2, z, z)


def _out_proj_kernel(m_ref, w_ref, x_ref, g_ref, x1_ref, h2_ref):
    x1 = x_ref[...] + jnp.dot(m_ref[...], w_ref[...], preferred_element_type=F32)
    x1_ref[...] = x1
    y = x1 * lax.rsqrt(jnp.mean(x1 * x1, axis=-1, keepdims=True) + RMS_EPS)
    h2_ref[...] = (y * g_ref[...]).astype(BF16)


def _out_proj(mix, w_out, x2, g2, tm=256):
    m = x2.shape[0]
    return pl.pallas_call(
        _out_proj_kernel,
        grid=(m // tm,),
        in_specs=[pl.BlockSpec((tm, D_MODEL), lambda i: (i, 0)),
                  pl.BlockSpec((D_MODEL, D_MODEL), lambda i: (0, 0)),
                  pl.BlockSpec((tm, D_MODEL), lambda i: (i, 0)),
                  pl.BlockSpec((1, D_MODEL), lambda i: (0, 0))],
        out_specs=[pl.BlockSpec((tm, D_MODEL), lambda i: (i, 0)),
                   pl.BlockSpec((tm, D_MODEL), lambda i: (i, 0))],
        out_shape=[jax.ShapeDtypeStruct((m, D_MODEL), F32),
                   jax.ShapeDtypeStruct((m, D_MODEL), BF16)],
        compiler_params=_cparams(("parallel",)),
        name="out_proj",
    )(mix, w_out, x2, g2)


def _ffn_up_kernel(h_ref, wv_ref, wg_ref, cwv_ref, cwg_ref, cbv_ref, cbg_ref, a_ref,
                   wvb_ref, wgb_ref, uv_ref, ug_ref, *, tm, sub, tiles_per_seq):
    halo = 8

    @pl.when(pl.program_id(1) == 0)
    def _():
        wvb_ref[...] = wv_ref[...].astype(BF16)
        wgb_ref[...] = wg_ref[...].astype(BF16)

    @pl.when((pl.program_id(1) % tiles_per_seq) == 0)
    def _():
        uv_ref[0:halo, :] = jnp.zeros((halo, uv_ref.shape[1]), F32)
        ug_ref[0:halo, :] = jnp.zeros((halo, ug_ref.shape[1]), F32)

    def conv(u_ref, cw_ref, cb_ref, r0):
        return (u_ref[halo - 2 + r0:halo - 2 + r0 + sub, :] * cw_ref[0:1, :]
                + u_ref[halo - 1 + r0:halo - 1 + r0 + sub, :] * cw_ref[1:2, :]
                + u_ref[halo + r0:halo + r0 + sub, :] * cw_ref[2:3, :] + cb_ref[...])

    for s in range(tm // sub):
        r0 = s * sub
        h = h_ref[r0:r0 + sub, :]
        uv_ref[halo + r0:halo + r0 + sub, :] = jnp.dot(h, wvb_ref[...], preferred_element_type=F32)
        ug_ref[halo + r0:halo + r0 + sub, :] = jnp.dot(h, wgb_ref[...], preferred_element_type=F32)
        val = conv(uv_ref, cwv_ref, cbv_ref, r0)
        gate = conv(ug_ref, cwg_ref, cbg_ref, r0)
        a_ref[r0:r0 + sub, :] = (gate * jax.nn.sigmoid(gate) * val).astype(BF16)
    uv_ref[0:halo, :] = uv_ref[tm:tm + halo, :]
    ug_ref[0:halo, :] = ug_ref[tm:tm + halo, :]


def _ffn_up(h2, w_up, conv_w, conv_b, seq, tm=1024, tn=512, sub=128):
    m = h2.shape[0]
    nj = D_FF // tn
    return pl.pallas_call(
        functools.partial(_ffn_up_kernel, tm=tm, sub=sub, tiles_per_seq=seq // tm),
        grid=(nj, m // tm),
        in_specs=[pl.BlockSpec((tm, D_MODEL), lambda j, i: (i, 0)),
                  pl.BlockSpec((D_MODEL, tn), lambda j, i: (0, j)),
                  pl.BlockSpec((D_MODEL, tn), lambda j, i: (0, nj + j)),
                  pl.BlockSpec((3, tn), lambda j, i: (0, j)),
                  pl.BlockSpec((3, tn), lambda j, i: (0, nj + j)),
                  pl.BlockSpec((1, tn), lambda j, i: (0, j)),
                  pl.BlockSpec((1, tn), lambda j, i: (0, nj + j))],
        out_specs=pl.BlockSpec((tm, tn), lambda j, i: (i, j)),
        out_shape=jax.ShapeDtypeStruct((m, D_FF), BF16),
        scratch_shapes=[pltpu.VMEM((D_MODEL, tn), BF16), pltpu.VMEM((D_MODEL, tn), BF16),
                        pltpu.VMEM((tm + 8, tn), F32), pltpu.VMEM((tm + 8, tn), F32)],
        compiler_params=_cparams(("parallel", "arbitrary")),
        name="ffn_up",
    )(h2, w_up, w_up, conv_w, conv_w, conv_b, conv_b)


def _ffn_down_kernel(a_ref, w_ref, x1_ref, o_ref, wb_ref):
    @pl.when(pl.program_id(1) == 0)
    def _():
        wb_ref[...] = w_ref[...].astype(BF16)

    o_ref[...] = x1_ref[...] + jnp.dot(a_ref[...], wb_ref[...], preferred_element_type=F32)


def _ffn_down(act, w_down, x1, tm=512, tn=512):
    m = act.shape[0]
    return pl.pallas_call(
        _ffn_down_kernel,
        grid=(D_MODEL // tn, m // tm),
        in_specs=[pl.BlockSpec((tm, D_FF), lambda j, i: (i, 0)),
                  pl.BlockSpec((D_FF, tn), lambda j, i: (0, j)),
                  pl.BlockSpec((tm, tn), lambda j, i: (i, j))],
        out_specs=pl.BlockSpec((tm, tn), lambda j, i: (i, j)),
        out_shape=jax.ShapeDtypeStruct((m, D_MODEL), F32),
        scratch_shapes=[pltpu.VMEM((D_FF, tn), BF16)],
        compiler_params=_cparams(("parallel", "arbitrary")),
        name="ffn_down",
    )(act, w_down, x1)


def _pad_cols(w, n):
    return jnp.pad(w, ((0, 0), (0, n - w.shape[1])))


def _pad_rows(w, n):
    return jnp.pad(w, ((0, n - w.shape[0]), (0, 0)))


def _layer(x2, batch, seq, norm1_g, w_in, rw_mu, rw_w0, rw_w2, rw_a0, rw_a2, rw_g2, rw_k_k, rw_k_a,
           rw_r_k, rw_gn_w, rw_gn_b, q_norm_g, k_norm_g, attn_sinks, w_branch, w_out,
           norm2_g, w_up, conv_w, conv_b, w_down):
    row = lambda v: v.reshape(1, -1).astype(F32)
    o3 = 3 * RW_DIM
    o4 = o3 + DECAY_LORA
    o5 = o4 + ICLR_LORA
    o6 = o5 + GATE_LORA
    oq = o6 + ATT_Q_DIM
    ok = oq + ATT_KV_DIM
    ov = ok + ATT_KV_DIM
    w_in_b = w_in.astype(BF16)
    w_in_p = jnp.concatenate([
        w_in_b[:, :o3], w_in_b[:, o6:oq], w_in_b[:, ov:],
        _pad_cols(w_in_b[:, o3:o4], LORA_W), _pad_cols(w_in_b[:, o4:o5], LORA_A), w_in_b[:, o5:o6],
        w_in_b[:, oq:ok], w_in_b[:, ok:ov]], axis=1)
    mu = rw_mu.reshape(1, -1)
    mu_l = jnp.concatenate([_pad_cols(mu[:, o3:o4], LORA_W), _pad_cols(mu[:, o4:o5], LORA_A),
                            mu[:, o5:o6]], axis=1)
    w2p = _pad_rows(rw_w2, LORA_W).astype(BF16)
    a2p = _pad_rows(rw_a2, LORA_A).astype(BF16)

    z = _in_proj(x2, row(norm1_g), w_in_p)
    prep = _rwkv_prep(z, seq, mu[:, :RW_DIM], mu[:, RW_DIM:2 * RW_DIM], mu[:, 2 * RW_DIM:o3], mu_l,
                      row(rw_w0), row(rw_a0), row(rw_k_k), row(rw_k_a), row(rw_r_k),
                      w2p, a2p, rw_g2.astype(BF16))
    o_rw = _rwkv_chunk(prep[:9], prep[9], prep[10], row(rw_gn_w), row(rw_gn_b), batch, seq)
    o_att = _swa(z, attn_sinks.astype(F32), jnp.tile(row(q_norm_g), (1, ATT_HEADS)),
                 jnp.tile(row(k_norm_g), (1, ATT_KV_HEADS)), seq)
    mix = _branch_mix(o_rw, o_att, w_branch[:RW_DIM].astype(BF16), w_branch[RW_DIM:].astype(BF16), z)
    x1, h2 = _out_proj(mix, w_out.astype(BF16), x2, row(norm2_g))
    act = _ffn_up(h2, w_up, conv_w.astype(F32), row(conv_b), seq)
    return _ffn_down(act, w_down, x1)


def kernel(x, norm1_g, w_in, rw_mu, rw_w0, rw_w2, rw_a0, rw_a2, rw_g2, rw_k_k, rw_k_a, rw_r_k,
           rw_gn_w, rw_gn_b, q_norm_g, k_norm_g, attn_sinks, w_branch, w_out, norm2_g, w_up,
           conv_w, conv_b, w_down):
    batch, seq, d = x.shape
    x2 = x.reshape(batch * seq, d)
    params = (norm1_g, w_in, rw_mu, rw_w0, rw_w2, rw_a0, rw_a2, rw_g2, rw_k_k, rw_k_a, rw_r_k,
              rw_gn_w, rw_gn_b, q_norm_g, k_norm_g, attn_sinks, w_branch, w_out, norm2_g, w_up,
              conv_w, conv_b, w_down)
    for layer in range(norm1_g.shape[0]):
        x2 = _layer(x2, batch, seq, *(p[layer] for p in params))
    return x2.reshape(batch, seq, d)
```

```python
import functools

import jax
import jax.numpy as jnp
import numpy as np
from jax import lax
from jax.experimental import pallas as pl
from jax.experimental.pallas import tpu as pltpu

F32 = jnp.float32
BF16 = jnp.bfloat16

LANES = 128
BF16_SUBLANES = 16
VMEM_LIMIT = 56 * 1024 * 1024

D_MODEL = 2048
RW_HEADS = 16
RW_HEAD = 64
RW_DIM = RW_HEADS * RW_HEAD
DECAY_LORA = 96
ICLR_LORA = 96
GATE_LORA = 256
GN_EPS = 64e-5
ATT_HEADS = 16
ATT_KV_HEADS = 4
ATT_GROUP = ATT_HEADS // ATT_KV_HEADS
ATT_HEAD = 64
ATT_Q_DIM = ATT_HEADS * ATT_HEAD
ATT_KV_DIM = ATT_KV_HEADS * ATT_HEAD
WINDOW = 128
BLOCK = 128
D_FF = 5632
RMS_EPS = 1e-6
NEG_BIG = -1e30
LOG2E = 1.4426950408889634

C_R = 0
C_K = RW_DIM
C_V = 2 * RW_DIM
C_Q = 3 * RW_DIM
C_GRW = C_Q + ATT_Q_DIM
C_GATT = C_GRW + D_MODEL
C_LORA = C_GATT + D_MODEL
LORA_W = 128
LORA_A = 128
LORA_COLS = LORA_W + LORA_A + GATE_LORA
C_KA = C_LORA + LORA_COLS
C_VA = C_KA + ATT_KV_DIM
Z_COLS = C_VA + ATT_KV_DIM

CHUNK = 64
PAIR = 2 * RW_HEAD
N_PAIRS = RW_DIM // PAIR


def _cparams(sem):
    return pltpu.CompilerParams(dimension_semantics=sem, vmem_limit_bytes=VMEM_LIMIT)


def _mm(a, b):
    return jnp.dot(a.astype(BF16), b.astype(BF16), preferred_element_type=F32)


def _mm_nt(a, b):
    return lax.dot_general(a.astype(BF16), b.astype(BF16), (((1,), (1,)), ((), ())),
                           preferred_element_type=F32)


def _mm_tn(a, b):
    return lax.dot_general(a.astype(BF16), b.astype(BF16), (((0,), (0,)), ((), ())),
                           preferred_element_type=F32)


def _mm_split(m01, x):
    hi = x.astype(BF16)
    lo = (x - hi.astype(F32)).astype(BF16)
    return (jnp.dot(m01, hi, preferred_element_type=F32)
            + jnp.dot(m01, lo, preferred_element_type=F32))


def _mm_split_r(x, m01):
    hi = x.astype(BF16)
    lo = (x - hi.astype(F32)).astype(BF16)
    return (jnp.dot(hi, m01, preferred_element_type=F32)
            + jnp.dot(lo, m01, preferred_element_type=F32))


def _head_ones(n, head, scale):
    r = lax.broadcasted_iota(jnp.int32, (n, n), 0) // head
    c = lax.broadcasted_iota(jnp.int32, (n, n), 1) // head
    return jnp.where(r == c, scale, 0.0).astype(BF16)


def _in_proj_kernel(x_ref, g_ref, w_ref, z_ref, h_ref):
    @pl.when(pl.program_id(1) == 0)
    def _():
        xf = x_ref[...]
        y = xf * lax.rsqrt(jnp.mean(xf * xf, axis=-1, keepdims=True) + RMS_EPS)
        h_ref[...] = (y * g_ref[...]).astype(BF16)

    z_ref[...] = jnp.dot(h_ref[...], w_ref[...], preferred_element_type=F32).astype(BF16)


def _in_proj(x2, g1, w_in_p, tm=1024, tn=1024):
    m, d = x2.shape
    n = w_in_p.shape[1]
    return pl.pallas_call(
        _in_proj_kernel,
        grid=(m // tm, n // tn),
        in_specs=[pl.BlockSpec((tm, d), lambda i, j: (i, 0)),
                  pl.BlockSpec((1, d), lambda i, j: (0, 0)),
                  pl.BlockSpec((d, tn), lambda i, j: (0, j))],
        out_specs=pl.BlockSpec((tm, tn), lambda i, j: (i, j)),
        out_shape=jax.ShapeDtypeStruct((m, n), BF16),
        scratch_shapes=[pltpu.VMEM((tm, d), BF16)],
        compiler_params=_cparams(("parallel", "arbitrary")),
        name="in_proj",
    )(x2, g1, w_in_p)


def _rwkv_prep_kernel(zr_ref, zk_ref, zv_ref, zl_ref, pr_ref, pk_ref, pv_ref, pli_ref,
                      mur_ref, muk_ref, muv_ref, mul_ref,
                      w0_ref, a0_ref, kk_ref, ka_ref, rk_ref, w2_ref, a2_ref, g2_ref,
                      at_ref, rt_ref, bt_ref, kt_ref, bg_ref, kg_ref, v_ref, g_ref, bonus_ref,
                      gT_ref, gm_ref, *, tm, tiles_per_seq):
    first = (pl.program_id(0) % tiles_per_seq) == 0
    keep = jnp.where(first, 0.0, 1.0)
    nc = tm // CHUNK
    last = slice(BF16_SUBLANES - 1, BF16_SUBLANES)

    def shifted(z, prev_last, mu):
        z = z.astype(F32)
        row = lax.broadcasted_iota(jnp.int32, z.shape, 0)
        zprev = jnp.where(row == 0, prev_last.astype(F32) * keep, pltpu.roll(z, 1, 0))
        return z + (zprev - z) * mu

    lo = shifted(zl_ref[...], pli_ref[last, :], mul_ref[...])
    tanh_wd = jnp.tanh(lo[:, :LORA_W]).astype(BF16)
    ad = lo[:, LORA_W:LORA_W + LORA_A].astype(BF16)
    sig_gd = jax.nn.sigmoid(lo[:, LORA_W + LORA_A:]).astype(BF16)

    ti = lax.broadcasted_iota(jnp.int32, (tm, tm), 0)
    si = lax.broadcasted_iota(jnp.int32, (tm, tm), 1)
    tri = jnp.where(((ti // CHUNK) == (si // CHUNK)) & (si <= ti), 1.0, 0.0).astype(BF16)
    ones_h = _head_ones(PAIR, RW_HEAD, 1.0)

    for p in range(N_PAIRS):
        cs = slice(p * PAIR, (p + 1) * PAIR)
        r = shifted(zr_ref[:, cs], pr_ref[last, cs], mur_ref[:, cs])
        k = shifted(zk_ref[:, cs], pk_ref[last, cs], muk_ref[:, cs])
        v = shifted(zv_ref[:, cs], pv_ref[last, cs], muv_ref[:, cs])

        wpre = w0_ref[:, cs] + jnp.dot(tanh_wd, w2_ref[:, cs], preferred_element_type=F32)
        xneg = -wpre
        softplus = jnp.maximum(xneg, 0.0) + jnp.log1p(jnp.exp(-jnp.abs(xneg)))
        w = -softplus - 0.5
        lw = -jnp.exp(w)
        a = jax.nn.sigmoid(a0_ref[:, cs] + jnp.dot(ad, a2_ref[:, cs], preferred_element_type=F32))
        g = jnp.dot(sig_gd, g2_ref[:, cs], preferred_element_type=F32)

        kk = k * kk_ref[:, cs]
        kk = kk / jnp.maximum(jnp.sqrt(_mm(kk * kk, ones_h)), 1e-12)
        k2 = k * (1.0 + (a - 1.0) * ka_ref[:, cs])
        bonus = _mm(r * k2 * rk_ref[:, cs], ones_h) * v
        ka_vec = -kk
        kb_vec = kk * a

        c = _mm_split(tri, lw)
        c3 = c.reshape(nc, CHUNK, PAIR)
        c_mid = c3[:, CHUNK // 2 - 1:CHUNK // 2, :]
        c_end = c3[:, CHUNK - 1:CHUNK, :]
        cm = jnp.broadcast_to(c_mid, c3.shape).reshape(tm, PAIR)
        cT = jnp.broadcast_to(c_end, c3.shape).reshape(tm, PAIR)

        e_in = jnp.exp(c - cm)
        e_out = jnp.exp(cm - c)
        e_end = jnp.exp(cT - c)
        at_ref[:, cs] = (ka_vec * jnp.exp(c - lw - cm)).astype(BF16)
        rt_ref[:, cs] = (r * e_in).astype(BF16)
        bt_ref[:, cs] = (kb_vec * e_out).astype(BF16)
        kt_ref[:, cs] = (k2 * e_out).astype(BF16)
        bg_ref[:, cs] = (kb_vec * e_end).astype(BF16)
        kg_ref[:, cs] = (k2 * e_end).astype(BF16)
        v_ref[:, cs] = v.astype(BF16)
        g_ref[:, cs] = g.astype(BF16)
        bonus_ref[:, cs] = bonus.astype(BF16)
        gT_ref[:, :, cs] = jnp.exp(c_end)
        gm_ref[:, :, cs] = jnp.exp(c_mid)


def _rwkv_prep(z, seq, mu_r, mu_k, mu_v, mu_l, w0, a0, k_k, k_a, r_k, w2p, a2p, g2, tm=256):
    m = z.shape[0]
    pb = tm // BF16_SUBLANES

    def cur(col0, width):
        return pl.BlockSpec((tm, width), lambda i, c0=col0 // width: (i, c0))

    def prev(col0, width):
        return pl.BlockSpec((BF16_SUBLANES, width),
                            lambda i, c0=col0 // width: (jnp.maximum(i * pb - 1, 0), c0))

    def whole(rows, width):
        return pl.BlockSpec((rows, width), lambda i: (0, 0))

    in_specs = [cur(C_R, RW_DIM), cur(C_K, RW_DIM), cur(C_V, RW_DIM), cur(C_LORA, LORA_COLS),
                prev(C_R, RW_DIM), prev(C_K, RW_DIM), prev(C_V, RW_DIM), prev(C_LORA, LORA_COLS),
                whole(1, RW_DIM), whole(1, RW_DIM), whole(1, RW_DIM), whole(1, LORA_COLS),
                whole(1, RW_DIM), whole(1, RW_DIM), whole(1, RW_DIM), whole(1, RW_DIM), whole(1, RW_DIM),
                whole(LORA_W, RW_DIM), whole(LORA_A, RW_DIM), whole(GATE_LORA, RW_DIM)]
    big = jax.ShapeDtypeStruct((m, RW_DIM), BF16)
    per_chunk = jax.ShapeDtypeStruct((m // CHUNK, 1, RW_DIM), F32)
    big_spec = pl.BlockSpec((tm, RW_DIM), lambda i: (i, 0))
    pc_spec = pl.BlockSpec((tm // CHUNK, 1, RW_DIM), lambda i: (i, 0, 0))
    return pl.pallas_call(
        functools.partial(_rwkv_prep_kernel, tm=tm, tiles_per_seq=seq // tm),
        grid=(m // tm,),
        in_specs=in_specs,
        out_specs=[big_spec] * 9 + [pc_spec] * 2,
        out_shape=[big] * 9 + [per_chunk] * 2,
        compiler_params=_cparams(("parallel",)),
        name="rwkv_prep",
    )(z, z, z, z, z, z, z, z, mu_r, mu_k, mu_v, mu_l, w0, a0, k_k, k_a, r_k, w2p, a2p, g2)


def _rwkv_chunk_kernel(at_ref, rt_ref, bt_ref, kt_ref, bg_ref, kg_ref, v_ref, g_ref, bonus_ref,
                       gT_ref, gm_ref, gnw_ref, gnb_ref, o_ref, s_ref, *, chunks, pairs):
    @pl.when(pl.program_id(2) == 0)
    def _():
        s_ref[...] = jnp.zeros_like(s_ref)

    T = CHUNK
    lane = lax.broadcasted_iota(jnp.int32, (T, PAIR), 1)
    head0 = lane < RW_HEAD
    ri = lax.broadcasted_iota(jnp.int32, (2 * T, 2 * T), 0)
    ci = lax.broadcasted_iota(jnp.int32, (2 * T, 2 * T), 1)
    same = (ri // T) == (ci // T)
    strict = same & ((ri % T) > (ci % T))
    incl = same & ((ri % T) >= (ci % T))
    eye = jnp.where(ri == ci, 1.0, 0.0)
    own = (ri // T) == (ci // RW_HEAD)

    items = [(p, c) for c in range(chunks) for p in range(pairs)]
    rows = lambda c: slice(c * T, (c + 1) * T)
    cols = lambda p: slice(p * PAIR, (p + 1) * PAIR)

    def stack(ref):
        out = []
        for p, c in items:
            x = ref[rows(c), cols(p)]
            zero = jnp.zeros_like(x)
            out.append(jnp.concatenate([jnp.where(head0, x, zero), jnp.where(head0, zero, x)], axis=0))
        return out

    each = lambda f, *ls: [f(*xs) for xs in zip(*ls)]
    La, Lr, Rb, Rk = stack(at_ref), stack(rt_ref), stack(bt_ref), stack(kt_ref)
    Rbg, Rkg, Vs = stack(bg_ref), stack(kg_ref), stack(v_ref)
    gm_row = [gm_ref[c][:, cols(p)] for p, c in items]

    AA = each(lambda la, lr, rb, rk: _mm_nt(jnp.concatenate([la, lr], axis=0),
                                            jnp.concatenate([rb, rk], axis=0)), La, Lr, Rb, Rk)
    N = [jnp.where(strict, aa[:2 * T, :2 * T], 0.0) for aa in AA]
    Aak = [jnp.where(strict, aa[:2 * T, 2 * T:], 0.0) for aa in AA]
    Arb = [jnp.where(incl, aa[2 * T:, :2 * T], 0.0) for aa in AA]
    Ark = [jnp.where(incl, aa[2 * T:, 2 * T:], 0.0) for aa in AA]
    W = each(lambda n: eye + n, N)
    P = N
    span = 2
    while span < T:
        P = each(lambda p_: _mm(p_, p_), P)
        W = each(lambda w, p_: w + _mm(w, p_), W, P)
        span *= 2
    AkV = each(_mm, Aak, Vs)
    X = each(lambda w, la, akv: _mm(w, jnp.concatenate([la, akv.astype(BF16)], axis=1)).astype(BF16),
             W, La, AkV)
    Z = each(lambda arb, ark, x, vs:
             _mm(jnp.concatenate([arb.astype(BF16), ark.astype(BF16)], axis=1),
                 jnp.concatenate([x, jnp.concatenate([jnp.zeros_like(vs), vs], axis=1)], axis=0)),
             Arb, Ark, X, Vs)
    Q = each(lambda z, lr, gm: (z[:, :PAIR] + lr.astype(F32)) * gm, Z, Lr, gm_row)
    Y0 = [z[:, PAIR:] for z in Z]
    Mbd = each(lambda rbg, x, gm: _mm_tn(rbg, x[:, :PAIR]) * gm, Rbg, X, gm_row)
    NcT = each(lambda x, vs, rbg, rkg: _mm_tn(jnp.concatenate([x[:, PAIR:], vs], axis=0),
                                              jnp.concatenate([rbg, rkg], axis=0)), X, Vs, Rbg, Rkg)

    S = [s_ref[p] for p in range(pairs)]
    for c in range(chunks):
        for p in range(pairs):
            i = c * pairs + p
            Ys = _mm_nt(Q[i], S[p]) + Y0[i]
            S[p] = S[p] * gT_ref[c][:, cols(p)] + _mm_nt(S[p], Mbd[i]) + NcT[i]
            mu = jnp.sum(Ys, axis=-1, keepdims=True) * (1.0 / RW_HEAD)
            d = jnp.where(own, Ys - mu, 0.0)
            var = jnp.sum(d * d, axis=-1, keepdims=True) * (1.0 / RW_HEAD)
            dn = d * lax.rsqrt(var + GN_EPS)
            yn = (dn[:T, :] + dn[T:, :]) * gnw_ref[:, cols(p)] + gnb_ref[:, cols(p)]
            out = (yn + bonus_ref[rows(c), cols(p)].astype(F32)) * g_ref[rows(c), cols(p)].astype(F32)
            o_ref[rows(c), cols(p)] = out.astype(BF16)
    for p in range(pairs):
        s_ref[p] = S[p]


def _rwkv_chunk(ops, gT, gm, gn_w, gn_b, batch, seq, tb=128, pairs=8):
    m = ops[0].shape[0]
    nb = seq // tb
    chunks = tb // CHUNK
    width = pairs * PAIR
    big_spec = pl.BlockSpec((tb, width), lambda b, p, j: (b * nb + j, p))
    pc_spec = pl.BlockSpec((chunks, 1, width), lambda b, p, j: (b * nb + j, 0, p))
    vec = pl.BlockSpec((1, width), lambda b, p, j: (0, p))
    return pl.pallas_call(
        functools.partial(_rwkv_chunk_kernel, chunks=chunks, pairs=pairs),
        grid=(batch, N_PAIRS // pairs, nb),
        in_specs=[big_spec] * 9 + [pc_spec] * 2 + [vec, vec],
        out_specs=big_spec,
        out_shape=jax.ShapeDtypeStruct((m, RW_DIM), BF16),
        scratch_shapes=[pltpu.VMEM((pairs, PAIR, PAIR), F32)],
        compiler_params=_cparams(("parallel", "parallel", "arbitrary")),
        name="rwkv_chunk",
    )(*ops, gT, gm, gn_w, gn_b)


def _swa_kernel(sink_ref, q_ref, kc_ref, kp_ref, vc_ref, vp_ref, qg_ref, kg_ref, o_ref, *,
                blocks_per_seq, slopes):
    first = (pl.program_id(0) % blocks_per_seq) == 0
    mean_h = _head_ones(LANES, ATT_HEAD, 1.0 / ATT_HEAD)

    def head_rms(x, gain):
        parts = []
        for b in range(x.shape[1] // LANES):
            xb = x[:, b * LANES:(b + 1) * LANES]
            ms = _mm_split_r(xb * xb, mean_h)
            parts.append(xb * lax.rsqrt(ms + RMS_EPS))
        return jnp.concatenate(parts, axis=1) * gain

    q = head_rms(q_ref[...].astype(F32), qg_ref[...]) * (ATT_HEAD ** -0.5 * LOG2E)
    q = q.astype(BF16)
    kcat = jnp.concatenate([kp_ref[...], kc_ref[...]], axis=0).astype(F32)
    kcat = head_rms(kcat, kg_ref[...]).astype(BF16)
    vcat = jnp.concatenate([vp_ref[...], vc_ref[...]], axis=0)

    qi = lax.broadcasted_iota(jnp.int32, (BLOCK, 2 * BLOCK), 0)
    kj = lax.broadcasted_iota(jnp.int32, (BLOCK, 2 * BLOCK), 1)
    dist_i = BLOCK + qi - kj
    first_key = jnp.where(first, BLOCK, 0)
    valid = (dist_i >= 0) & (dist_i < WINDOW) & (kj >= first_key)
    neg_dist = jnp.where(valid, -dist_i.astype(F32), NEG_BIG)

    outs = []
    for h in range(ATT_HEADS):
        j = h // ATT_GROUP
        qh = q[:, h * ATT_HEAD:(h + 1) * ATT_HEAD]
        kh = kcat[:, j * ATT_HEAD:(j + 1) * ATT_HEAD]
        vh = vcat[:, j * ATT_HEAD:(j + 1) * ATT_HEAD]
        s = _mm_nt(qh, kh) + (slopes[h] * LOG2E) * neg_dist
        sink = sink_ref[h] * LOG2E
        mx = jnp.maximum(jnp.max(s, axis=-1, keepdims=True), sink)
        p = jnp.exp2(s - mx)
        denom = jnp.sum(p, axis=-1, keepdims=True) + jnp.exp2(sink - mx)
        outs.append(_mm(p, vh) * (1.0 / denom))
    o_ref[...] = jnp.concatenate(outs, axis=1).astype(BF16)


def _swa(z, sinks, q_gain_t, k_gain_t, seq):
    m = z.shape[0]
    nblk = m // BLOCK
    bps = seq // BLOCK
    slopes = tuple(float(s) for s in
                   np.exp2(-8.0 * np.arange(1, ATT_HEADS + 1, dtype=np.float32) / ATT_HEADS).astype(np.float32))
    kv_cur = lambda c0: pl.BlockSpec((BLOCK, ATT_KV_DIM), lambda n, c=c0 // ATT_KV_DIM: (n, c))
    kv_prev = lambda c0: pl.BlockSpec((BLOCK, ATT_KV_DIM),
                                      lambda n, c=c0 // ATT_KV_DIM: (jnp.maximum(n - 1, 0), c))
    return pl.pallas_call(
        functools.partial(_swa_kernel, blocks_per_seq=bps, slopes=slopes),
        grid=(nblk,),
        in_specs=[pl.BlockSpec(memory_space=pltpu.SMEM),
                  pl.BlockSpec((BLOCK, ATT_Q_DIM), lambda n: (n, C_Q // ATT_Q_DIM)),
                  kv_cur(C_KA), kv_prev(C_KA), kv_cur(C_VA), kv_prev(C_VA),
                  pl.BlockSpec((1, ATT_Q_DIM), lambda n: (0, 0)),
                  pl.BlockSpec((1, ATT_KV_DIM), lambda n: (0, 0))],
        out_specs=pl.BlockSpec((BLOCK, ATT_Q_DIM), lambda n: (n, 0)),
        out_shape=jax.ShapeDtypeStruct((m, ATT_Q_DIM), BF16),
        compiler_params=_cparams(("parallel",)),
        name="swa",
    )(sinks, z, z, z, z, z, q_gain_t, k_gain_t)


def _branch_mix_kernel(orw_ref, oatt_ref, wb1_ref, wb2_ref, zg1_ref, zg2_ref, m_ref):
    p_rw = jnp.dot(orw_ref[...], wb1_ref[...], preferred_element_type=F32)
    p_att = jnp.dot(oatt_ref[...], wb2_ref[...], preferred_element_type=F32)
    g_rw = jax.nn.sigmoid(zg1_ref[...].astype(F32))
    g_att = jax.nn.sigmoid(zg2_ref[...].astype(F32))
    m_ref[...] = (g_rw * p_rw + g_att * p_att).astype(BF16)


def _branch_mix(o_rw, o_att, wb1, wb2, z, tm=1024, tn=1024):
    m = o_rw.shape[0]
    return pl.pallas_call(
        _branch_mix_kernel,
        grid=(m // tm, D_MODEL // tn),
        in_specs=[pl.BlockSpec((tm, RW_DIM), lambda i, j: (i, 0)),
                  pl.BlockSpec((tm, ATT_Q_DIM), lambda i, j: (i, 0)),
                  pl.BlockSpec((RW_DIM, tn), lambda i, j: (0, j)),
                  pl.BlockSpec((ATT_Q_DIM, tn), lambda i, j: (0, j)),
                  pl.BlockSpec((tm, tn), lambda i, j: (i, C_GRW // tn + j)),
                  pl.BlockSpec((tm, tn), lambda i, j: (i, C_GATT // tn + j))],
        out_specs=pl.BlockSpec((tm, tn), lambda i, j: (i, j)),
        out_shape=jax.ShapeDtypeStruct((m, D_MODEL), BF16),
        compiler_params=_cparams(("parallel", "arbitrary")),
        name="branch_mix",
    )(o_rw, o_att, wb1, wb2, z, z)


def _out_proj_kernel(m_ref, w_ref, x_ref, g_ref, x1_ref, h2_ref):
    x1 = x_ref[...] + jnp.dot(m_ref[...], w_ref[...], preferred_element_type=F32)
    x1_ref[...] = x1
    y = x1 * lax.rsqrt(jnp.mean(x1 * x1, axis=-1, keepdims=True) + RMS_EPS)
    h2_ref[...] = (y * g_ref[...]).astype(BF16)


def _out_proj(mix, w_out, x2, g2, tm=256):
    m = x2.shape[0]
    return pl.pallas_call(
        _out_proj_kernel,
        grid=(m // tm,),
        in_specs=[pl.BlockSpec((tm, D_MODEL), lambda i: (i, 0)),
                  pl.BlockSpec((D_MODEL, D_MODEL), lambda i: (0, 0)),
                  pl.BlockSpec((tm, D_MODEL), lambda i: (i, 0)),
                  pl.BlockSpec((1, D_MODEL), lambda i: (0, 0))],
        out_specs=[pl.BlockSpec((tm, D_MODEL), lambda i: (i, 0)),
                   pl.BlockSpec((tm, D_MODEL), lambda i: (i, 0))],
        out_shape=[jax.ShapeDtypeStruct((m, D_MODEL), F32),
                   jax.ShapeDtypeStruct((m, D_MODEL), BF16)],
        compiler_params=_cparams(("parallel",)),
        name="out_proj",
    )(mix, w_out, x2, g2)


def _ffn_up_kernel(h_ref, wv_ref, wg_ref, cwv_ref, cwg_ref, cbv_ref, cbg_ref, a_ref,
                   wvb_ref, wgb_ref, uv_ref, ug_ref, *, tm, sub, tiles_per_seq):
    halo = 8

    @pl.when(pl.program_id(1) == 0)
    def _():
        wvb_ref[...] = wv_ref[...].astype(BF16)
        wgb_ref[...] = wg_ref[...].astype(BF16)

    @pl.when((pl.program_id(1) % tiles_per_seq) == 0)
    def _():
        uv_ref[0:halo, :] = jnp.zeros((halo, uv_ref.shape[1]), F32)
        ug_ref[0:halo, :] = jnp.zeros((halo, ug_ref.shape[1]), F32)

    def conv(u_ref, cw_ref, cb_ref, r0):
        return (u_ref[halo - 2 + r0:halo - 2 + r0 + sub, :] * cw_ref[0:1, :]
                + u_ref[halo - 1 + r0:halo - 1 + r0 + sub, :] * cw_ref[1:2, :]
                + u_ref[halo + r0:halo + r0 + sub, :] * cw_ref[2:3, :] + cb_ref[...])

    for s in range(tm // sub):
        r0 = s * sub
        h = h_ref[r0:r0 + sub, :]
        uv_ref[halo + r0:halo + r0 + sub, :] = jnp.dot(h, wvb_ref[...], preferred_element_type=F32)
        ug_ref[halo + r0:halo + r0 + sub, :] = jnp.dot(h, wgb_ref[...], preferred_element_type=F32)
        val = conv(uv_ref, cwv_ref, cbv_ref, r0)
        gate = conv(ug_ref, cwg_ref, cbg_ref, r0)
        a_ref[r0:r0 + sub, :] = (gate * jax.nn.sigmoid(gate) * val).astype(BF16)
    uv_ref[0:halo, :] = uv_ref[tm:tm + halo, :]
    ug_ref[0:halo, :] = ug_ref[tm:tm + halo, :]


def _ffn_up(h2, w_up, conv_w, conv_b, seq, tm=1024, tn=512, sub=128):
    m = h2.shape[0]
    nj = D_FF // tn
    return pl.pallas_call(
        functools.partial(_ffn_up_kernel, tm=tm, sub=sub, tiles_per_seq=seq // tm),
        grid=(nj, m // tm),
        in_specs=[pl.BlockSpec((tm, D_MODEL), lambda j, i: (i, 0)),
                  pl.BlockSpec((D_MODEL, tn), lambda j, i: (0, j)),
                  pl.BlockSpec((D_MODEL, tn), lambda j, i: (0, nj + j)),
                  pl.BlockSpec((3, tn), lambda j, i: (0, j)),
                  pl.BlockSpec((3, tn), lambda j, i: (0, nj + j)),
                  pl.BlockSpec((1, tn), lambda j, i: (0, j)),
                  pl.BlockSpec((1, tn), lambda j, i: (0, nj + j))],
        out_specs=pl.BlockSpec((tm, tn), lambda j, i: (i, j)),
        out_shape=jax.ShapeDtypeStruct((m, D_FF), BF16),
        scratch_shapes=[pltpu.VMEM((D_MODEL, tn), BF16), pltpu.VMEM((D_MODEL, tn), BF16),
                        pltpu.VMEM((tm + 8, tn), F32), pltpu.VMEM((tm + 8, tn), F32)],
        compiler_params=_cparams(("parallel", "arbitrary")),
        name="ffn_up",
    )(h2, w_up, w_up, conv_w, conv_w, conv_b, conv_b)


def _ffn_down_kernel(a_ref, w_ref, x1_ref, o_ref, wb_ref):
    @pl.when(pl.program_id(1) == 0)
    def _():
        wb_ref[...] = w_ref[...].astype(BF16)

    o_ref[...] = x1_ref[...] + jnp.dot(a_ref[...], wb_ref[...], preferred_element_type=F32)


def _ffn_down(act, w_down, x1, tm=512, tn=512):
    m = act.shape[0]
    return pl.pallas_call(
        _ffn_down_kernel,
        grid=(D_MODEL // tn, m // tm),
        in_specs=[pl.BlockSpec((tm, D_FF), lambda j, i: (i, 0)),
                  pl.BlockSpec((D_FF, tn), lambda j, i: (0, j)),
                  pl.BlockSpec((tm, tn), lambda j, i: (i, j))],
        out_specs=pl.BlockSpec((tm, tn), lambda j, i: (i, j)),
        out_shape=jax.ShapeDtypeStruct((m, D_MODEL), F32),
        scratch_shapes=[pltpu.VMEM((D_FF, tn), BF16)],
        compiler_params=_cparams(("parallel", "arbitrary")),
        name="ffn_down",
    )(act, w_down, x1)


def _pad_cols(w, n):
    return jnp.pad(w, ((0, 0), (0, n - w.shape[1])))


def _pad_rows(w, n):
    return jnp.pad(w, ((0, n - w.shape[0]), (0, 0)))


def _layer(x2, batch, seq, norm1_g, w_in, rw_mu, rw_w0, rw_w2, rw_a0, rw_a2, rw_g2, rw_k_k, rw_k_a,
           rw_r_k, rw_gn_w, rw_gn_b, q_norm_g, k_norm_g, attn_sinks, w_branch, w_out,
           norm2_g, w_up, conv_w, conv_b, w_down):
    row = lambda v: v.reshape(1, -1).astype(F32)
    o3 = 3 * RW_DIM
    o4 = o3 + DECAY_LORA
    o5 = o4 + ICLR_LORA
    o6 = o5 + GATE_LORA
    oq = o6 + ATT_Q_DIM
    ok = oq + ATT_KV_DIM
    ov = ok + ATT_KV_DIM
    wcol = lambda a, b: w_in[:, a:b].astype(BF16)
    w_in_p = jnp.concatenate([
        wcol(0, o3), wcol(o6, oq), wcol(ov, w_in.shape[1]),
        _pad_cols(wcol(o3, o4), LORA_W), _pad_cols(wcol(o4, o5), LORA_A), wcol(o5, o6),
        wcol(oq, ok), wcol(ok, ov)], axis=1)
    mu = rw_mu.reshape(1, -1)
    mu_l = jnp.concatenate([_pad_cols(mu[:, o3:o4], LORA_W), _pad_cols(mu[:, o4:o5], LORA_A),
                            mu[:, o5:o6]], axis=1)
    w2p = _pad_rows(rw_w2, LORA_W).astype(BF16)
    a2p = _pad_rows(rw_a2, LORA_A).astype(BF16)

    z = _in_proj(x2, row(norm1_g), w_in_p)
    prep = _rwkv_prep(z, seq, mu[:, :RW_DIM], mu[:, RW_DIM:2 * RW_DIM], mu[:, 2 * RW_DIM:o3], mu_l,
                      row(rw_w0), row(rw_a0), row(rw_k_k), row(rw_k_a), row(rw_r_k),
                      w2p, a2p, rw_g2.astype(BF16))
    o_rw = _rwkv_chunk(prep[:9], prep[9], prep[10], row(rw_gn_w), row(rw_gn_b), batch, seq)
    o_att = _swa(z, attn_sinks.astype(F32), jnp.tile(row(q_norm_g), (1, ATT_HEADS)),
                 jnp.tile(row(k_norm_g), (1, ATT_KV_HEADS)), seq)
    mix = _branch_mix(o_rw, o_att, w_branch[:RW_DIM].astype(BF16), w_branch[RW_DIM:].astype(BF16), z)
    x1, h2 = _out_proj(mix, w_out.astype(BF16), x2, row(norm2_g))
    act = _ffn_up(h2, w_up, conv_w.astype(F32), row(conv_b), seq)
    return _ffn_down(act, w_down, x1)


def kernel(x, norm1_g, w_in, rw_mu, rw_w0, rw_w2, rw_a0, rw_a2, rw_g2, rw_k_k, rw_k_a, rw_r_k,
           rw_gn_w, rw_gn_b, q_norm_g, k_norm_g, attn_sinks, w_branch, w_out, norm2_g, w_up,
           conv_w, conv_b, w_down):
    batch, seq, d = x.shape
    x2 = x.reshape(batch * seq, d)
    params = (norm1_g, w_in, rw_mu, rw_w0, rw_w2, rw_a0, rw_a2, rw_g2, rw_k_k, rw_k_a, rw_r_k,
              rw_gn_w, rw_gn_b, q_norm_g, k_norm_g, attn_sinks, w_branch, w_out, norm2_g, w_up,
              conv_w, conv_b, w_down)
    for layer in range(norm1_g.shape[0]):
        x2 = _layer(x2, batch, seq, *(p[layer] for p in params))
    return x2.reshape(batch, seq, d)
```

```python
import functools

import jax
import jax.numpy as jnp
import numpy as np
from jax import lax
from jax.experimental import pallas as pl
from jax.experimental.pallas import tpu as pltpu

F32 = jnp.float32
BF16 = jnp.bfloat16

LANES = 128
BF16_SUBLANES = 16
VMEM_LIMIT = 56 * 1024 * 1024

D_MODEL = 2048
RW_HEADS = 16
RW_HEAD = 64
RW_DIM = RW_HEADS * RW_HEAD
DECAY_LORA = 96
ICLR_LORA = 96
GATE_LORA = 256
GN_EPS = 64e-5
ATT_HEADS = 16
ATT_KV_HEADS = 4
ATT_GROUP = ATT_HEADS // ATT_KV_HEADS
ATT_HEAD = 64
ATT_Q_DIM = ATT_HEADS * ATT_HEAD
ATT_KV_DIM = ATT_KV_HEADS * ATT_HEAD
WINDOW = 128
BLOCK = 128
D_FF = 5632
RMS_EPS = 1e-6
NEG_BIG = -1e30
LOG2E = 1.4426950408889634

C_R = 0
C_K = RW_DIM
C_V = 2 * RW_DIM
C_Q = 3 * RW_DIM
C_GRW = C_Q + ATT_Q_DIM
C_GATT = C_GRW + D_MODEL
C_LORA = C_GATT + D_MODEL
LORA_W = 128
LORA_A = 128
LORA_COLS = LORA_W + LORA_A + GATE_LORA
C_KA = C_LORA + LORA_COLS
C_VA = C_KA + ATT_KV_DIM
Z_COLS = C_VA + ATT_KV_DIM

CHUNK = 64
PAIR = 2 * RW_HEAD
N_PAIRS = RW_DIM // PAIR


def _cparams(sem):
    return pltpu.CompilerParams(dimension_semantics=sem, vmem_limit_bytes=VMEM_LIMIT)


def _mm(a, b):
    return jnp.dot(a.astype(BF16), b.astype(BF16), preferred_element_type=F32)


def _mm_nt(a, b):
    return lax.dot_general(a.astype(BF16), b.astype(BF16), (((1,), (1,)), ((), ())),
                           preferred_element_type=F32)


def _mm_tn(a, b):
    return lax.dot_general(a.astype(BF16), b.astype(BF16), (((0,), (0,)), ((), ())),
                           preferred_element_type=F32)


def _mm_split(m01, x):
    hi = x.astype(BF16)
    lo = (x - hi.astype(F32)).astype(BF16)
    return (jnp.dot(m01, hi, preferred_element_type=F32)
            + jnp.dot(m01, lo, preferred_element_type=F32))


def _mm_split_r(x, m01):
    hi = x.astype(BF16)
    lo = (x - hi.astype(F32)).astype(BF16)
    return (jnp.dot(hi, m01, preferred_element_type=F32)
            + jnp.dot(lo, m01, preferred_element_type=F32))


def _head_ones(n, head, scale):
    r = lax.broadcasted_iota(jnp.int32, (n, n), 0) // head
    c = lax.broadcasted_iota(jnp.int32, (n, n), 1) // head
    return jnp.where(r == c, scale, 0.0).astype(BF16)


def _in_proj_kernel(x_ref, g_ref, w_ref, z_ref, h_ref):
    @pl.when(pl.program_id(1) == 0)
    def _():
        xf = x_ref[...]
        y = xf * lax.rsqrt(jnp.mean(xf * xf, axis=-1, keepdims=True) + RMS_EPS)
        h_ref[...] = (y * g_ref[...]).astype(BF16)

    z_ref[...] = jnp.dot(h_ref[...], w_ref[...], preferred_element_type=F32).astype(BF16)


def _in_proj(x2, g1, w_in_p, tm=1024, tn=1024):
    m, d = x2.shape
    n = w_in_p.shape[1]
    return pl.pallas_call(
        _in_proj_kernel,
        grid=(m // tm, n // tn),
        in_specs=[pl.BlockSpec((tm, d), lambda i, j: (i, 0)),
                  pl.BlockSpec((1, d), lambda i, j: (0, 0)),
                  pl.BlockSpec((d, tn), lambda i, j: (0, j))],
        out_specs=pl.BlockSpec((tm, tn), lambda i, j: (i, j)),
        out_shape=jax.ShapeDtypeStruct((m, n), BF16),
        scratch_shapes=[pltpu.VMEM((tm, d), BF16)],
        compiler_params=_cparams(("parallel", "arbitrary")),
        name="in_proj",
    )(x2, g1, w_in_p)


def _rwkv_prep_kernel(zr_ref, zk_ref, zv_ref, zl_ref, pr_ref, pk_ref, pv_ref, pli_ref,
                      mur_ref, muk_ref, muv_ref, mul_ref,
                      w0_ref, a0_ref, kk_ref, ka_ref, rk_ref, w2_ref, a2_ref, g2_ref,
                      at_ref, rt_ref, bt_ref, kt_ref, bg_ref, kg_ref, v_ref, g_ref, bonus_ref,
                      gT_ref, gm_ref, *, tm, first):
    keep = jnp.where(first, 0.0, 1.0)
    nc = tm // CHUNK
    last = slice(BF16_SUBLANES - 1, BF16_SUBLANES)

    def shifted(z, prev_last, mu):
        z = z.astype(F32)
        row = lax.broadcasted_iota(jnp.int32, z.shape, 0)
        zprev = jnp.where(row == 0, prev_last.astype(F32) * keep, pltpu.roll(z, 1, 0))
        return z + (zprev - z) * mu

    lo = shifted(zl_ref[...], pli_ref[last, :], mul_ref[...])
    tanh_wd = jnp.tanh(lo[:, :LORA_W]).astype(BF16)
    ad = lo[:, LORA_W:LORA_W + LORA_A].astype(BF16)
    sig_gd = jax.nn.sigmoid(lo[:, LORA_W + LORA_A:]).astype(BF16)

    ti = lax.broadcasted_iota(jnp.int32, (tm, tm), 0)
    si = lax.broadcasted_iota(jnp.int32, (tm, tm), 1)
    tri = jnp.where(((ti // CHUNK) == (si // CHUNK)) & (si <= ti), 1.0, 0.0).astype(BF16)
    ones_h = _head_ones(PAIR, RW_HEAD, 1.0)
    yield

    for p in range(N_PAIRS):
        cs = slice(p * PAIR, (p + 1) * PAIR)
        r = shifted(zr_ref[:, cs], pr_ref[last, cs], mur_ref[:, cs])
        k = shifted(zk_ref[:, cs], pk_ref[last, cs], muk_ref[:, cs])
        v = shifted(zv_ref[:, cs], pv_ref[last, cs], muv_ref[:, cs])

        wpre = w0_ref[:, cs] + jnp.dot(tanh_wd, w2_ref[:, cs], preferred_element_type=F32)
        w = jnp.minimum(wpre, 0.0) - jnp.log1p(jnp.exp(-jnp.abs(wpre))) - 0.5
        lw = -jnp.exp(w)
        a = jax.nn.sigmoid(a0_ref[:, cs] + jnp.dot(ad, a2_ref[:, cs], preferred_element_type=F32))
        g = jnp.dot(sig_gd, g2_ref[:, cs], preferred_element_type=F32)

        kk = k * kk_ref[:, cs]
        kk = kk * lax.rsqrt(jnp.maximum(_mm(kk * kk, ones_h), 1e-24))
        k2 = k * (1.0 + (a - 1.0) * ka_ref[:, cs])
        bonus = _mm(r * k2 * rk_ref[:, cs], ones_h) * v
        ka_vec = -kk
        kb_vec = kk * a

        c = _mm_split(tri, lw)
        c3 = c.reshape(nc, CHUNK, PAIR)
        c_mid = c3[:, CHUNK // 2 - 1:CHUNK // 2, :]
        c_end = c3[:, CHUNK - 1:CHUNK, :]
        cm = jnp.broadcast_to(c_mid, c3.shape).reshape(tm, PAIR)
        cT = jnp.broadcast_to(c_end, c3.shape).reshape(tm, PAIR)

        e_in = jnp.exp(c - cm)
        e_out = jnp.exp(cm - c)
        e_end = jnp.exp(cT - c)
        at_ref[:, cs] = (ka_vec * jnp.exp(c - lw - cm)).astype(BF16)
        rt_ref[:, cs] = (r * e_in).astype(BF16)
        bt_ref[:, cs] = (kb_vec * e_out).astype(BF16)
        kt_ref[:, cs] = (k2 * e_out).astype(BF16)
        bg_ref[:, cs] = (kb_vec * e_end).astype(BF16)
        kg_ref[:, cs] = (k2 * e_end).astype(BF16)
        v_ref[:, cs] = v.astype(BF16)
        g_ref[:, cs] = g.astype(BF16)
        bonus_ref[:, cs] = bonus.astype(BF16)
        gT_ref[:, :, cs] = jnp.exp(c_end)
        gm_ref[:, :, cs] = jnp.exp(c_mid)
        yield


def _rwkv_kernel(*refs, tb, nb):
    prep_in, (gnw_ref, gnb_ref, o_ref), scratch = refs[:20], refs[20:23], refs[23:]
    operands, s_ref = scratch[:11], scratch[11]
    j = pl.program_id(1)
    prep = functools.partial(_rwkv_prep_kernel, *prep_in, *operands, tm=tb, first=j == 0)
    chunk = functools.partial(_rwkv_chunk_kernel, *operands, gnw_ref, gnb_ref, o_ref, s_ref,
                              chunks=tb // CHUNK, pairs=N_PAIRS, first=j == 1)

    @pl.when(j == 0)
    def _():
        for _ in prep():
            pass

    @pl.when((j > 0) & (j < nb))
    def _():
        chunk(filler=prep())

    @pl.when(j == nb)
    def _():
        chunk()


def _rwkv(z, batch, seq, mu_r, mu_k, mu_v, mu_l, w0, a0, k_k, k_a, r_k, w2p, a2p, g2, gn_w, gn_b, tb=128):
    m = z.shape[0]
    nb = seq // tb
    pb = tb // BF16_SUBLANES
    blk = lambda b, j: b * nb + jnp.minimum(j, nb - 1)

    def cur(col0, width):
        return pl.BlockSpec((tb, width), lambda b, j, c0=col0 // width: (blk(b, j), c0))

    def prev(col0, width):
        return pl.BlockSpec((BF16_SUBLANES, width),
                            lambda b, j, c0=col0 // width: (jnp.maximum(blk(b, j) * pb - 1, 0), c0))

    def whole(rows, width):
        return pl.BlockSpec((rows, width), lambda b, j: (0, 0))

    in_specs = [cur(C_R, RW_DIM), cur(C_K, RW_DIM), cur(C_V, RW_DIM), cur(C_LORA, LORA_COLS),
                prev(C_R, RW_DIM), prev(C_K, RW_DIM), prev(C_V, RW_DIM), prev(C_LORA, LORA_COLS),
                whole(1, RW_DIM), whole(1, RW_DIM), whole(1, RW_DIM), whole(1, LORA_COLS),
                whole(1, RW_DIM), whole(1, RW_DIM), whole(1, RW_DIM), whole(1, RW_DIM), whole(1, RW_DIM),
                whole(LORA_W, RW_DIM), whole(LORA_A, RW_DIM), whole(GATE_LORA, RW_DIM),
                whole(1, RW_DIM), whole(1, RW_DIM)]
    big = pltpu.VMEM((tb, RW_DIM), BF16)
    per_chunk = pltpu.VMEM((tb // CHUNK, 1, RW_DIM), F32)
    return pl.pallas_call(
        functools.partial(_rwkv_kernel, tb=tb, nb=nb),
        grid=(batch, nb + 1),
        in_specs=in_specs,
        out_specs=pl.BlockSpec((tb, RW_DIM), lambda b, j: (b * nb + jnp.maximum(j - 1, 0), 0)),
        out_shape=jax.ShapeDtypeStruct((m, RW_DIM), BF16),
        scratch_shapes=[big] * 9 + [per_chunk] * 2 + [pltpu.VMEM((N_PAIRS, PAIR, PAIR), F32)],
        compiler_params=_cparams(("parallel", "arbitrary")),
        name="rwkv",
    )(z, z, z, z, z, z, z, z, mu_r, mu_k, mu_v, mu_l, w0, a0, k_k, k_a, r_k, w2p, a2p, g2, gn_w, gn_b)


def _rwkv_chunk_kernel(at_ref, rt_ref, bt_ref, kt_ref, bg_ref, kg_ref, v_ref, g_ref, bonus_ref,
                       gT_ref, gm_ref, gnw_ref, gnb_ref, o_ref, s_ref, *, chunks, pairs, first,
                       filler=None):
    @pl.when(first)
    def _():
        s_ref[...] = jnp.zeros_like(s_ref)

    T = CHUNK
    lane = lax.broadcasted_iota(jnp.int32, (T, PAIR), 1)
    head0 = lane < RW_HEAD
    ri = lax.broadcasted_iota(jnp.int32, (2 * T, 2 * T), 0)
    ci = lax.broadcasted_iota(jnp.int32, (2 * T, 2 * T), 1)
    same = (ri // T) == (ci // T)
    strict = same & ((ri % T) > (ci % T))
    incl = same & ((ri % T) >= (ci % T))
    eye = jnp.where(ri == ci, 1.0, 0.0)
    own = (ri // T) == (ci // RW_HEAD)

    items = [(p, c) for c in range(chunks) for p in range(pairs)]
    rows = lambda c: slice(c * T, (c + 1) * T)
    cols = lambda p: slice(p * PAIR, (p + 1) * PAIR)

    def stack(ref):
        out = []
        for p, c in items:
            x = ref[rows(c), cols(p)]
            zero = jnp.zeros_like(x)
            out.append(jnp.concatenate([jnp.where(head0, x, zero), jnp.where(head0, zero, x)], axis=0))
        return out

    each = lambda f, *ls: [f(*xs) for xs in zip(*ls)]
    La, Lr, Rb, Rk = stack(at_ref), stack(rt_ref), stack(bt_ref), stack(kt_ref)
    Rbg, Rkg, Vs = stack(bg_ref), stack(kg_ref), stack(v_ref)
    gm_row = [gm_ref[c][:, cols(p)] for p, c in items]
    gT_row = [gT_ref[c][:, cols(p)] for p, c in items]
    gate = [g_ref[rows(c), cols(p)] for p, c in items]
    bonus = [bonus_ref[rows(c), cols(p)] for p, c in items]
    filler = iter(()) if filler is None else filler
    tick = lambda: next(filler, None)

    AA = each(lambda la, lr, rb, rk: _mm_nt(jnp.concatenate([la, lr], axis=0),
                                            jnp.concatenate([rb, rk], axis=0)), La, Lr, Rb, Rk)
    tick()
    N = [jnp.where(strict, aa[:2 * T, :2 * T], 0.0) for aa in AA]
    Aak = [jnp.where(strict, aa[:2 * T, 2 * T:], 0.0) for aa in AA]
    Arb = [jnp.where(incl, aa[2 * T:, :2 * T], 0.0) for aa in AA]
    Ark = [jnp.where(incl, aa[2 * T:, 2 * T:], 0.0) for aa in AA]
    W = each(lambda n: eye + n, N)
    P = N
    span = 2
    while span < T:
        P = each(lambda p_: _mm(p_, p_), P)
        tick()
        W = each(lambda w, p_: w + _mm(w, p_), W, P)
        tick()
        span *= 2
    AkV = each(_mm, Aak, Vs)
    X = each(lambda w, la, akv: _mm(w, jnp.concatenate([la, akv.astype(BF16)], axis=1)).astype(BF16),
             W, La, AkV)
    Z = each(lambda arb, ark, x, vs:
             _mm(jnp.concatenate([arb.astype(BF16), ark.astype(BF16)], axis=1),
                 jnp.concatenate([x, jnp.concatenate([jnp.zeros_like(vs), vs], axis=1)], axis=0)),
             Arb, Ark, X, Vs)
    Q = each(lambda z, lr, gm: (z[:, :PAIR] + lr.astype(F32)) * gm, Z, Lr, gm_row)
    Y0 = [z[:, PAIR:] for z in Z]
    Mbd = each(lambda rbg, x, gm: _mm_tn(rbg, x[:, :PAIR]) * gm, Rbg, X, gm_row)
    NcT = each(lambda x, vs, rbg, rkg: _mm_tn(jnp.concatenate([x[:, PAIR:], vs], axis=0),
                                              jnp.concatenate([rbg, rkg], axis=0)), X, Vs, Rbg, Rkg)
    for _ in filler:
        pass

    S = [s_ref[p] for p in range(pairs)]
    for c in range(chunks):
        for p in range(pairs):
            i = c * pairs + p
            Ys = _mm_nt(Q[i], S[p]) + Y0[i]
            S[p] = S[p] * gT_row[i] + _mm_nt(S[p], Mbd[i]) + NcT[i]
            mu = jnp.sum(Ys, axis=-1, keepdims=True) * (1.0 / RW_HEAD)
            d = jnp.where(own, Ys - mu, 0.0)
            var = jnp.sum(d * d, axis=-1, keepdims=True) * (1.0 / RW_HEAD)
            dn = d * lax.rsqrt(var + GN_EPS)
            yn = (dn[:T, :] + dn[T:, :]) * gnw_ref[:, cols(p)] + gnb_ref[:, cols(p)]
            out = (yn + bonus[i].astype(F32)) * gate[i].astype(F32)
            o_ref[rows(c), cols(p)] = out.astype(BF16)
    for p in range(pairs):
        s_ref[p] = S[p]


def _swa_kernel(sink_ref, q_ref, kc_ref, kp_ref, vc_ref, vp_ref, qg_ref, kg_ref, o_ref, *,
                blocks_per_seq, slopes):
    first = (pl.program_id(0) % blocks_per_seq) == 0
    mean_h = _head_ones(LANES, ATT_HEAD, 1.0 / ATT_HEAD)

    def head_rms(x, gain):
        parts = []
        for b in range(x.shape[1] // LANES):
            xb = x[:, b * LANES:(b + 1) * LANES]
            ms = _mm_split_r(xb * xb, mean_h)
            parts.append(xb * lax.rsqrt(ms + RMS_EPS))
        return jnp.concatenate(parts, axis=1) * gain

    q = head_rms(q_ref[...].astype(F32), qg_ref[...]) * (ATT_HEAD ** -0.5 * LOG2E)
    q = q.astype(BF16)
    kcat = jnp.concatenate([kp_ref[...], kc_ref[...]], axis=0).astype(F32)
    kcat = head_rms(kcat, kg_ref[...]).astype(BF16)
    vcat = jnp.concatenate([vp_ref[...], vc_ref[...]], axis=0)

    qi = lax.broadcasted_iota(jnp.int32, (BLOCK, 2 * BLOCK), 0)
    kj = lax.broadcasted_iota(jnp.int32, (BLOCK, 2 * BLOCK), 1)
    dist_i = BLOCK + qi - kj
    first_key = jnp.where(first, BLOCK, 0)
    valid = (dist_i >= 0) & (dist_i < WINDOW) & (kj >= first_key)
    neg_dist = jnp.where(valid, -dist_i.astype(F32), NEG_BIG)

    outs = []
    for h in range(ATT_HEADS):
        j = h // ATT_GROUP
        qh = q[:, h * ATT_HEAD:(h + 1) * ATT_HEAD]
        kh = kcat[:, j * ATT_HEAD:(j + 1) * ATT_HEAD]
        vh = vcat[:, j * ATT_HEAD:(j + 1) * ATT_HEAD]
        s = _mm_nt(qh, kh) + (slopes[h] * LOG2E) * neg_dist
        sink = sink_ref[h] * LOG2E
        mx = jnp.maximum(jnp.max(s, axis=-1, keepdims=True), sink)
        p = jnp.exp2(s - mx)
        denom = jnp.sum(p, axis=-1, keepdims=True) + jnp.exp2(sink - mx)
        outs.append(_mm(p, vh) * (1.0 / denom))
    o_ref[...] = jnp.concatenate(outs, axis=1).astype(BF16)


def _swa(z, sinks, q_gain_t, k_gain_t, seq):
    m = z.shape[0]
    nblk = m // BLOCK
    bps = seq // BLOCK
    slopes = tuple(float(s) for s in
                   np.exp2(-8.0 * np.arange(1, ATT_HEADS + 1, dtype=np.float32) / ATT_HEADS).astype(np.float32))
    kv_cur = lambda c0: pl.BlockSpec((BLOCK, ATT_KV_DIM), lambda n, c=c0 // ATT_KV_DIM: (n, c))
    kv_prev = lambda c0: pl.BlockSpec((BLOCK, ATT_KV_DIM),
                                      lambda n, c=c0 // ATT_KV_DIM: (jnp.maximum(n - 1, 0), c))
    return pl.pallas_call(
        functools.partial(_swa_kernel, blocks_per_seq=bps, slopes=slopes),
        grid=(nblk,),
        in_specs=[pl.BlockSpec(memory_space=pltpu.SMEM),
                  pl.BlockSpec((BLOCK, ATT_Q_DIM), lambda n: (n, C_Q // ATT_Q_DIM)),
                  kv_cur(C_KA), kv_prev(C_KA), kv_cur(C_VA), kv_prev(C_VA),
                  pl.BlockSpec((1, ATT_Q_DIM), lambda n: (0, 0)),
                  pl.BlockSpec((1, ATT_KV_DIM), lambda n: (0, 0))],
        out_specs=pl.BlockSpec((BLOCK, ATT_Q_DIM), lambda n: (n, 0)),
        out_shape=jax.ShapeDtypeStruct((m, ATT_Q_DIM), BF16),
        compiler_params=_cparams(("parallel",)),
        name="swa",
    )(sinks, z, z, z, z, z, q_gain_t, k_gain_t)


def _branch_mix_kernel(orw_ref, oatt_ref, wb1_ref, wb2_ref, zg1_ref, zg2_ref, m_ref):
    p_rw = jnp.dot(orw_ref[...], wb1_ref[...], preferred_element_type=F32)
    p_att = jnp.dot(oatt_ref[...], wb2_ref[...], preferred_element_type=F32)
    g_rw = jax.nn.sigmoid(zg1_ref[...].astype(F32))
    g_att = jax.nn.sigmoid(zg2_ref[...].astype(F32))
    m_ref[...] = (g_rw * p_rw + g_att * p_att).astype(BF16)


def _branch_mix(o_rw, o_att, wb1, wb2, z, tm=1024, tn=1024):
    m = o_rw.shape[0]
    return pl.pallas_call(
        _branch_mix_kernel,
        grid=(m // tm, D_MODEL // tn),
        in_specs=[pl.BlockSpec((tm, RW_DIM), lambda i, j: (i, 0)),
                  pl.BlockSpec((tm, ATT_Q_DIM), lambda i, j: (i, 0)),
                  pl.BlockSpec((RW_DIM, tn), lambda i, j: (0, j)),
                  pl.BlockSpec((ATT_Q_DIM, tn), lambda i, j: (0, j)),
                  pl.BlockSpec((tm, tn), lambda i, j: (i, C_GRW // tn + j)),
                  pl.BlockSpec((tm, tn), lambda i, j: (i, C_GATT // tn + j))],
        out_specs=pl.BlockSpec((tm, tn), lambda i, j: (i, j)),
        out_shape=jax.ShapeDtypeStruct((m, D_MODEL), BF16),
        compiler_params=_cparams(("parallel", "arbitrary")),
        name="branch_mix",
    )(o_rw, o_att, wb1, wb2, z, z)


def _out_proj_kernel(m_ref, w_ref, x_ref, g_ref, x1_ref, h2_ref):
    x1 = x_ref[...] + jnp.dot(m_ref[...], w_ref[...], preferred_element_type=F32)
    x1_ref[...] = x1
    y = x1 * lax.rsqrt(jnp.mean(x1 * x1, axis=-1, keepdims=True) + RMS_EPS)
    h2_ref[...] = (y * g_ref[...]).astype(BF16)


def _out_proj(mix, w_out, x2, g2, tm=256):
    m = x2.shape[0]
    return pl.pallas_call(
        _out_proj_kernel,
        grid=(m // tm,),
        in_specs=[pl.BlockSpec((tm, D_MODEL), lambda i: (i, 0)),
                  pl.BlockSpec((D_MODEL, D_MODEL), lambda i: (0, 0)),
                  pl.BlockSpec((tm, D_MODEL), lambda i: (i, 0)),
                  pl.BlockSpec((1, D_MODEL), lambda i: (0, 0))],
        out_specs=[pl.BlockSpec((tm, D_MODEL), lambda i: (i, 0)),
                   pl.BlockSpec((tm, D_MODEL), lambda i: (i, 0))],
        out_shape=[jax.ShapeDtypeStruct((m, D_MODEL), F32),
                   jax.ShapeDtypeStruct((m, D_MODEL), BF16)],
        compiler_params=_cparams(("parallel",)),
        name="out_proj",
    )(mix, w_out, x2, g2)


def _ffn_up_kernel(h_ref, wv_ref, wg_ref, cwv_ref, cwg_ref, cbv_ref, cbg_ref, a_ref,
                   wvb_ref, wgb_ref, uv_ref, ug_ref, *, tm, sub, tiles_per_seq):
    halo = 8

    @pl.when(pl.program_id(1) == 0)
    def _():
        wvb_ref[...] = wv_ref[...].astype(BF16)
        wgb_ref[...] = wg_ref[...].astype(BF16)

    @pl.when((pl.program_id(1) % tiles_per_seq) == 0)
    def _():
        uv_ref[0:halo, :] = jnp.zeros((halo, uv_ref.shape[1]), F32)
        ug_ref[0:halo, :] = jnp.zeros((halo, ug_ref.shape[1]), F32)

    def conv(u_ref, cw_ref, cb_ref, r0):
        return (u_ref[halo - 2 + r0:halo - 2 + r0 + sub, :] * cw_ref[0:1, :]
                + u_ref[halo - 1 + r0:halo - 1 + r0 + sub, :] * cw_ref[1:2, :]
                + u_ref[halo + r0:halo + r0 + sub, :] * cw_ref[2:3, :] + cb_ref[...])

    for s in range(tm // sub):
        r0 = s * sub
        h = h_ref[r0:r0 + sub, :]
        uv_ref[halo + r0:halo + r0 + sub, :] = jnp.dot(h, wvb_ref[...], preferred_element_type=F32)
        ug_ref[halo + r0:halo + r0 + sub, :] = jnp.dot(h, wgb_ref[...], preferred_element_type=F32)
        val = conv(uv_ref, cwv_ref, cbv_ref, r0)
        gate = conv(ug_ref, cwg_ref, cbg_ref, r0)
        a_ref[r0:r0 + sub, :] = (gate * jax.nn.sigmoid(gate) * val).astype(BF16)
    uv_ref[0:halo, :] = uv_ref[tm:tm + halo, :]
    ug_ref[0:halo, :] = ug_ref[tm:tm + halo, :]


def _ffn_up(h2, w_up, conv_w, conv_b, seq, tm=1024, tn=512, sub=128):
    m = h2.shape[0]
    nj = D_FF // tn
    return pl.pallas_call(
        functools.partial(_ffn_up_kernel, tm=tm, sub=sub, tiles_per_seq=seq // tm),
        grid=(nj, m // tm),
        in_specs=[pl.BlockSpec((tm, D_MODEL), lambda j, i: (i, 0)),
                  pl.BlockSpec((D_MODEL, tn), lambda j, i: (0, j)),
                  pl.BlockSpec((D_MODEL, tn), lambda j, i: (0, nj + j)),
                  pl.BlockSpec((3, tn), lambda j, i: (0, j)),
                  pl.BlockSpec((3, tn), lambda j, i: (0, nj + j)),
                  pl.BlockSpec((1, tn), lambda j, i: (0, j)),
                  pl.BlockSpec((1, tn), lambda j, i: (0, nj + j))],
        out_specs=pl.BlockSpec((tm, tn), lambda j, i: (i, j)),
        out_shape=jax.ShapeDtypeStruct((m, D_FF), BF16),
        scratch_shapes=[pltpu.VMEM((D_MODEL, tn), BF16), pltpu.VMEM((D_MODEL, tn), BF16),
                        pltpu.VMEM((tm + 8, tn), F32), pltpu.VMEM((tm + 8, tn), F32)],
        compiler_params=_cparams(("parallel", "arbitrary")),
        name="ffn_up",
    )(h2, w_up, w_up, conv_w, conv_w, conv_b, conv_b)


def _ffn_down_kernel(a_ref, w_ref, x1_ref, o_ref, wb_ref):
    @pl.when(pl.program_id(1) == 0)
    def _():
        wb_ref[...] = w_ref[...].astype(BF16)

    o_ref[...] = x1_ref[...] + jnp.dot(a_ref[...], wb_ref[...], preferred_element_type=F32)


def _ffn_down(act, w_down, x1, tm=512, tn=512):
    m = act.shape[0]
    return pl.pallas_call(
        _ffn_down_kernel,
        grid=(D_MODEL // tn, m // tm),
        in_specs=[pl.BlockSpec((tm, D_FF), lambda j, i: (i, 0)),
                  pl.BlockSpec((D_FF, tn), lambda j, i: (0, j)),
                  pl.BlockSpec((tm, tn), lambda j, i: (i, j))],
        out_specs=pl.BlockSpec((tm, tn), lambda j, i: (i, j)),
        out_shape=jax.ShapeDtypeStruct((m, D_MODEL), F32),
        scratch_shapes=[pltpu.VMEM((D_FF, tn), BF16)],
        compiler_params=_cparams(("parallel", "arbitrary")),
        name="ffn_down",
    )(act, w_down, x1)


def _pad_cols(w, n):
    return jnp.pad(w, ((0, 0), (0, n - w.shape[1])))


def _pad_rows(w, n):
    return jnp.pad(w, ((0, n - w.shape[0]), (0, 0)))


def _layer(x2, batch, seq, norm1_g, w_in, rw_mu, rw_w0, rw_w2, rw_a0, rw_a2, rw_g2, rw_k_k, rw_k_a,
           rw_r_k, rw_gn_w, rw_gn_b, q_norm_g, k_norm_g, attn_sinks, w_branch, w_out,
           norm2_g, w_up, conv_w, conv_b, w_down):
    row = lambda v: v.reshape(1, -1).astype(F32)
    o3 = 3 * RW_DIM
    o4 = o3 + DECAY_LORA
    o5 = o4 + ICLR_LORA
    o6 = o5 + GATE_LORA
    oq = o6 + ATT_Q_DIM
    ok = oq + ATT_KV_DIM
    ov = ok + ATT_KV_DIM
    wcol = lambda a, b: w_in[:, a:b].astype(BF16)
    w_in_p = jnp.concatenate([
        wcol(0, o3), wcol(o6, oq), wcol(ov, w_in.shape[1]),
        _pad_cols(wcol(o3, o4), LORA_W), _pad_cols(wcol(o4, o5), LORA_A), wcol(o5, o6),
        wcol(oq, ok), wcol(ok, ov)], axis=1)
    mu = rw_mu.reshape(1, -1)
    mu_l = jnp.concatenate([_pad_cols(mu[:, o3:o4], LORA_W), _pad_cols(mu[:, o4:o5], LORA_A),
                            mu[:, o5:o6]], axis=1)
    w2p = _pad_rows(rw_w2, LORA_W).astype(BF16)
    a2p = _pad_rows(rw_a2, LORA_A).astype(BF16)

    z = _in_proj(x2, row(norm1_g), w_in_p)
    o_rw = _rwkv(z, batch, seq, mu[:, :RW_DIM], mu[:, RW_DIM:2 * RW_DIM], mu[:, 2 * RW_DIM:o3], mu_l,
                 row(rw_w0), row(rw_a0), row(rw_k_k), row(rw_k_a), row(rw_r_k),
                 w2p, a2p, rw_g2.astype(BF16), row(rw_gn_w), row(rw_gn_b))
    o_att = _swa(z, attn_sinks.astype(F32), jnp.tile(row(q_norm_g), (1, ATT_HEADS)),
                 jnp.tile(row(k_norm_g), (1, ATT_KV_HEADS)), seq)
    mix = _branch_mix(o_rw, o_att, w_branch[:RW_DIM].astype(BF16), w_branch[RW_DIM:].astype(BF16), z)
    x1, h2 = _out_proj(mix, w_out.astype(BF16), x2, row(norm2_g))
    act = _ffn_up(h2, w_up, conv_w.astype(F32), row(conv_b), seq)
    return _ffn_down(act, w_down, x1)


def kernel(x, norm1_g, w_in, rw_mu, rw_w0, rw_w2, rw_a0, rw_a2, rw_g2, rw_k_k, rw_k_a, rw_r_k,
           rw_gn_w, rw_gn_b, q_norm_g, k_norm_g, attn_sinks, w_branch, w_out, norm2_g, w_up,
           conv_w, conv_b, w_down):
    batch, seq, d = x.shape
    x2 = x.reshape(batch * seq, d)
    params = (norm1_g, w_in, rw_mu, rw_w0, rw_w2, rw_a0, rw_a2, rw_g2, rw_k_k, rw_k_a, rw_r_k,
              rw_gn_w, rw_gn_b, q_norm_g, k_norm_g, attn_sinks, w_branch, w_out, norm2_g, w_up,
              conv_w, conv_b, w_down)
    for layer in range(norm1_g.shape[0]):
        x2 = _layer(x2, batch, seq, *(p[layer] for p in params))
    return x2.reshape(batch, seq, d)
```

```python
import functools

import jax
import jax.numpy as jnp
import numpy as np
from jax import lax
from jax.experimental import pallas as pl
from jax.experimental.pallas import tpu as pltpu

F32 = jnp.float32
BF16 = jnp.bfloat16

LANES = 128
BF16_SUBLANES = 16
VMEM_LIMIT = 56 * 1024 * 1024

D_MODEL = 2048
RW_HEADS = 16
RW_HEAD = 64
RW_DIM = RW_HEADS * RW_HEAD
DECAY_LORA = 96
ICLR_LORA = 96
GATE_LORA = 256
GN_EPS = 64e-5
ATT_HEADS = 16
ATT_KV_HEADS = 4
ATT_GROUP = ATT_HEADS // ATT_KV_HEADS
ATT_HEAD = 64
ATT_Q_DIM = ATT_HEADS * ATT_HEAD
ATT_KV_DIM = ATT_KV_HEADS * ATT_HEAD
WINDOW = 128
BLOCK = 128
D_FF = 5632
RMS_EPS = 1e-6
NEG_BIG = -1e30
LOG2E = 1.4426950408889634

C_R = 0
C_K = RW_DIM
C_V = 2 * RW_DIM
C_Q = 3 * RW_DIM
C_GRW = C_Q + ATT_Q_DIM
C_GATT = C_GRW + D_MODEL
C_LORA = C_GATT + D_MODEL
LORA_W = 128
LORA_A = 128
LORA_COLS = LORA_W + LORA_A + GATE_LORA
C_KA = C_LORA + LORA_COLS
C_VA = C_KA + ATT_KV_DIM
Z_COLS = C_VA + ATT_KV_DIM

CHUNK = 64
PAIR = 2 * RW_HEAD
N_PAIRS = RW_DIM // PAIR


def _cparams(sem):
    return pltpu.CompilerParams(dimension_semantics=sem, vmem_limit_bytes=VMEM_LIMIT)


def _mm(a, b):
    return jnp.dot(a.astype(BF16), b.astype(BF16), preferred_element_type=F32)


def _mm_nt(a, b):
    return lax.dot_general(a.astype(BF16), b.astype(BF16), (((1,), (1,)), ((), ())),
                           preferred_element_type=F32)


def _mm_tn(a, b):
    return lax.dot_general(a.astype(BF16), b.astype(BF16), (((0,), (0,)), ((), ())),
                           preferred_element_type=F32)


def _mm_split(m01, x):
    hi = x.astype(BF16)
    lo = (x - hi.astype(F32)).astype(BF16)
    return (jnp.dot(m01, hi, preferred_element_type=F32)
            + jnp.dot(m01, lo, preferred_element_type=F32))


def _mm_split_r(x, m01):
    hi = x.astype(BF16)
    lo = (x - hi.astype(F32)).astype(BF16)
    return (jnp.dot(hi, m01, preferred_element_type=F32)
            + jnp.dot(lo, m01, preferred_element_type=F32))


def _head_ones(n, head, scale):
    r = lax.broadcasted_iota(jnp.int32, (n, n), 0) // head
    c = lax.broadcasted_iota(jnp.int32, (n, n), 1) // head
    return jnp.where(r == c, scale, 0.0).astype(BF16)


def _w_in_layout_kernel(w_ref, o_ref, *, moves, pad):
    o_ref[:, pad[0]:pad[1]] = jnp.zeros((o_ref.shape[0], pad[1] - pad[0]), BF16)
    for dst, src, width in moves:
        o_ref[:, dst:dst + width] = w_ref[:, src:src + width].astype(BF16)


def _w_in_layout(w_in, rows=256):
    d, n = w_in.shape
    o3 = 3 * RW_DIM
    o4 = o3 + DECAY_LORA
    o5 = o4 + ICLR_LORA
    o6 = o5 + GATE_LORA
    oq = o6 + ATT_Q_DIM
    ok = oq + ATT_KV_DIM
    ov = ok + ATT_KV_DIM
    moves = ((C_R, 0, o3), (C_Q, o6, ATT_Q_DIM), (C_GRW, ov, 2 * D_MODEL),
             (C_LORA, o3, DECAY_LORA), (C_LORA + LORA_W, o4, ICLR_LORA),
             (C_LORA + LORA_W + LORA_A, o5, GATE_LORA), (C_KA, oq, ATT_KV_DIM), (C_VA, ok, ATT_KV_DIM))
    return pl.pallas_call(
        functools.partial(_w_in_layout_kernel, moves=moves, pad=(C_LORA, C_LORA + LORA_W + LORA_A)),
        grid=(d // rows,),
        in_specs=[pl.BlockSpec((rows, n), lambda i: (i, 0))],
        out_specs=pl.BlockSpec((rows, Z_COLS), lambda i: (i, 0)),
        out_shape=jax.ShapeDtypeStruct((d, Z_COLS), BF16),
        compiler_params=_cparams(("parallel",)),
        name="w_in_layout",
    )(w_in)


def _in_proj_kernel(x_ref, g_ref, w_ref, z_ref, h_ref):
    @pl.when(pl.program_id(1) == 0)
    def _():
        xf = x_ref[...]
        y = xf * lax.rsqrt(jnp.mean(xf * xf, axis=-1, keepdims=True) + RMS_EPS)
        h_ref[...] = (y * g_ref[...]).astype(BF16)

    z_ref[...] = jnp.dot(h_ref[...], w_ref[...], preferred_element_type=F32).astype(BF16)


def _in_proj(x2, g1, w_in_p, tm=1024, tn=1024):
    m, d = x2.shape
    n = w_in_p.shape[1]
    return pl.pallas_call(
        _in_proj_kernel,
        grid=(m // tm, n // tn),
        in_specs=[pl.BlockSpec((tm, d), lambda i, j: (i, 0)),
                  pl.BlockSpec((1, d), lambda i, j: (0, 0)),
                  pl.BlockSpec((d, tn), lambda i, j: (0, j))],
        out_specs=pl.BlockSpec((tm, tn), lambda i, j: (i, j)),
        out_shape=jax.ShapeDtypeStruct((m, n), BF16),
        scratch_shapes=[pltpu.VMEM((tm, d), BF16)],
        compiler_params=_cparams(("parallel", "arbitrary")),
        name="in_proj",
    )(x2, g1, w_in_p)


def _rwkv_prep_kernel(zr_ref, zk_ref, zv_ref, zl_ref, pr_ref, pk_ref, pv_ref, pli_ref,
                      mur_ref, muk_ref, muv_ref, mul_ref,
                      w0_ref, a0_ref, kk_ref, ka_ref, rk_ref, w2_ref, a2_ref, g2_ref,
                      at_ref, rt_ref, bt_ref, kt_ref, bg_ref, kg_ref, v_ref, g_ref, bonus_ref,
                      gT_ref, gm_ref, *, tm, first):
    keep = jnp.where(first, 0.0, 1.0)
    nc = tm // CHUNK
    last = slice(BF16_SUBLANES - 1, BF16_SUBLANES)

    def shifted(z, prev_last, mu):
        z = z.astype(F32)
        row = lax.broadcasted_iota(jnp.int32, z.shape, 0)
        zprev = jnp.where(row == 0, prev_last.astype(F32) * keep, pltpu.roll(z, 1, 0))
        return z + (zprev - z) * mu

    lo = shifted(zl_ref[...], pli_ref[last, :], mul_ref[...])
    tanh_wd = jnp.tanh(lo[:, :LORA_W]).astype(BF16)
    ad = lo[:, LORA_W:LORA_W + LORA_A].astype(BF16)
    sig_gd = jax.nn.sigmoid(lo[:, LORA_W + LORA_A:]).astype(BF16)

    ti = lax.broadcasted_iota(jnp.int32, (tm, tm), 0)
    si = lax.broadcasted_iota(jnp.int32, (tm, tm), 1)
    tri = jnp.where(((ti // CHUNK) == (si // CHUNK)) & (si <= ti), 1.0, 0.0).astype(BF16)
    ones_h = _head_ones(PAIR, RW_HEAD, 1.0)
    yield

    for p in range(N_PAIRS):
        cs = slice(p * PAIR, (p + 1) * PAIR)
        r = shifted(zr_ref[:, cs], pr_ref[last, cs], mur_ref[:, cs])
        k = shifted(zk_ref[:, cs], pk_ref[last, cs], muk_ref[:, cs])
        v = shifted(zv_ref[:, cs], pv_ref[last, cs], muv_ref[:, cs])

        wpre = w0_ref[:, cs] + jnp.dot(tanh_wd, w2_ref[:, cs], preferred_element_type=F32)
        w = jnp.minimum(wpre, 0.0) - jnp.log1p(jnp.exp(-jnp.abs(wpre))) - 0.5
        lw = -jnp.exp(w)
        a = jax.nn.sigmoid(a0_ref[:, cs] + jnp.dot(ad, a2_ref[:, cs], preferred_element_type=F32))
        g = jnp.dot(sig_gd, g2_ref[:, cs], preferred_element_type=F32)

        kk = k * kk_ref[:, cs]
        kk = kk * lax.rsqrt(jnp.maximum(_mm(kk * kk, ones_h), 1e-24))
        k2 = k * (1.0 + (a - 1.0) * ka_ref[:, cs])
        bonus = _mm(r * k2 * rk_ref[:, cs], ones_h) * v
        ka_vec = -kk
        kb_vec = kk * a

        c = _mm_split(tri, lw)
        c3 = c.reshape(nc, CHUNK, PAIR)
        c_mid = c3[:, CHUNK // 2 - 1:CHUNK // 2, :]
        c_end = c3[:, CHUNK - 1:CHUNK, :]
        cm = jnp.broadcast_to(c_mid, c3.shape).reshape(tm, PAIR)
        cT = jnp.broadcast_to(c_end, c3.shape).reshape(tm, PAIR)

        e_in = jnp.exp(c - cm)
        e_out = jnp.exp(cm - c)
        e_end = jnp.exp(cT - c)
        at_ref[:, cs] = (ka_vec * jnp.exp(c - lw - cm)).astype(BF16)
        rt_ref[:, cs] = (r * e_in).astype(BF16)
        bt_ref[:, cs] = (kb_vec * e_out).astype(BF16)
        kt_ref[:, cs] = (k2 * e_out).astype(BF16)
        bg_ref[:, cs] = (kb_vec * e_end).astype(BF16)
        kg_ref[:, cs] = (k2 * e_end).astype(BF16)
        v_ref[:, cs] = v.astype(BF16)
        g_ref[:, cs] = g.astype(BF16)
        bonus_ref[:, cs] = bonus.astype(BF16)
        gT_ref[:, :, cs] = jnp.exp(c_end)
        gm_ref[:, :, cs] = jnp.exp(c_mid)
        yield


def _rwkv_kernel(*refs, tb, nb):
    prep_in, (gnw_ref, gnb_ref, o_ref), scratch = refs[:20], refs[20:23], refs[23:]
    operands, s_ref = scratch[:11], scratch[11]
    j = pl.program_id(1)
    prep = functools.partial(_rwkv_prep_kernel, *prep_in, *operands, tm=tb, first=j == 0)
    chunk = functools.partial(_rwkv_chunk_kernel, *operands, gnw_ref, gnb_ref, o_ref, s_ref,
                              chunks=tb // CHUNK, pairs=N_PAIRS, first=j == 1)

    @pl.when(j == 0)
    def _():
        for _ in prep():
            pass

    @pl.when((j > 0) & (j < nb))
    def _():
        chunk(filler=prep())

    @pl.when(j == nb)
    def _():
        chunk()


def _rwkv(z, batch, seq, mu_r, mu_k, mu_v, mu_l, w0, a0, k_k, k_a, r_k, w2p, a2p, g2, gn_w, gn_b, tb=128):
    m = z.shape[0]
    nb = seq // tb
    pb = tb // BF16_SUBLANES
    blk = lambda b, j: b * nb + jnp.minimum(j, nb - 1)

    def cur(col0, width):
        return pl.BlockSpec((tb, width), lambda b, j, c0=col0 // width: (blk(b, j), c0))

    def prev(col0, width):
        return pl.BlockSpec((BF16_SUBLANES, width),
                            lambda b, j, c0=col0 // width: (jnp.maximum(blk(b, j) * pb - 1, 0), c0))

    def whole(rows, width):
        return pl.BlockSpec((rows, width), lambda b, j: (0, 0))

    in_specs = [cur(C_R, RW_DIM), cur(C_K, RW_DIM), cur(C_V, RW_DIM), cur(C_LORA, LORA_COLS),
                prev(C_R, RW_DIM), prev(C_K, RW_DIM), prev(C_V, RW_DIM), prev(C_LORA, LORA_COLS),
                whole(1, RW_DIM), whole(1, RW_DIM), whole(1, RW_DIM), whole(1, LORA_COLS),
                whole(1, RW_DIM), whole(1, RW_DIM), whole(1, RW_DIM), whole(1, RW_DIM), whole(1, RW_DIM),
                whole(LORA_W, RW_DIM), whole(LORA_A, RW_DIM), whole(GATE_LORA, RW_DIM),
                whole(1, RW_DIM), whole(1, RW_DIM)]
    big = pltpu.VMEM((tb, RW_DIM), BF16)
    per_chunk = pltpu.VMEM((tb // CHUNK, 1, RW_DIM), F32)
    return pl.pallas_call(
        functools.partial(_rwkv_kernel, tb=tb, nb=nb),
        grid=(batch, nb + 1),
        in_specs=in_specs,
        out_specs=pl.BlockSpec((tb, RW_DIM), lambda b, j: (b * nb + jnp.maximum(j - 1, 0), 0)),
        out_shape=jax.ShapeDtypeStruct((m, RW_DIM), BF16),
        scratch_shapes=[big] * 9 + [per_chunk] * 2 + [pltpu.VMEM((N_PAIRS, PAIR, PAIR), F32)],
        compiler_params=_cparams(("parallel", "arbitrary")),
        name="rwkv",
    )(z, z, z, z, z, z, z, z, mu_r, mu_k, mu_v, mu_l, w0, a0, k_k, k_a, r_k, w2p, a2p, g2, gn_w, gn_b)


def _rwkv_chunk_kernel(at_ref, rt_ref, bt_ref, kt_ref, bg_ref, kg_ref, v_ref, g_ref, bonus_ref,
                       gT_ref, gm_ref, gnw_ref, gnb_ref, o_ref, s_ref, *, chunks, pairs, first,
                       filler=None):
    @pl.when(first)
    def _():
        s_ref[...] = jnp.zeros_like(s_ref)

    T = CHUNK
    lane = lax.broadcasted_iota(jnp.int32, (T, PAIR), 1)
    head0 = lane < RW_HEAD
    ri = lax.broadcasted_iota(jnp.int32, (2 * T, 2 * T), 0)
    ci = lax.broadcasted_iota(jnp.int32, (2 * T, 2 * T), 1)
    same = (ri // T) == (ci // T)
    strict = same & ((ri % T) > (ci % T))
    incl = same & ((ri % T) >= (ci % T))
    eye = jnp.where(ri == ci, 1.0, 0.0)
    own = (ri // T) == (ci // RW_HEAD)

    items = [(p, c) for c in range(chunks) for p in range(pairs)]
    rows = lambda c: slice(c * T, (c + 1) * T)
    cols = lambda p: slice(p * PAIR, (p + 1) * PAIR)

    def stack(ref):
        out = []
        for p, c in items:
            x = ref[rows(c), cols(p)]
            zero = jnp.zeros_like(x)
            out.append(jnp.concatenate([jnp.where(head0, x, zero), jnp.where(head0, zero, x)], axis=0))
        return out

    each = lambda f, *ls: [f(*xs) for xs in zip(*ls)]
    La, Lr, Rb, Rk = stack(at_ref), stack(rt_ref), stack(bt_ref), stack(kt_ref)
    Rbg, Rkg, Vs = stack(bg_ref), stack(kg_ref), stack(v_ref)
    gm_row = [gm_ref[c][:, cols(p)] for p, c in items]
    gT_row = [gT_ref[c][:, cols(p)] for p, c in items]
    gate = [g_ref[rows(c), cols(p)] for p, c in items]
    bonus = [bonus_ref[rows(c), cols(p)] for p, c in items]
    filler = iter(()) if filler is None else filler
    tick = lambda: next(filler, None)

    AA = each(lambda la, lr, rb, rk: _mm_nt(jnp.concatenate([la, lr], axis=0),
                                            jnp.concatenate([rb, rk], axis=0)), La, Lr, Rb, Rk)
    tick()
    N = [jnp.where(strict, aa[:2 * T, :2 * T], 0.0) for aa in AA]
    Aak = [jnp.where(strict, aa[:2 * T, 2 * T:], 0.0) for aa in AA]
    Arb = [jnp.where(incl, aa[2 * T:, :2 * T], 0.0) for aa in AA]
    Ark = [jnp.where(incl, aa[2 * T:, 2 * T:], 0.0) for aa in AA]
    W = each(lambda n: eye + n, N)
    P = N
    span = 2
    while span < T:
        P = each(lambda p_: _mm(p_, p_), P)
        tick()
        W = each(lambda w, p_: w + _mm(w, p_), W, P)
        tick()
        span *= 2
    AkV = each(_mm, Aak, Vs)
    X = each(lambda w, la, akv: _mm(w, jnp.concatenate([la, akv.astype(BF16)], axis=1)).astype(BF16),
             W, La, AkV)
    Z = each(lambda arb, ark, x, vs:
             _mm(jnp.concatenate([arb.astype(BF16), ark.astype(BF16)], axis=1),
                 jnp.concatenate([x, jnp.concatenate([jnp.zeros_like(vs), vs], axis=1)], axis=0)),
             Arb, Ark, X, Vs)
    Q = each(lambda z, lr, gm: (z[:, :PAIR] + lr.astype(F32)) * gm, Z, Lr, gm_row)
    Y0 = [z[:, PAIR:] for z in Z]
    Mbd = each(lambda rbg, x, gm: _mm_tn(rbg, x[:, :PAIR]) * gm, Rbg, X, gm_row)
    NcT = each(lambda x, vs, rbg, rkg: _mm_tn(jnp.concatenate([x[:, PAIR:], vs], axis=0),
                                              jnp.concatenate([rbg, rkg], axis=0)), X, Vs, Rbg, Rkg)
    for _ in filler:
        pass

    S = [s_ref[p] for p in range(pairs)]
    for c in range(chunks):
        for p in range(pairs):
            i = c * pairs + p
            Ys = _mm_nt(Q[i], S[p]) + Y0[i]
            S[p] = S[p] * gT_row[i] + _mm_nt(S[p], Mbd[i]) + NcT[i]
            mu = jnp.sum(Ys, axis=-1, keepdims=True) * (1.0 / RW_HEAD)
            d = jnp.where(own, Ys - mu, 0.0)
            var = jnp.sum(d * d, axis=-1, keepdims=True) * (1.0 / RW_HEAD)
            dn = d * lax.rsqrt(var + GN_EPS)
            yn = (dn[:T, :] + dn[T:, :]) * gnw_ref[:, cols(p)] + gnb_ref[:, cols(p)]
            out = (yn + bonus[i].astype(F32)) * gate[i].astype(F32)
            o_ref[rows(c), cols(p)] = out.astype(BF16)
    for p in range(pairs):
        s_ref[p] = S[p]


def _swa_kernel(sink_ref, q_ref, kc_ref, kp_ref, vc_ref, vp_ref, qg_ref, kg_ref, o_ref, *,
                blocks_per_seq, slopes):
    first = (pl.program_id(0) % blocks_per_seq) == 0
    mean_h = _head_ones(LANES, ATT_HEAD, 1.0 / ATT_HEAD)

    def head_rms(x, gain):
        parts = []
        for b in range(x.shape[1] // LANES):
            xb = x[:, b * LANES:(b + 1) * LANES]
            ms = _mm_split_r(xb * xb, mean_h)
            parts.append(xb * lax.rsqrt(ms + RMS_EPS))
        return jnp.concatenate(parts, axis=1) * gain

    q = head_rms(q_ref[...].astype(F32), qg_ref[...]) * (ATT_HEAD ** -0.5 * LOG2E)
    q = q.astype(BF16)
    kcat = jnp.concatenate([kp_ref[...], kc_ref[...]], axis=0).astype(F32)
    kcat = head_rms(kcat, kg_ref[...]).astype(BF16)
    vcat = jnp.concatenate([vp_ref[...], vc_ref[...]], axis=0)

    qi = lax.broadcasted_iota(jnp.int32, (BLOCK, 2 * BLOCK), 0)
    kj = lax.broadcasted_iota(jnp.int32, (BLOCK, 2 * BLOCK), 1)
    dist_i = BLOCK + qi - kj
    first_key = jnp.where(first, BLOCK, 0)
    valid = (dist_i >= 0) & (dist_i < WINDOW) & (kj >= first_key)
    neg_dist = jnp.where(valid, -dist_i.astype(F32), NEG_BIG)

    outs = []
    for h in range(ATT_HEADS):
        j = h // ATT_GROUP
        qh = q[:, h * ATT_HEAD:(h + 1) * ATT_HEAD]
        kh = kcat[:, j * ATT_HEAD:(j + 1) * ATT_HEAD]
        vh = vcat[:, j * ATT_HEAD:(j + 1) * ATT_HEAD]
        s = _mm_nt(qh, kh) + (slopes[h] * LOG2E) * neg_dist
        sink = sink_ref[h] * LOG2E
        mx = jnp.maximum(jnp.max(s, axis=-1, keepdims=True), sink)
        p = jnp.exp2(s - mx)
        denom = jnp.sum(p, axis=-1, keepdims=True) + jnp.exp2(sink - mx)
        outs.append(_mm(p, vh) * (1.0 / denom))
    o_ref[...] = jnp.concatenate(outs, axis=1).astype(BF16)


def _swa(z, sinks, q_gain_t, k_gain_t, seq):
    m = z.shape[0]
    nblk = m // BLOCK
    bps = seq // BLOCK
    slopes = tuple(float(s) for s in
                   np.exp2(-8.0 * np.arange(1, ATT_HEADS + 1, dtype=np.float32) / ATT_HEADS).astype(np.float32))
    kv_cur = lambda c0: pl.BlockSpec((BLOCK, ATT_KV_DIM), lambda n, c=c0 // ATT_KV_DIM: (n, c))
    kv_prev = lambda c0: pl.BlockSpec((BLOCK, ATT_KV_DIM),
                                      lambda n, c=c0 // ATT_KV_DIM: (jnp.maximum(n - 1, 0), c))
    return pl.pallas_call(
        functools.partial(_swa_kernel, blocks_per_seq=bps, slopes=slopes),
        grid=(nblk,),
        in_specs=[pl.BlockSpec(memory_space=pltpu.SMEM),
                  pl.BlockSpec((BLOCK, ATT_Q_DIM), lambda n: (n, C_Q // ATT_Q_DIM)),
                  kv_cur(C_KA), kv_prev(C_KA), kv_cur(C_VA), kv_prev(C_VA),
                  pl.BlockSpec((1, ATT_Q_DIM), lambda n: (0, 0)),
                  pl.BlockSpec((1, ATT_KV_DIM), lambda n: (0, 0))],
        out_specs=pl.BlockSpec((BLOCK, ATT_Q_DIM), lambda n: (n, 0)),
        out_shape=jax.ShapeDtypeStruct((m, ATT_Q_DIM), BF16),
        compiler_params=_cparams(("parallel",)),
        name="swa",
    )(sinks, z, z, z, z, z, q_gain_t, k_gain_t)


def _mix_out_kernel(orw_ref, oatt_ref, wb1_ref, wb2_ref, zg_ref, wo_ref, x_ref, g_ref, x1_ref, h2_ref):
    p_rw = jnp.dot(orw_ref[...], wb1_ref[...], preferred_element_type=F32)
    p_att = jnp.dot(oatt_ref[...], wb2_ref[...], preferred_element_type=F32)
    g_rw = jax.nn.sigmoid(zg_ref[:, :D_MODEL].astype(F32))
    g_att = jax.nn.sigmoid(zg_ref[:, D_MODEL:].astype(F32))
    mix = (g_rw * p_rw + g_att * p_att).astype(BF16)
    x1 = x_ref[...] + jnp.dot(mix, wo_ref[...], preferred_element_type=F32)
    x1_ref[...] = x1
    y = x1 * lax.rsqrt(jnp.mean(x1 * x1, axis=-1, keepdims=True) + RMS_EPS)
    h2_ref[...] = (y * g_ref[...]).astype(BF16)


def _mix_out(o_rw, o_att, wb1, wb2, z, w_out, x2, g2, tm=256):
    m = x2.shape[0]
    const = lambda rows, cols: pl.BlockSpec((rows, cols), lambda i: (0, 0))
    rows = lambda cols, c0=0: pl.BlockSpec((tm, cols), lambda i: (i, c0))
    return pl.pallas_call(
        _mix_out_kernel,
        grid=(m // tm,),
        in_specs=[rows(RW_DIM), rows(ATT_Q_DIM), const(RW_DIM, D_MODEL), const(ATT_Q_DIM, D_MODEL),
                  rows(2 * D_MODEL, C_GRW // (2 * D_MODEL)), const(D_MODEL, D_MODEL),
                  rows(D_MODEL), const(1, D_MODEL)],
        out_specs=[rows(D_MODEL), rows(D_MODEL)],
        out_shape=[jax.ShapeDtypeStruct((m, D_MODEL), F32),
                   jax.ShapeDtypeStruct((m, D_MODEL), BF16)],
        compiler_params=_cparams(("parallel",)),
        name="mix_out",
    )(o_rw, o_att, wb1, wb2, z, w_out, x2, g2)


def _ffn_up_kernel(h_ref, wv_ref, wg_ref, cwv_ref, cwg_ref, cbv_ref, cbg_ref, a_ref,
                   wvb_ref, wgb_ref, uv_ref, ug_ref, *, tm, sub, tiles_per_seq):
    halo = 8

    @pl.when(pl.program_id(1) == 0)
    def _():
        wvb_ref[...] = wv_ref[...].astype(BF16)
        wgb_ref[...] = wg_ref[...].astype(BF16)

    @pl.when((pl.program_id(1) % tiles_per_seq) == 0)
    def _():
        uv_ref[0:halo, :] = jnp.zeros((halo, uv_ref.shape[1]), F32)
        ug_ref[0:halo, :] = jnp.zeros((halo, ug_ref.shape[1]), F32)

    def conv(u_ref, cw_ref, cb_ref, r0):
        return (u_ref[halo - 2 + r0:halo - 2 + r0 + sub, :] * cw_ref[0:1, :]
                + u_ref[halo - 1 + r0:halo - 1 + r0 + sub, :] * cw_ref[1:2, :]
                + u_ref[halo + r0:halo + r0 + sub, :] * cw_ref[2:3, :] + cb_ref[...])

    for s in range(tm // sub):
        r0 = s * sub
        h = h_ref[r0:r0 + sub, :]
        uv_ref[halo + r0:halo + r0 + sub, :] = jnp.dot(h, wvb_ref[...], preferred_element_type=F32)
        ug_ref[halo + r0:halo + r0 + sub, :] = jnp.dot(h, wgb_ref[...], preferred_element_type=F32)
        val = conv(uv_ref, cwv_ref, cbv_ref, r0)
        gate = conv(ug_ref, cwg_ref, cbg_ref, r0)
        a_ref[r0:r0 + sub, :] = (gate * jax.nn.sigmoid(gate) * val).astype(BF16)
    uv_ref[0:halo, :] = uv_ref[tm:tm + halo, :]
    ug_ref[0:halo, :] = ug_ref[tm:tm + halo, :]


def _ffn_up(h2, w_up, conv_w, conv_b, seq, tm=1024, tn=512, sub=128):
    m = h2.shape[0]
    nj = D_FF // tn
    return pl.pallas_call(
        functools.partial(_ffn_up_kernel, tm=tm, sub=sub, tiles_per_seq=seq // tm),
        grid=(nj, m // tm),
        in_specs=[pl.BlockSpec((tm, D_MODEL), lambda j, i: (i, 0)),
                  pl.BlockSpec((D_MODEL, tn), lambda j, i: (0, j)),
                  pl.BlockSpec((D_MODEL, tn), lambda j, i: (0, nj + j)),
                  pl.BlockSpec((3, tn), lambda j, i: (0, j)),
                  pl.BlockSpec((3, tn), lambda j, i: (0, nj + j)),
                  pl.BlockSpec((1, tn), lambda j, i: (0, j)),
                  pl.BlockSpec((1, tn), lambda j, i: (0, nj + j))],
        out_specs=pl.BlockSpec((tm, tn), lambda j, i: (i, j)),
        out_shape=jax.ShapeDtypeStruct((m, D_FF), BF16),
        scratch_shapes=[pltpu.VMEM((D_MODEL, tn), BF16), pltpu.VMEM((D_MODEL, tn), BF16),
                        pltpu.VMEM((tm + 8, tn), F32), pltpu.VMEM((tm + 8, tn), F32)],
        compiler_params=_cparams(("parallel", "arbitrary")),
        name="ffn_up",
    )(h2, w_up, w_up, conv_w, conv_w, conv_b, conv_b)


def _ffn_down_kernel(a_ref, w_ref, x1_ref, o_ref, wb_ref):
    @pl.when(pl.program_id(1) == 0)
    def _():
        wb_ref[...] = w_ref[...].astype(BF16)

    o_ref[...] = x1_ref[...] + jnp.dot(a_ref[...], wb_ref[...], preferred_element_type=F32)


def _ffn_down(act, w_down, x1, tm=512, tn=512):
    m = act.shape[0]
    return pl.pallas_call(
        _ffn_down_kernel,
        grid=(D_MODEL // tn, m // tm),
        in_specs=[pl.BlockSpec((tm, D_FF), lambda j, i: (i, 0)),
                  pl.BlockSpec((D_FF, tn), lambda j, i: (0, j)),
                  pl.BlockSpec((tm, tn), lambda j, i: (i, j))],
        out_specs=pl.BlockSpec((tm, tn), lambda j, i: (i, j)),
        out_shape=jax.ShapeDtypeStruct((m, D_MODEL), F32),
        scratch_shapes=[pltpu.VMEM((D_FF, tn), BF16)],
        compiler_params=_cparams(("parallel", "arbitrary")),
        name="ffn_down",
    )(act, w_down, x1)


def _pad_cols(w, n):
    return jnp.pad(w, ((0, 0), (0, n - w.shape[1])))


def _pad_rows(w, n):
    return jnp.pad(w, ((0, n - w.shape[0]), (0, 0)))


def _layer(x2, batch, seq, norm1_g, w_in, rw_mu, rw_w0, rw_w2, rw_a0, rw_a2, rw_g2, rw_k_k, rw_k_a,
           rw_r_k, rw_gn_w, rw_gn_b, q_norm_g, k_norm_g, attn_sinks, w_branch, w_out,
           norm2_g, w_up, conv_w, conv_b, w_down):
    row = lambda v: v.reshape(1, -1).astype(F32)
    o3 = 3 * RW_DIM
    o4 = o3 + DECAY_LORA
    o5 = o4 + ICLR_LORA
    o6 = o5 + GATE_LORA
    w_in_p = _w_in_layout(w_in)
    mu = rw_mu.reshape(1, -1)
    mu_l = jnp.concatenate([_pad_cols(mu[:, o3:o4], LORA_W), _pad_cols(mu[:, o4:o5], LORA_A),
                            mu[:, o5:o6]], axis=1)
    w2p = _pad_rows(rw_w2, LORA_W).astype(BF16)
    a2p = _pad_rows(rw_a2, LORA_A).astype(BF16)

    z = _in_proj(x2, row(norm1_g), w_in_p)
    o_rw = _rwkv(z, batch, seq, mu[:, :RW_DIM], mu[:, RW_DIM:2 * RW_DIM], mu[:, 2 * RW_DIM:o3], mu_l,
                 row(rw_w0), row(rw_a0), row(rw_k_k), row(rw_k_a), row(rw_r_k),
                 w2p, a2p, rw_g2.astype(BF16), row(rw_gn_w), row(rw_gn_b))
    o_att = _swa(z, attn_sinks.astype(F32), jnp.tile(row(q_norm_g), (1, ATT_HEADS)),
                 jnp.tile(row(k_norm_g), (1, ATT_KV_HEADS)), seq)
    x1, h2 = _mix_out(o_rw, o_att, w_branch[:RW_DIM].astype(BF16), w_branch[RW_DIM:].astype(BF16), z,
                      w_out.astype(BF16), x2, row(norm2_g))
    act = _ffn_up(h2, w_up, conv_w.astype(F32), row(conv_b), seq)
    return _ffn_down(act, w_down, x1)


def kernel(x, norm1_g, w_in, rw_mu, rw_w0, rw_w2, rw_a0, rw_a2, rw_g2, rw_k_k, rw_k_a, rw_r_k,
           rw_gn_w, rw_gn_b, q_norm_g, k_norm_g, attn_sinks, w_branch, w_out, norm2_g, w_up,
           conv_w, conv_b, w_down):
    batch, seq, d = x.shape
    x2 = x.reshape(batch * seq, d)
    params = (norm1_g, w_in, rw_mu, rw_w0, rw_w2, rw_a0, rw_a2, rw_g2, rw_k_k, rw_k_a, rw_r_k,
              rw_gn_w, rw_gn_b, q_norm_g, k_norm_g, attn_sinks, w_branch, w_out, norm2_g, w_up,
              conv_w, conv_b, w_down)
    for layer in range(norm1_g.shape[0]):
        x2 = _layer(x2, batch, seq, *(p[layer] for p in params))
    return x2.reshape(batch, seq, d)
```

```python
import functools

import jax
import jax.numpy as jnp
import numpy as np
from jax import lax
from jax.experimental import pallas as pl
from jax.experimental.pallas import tpu as pltpu

F32 = jnp.float32
BF16 = jnp.bfloat16

LANES = 128
BF16_SUBLANES = 16
VMEM_LIMIT = 56 * 1024 * 1024

D_MODEL = 2048
RW_HEADS = 16
RW_HEAD = 64
RW_DIM = RW_HEADS * RW_HEAD
DECAY_LORA = 96
ICLR_LORA = 96
GATE_LORA = 256
GN_EPS = 64e-5
ATT_HEADS = 16
ATT_KV_HEADS = 4
ATT_GROUP = ATT_HEADS // ATT_KV_HEADS
ATT_HEAD = 64
ATT_Q_DIM = ATT_HEADS * ATT_HEAD
ATT_KV_DIM = ATT_KV_HEADS * ATT_HEAD
WINDOW = 128
BLOCK = 128
D_FF = 5632
RMS_EPS = 1e-6
NEG_BIG = -1e30
LOG2E = 1.4426950408889634

C_R = 0
C_K = RW_DIM
C_V = 2 * RW_DIM
C_Q = 3 * RW_DIM
C_GRW = C_Q + ATT_Q_DIM
C_GATT = C_GRW + D_MODEL
C_LORA = C_GATT + D_MODEL
LORA_W = 128
LORA_A = 128
LORA_COLS = LORA_W + LORA_A + GATE_LORA
C_KA = C_LORA + LORA_COLS
C_VA = C_KA + ATT_KV_DIM
Z_COLS = C_VA + ATT_KV_DIM

CHUNK = 64
PAIR = 2 * RW_HEAD
N_PAIRS = RW_DIM // PAIR


def _cparams(sem):
    return pltpu.CompilerParams(dimension_semantics=sem, vmem_limit_bytes=VMEM_LIMIT)


def _mm(a, b):
    return jnp.dot(a.astype(BF16), b.astype(BF16), preferred_element_type=F32)


def _mm_nt(a, b):
    return lax.dot_general(a.astype(BF16), b.astype(BF16), (((1,), (1,)), ((), ())),
                           preferred_element_type=F32)


def _mm_tn(a, b):
    return lax.dot_general(a.astype(BF16), b.astype(BF16), (((0,), (0,)), ((), ())),
                           preferred_element_type=F32)


def _mm_split(m01, x):
    hi = x.astype(BF16)
    lo = (x - hi.astype(F32)).astype(BF16)
    return (jnp.dot(m01, hi, preferred_element_type=F32)
            + jnp.dot(m01, lo, preferred_element_type=F32))


def _mm_split_r(x, m01):
    hi = x.astype(BF16)
    lo = (x - hi.astype(F32)).astype(BF16)
    return (jnp.dot(hi, m01, preferred_element_type=F32)
            + jnp.dot(lo, m01, preferred_element_type=F32))


def _head_ones(n, head, scale):
    r = lax.broadcasted_iota(jnp.int32, (n, n), 0) // head
    c = lax.broadcasted_iota(jnp.int32, (n, n), 1) // head
    return jnp.where(r == c, scale, 0.0).astype(BF16)


def _w_in_layout_kernel(wt_ref, o_ref, *, moves, pad):
    o_ref[pad[0]:pad[1], :] = jnp.zeros((pad[1] - pad[0], o_ref.shape[1]), BF16)
    for dst, src, width in moves:
        o_ref[dst:dst + width, :] = wt_ref[src:src + width, :].astype(BF16)


def _w_in_layout(w_in_layers, layer, cols=256):
    _, d, n = w_in_layers.shape
    wt_layers = jnp.swapaxes(w_in_layers, 1, 2)
    o3 = 3 * RW_DIM
    o4 = o3 + DECAY_LORA
    o5 = o4 + ICLR_LORA
    o6 = o5 + GATE_LORA
    oq = o6 + ATT_Q_DIM
    ok = oq + ATT_KV_DIM
    ov = ok + ATT_KV_DIM
    moves = ((C_R, 0, o3), (C_Q, o6, ATT_Q_DIM), (C_GRW, ov, 2 * D_MODEL),
             (C_LORA, o3, DECAY_LORA), (C_LORA + LORA_W, o4, ICLR_LORA),
             (C_LORA + LORA_W + LORA_A, o5, GATE_LORA), (C_KA, oq, ATT_KV_DIM), (C_VA, ok, ATT_KV_DIM))
    return pl.pallas_call(
        functools.partial(_w_in_layout_kernel, moves=moves, pad=(C_LORA, C_LORA + LORA_W + LORA_A)),
        grid=(d // cols,),
        in_specs=[pl.BlockSpec((None, n, cols), lambda i: (layer, 0, i))],
        out_specs=pl.BlockSpec((Z_COLS, cols), lambda i: (0, i)),
        out_shape=jax.ShapeDtypeStruct((Z_COLS, d), BF16),
        compiler_params=_cparams(("parallel",)),
        name="w_in_layout",
    )(wt_layers)


def _in_proj_kernel(x_ref, g_ref, wt_ref, z_ref, h_ref):
    @pl.when(pl.program_id(1) == 0)
    def _():
        xf = x_ref[...]
        y = xf * lax.rsqrt(jnp.mean(xf * xf, axis=-1, keepdims=True) + RMS_EPS)
        h_ref[...] = (y * g_ref[...]).astype(BF16)

    z_ref[...] = _mm_nt(h_ref[...], wt_ref[...]).astype(BF16)


def _in_proj(x2, g1, w_in_t, tm=1024, tn=1024):
    m, d = x2.shape
    n = w_in_t.shape[0]
    return pl.pallas_call(
        _in_proj_kernel,
        grid=(m // tm, n // tn),
        in_specs=[pl.BlockSpec((tm, d), lambda i, j: (i, 0)),
                  pl.BlockSpec((1, d), lambda i, j: (0, 0)),
                  pl.BlockSpec((tn, d), lambda i, j: (j, 0))],
        out_specs=pl.BlockSpec((tm, tn), lambda i, j: (i, j)),
        out_shape=jax.ShapeDtypeStruct((m, n), BF16),
        scratch_shapes=[pltpu.VMEM((tm, d), BF16)],
        compiler_params=_cparams(("parallel", "arbitrary")),
        name="in_proj",
    )(x2, g1, w_in_t)


def _rwkv_prep_kernel(zr_ref, zk_ref, zv_ref, zl_ref, pr_ref, pk_ref, pv_ref, pli_ref,
                      mur_ref, muk_ref, muv_ref, mul_ref,
                      w0_ref, a0_ref, kk_ref, ka_ref, rk_ref, w2_ref, a2_ref, g2_ref,
                      at_ref, rt_ref, bt_ref, kt_ref, bg_ref, kg_ref, v_ref, g_ref, bonus_ref,
                      gT_ref, gm_ref, *, tm, first):
    keep = jnp.where(first, 0.0, 1.0)
    nc = tm // CHUNK
    last = slice(BF16_SUBLANES - 1, BF16_SUBLANES)

    def shifted(z, prev_last, mu):
        z = z.astype(F32)
        row = lax.broadcasted_iota(jnp.int32, z.shape, 0)
        zprev = jnp.where(row == 0, prev_last.astype(F32) * keep, pltpu.roll(z, 1, 0))
        return z + (zprev - z) * mu

    lo = shifted(zl_ref[...], pli_ref[last, :], mul_ref[...])
    tanh_wd = jnp.tanh(lo[:, :LORA_W]).astype(BF16)
    ad = lo[:, LORA_W:LORA_W + LORA_A].astype(BF16)
    sig_gd = jax.nn.sigmoid(lo[:, LORA_W + LORA_A:]).astype(BF16)

    ti = lax.broadcasted_iota(jnp.int32, (tm, tm), 0)
    si = lax.broadcasted_iota(jnp.int32, (tm, tm), 1)
    tri = jnp.where(((ti // CHUNK) == (si // CHUNK)) & (si <= ti), 1.0, 0.0).astype(BF16)
    ones_h = _head_ones(PAIR, RW_HEAD, 1.0)
    yield

    for p in range(N_PAIRS):
        cs = slice(p * PAIR, (p + 1) * PAIR)
        r = shifted(zr_ref[:, cs], pr_ref[last, cs], mur_ref[:, cs])
        k = shifted(zk_ref[:, cs], pk_ref[last, cs], muk_ref[:, cs])
        v = shifted(zv_ref[:, cs], pv_ref[last, cs], muv_ref[:, cs])

        wpre = w0_ref[:, cs] + jnp.dot(tanh_wd, w2_ref[:, cs], preferred_element_type=F32)
        w = jnp.minimum(wpre, 0.0) - jnp.log1p(jnp.exp(-jnp.abs(wpre))) - 0.5
        lw = -jnp.exp(w)
        a = jax.nn.sigmoid(a0_ref[:, cs] + jnp.dot(ad, a2_ref[:, cs], preferred_element_type=F32))
        g = jnp.dot(sig_gd, g2_ref[:, cs], preferred_element_type=F32)

        kk = k * kk_ref[:, cs]
        kk = kk * lax.rsqrt(jnp.maximum(_mm(kk * kk, ones_h), 1e-24))
        k2 = k * (1.0 + (a - 1.0) * ka_ref[:, cs])
        bonus = _mm(r * k2 * rk_ref[:, cs], ones_h) * v
        ka_vec = -kk
        kb_vec = kk * a

        c = _mm_split(tri, lw)
        c3 = c.reshape(nc, CHUNK, PAIR)
        c_mid = c3[:, CHUNK // 2 - 1:CHUNK // 2, :]
        c_end = c3[:, CHUNK - 1:CHUNK, :]
        cm = jnp.broadcast_to(c_mid, c3.shape).reshape(tm, PAIR)
        cT = jnp.broadcast_to(c_end, c3.shape).reshape(tm, PAIR)

        e_in = jnp.exp(c - cm)
        e_out = jnp.exp(cm - c)
        e_end = jnp.exp(cT - c)
        at_ref[:, cs] = (ka_vec * jnp.exp(c - lw - cm)).astype(BF16)
        rt_ref[:, cs] = (r * e_in).astype(BF16)
        bt_ref[:, cs] = (kb_vec * e_out).astype(BF16)
        kt_ref[:, cs] = (k2 * e_out).astype(BF16)
        bg_ref[:, cs] = (kb_vec * e_end).astype(BF16)
        kg_ref[:, cs] = (k2 * e_end).astype(BF16)
        v_ref[:, cs] = v.astype(BF16)
        g_ref[:, cs] = g.astype(BF16)
        bonus_ref[:, cs] = bonus.astype(BF16)
        gT_ref[:, :, cs] = jnp.exp(c_end)
        gm_ref[:, :, cs] = jnp.exp(c_mid)
        yield


def _rwkv_kernel(*refs, tb, nb):
    prep_in, (gnw_ref, gnb_ref, o_ref), scratch = refs[:20], refs[20:23], refs[23:]
    operands, s_ref = scratch[:11], scratch[11]
    j = pl.program_id(1)
    prep = functools.partial(_rwkv_prep_kernel, *prep_in, *operands, tm=tb, first=j == 0)
    chunk = functools.partial(_rwkv_chunk_kernel, *operands, gnw_ref, gnb_ref, o_ref, s_ref,
                              chunks=tb // CHUNK, pairs=N_PAIRS, first=j == 1)

    @pl.when(j == 0)
    def _():
        for _ in prep():
            pass

    @pl.when((j > 0) & (j < nb))
    def _():
        chunk(filler=prep())

    @pl.when(j == nb)
    def _():
        chunk()


def _rwkv(z, batch, seq, mu_r, mu_k, mu_v, mu_l, w0, a0, k_k, k_a, r_k, w2p, a2p, g2, gn_w, gn_b, tb=128):
    m = z.shape[0]
    nb = seq // tb
    pb = tb // BF16_SUBLANES
    blk = lambda b, j: b * nb + jnp.minimum(j, nb - 1)

    def cur(col0, width):
        return pl.BlockSpec((tb, width), lambda b, j, c0=col0 // width: (blk(b, j), c0))

    def prev(col0, width):
        return pl.BlockSpec((BF16_SUBLANES, width),
                            lambda b, j, c0=col0 // width: (jnp.maximum(blk(b, j) * pb - 1, 0), c0))

    def whole(rows, width):
        return pl.BlockSpec((rows, width), lambda b, j: (0, 0))

    in_specs = [cur(C_R, RW_DIM), cur(C_K, RW_DIM), cur(C_V, RW_DIM), cur(C_LORA, LORA_COLS),
                prev(C_R, RW_DIM), prev(C_K, RW_DIM), prev(C_V, RW_DIM), prev(C_LORA, LORA_COLS),
                whole(1, RW_DIM), whole(1, RW_DIM), whole(1, RW_DIM), whole(1, LORA_COLS),
                whole(1, RW_DIM), whole(1, RW_DIM), whole(1, RW_DIM), whole(1, RW_DIM), whole(1, RW_DIM),
                whole(LORA_W, RW_DIM), whole(LORA_A, RW_DIM), whole(GATE_LORA, RW_DIM),
                whole(1, RW_DIM), whole(1, RW_DIM)]
    big = pltpu.VMEM((tb, RW_DIM), BF16)
    per_chunk = pltpu.VMEM((tb // CHUNK, 1, RW_DIM), F32)
    return pl.pallas_call(
        functools.partial(_rwkv_kernel, tb=tb, nb=nb),
        grid=(batch, nb + 1),
        in_specs=in_specs,
        out_specs=pl.BlockSpec((tb, RW_DIM), lambda b, j: (b * nb + jnp.maximum(j - 1, 0), 0)),
        out_shape=jax.ShapeDtypeStruct((m, RW_DIM), BF16),
        scratch_shapes=[big] * 9 + [per_chunk] * 2 + [pltpu.VMEM((N_PAIRS, PAIR, PAIR), F32)],
        compiler_params=_cparams(("parallel", "arbitrary")),
        name="rwkv",
    )(z, z, z, z, z, z, z, z, mu_r, mu_k, mu_v, mu_l, w0, a0, k_k, k_a, r_k, w2p, a2p, g2, gn_w, gn_b)


def _rwkv_chunk_kernel(at_ref, rt_ref, bt_ref, kt_ref, bg_ref, kg_ref, v_ref, g_ref, bonus_ref,
                       gT_ref, gm_ref, gnw_ref, gnb_ref, o_ref, s_ref, *, chunks, pairs, first,
                       filler=None):
    @pl.when(first)
    def _():
        s_ref[...] = jnp.zeros_like(s_ref)

    T = CHUNK
    lane = lax.broadcasted_iota(jnp.int32, (T, PAIR), 1)
    head0 = lane < RW_HEAD
    ri = lax.broadcasted_iota(jnp.int32, (2 * T, 2 * T), 0)
    ci = lax.broadcasted_iota(jnp.int32, (2 * T, 2 * T), 1)
    same = (ri // T) == (ci // T)
    strict = same & ((ri % T) > (ci % T))
    incl = same & ((ri % T) >= (ci % T))
    eye = jnp.where(ri == ci, 1.0, 0.0)
    own = (ri // T) == (ci // RW_HEAD)

    items = [(p, c) for c in range(chunks) for p in range(pairs)]
    rows = lambda c: slice(c * T, (c + 1) * T)
    cols = lambda p: slice(p * PAIR, (p + 1) * PAIR)

    def stack(ref):
        out = []
        for p, c in items:
            x = ref[rows(c), cols(p)]
            zero = jnp.zeros_like(x)
            out.append(jnp.concatenate([jnp.where(head0, x, zero), jnp.where(head0, zero, x)], axis=0))
        return out

    each = lambda f, *ls: [f(*xs) for xs in zip(*ls)]
    La, Lr, Rb, Rk = stack(at_ref), stack(rt_ref), stack(bt_ref), stack(kt_ref)
    Rbg, Rkg, Vs = stack(bg_ref), stack(kg_ref), stack(v_ref)
    gm_row = [gm_ref[c][:, cols(p)] for p, c in items]
    gT_row = [gT_ref[c][:, cols(p)] for p, c in items]
    gate = [g_ref[rows(c), cols(p)] for p, c in items]
    bonus = [bonus_ref[rows(c), cols(p)] for p, c in items]
    filler = iter(()) if filler is None else filler
    tick = lambda: next(filler, None)

    AA = each(lambda la, lr, rb, rk: _mm_nt(jnp.concatenate([la, lr], axis=0),
                                            jnp.concatenate([rb, rk], axis=0)), La, Lr, Rb, Rk)
    tick()
    N = [jnp.where(strict, aa[:2 * T, :2 * T], 0.0) for aa in AA]
    Aak = [jnp.where(strict, aa[:2 * T, 2 * T:], 0.0) for aa in AA]
    Arb = [jnp.where(incl, aa[2 * T:, :2 * T], 0.0) for aa in AA]
    Ark = [jnp.where(incl, aa[2 * T:, 2 * T:], 0.0) for aa in AA]
    W = each(lambda n: eye + n, N)
    P = N
    span = 2
    while span < T:
        P = each(lambda p_: _mm(p_, p_), P)
        tick()
        W = each(lambda w, p_: w + _mm(w, p_), W, P)
        tick()
        span *= 2
    AkV = each(_mm, Aak, Vs)
    X = each(lambda w, la, akv: _mm(w, jnp.concatenate([la, akv.astype(BF16)], axis=1)).astype(BF16),
             W, La, AkV)
    Z = each(lambda arb, ark, x, vs:
             _mm(jnp.concatenate([arb.astype(BF16), ark.astype(BF16)], axis=1),
                 jnp.concatenate([x, jnp.concatenate([jnp.zeros_like(vs), vs], axis=1)], axis=0)),
             Arb, Ark, X, Vs)
    Q = each(lambda z, lr, gm: (z[:, :PAIR] + lr.astype(F32)) * gm, Z, Lr, gm_row)
    Y0 = [z[:, PAIR:] for z in Z]
    Mbd = each(lambda rbg, x, gm: _mm_tn(rbg, x[:, :PAIR]) * gm, Rbg, X, gm_row)
    NcT = each(lambda x, vs, rbg, rkg: _mm_tn(jnp.concatenate([x[:, PAIR:], vs], axis=0),
                                              jnp.concatenate([rbg, rkg], axis=0)), X, Vs, Rbg, Rkg)
    for _ in filler:
        pass

    S = [s_ref[p] for p in range(pairs)]
    for c in range(chunks):
        for p in range(pairs):
            i = c * pairs + p
            Ys = _mm_nt(Q[i], S[p]) + Y0[i]
            S[p] = S[p] * gT_row[i] + _mm_nt(S[p], Mbd[i]) + NcT[i]
            mu = jnp.sum(Ys, axis=-1, keepdims=True) * (1.0 / RW_HEAD)
            d = jnp.where(own, Ys - mu, 0.0)
            var = jnp.sum(d * d, axis=-1, keepdims=True) * (1.0 / RW_HEAD)
            dn = d * lax.rsqrt(var + GN_EPS)
            yn = (dn[:T, :] + dn[T:, :]) * gnw_ref[:, cols(p)] + gnb_ref[:, cols(p)]
            out = (yn + bonus[i].astype(F32)) * gate[i].astype(F32)
            o_ref[rows(c), cols(p)] = out.astype(BF16)
    for p in range(pairs):
        s_ref[p] = S[p]


def _swa_kernel(sink_ref, q_ref, kc_ref, kp_ref, vc_ref, vp_ref, qg_ref, kg_ref, o_ref, *,
                blocks_per_seq, slopes):
    first = (pl.program_id(0) % blocks_per_seq) == 0
    mean_h = _head_ones(LANES, ATT_HEAD, 1.0 / ATT_HEAD)

    def head_rms(x, gain):
        parts = []
        for b in range(x.shape[1] // LANES):
            xb = x[:, b * LANES:(b + 1) * LANES]
            ms = _mm_split_r(xb * xb, mean_h)
            parts.append(xb * lax.rsqrt(ms + RMS_EPS))
        return jnp.concatenate(parts, axis=1) * gain

    q = head_rms(q_ref[...].astype(F32), qg_ref[...]) * (ATT_HEAD ** -0.5 * LOG2E)
    q = q.astype(BF16)
    kcat = jnp.concatenate([kp_ref[...], kc_ref[...]], axis=0).astype(F32)
    kcat = head_rms(kcat, kg_ref[...]).astype(BF16)
    vcat = jnp.concatenate([vp_ref[...], vc_ref[...]], axis=0)

    qi = lax.broadcasted_iota(jnp.int32, (BLOCK, 2 * BLOCK), 0)
    kj = lax.broadcasted_iota(jnp.int32, (BLOCK, 2 * BLOCK), 1)
    dist_i = BLOCK + qi - kj
    first_key = jnp.where(first, BLOCK, 0)
    valid = (dist_i >= 0) & (dist_i < WINDOW) & (kj >= first_key)
    neg_dist = jnp.where(valid, -dist_i.astype(F32), NEG_BIG)

    outs = []
    for h in range(ATT_HEADS):
        j = h // ATT_GROUP
        qh = q[:, h * ATT_HEAD:(h + 1) * ATT_HEAD]
        kh = kcat[:, j * ATT_HEAD:(j + 1) * ATT_HEAD]
        vh = vcat[:, j * ATT_HEAD:(j + 1) * ATT_HEAD]
        s = _mm_nt(qh, kh) + (slopes[h] * LOG2E) * neg_dist
        sink = sink_ref[h] * LOG2E
        mx = jnp.maximum(jnp.max(s, axis=-1, keepdims=True), sink)
        p = jnp.exp2(s - mx)
        denom = jnp.sum(p, axis=-1, keepdims=True) + jnp.exp2(sink - mx)
        outs.append(_mm(p, vh) * (1.0 / denom))
    o_ref[...] = jnp.concatenate(outs, axis=1).astype(BF16)


def _swa(z, sinks, q_gain_t, k_gain_t, seq):
    m = z.shape[0]
    nblk = m // BLOCK
    bps = seq // BLOCK
    slopes = tuple(float(s) for s in
                   np.exp2(-8.0 * np.arange(1, ATT_HEADS + 1, dtype=np.float32) / ATT_HEADS).astype(np.float32))
    kv_cur = lambda c0: pl.BlockSpec((BLOCK, ATT_KV_DIM), lambda n, c=c0 // ATT_KV_DIM: (n, c))
    kv_prev = lambda c0: pl.BlockSpec((BLOCK, ATT_KV_DIM),
                                      lambda n, c=c0 // ATT_KV_DIM: (jnp.maximum(n - 1, 0), c))
    return pl.pallas_call(
        functools.partial(_swa_kernel, blocks_per_seq=bps, slopes=slopes),
        grid=(nblk,),
        in_specs=[pl.BlockSpec(memory_space=pltpu.SMEM),
                  pl.BlockSpec((BLOCK, ATT_Q_DIM), lambda n: (n, C_Q // ATT_Q_DIM)),
                  kv_cur(C_KA), kv_prev(C_KA), kv_cur(C_VA), kv_prev(C_VA),
                  pl.BlockSpec((1, ATT_Q_DIM), lambda n: (0, 0)),
                  pl.BlockSpec((1, ATT_KV_DIM), lambda n: (0, 0))],
        out_specs=pl.BlockSpec((BLOCK, ATT_Q_DIM), lambda n: (n, 0)),
        out_shape=jax.ShapeDtypeStruct((m, ATT_Q_DIM), BF16),
        compiler_params=_cparams(("parallel",)),
        name="swa",
    )(sinks, z, z, z, z, z, q_gain_t, k_gain_t)


def _mix_out_kernel(orw_ref, oatt_ref, wb1_ref, wb2_ref, zg_ref, wo_ref, x_ref, g_ref, x1_ref, h2_ref):
    p_rw = jnp.dot(orw_ref[...], wb1_ref[...], preferred_element_type=F32)
    p_att = jnp.dot(oatt_ref[...], wb2_ref[...], preferred_element_type=F32)
    g_rw = jax.nn.sigmoid(zg_ref[:, :D_MODEL].astype(F32))
    g_att = jax.nn.sigmoid(zg_ref[:, D_MODEL:].astype(F32))
    mix = (g_rw * p_rw + g_att * p_att).astype(BF16)
    x1 = x_ref[...] + jnp.dot(mix, wo_ref[...], preferred_element_type=F32)
    x1_ref[...] = x1
    y = x1 * lax.rsqrt(jnp.mean(x1 * x1, axis=-1, keepdims=True) + RMS_EPS)
    h2_ref[...] = (y * g_ref[...]).astype(BF16)


def _mix_out(o_rw, o_att, wb1, wb2, z, w_out, x2, g2, tm=256):
    m = x2.shape[0]
    const = lambda rows, cols: pl.BlockSpec((rows, cols), lambda i: (0, 0))
    rows = lambda cols, c0=0: pl.BlockSpec((tm, cols), lambda i: (i, c0))
    return pl.pallas_call(
        _mix_out_kernel,
        grid=(m // tm,),
        in_specs=[rows(RW_DIM), rows(ATT_Q_DIM), const(RW_DIM, D_MODEL), const(ATT_Q_DIM, D_MODEL),
                  rows(2 * D_MODEL, C_GRW // (2 * D_MODEL)), const(D_MODEL, D_MODEL),
                  rows(D_MODEL), const(1, D_MODEL)],
        out_specs=[rows(D_MODEL), rows(D_MODEL)],
        out_shape=[jax.ShapeDtypeStruct((m, D_MODEL), F32),
                   jax.ShapeDtypeStruct((m, D_MODEL), BF16)],
        compiler_params=_cparams(("parallel",)),
        name="mix_out",
    )(o_rw, o_att, wb1, wb2, z, w_out, x2, g2)


def _ffn_up_kernel(h_ref, wv_ref, wg_ref, cwv_ref, cwg_ref, cbv_ref, cbg_ref, a_ref,
                   wvb_ref, wgb_ref, uv_ref, ug_ref, *, tm, sub, tiles_per_seq):
    halo = 8

    @pl.when(pl.program_id(1) == 0)
    def _():
        wvb_ref[...] = wv_ref[...].astype(BF16)
        wgb_ref[...] = wg_ref[...].astype(BF16)

    @pl.when((pl.program_id(1) % tiles_per_seq) == 0)
    def _():
        uv_ref[0:halo, :] = jnp.zeros((halo, uv_ref.shape[1]), F32)
        ug_ref[0:halo, :] = jnp.zeros((halo, ug_ref.shape[1]), F32)

    def conv(u_ref, cw_ref, cb_ref, r0):
        return (u_ref[halo - 2 + r0:halo - 2 + r0 + sub, :] * cw_ref[0:1, :]
                + u_ref[halo - 1 + r0:halo - 1 + r0 + sub, :] * cw_ref[1:2, :]
                + u_ref[halo + r0:halo + r0 + sub, :] * cw_ref[2:3, :] + cb_ref[...])

    for s in range(tm // sub):
        r0 = s * sub
        h = h_ref[r0:r0 + sub, :]
        uv_ref[halo + r0:halo + r0 + sub, :] = jnp.dot(h, wvb_ref[...], preferred_element_type=F32)
        ug_ref[halo + r0:halo + r0 + sub, :] = jnp.dot(h, wgb_ref[...], preferred_element_type=F32)
        val = conv(uv_ref, cwv_ref, cbv_ref, r0)
        gate = conv(ug_ref, cwg_ref, cbg_ref, r0)
        a_ref[r0:r0 + sub, :] = (gate * jax.nn.sigmoid(gate) * val).astype(BF16)
    uv_ref[0:halo, :] = uv_ref[tm:tm + halo, :]
    ug_ref[0:halo, :] = ug_ref[tm:tm + halo, :]


def _ffn_up(h2, w_up, conv_w, conv_b, seq, tm=1024, tn=512, sub=128):
    m = h2.shape[0]
    nj = D_FF // tn
    return pl.pallas_call(
        functools.partial(_ffn_up_kernel, tm=tm, sub=sub, tiles_per_seq=seq // tm),
        grid=(nj, m // tm),
        in_specs=[pl.BlockSpec((tm, D_MODEL), lambda j, i: (i, 0)),
                  pl.BlockSpec((D_MODEL, tn), lambda j, i: (0, j)),
                  pl.BlockSpec((D_MODEL, tn), lambda j, i: (0, nj + j)),
                  pl.BlockSpec((3, tn), lambda j, i: (0, j)),
                  pl.BlockSpec((3, tn), lambda j, i: (0, nj + j)),
                  pl.BlockSpec((1, tn), lambda j, i: (0, j)),
                  pl.BlockSpec((1, tn), lambda j, i: (0, nj + j))],
        out_specs=pl.BlockSpec((tm, tn), lambda j, i: (i, j)),
        out_shape=jax.ShapeDtypeStruct((m, D_FF), BF16),
        scratch_shapes=[pltpu.VMEM((D_MODEL, tn), BF16), pltpu.VMEM((D_MODEL, tn), BF16),
                        pltpu.VMEM((tm + 8, tn), F32), pltpu.VMEM((tm + 8, tn), F32)],
        compiler_params=_cparams(("parallel", "arbitrary")),
        name="ffn_up",
    )(h2, w_up, w_up, conv_w, conv_w, conv_b, conv_b)


def _ffn_down_kernel(a_ref, w_ref, x1_ref, o_ref, wb_ref):
    @pl.when(pl.program_id(1) == 0)
    def _():
        wb_ref[...] = w_ref[...].astype(BF16)

    o_ref[...] = x1_ref[...] + jnp.dot(a_ref[...], wb_ref[...], preferred_element_type=F32)


def _ffn_down(act, w_down, x1, tm=512, tn=512):
    m = act.shape[0]
    return pl.pallas_call(
        _ffn_down_kernel,
        grid=(D_MODEL // tn, m // tm),
        in_specs=[pl.BlockSpec((tm, D_FF), lambda j, i: (i, 0)),
                  pl.BlockSpec((D_FF, tn), lambda j, i: (0, j)),
                  pl.BlockSpec((tm, tn), lambda j, i: (i, j))],
        out_specs=pl.BlockSpec((tm, tn), lambda j, i: (i, j)),
        out_shape=jax.ShapeDtypeStruct((m, D_MODEL), F32),
        scratch_shapes=[pltpu.VMEM((D_FF, tn), BF16)],
        compiler_params=_cparams(("parallel", "arbitrary")),
        name="ffn_down",
    )(act, w_down, x1)


def _pad_cols(w, n):
    return jnp.pad(w, ((0, 0), (0, n - w.shape[1])))


def _pad_rows(w, n):
    return jnp.pad(w, ((0, n - w.shape[0]), (0, 0)))


def _layer(x2, batch, seq, w_in_p, norm1_g, rw_mu, rw_w0, rw_w2, rw_a0, rw_a2, rw_g2, rw_k_k, rw_k_a,
           rw_r_k, rw_gn_w, rw_gn_b, q_norm_g, k_norm_g, attn_sinks, w_branch, w_out,
           norm2_g, w_up, conv_w, conv_b, w_down):
    row = lambda v: v.reshape(1, -1).astype(F32)
    o3 = 3 * RW_DIM
    o4 = o3 + DECAY_LORA
    o5 = o4 + ICLR_LORA
    o6 = o5 + GATE_LORA
    mu = rw_mu.reshape(1, -1)
    mu_l = jnp.concatenate([_pad_cols(mu[:, o3:o4], LORA_W), _pad_cols(mu[:, o4:o5], LORA_A),
                            mu[:, o5:o6]], axis=1)
    w2p = _pad_rows(rw_w2, LORA_W).astype(BF16)
    a2p = _pad_rows(rw_a2, LORA_A).astype(BF16)

    z = _in_proj(x2, row(norm1_g), w_in_p)
    o_rw = _rwkv(z, batch, seq, mu[:, :RW_DIM], mu[:, RW_DIM:2 * RW_DIM], mu[:, 2 * RW_DIM:o3], mu_l,
                 row(rw_w0), row(rw_a0), row(rw_k_k), row(rw_k_a), row(rw_r_k),
                 w2p, a2p, rw_g2.astype(BF16), row(rw_gn_w), row(rw_gn_b))
    o_att = _swa(z, attn_sinks.astype(F32), jnp.tile(row(q_norm_g), (1, ATT_HEADS)),
                 jnp.tile(row(k_norm_g), (1, ATT_KV_HEADS)), seq)
    x1, h2 = _mix_out(o_rw, o_att, w_branch[:RW_DIM].astype(BF16), w_branch[RW_DIM:].astype(BF16), z,
                      w_out.astype(BF16), x2, row(norm2_g))
    act = _ffn_up(h2, w_up, conv_w.astype(F32), row(conv_b), seq)
    return _ffn_down(act, w_down, x1)


def kernel(x, norm1_g, w_in, rw_mu, rw_w0, rw_w2, rw_a0, rw_a2, rw_g2, rw_k_k, rw_k_a, rw_r_k,
           rw_gn_w, rw_gn_b, q_norm_g, k_norm_g, attn_sinks, w_branch, w_out, norm2_g, w_up,
           conv_w, conv_b, w_down):
    batch, seq, d = x.shape
    x2 = x.reshape(batch * seq, d)
    params = (norm1_g, rw_mu, rw_w0, rw_w2, rw_a0, rw_a2, rw_g2, rw_k_k, rw_k_a, rw_r_k,
              rw_gn_w, rw_gn_b, q_norm_g, k_norm_g, attn_sinks, w_branch, w_out, norm2_g, w_up,
              conv_w, conv_b, w_down)
    for layer in range(norm1_g.shape[0]):
        x2 = _layer(x2, batch, seq, _w_in_layout(w_in, layer), *(p[layer] for p in params))
    return x2.reshape(batch, seq, d)
```

```python
import functools

import jax
import jax.numpy as jnp
import numpy as np
from jax import lax
from jax.experimental import pallas as pl
from jax.experimental.pallas import tpu as pltpu

F32 = jnp.float32
BF16 = jnp.bfloat16

LANES = 128
BF16_SUBLANES = 16
VMEM_LIMIT = 56 * 1024 * 1024

D_MODEL = 2048
RW_HEADS = 16
RW_HEAD = 64
RW_DIM = RW_HEADS * RW_HEAD
DECAY_LORA = 96
ICLR_LORA = 96
GATE_LORA = 256
GN_EPS = 64e-5
ATT_HEADS = 16
ATT_KV_HEADS = 4
ATT_GROUP = ATT_HEADS // ATT_KV_HEADS
ATT_HEAD = 64
ATT_Q_DIM = ATT_HEADS * ATT_HEAD
ATT_KV_DIM = ATT_KV_HEADS * ATT_HEAD
WINDOW = 128
BLOCK = 128
D_FF = 5632
RMS_EPS = 1e-6
NEG_BIG = -1e30
LOG2E = 1.4426950408889634

C_R = 0
C_K = RW_DIM
C_V = 2 * RW_DIM
C_Q = 3 * RW_DIM
C_GRW = C_Q + ATT_Q_DIM
C_GATT = C_GRW + D_MODEL
C_LORA = C_GATT + D_MODEL
LORA_W = 128
LORA_A = 128
LORA_COLS = LORA_W + LORA_A + GATE_LORA
C_KA = C_LORA + LORA_COLS
C_VA = C_KA + ATT_KV_DIM
Z_COLS = C_VA + ATT_KV_DIM

CHUNK = 64
PAIR = 2 * RW_HEAD
N_PAIRS = RW_DIM // PAIR


def _cparams(sem):
    return pltpu.CompilerParams(dimension_semantics=sem, vmem_limit_bytes=VMEM_LIMIT)


def _mm(a, b):
    return jnp.dot(a.astype(BF16), b.astype(BF16), preferred_element_type=F32)


def _mm_nt(a, b):
    return lax.dot_general(a.astype(BF16), b.astype(BF16), (((1,), (1,)), ((), ())),
                           preferred_element_type=F32)


def _mm_tn(a, b):
    return lax.dot_general(a.astype(BF16), b.astype(BF16), (((0,), (0,)), ((), ())),
                           preferred_element_type=F32)


def _mm_split(m01, x):
    hi = x.astype(BF16)
    lo = (x - hi.astype(F32)).astype(BF16)
    return (jnp.dot(m01, hi, preferred_element_type=F32)
            + jnp.dot(m01, lo, preferred_element_type=F32))


def _mm_split_r(x, m01):
    hi = x.astype(BF16)
    lo = (x - hi.astype(F32)).astype(BF16)
    return (jnp.dot(hi, m01, preferred_element_type=F32)
            + jnp.dot(lo, m01, preferred_element_type=F32))


def _head_ones(n, head, scale):
    r = lax.broadcasted_iota(jnp.int32, (n, n), 0) // head
    c = lax.broadcasted_iota(jnp.int32, (n, n), 1) // head
    return jnp.where(r == c, scale, 0.0).astype(BF16)


def _w_in_layout_kernel(wt_ref, o_ref, *, moves, pad):
    o_ref[pad[0]:pad[1], :] = jnp.zeros((pad[1] - pad[0], o_ref.shape[1]), BF16)
    for dst, src, width in moves:
        o_ref[dst:dst + width, :] = wt_ref[src:src + width, :].astype(BF16)


def _w_in_layout(w_in_layers, layer, cols=256):
    _, d, n = w_in_layers.shape
    wt_layers = jnp.swapaxes(w_in_layers, 1, 2)
    o3 = 3 * RW_DIM
    o4 = o3 + DECAY_LORA
    o5 = o4 + ICLR_LORA
    o6 = o5 + GATE_LORA
    oq = o6 + ATT_Q_DIM
    ok = oq + ATT_KV_DIM
    ov = ok + ATT_KV_DIM
    moves = ((C_R, 0, o3), (C_Q, o6, ATT_Q_DIM), (C_GRW, ov, 2 * D_MODEL),
             (C_LORA, o3, DECAY_LORA), (C_LORA + LORA_W, o4, ICLR_LORA),
             (C_LORA + LORA_W + LORA_A, o5, GATE_LORA), (C_KA, oq, ATT_KV_DIM), (C_VA, ok, ATT_KV_DIM))
    return pl.pallas_call(
        functools.partial(_w_in_layout_kernel, moves=moves, pad=(C_LORA, C_LORA + LORA_W + LORA_A)),
        grid=(d // cols,),
        in_specs=[pl.BlockSpec((None, n, cols), lambda i: (layer, 0, i))],
        out_specs=pl.BlockSpec((Z_COLS, cols), lambda i: (0, i)),
        out_shape=jax.ShapeDtypeStruct((Z_COLS, d), BF16),
        compiler_params=_cparams(("parallel",)),
        name="w_in_layout",
    )(wt_layers)


def _in_proj_kernel(x_ref, g_ref, wt_ref, z_ref, h_ref):
    @pl.when(pl.program_id(1) == 0)
    def _():
        xf = x_ref[...]
        y = xf * lax.rsqrt(jnp.mean(xf * xf, axis=-1, keepdims=True) + RMS_EPS)
        h_ref[...] = (y * g_ref[...]).astype(BF16)

    z_ref[...] = _mm_nt(h_ref[...], wt_ref[...]).astype(BF16)


def _in_proj(x2, g1, w_in_t, tm=1024, tn=1024):
    m, d = x2.shape
    n = w_in_t.shape[0]
    return pl.pallas_call(
        _in_proj_kernel,
        grid=(m // tm, n // tn),
        in_specs=[pl.BlockSpec((tm, d), lambda i, j: (i, 0)),
                  pl.BlockSpec((1, d), lambda i, j: (0, 0)),
                  pl.BlockSpec((tn, d), lambda i, j: (j, 0))],
        out_specs=pl.BlockSpec((tm, tn), lambda i, j: (i, j)),
        out_shape=jax.ShapeDtypeStruct((m, n), BF16),
        scratch_shapes=[pltpu.VMEM((tm, d), BF16)],
        compiler_params=_cparams(("parallel", "arbitrary")),
        name="in_proj",
    )(x2, g1, w_in_t)


def _rwkv_prep_kernel(zr_ref, zk_ref, zv_ref, zl_ref, pr_ref, pk_ref, pv_ref, pli_ref,
                      mur_ref, muk_ref, muv_ref, mul_ref,
                      w0_ref, a0_ref, kk_ref, ka_ref, rk_ref, w2_ref, a2_ref, g2_ref,
                      at_ref, rt_ref, bt_ref, kt_ref, bg_ref, kg_ref, v_ref, g_ref, bonus_ref,
                      gT_ref, gm_ref, *, tm, first):
    keep = jnp.where(first, 0.0, 1.0)
    nc = tm // CHUNK
    last = slice(BF16_SUBLANES - 1, BF16_SUBLANES)

    def shifted(z, prev_last, mu):
        z = z.astype(F32)
        row = lax.broadcasted_iota(jnp.int32, z.shape, 0)
        zprev = jnp.where(row == 0, prev_last.astype(F32) * keep, pltpu.roll(z, 1, 0))
        return z + (zprev - z) * mu

    lo = shifted(zl_ref[...], pli_ref[last, :], mul_ref[...])
    tanh_wd = jnp.tanh(lo[:, :LORA_W]).astype(BF16)
    ad = lo[:, LORA_W:LORA_W + LORA_A].astype(BF16)
    sig_gd = jax.nn.sigmoid(lo[:, LORA_W + LORA_A:]).astype(BF16)

    ti = lax.broadcasted_iota(jnp.int32, (tm, tm), 0)
    si = lax.broadcasted_iota(jnp.int32, (tm, tm), 1)
    tri = jnp.where(((ti // CHUNK) == (si // CHUNK)) & (si <= ti), 1.0, 0.0).astype(BF16)
    ones_h = _head_ones(PAIR, RW_HEAD, 1.0)
    w_lora = jnp.dot(tanh_wd, w2_ref[...], preferred_element_type=F32)
    a_lora = jnp.dot(ad, a2_ref[...], preferred_element_type=F32)
    g_ref[...] = jnp.dot(sig_gd, g2_ref[...], preferred_element_type=F32).astype(BF16)
    yield

    stash = []
    for p in range(N_PAIRS):
        cs = slice(p * PAIR, (p + 1) * PAIR)
        r = shifted(zr_ref[:, cs], pr_ref[last, cs], mur_ref[:, cs])
        k = shifted(zk_ref[:, cs], pk_ref[last, cs], muk_ref[:, cs])
        v = shifted(zv_ref[:, cs], pv_ref[last, cs], muv_ref[:, cs])

        wpre = w0_ref[:, cs] + w_lora[:, cs]
        w = jnp.minimum(wpre, 0.0) - jnp.log1p(jnp.exp(-jnp.abs(wpre))) - 0.5
        lw = -jnp.exp(w)
        a = jax.nn.sigmoid(a0_ref[:, cs] + a_lora[:, cs])

        kk = k * kk_ref[:, cs]
        kk = kk * lax.rsqrt(jnp.maximum(_mm(kk * kk, ones_h), 1e-24))
        k2 = k * (1.0 + (a - 1.0) * ka_ref[:, cs])
        bonus_ref[:, cs] = (_mm(r * k2 * rk_ref[:, cs], ones_h) * v).astype(BF16)
        v_ref[:, cs] = v.astype(BF16)
        stash.append((r, k2, -kk, kk * a, lw))
        yield

    c_all = _mm_split(tri, jnp.concatenate([st[4] for st in stash], axis=1))
    yield

    for p in range(N_PAIRS):
        cs = slice(p * PAIR, (p + 1) * PAIR)
        r, k2, ka_vec, kb_vec, lw = stash[p]
        c = c_all[:, cs]
        c3 = c.reshape(nc, CHUNK, PAIR)
        c_mid = c3[:, CHUNK // 2 - 1:CHUNK // 2, :]
        c_end = c3[:, CHUNK - 1:CHUNK, :]
        cm = jnp.broadcast_to(c_mid, c3.shape).reshape(tm, PAIR)
        cT = jnp.broadcast_to(c_end, c3.shape).reshape(tm, PAIR)

        e_in = jnp.exp(c - cm)
        e_out = jnp.exp(cm - c)
        e_end = jnp.exp(cT - c)
        at_ref[:, cs] = (ka_vec * jnp.exp(c - lw - cm)).astype(BF16)
        rt_ref[:, cs] = (r * e_in).astype(BF16)
        bt_ref[:, cs] = (kb_vec * e_out).astype(BF16)
        kt_ref[:, cs] = (k2 * e_out).astype(BF16)
        bg_ref[:, cs] = (kb_vec * e_end).astype(BF16)
        kg_ref[:, cs] = (k2 * e_end).astype(BF16)
        gT_ref[:, :, cs] = jnp.exp(c_end)
        gm_ref[:, :, cs] = jnp.exp(c_mid)
        yield


def _rwkv_kernel(*refs, tb, nb):
    prep_in, (gnw_ref, gnb_ref, o_ref), scratch = refs[:20], refs[20:23], refs[23:]
    operands, s_ref = scratch[:11], scratch[11]
    j = pl.program_id(1)
    prep = functools.partial(_rwkv_prep_kernel, *prep_in, *operands, tm=tb, first=j == 0)
    chunk = functools.partial(_rwkv_chunk_kernel, *operands, gnw_ref, gnb_ref, o_ref, s_ref,
                              chunks=tb // CHUNK, pairs=N_PAIRS, first=j == 1)

    @pl.when(j == 0)
    def _():
        for _ in prep():
            pass

    @pl.when((j > 0) & (j < nb))
    def _():
        chunk(filler=prep())

    @pl.when(j == nb)
    def _():
        chunk()


def _rwkv(z, batch, seq, mu_r, mu_k, mu_v, mu_l, w0, a0, k_k, k_a, r_k, w2p, a2p, g2, gn_w, gn_b, tb=128):
    m = z.shape[0]
    nb = seq // tb
    pb = tb // BF16_SUBLANES
    blk = lambda b, j: b * nb + jnp.minimum(j, nb - 1)

    def cur(col0, width):
        return pl.BlockSpec((tb, width), lambda b, j, c0=col0 // width: (blk(b, j), c0))

    def prev(col0, width):
        return pl.BlockSpec((BF16_SUBLANES, width),
                            lambda b, j, c0=col0 // width: (jnp.maximum(blk(b, j) * pb - 1, 0), c0))

    def whole(rows, width):
        return pl.BlockSpec((rows, width), lambda b, j: (0, 0))

    in_specs = [cur(C_R, RW_DIM), cur(C_K, RW_DIM), cur(C_V, RW_DIM), cur(C_LORA, LORA_COLS),
                prev(C_R, RW_DIM), prev(C_K, RW_DIM), prev(C_V, RW_DIM), prev(C_LORA, LORA_COLS),
                whole(1, RW_DIM), whole(1, RW_DIM), whole(1, RW_DIM), whole(1, LORA_COLS),
                whole(1, RW_DIM), whole(1, RW_DIM), whole(1, RW_DIM), whole(1, RW_DIM), whole(1, RW_DIM),
                whole(LORA_W, RW_DIM), whole(LORA_A, RW_DIM), whole(GATE_LORA, RW_DIM),
                whole(1, RW_DIM), whole(1, RW_DIM)]
    big = pltpu.VMEM((tb, RW_DIM), BF16)
    per_chunk = pltpu.VMEM((tb // CHUNK, 1, RW_DIM), F32)
    return pl.pallas_call(
        functools.partial(_rwkv_kernel, tb=tb, nb=nb),
        grid=(batch, nb + 1),
        in_specs=in_specs,
        out_specs=pl.BlockSpec((tb, RW_DIM), lambda b, j: (b * nb + jnp.maximum(j - 1, 0), 0)),
        out_shape=jax.ShapeDtypeStruct((m, RW_DIM), BF16),
        scratch_shapes=[big] * 9 + [per_chunk] * 2 + [pltpu.VMEM((N_PAIRS, PAIR, PAIR), F32)],
        compiler_params=_cparams(("parallel", "arbitrary")),
        name="rwkv",
    )(z, z, z, z, z, z, z, z, mu_r, mu_k, mu_v, mu_l, w0, a0, k_k, k_a, r_k, w2p, a2p, g2, gn_w, gn_b)


def _rwkv_chunk_kernel(at_ref, rt_ref, bt_ref, kt_ref, bg_ref, kg_ref, v_ref, g_ref, bonus_ref,
                       gT_ref, gm_ref, gnw_ref, gnb_ref, o_ref, s_ref, *, chunks, pairs, first,
                       filler=None):
    @pl.when(first)
    def _():
        s_ref[...] = jnp.zeros_like(s_ref)

    T = CHUNK
    lane = lax.broadcasted_iota(jnp.int32, (T, PAIR), 1)
    head0 = lane < RW_HEAD
    ri = lax.broadcasted_iota(jnp.int32, (2 * T, 2 * T), 0)
    ci = lax.broadcasted_iota(jnp.int32, (2 * T, 2 * T), 1)
    same = (ri // T) == (ci // T)
    strict = same & ((ri % T) > (ci % T))
    incl = same & ((ri % T) >= (ci % T))
    eye = jnp.where(ri == ci, 1.0, 0.0)
    own = (ri // T) == (ci // RW_HEAD)

    items = [(p, c) for c in range(chunks) for p in range(pairs)]
    rows = lambda c: slice(c * T, (c + 1) * T)
    cols = lambda p: slice(p * PAIR, (p + 1) * PAIR)

    def stack(ref):
        out = []
        for p, c in items:
            x = ref[rows(c), cols(p)]
            zero = jnp.zeros_like(x)
            out.append(jnp.concatenate([jnp.where(head0, x, zero), jnp.where(head0, zero, x)], axis=0))
        return out

    each = lambda f, *ls: [f(*xs) for xs in zip(*ls)]
    La, Lr, Rb, Rk = stack(at_ref), stack(rt_ref), stack(bt_ref), stack(kt_ref)
    Rbg, Rkg, Vs = stack(bg_ref), stack(kg_ref), stack(v_ref)
    gm_row = [gm_ref[c][:, cols(p)] for p, c in items]
    gT_row = [gT_ref[c][:, cols(p)] for p, c in items]
    gate = [g_ref[rows(c), cols(p)] for p, c in items]
    bonus = [bonus_ref[rows(c), cols(p)] for p, c in items]
    filler = iter(()) if filler is None else filler
    tick = lambda: next(filler, None)

    AA = each(lambda la, lr, rb, rk: _mm_nt(jnp.concatenate([la, lr], axis=0),
                                            jnp.concatenate([rb, rk], axis=0)), La, Lr, Rb, Rk)
    tick()
    N = [jnp.where(strict, aa[:2 * T, :2 * T], 0.0) for aa in AA]
    Aak = [jnp.where(strict, aa[:2 * T, 2 * T:], 0.0) for aa in AA]
    Arb = [jnp.where(incl, aa[2 * T:, :2 * T], 0.0) for aa in AA]
    Ark = [jnp.where(incl, aa[2 * T:, 2 * T:], 0.0) for aa in AA]
    def live_rows(x, t0):
        return x if t0 == 0 else jnp.concatenate([x[t0:T, :], x[T + t0:, :]], axis=0)

    def all_rows(y, t0):
        if t0 == 0:
            return y
        zero = jnp.zeros((t0, y.shape[1]), y.dtype)
        return jnp.concatenate([zero, y[:T - t0, :], zero, y[T - t0:, :]], axis=0)

    W = each(lambda n: eye + n, N)
    P = N
    span = 2
    while span < T:
        t0 = span if span % BF16_SUBLANES == 0 else 0
        P = each(lambda p_: all_rows(_mm(live_rows(p_, t0), p_), t0), P)
        tick()
        W = each(lambda w, p_: w + all_rows(_mm(live_rows(w, t0), p_), t0), W, P)
        tick()
        span *= 2
    AkV = each(_mm, Aak, Vs)
    tick()
    X = each(lambda w, la, akv: _mm(w, jnp.concatenate([la, akv.astype(BF16)], axis=1)).astype(BF16),
             W, La, AkV)
    tick()
    Z = each(lambda arb, ark, x, vs:
             _mm(jnp.concatenate([arb.astype(BF16), ark.astype(BF16)], axis=1),
                 jnp.concatenate([x, jnp.concatenate([jnp.zeros_like(vs), vs], axis=1)], axis=0)),
             Arb, Ark, X, Vs)
    tick()
    Q = each(lambda z, lr, gm: (z[:, :PAIR] + lr.astype(F32)) * gm, Z, Lr, gm_row)
    Y0 = [z[:, PAIR:] for z in Z]
    tick()
    Mbd = each(lambda rbg, x, gm: _mm_tn(rbg, x[:, :PAIR]) * gm, Rbg, X, gm_row)
    NcT = each(lambda x, vs, rbg, rkg: _mm_tn(jnp.concatenate([x[:, PAIR:], vs], axis=0),
                                              jnp.concatenate([rbg, rkg], axis=0)), X, Vs, Rbg, Rkg)
    for _ in filler:
        pass

    S = [s_ref[p] for p in range(pairs)]
    for c in range(chunks):
        for p in range(pairs):
            i = c * pairs + p
            Ys = _mm_nt(Q[i], S[p]) + Y0[i]
            S[p] = S[p] * gT_row[i] + _mm_nt(S[p], Mbd[i]) + NcT[i]
            mu = jnp.sum(Ys, axis=-1, keepdims=True) * (1.0 / RW_HEAD)
            d = jnp.where(own, Ys - mu, 0.0)
            var = jnp.sum(d * d, axis=-1, keepdims=True) * (1.0 / RW_HEAD)
            dn = d * lax.rsqrt(var + GN_EPS)
            yn = (dn[:T, :] + dn[T:, :]) * gnw_ref[:, cols(p)] + gnb_ref[:, cols(p)]
            out = (yn + bonus[i].astype(F32)) * gate[i].astype(F32)
            o_ref[rows(c), cols(p)] = out.astype(BF16)
    for p in range(pairs):
        s_ref[p] = S[p]


def _swa_kernel(sink_ref, q_ref, kc_ref, kp_ref, vc_ref, vp_ref, qg_ref, kg_ref, o_ref, *,
                blocks_per_seq, slopes):
    first = (pl.program_id(0) % blocks_per_seq) == 0
    mean_h = _head_ones(LANES, ATT_HEAD, 1.0 / ATT_HEAD)

    def head_rms(x, gain):
        parts = []
        for b in range(x.shape[1] // LANES):
            xb = x[:, b * LANES:(b + 1) * LANES]
            ms = _mm_split_r(xb * xb, mean_h)
            parts.append(xb * lax.rsqrt(ms + RMS_EPS))
        return jnp.concatenate(parts, axis=1) * gain

    q = head_rms(q_ref[...].astype(F32), qg_ref[...]) * (ATT_HEAD ** -0.5 * LOG2E)
    q = q.astype(BF16)
    kcat = jnp.concatenate([kp_ref[...], kc_ref[...]], axis=0).astype(F32)
    kcat = head_rms(kcat, kg_ref[...]).astype(BF16)
    vcat = jnp.concatenate([vp_ref[...], vc_ref[...]], axis=0)

    qi = lax.broadcasted_iota(jnp.int32, (BLOCK, 2 * BLOCK), 0)
    kj = lax.broadcasted_iota(jnp.int32, (BLOCK, 2 * BLOCK), 1)
    dist_i = BLOCK + qi - kj
    first_key = jnp.where(first, BLOCK, 0)
    valid = (dist_i >= 0) & (dist_i < WINDOW) & (kj >= first_key)
    neg_dist = jnp.where(valid, -dist_i.astype(F32), NEG_BIG)

    outs = []
    for h in range(ATT_HEADS):
        j = h // ATT_GROUP
        qh = q[:, h * ATT_HEAD:(h + 1) * ATT_HEAD]
        kh = kcat[:, j * ATT_HEAD:(j + 1) * ATT_HEAD]
        vh = vcat[:, j * ATT_HEAD:(j + 1) * ATT_HEAD]
        s = _mm_nt(qh, kh) + (slopes[h] * LOG2E) * neg_dist
        sink = sink_ref[h] * LOG2E
        mx = jnp.maximum(jnp.max(s, axis=-1, keepdims=True), sink)
        p = jnp.exp2(s - mx)
        denom = jnp.sum(p, axis=-1, keepdims=True) + jnp.exp2(sink - mx)
        outs.append(_mm(p, vh) * (1.0 / denom))
    o_ref[...] = jnp.concatenate(outs, axis=1).astype(BF16)


def _swa(z, sinks, q_gain_t, k_gain_t, seq):
    m = z.shape[0]
    nblk = m // BLOCK
    bps = seq // BLOCK
    slopes = tuple(float(s) for s in
                   np.exp2(-8.0 * np.arange(1, ATT_HEADS + 1, dtype=np.float32) / ATT_HEADS).astype(np.float32))
    kv_cur = lambda c0: pl.BlockSpec((BLOCK, ATT_KV_DIM), lambda n, c=c0 // ATT_KV_DIM: (n, c))
    kv_prev = lambda c0: pl.BlockSpec((BLOCK, ATT_KV_DIM),
                                      lambda n, c=c0 // ATT_KV_DIM: (jnp.maximum(n - 1, 0), c))
    return pl.pallas_call(
        functools.partial(_swa_kernel, blocks_per_seq=bps, slopes=slopes),
        grid=(nblk,),
        in_specs=[pl.BlockSpec(memory_space=pltpu.SMEM),
                  pl.BlockSpec((BLOCK, ATT_Q_DIM), lambda n: (n, C_Q // ATT_Q_DIM)),
                  kv_cur(C_KA), kv_prev(C_KA), kv_cur(C_VA), kv_prev(C_VA),
                  pl.BlockSpec((1, ATT_Q_DIM), lambda n: (0, 0)),
                  pl.BlockSpec((1, ATT_KV_DIM), lambda n: (0, 0))],
        out_specs=pl.BlockSpec((BLOCK, ATT_Q_DIM), lambda n: (n, 0)),
        out_shape=jax.ShapeDtypeStruct((m, ATT_Q_DIM), BF16),
        compiler_params=_cparams(("parallel",)),
        name="swa",
    )(sinks, z, z, z, z, z, q_gain_t, k_gain_t)


def _mix_out_kernel(orw_ref, oatt_ref, wb1_ref, wb2_ref, zg_ref, wo_ref, x_ref, g_ref, x1_ref, h2_ref):
    p_rw = jnp.dot(orw_ref[...], wb1_ref[...], preferred_element_type=F32)
    p_att = jnp.dot(oatt_ref[...], wb2_ref[...], preferred_element_type=F32)
    g_rw = jax.nn.sigmoid(zg_ref[:, :D_MODEL].astype(F32))
    g_att = jax.nn.sigmoid(zg_ref[:, D_MODEL:].astype(F32))
    mix = (g_rw * p_rw + g_att * p_att).astype(BF16)
    x1 = x_ref[...] + jnp.dot(mix, wo_ref[...], preferred_element_type=F32)
    x1_ref[...] = x1
    y = x1 * lax.rsqrt(jnp.mean(x1 * x1, axis=-1, keepdims=True) + RMS_EPS)
    h2_ref[...] = (y * g_ref[...]).astype(BF16)


def _mix_out(o_rw, o_att, wb1, wb2, z, w_out, x2, g2, tm=256):
    m = x2.shape[0]
    const = lambda rows, cols: pl.BlockSpec((rows, cols), lambda i: (0, 0))
    rows = lambda cols, c0=0: pl.BlockSpec((tm, cols), lambda i: (i, c0))
    return pl.pallas_call(
        _mix_out_kernel,
        grid=(m // tm,),
        in_specs=[rows(RW_DIM), rows(ATT_Q_DIM), const(RW_DIM, D_MODEL), const(ATT_Q_DIM, D_MODEL),
                  rows(2 * D_MODEL, C_GRW // (2 * D_MODEL)), const(D_MODEL, D_MODEL),
                  rows(D_MODEL), const(1, D_MODEL)],
        out_specs=[rows(D_MODEL), rows(D_MODEL)],
        out_shape=[jax.ShapeDtypeStruct((m, D_MODEL), F32),
                   jax.ShapeDtypeStruct((m, D_MODEL), BF16)],
        compiler_params=_cparams(("parallel",)),
        name="mix_out",
    )(o_rw, o_att, wb1, wb2, z, w_out, x2, g2)


def _ffn_up_kernel(h_ref, wv_ref, wg_ref, cwv_ref, cwg_ref, cbv_ref, cbg_ref, a_ref,
                   wvb_ref, wgb_ref, cv_ref, cg_ref, *, tm, sub, tiles_per_seq):
    halo = 8

    @pl.when(pl.program_id(1) == 0)
    def _():
        wvb_ref[...] = wv_ref[...].astype(BF16)
        wgb_ref[...] = wg_ref[...].astype(BF16)

    @pl.when((pl.program_id(1) % tiles_per_seq) == 0)
    def _():
        cv_ref[...] = jnp.zeros_like(cv_ref)
        cg_ref[...] = jnp.zeros_like(cg_ref)

    row = lax.broadcasted_iota(jnp.int32, (sub, a_ref.shape[1]), 0)

    def conv(u, carry_ref, cw_ref, cb_ref):
        p1 = carry_ref[halo - 1:halo, :]
        p2 = carry_ref[halo - 2:halo - 1, :]
        prev1 = jnp.where(row == 0, p1, pltpu.roll(u, 1, 0))
        prev2 = jnp.where(row == 0, p2, jnp.where(row == 1, p1, pltpu.roll(u, 2, 0)))
        carry_ref[...] = u[sub - halo:sub, :]
        return prev2 * cw_ref[0:1, :] + prev1 * cw_ref[1:2, :] + u * cw_ref[2:3, :] + cb_ref[...]

    for s in range(tm // sub):
        h = h_ref[s * sub:(s + 1) * sub, :]
        val = conv(jnp.dot(h, wvb_ref[...], preferred_element_type=F32), cv_ref, cwv_ref, cbv_ref)
        gate = conv(jnp.dot(h, wgb_ref[...], preferred_element_type=F32), cg_ref, cwg_ref, cbg_ref)
        a_ref[s * sub:(s + 1) * sub, :] = (gate * jax.nn.sigmoid(gate) * val).astype(BF16)


def _ffn_up(h2, w_up, conv_w, conv_b, seq, tm=1024, tn=512, sub=256):
    m = h2.shape[0]
    nj = D_FF // tn
    return pl.pallas_call(
        functools.partial(_ffn_up_kernel, tm=tm, sub=sub, tiles_per_seq=seq // tm),
        grid=(nj, m // tm),
        in_specs=[pl.BlockSpec((tm, D_MODEL), lambda j, i: (i, 0)),
                  pl.BlockSpec((D_MODEL, tn), lambda j, i: (0, j)),
                  pl.BlockSpec((D_MODEL, tn), lambda j, i: (0, nj + j)),
                  pl.BlockSpec((3, tn), lambda j, i: (0, j)),
                  pl.BlockSpec((3, tn), lambda j, i: (0, nj + j)),
                  pl.BlockSpec((1, tn), lambda j, i: (0, j)),
                  pl.BlockSpec((1, tn), lambda j, i: (0, nj + j))],
        out_specs=pl.BlockSpec((tm, tn), lambda j, i: (i, j)),
        out_shape=jax.ShapeDtypeStruct((m, D_FF), BF16),
        scratch_shapes=[pltpu.VMEM((D_MODEL, tn), BF16), pltpu.VMEM((D_MODEL, tn), BF16),
                        pltpu.VMEM((8, tn), F32), pltpu.VMEM((8, tn), F32)],
        compiler_params=_cparams(("parallel", "arbitrary")),
        name="ffn_up",
    )(h2, w_up, w_up, conv_w, conv_w, conv_b, conv_b)


def _ffn_down_kernel(a_ref, w_ref, x1_ref, o_ref, wb_ref):
    @pl.when(pl.program_id(1) == 0)
    def _():
        wb_ref[...] = w_ref[...].astype(BF16)

    o_ref[...] = x1_ref[...] + jnp.dot(a_ref[...], wb_ref[...], preferred_element_type=F32)


def _ffn_down(act, w_down, x1, tm=512, tn=512):
    m = act.shape[0]
    return pl.pallas_call(
        _ffn_down_kernel,
        grid=(D_MODEL // tn, m // tm),
        in_specs=[pl.BlockSpec((tm, D_FF), lambda j, i: (i, 0)),
                  pl.BlockSpec((D_FF, tn), lambda j, i: (0, j)),
                  pl.BlockSpec((tm, tn), lambda j, i: (i, j))],
        out_specs=pl.BlockSpec((tm, tn), lambda j, i: (i, j)),
        out_shape=jax.ShapeDtypeStruct((m, D_MODEL), F32),
        scratch_shapes=[pltpu.VMEM((D_FF, tn), BF16)],
        compiler_params=_cparams(("parallel", "arbitrary")),
        name="ffn_down",
    )(act, w_down, x1)


def _pad_cols(w, n):
    return jnp.pad(w, ((0, 0), (0, n - w.shape[1])))


def _pad_rows(w, n):
    return jnp.pad(w, ((0, n - w.shape[0]), (0, 0)))


def _layer(x2, batch, seq, w_in_p, norm1_g, rw_mu, rw_w0, rw_w2, rw_a0, rw_a2, rw_g2, rw_k_k, rw_k_a,
           rw_r_k, rw_gn_w, rw_gn_b, q_norm_g, k_norm_g, attn_sinks, w_branch, w_out,
           norm2_g, w_up, conv_w, conv_b, w_down):
    row = lambda v: v.reshape(1, -1).astype(F32)
    o3 = 3 * RW_DIM
    o4 = o3 + DECAY_LORA
    o5 = o4 + ICLR_LORA
    o6 = o5 + GATE_LORA
    mu = rw_mu.reshape(1, -1)
    mu_l = jnp.concatenate([_pad_cols(mu[:, o3:o4], LORA_W), _pad_cols(mu[:, o4:o5], LORA_A),
                            mu[:, o5:o6]], axis=1)
    w2p = _pad_rows(rw_w2, LORA_W).astype(BF16)
    a2p = _pad_rows(rw_a2, LORA_A).astype(BF16)

    z = _in_proj(x2, row(norm1_g), w_in_p)
    o_rw = _rwkv(z, batch, seq, mu[:, :RW_DIM], mu[:, RW_DIM:2 * RW_DIM], mu[:, 2 * RW_DIM:o3], mu_l,
                 row(rw_w0), row(rw_a0), row(rw_k_k), row(rw_k_a), row(rw_r_k),
                 w2p, a2p, rw_g2.astype(BF16), row(rw_gn_w), row(rw_gn_b))
    o_att = _swa(z, attn_sinks.astype(F32), jnp.tile(row(q_norm_g), (1, ATT_HEADS)),
                 jnp.tile(row(k_norm_g), (1, ATT_KV_HEADS)), seq)
    x1, h2 = _mix_out(o_rw, o_att, w_branch[:RW_DIM].astype(BF16), w_branch[RW_DIM:].astype(BF16), z,
                      w_out.astype(BF16), x2, row(norm2_g))
    act = _ffn_up(h2, w_up, conv_w.astype(F32), row(conv_b), seq)
    return _ffn_down(act, w_down, x1)


def kernel(x, norm1_g, w_in, rw_mu, rw_w0, rw_w2, rw_a0, rw_a2, rw_g2, rw_k_k, rw_k_a, rw_r_k,
           rw_gn_w, rw_gn_b, q_norm_g, k_norm_g, attn_sinks, w_branch, w_out, norm2_g, w_up,
           conv_w, conv_b, w_down):
    batch, seq, d = x.shape
    x2 = x.reshape(batch * seq, d)
    params = (norm1_g, rw_mu, rw_w0, rw_w2, rw_a0, rw_a2, rw_g2, rw_k_k, rw_k_a, rw_r_k,
              rw_gn_w, rw_gn_b, q_norm_g, k_norm_g, attn_sinks, w_branch, w_out, norm2_g, w_up,
              conv_w, conv_b, w_down)
    for layer in range(norm1_g.shape[0]):
        x2 = _layer(x2, batch, seq, _w_in_layout(w_in, layer), *(p[layer] for p in params))
    return x2.reshape(batch, seq, d)
```

```python
import functools

import jax
import jax.numpy as jnp
import numpy as np
from jax import lax
from jax.experimental import pallas as pl
from jax.experimental.pallas import tpu as pltpu

F32 = jnp.float32
BF16 = jnp.bfloat16

LANES = 128
BF16_SUBLANES = 16
VMEM_LIMIT = 56 * 1024 * 1024

D_MODEL = 2048
RW_HEADS = 16
RW_HEAD = 64
RW_DIM = RW_HEADS * RW_HEAD
DECAY_LORA = 96
ICLR_LORA = 96
GATE_LORA = 256
GN_EPS = 64e-5
ATT_HEADS = 16
ATT_KV_HEADS = 4
ATT_GROUP = ATT_HEADS // ATT_KV_HEADS
ATT_HEAD = 64
ATT_Q_DIM = ATT_HEADS * ATT_HEAD
ATT_KV_DIM = ATT_KV_HEADS * ATT_HEAD
WINDOW = 128
BLOCK = 128
D_FF = 5632
RMS_EPS = 1e-6
NEG_BIG = -1e30
LOG2E = 1.4426950408889634

C_R = 0
C_K = RW_DIM
C_V = 2 * RW_DIM
C_Q = 3 * RW_DIM
C_GRW = C_Q + ATT_Q_DIM
C_GATT = C_GRW + D_MODEL
C_LORA = C_GATT + D_MODEL
LORA_W = 128
LORA_A = 128
LORA_COLS = LORA_W + LORA_A + GATE_LORA
C_KA = C_LORA + LORA_COLS
C_VA = C_KA + ATT_KV_DIM
Z_COLS = C_VA + ATT_KV_DIM

CHUNK = 64
PAIR = 2 * RW_HEAD
N_PAIRS = RW_DIM // PAIR


def _cparams(sem):
    return pltpu.CompilerParams(dimension_semantics=sem, vmem_limit_bytes=VMEM_LIMIT)


def _mm(a, b):
    return jnp.dot(a.astype(BF16), b.astype(BF16), preferred_element_type=F32)


def _mm_nt(a, b):
    return lax.dot_general(a.astype(BF16), b.astype(BF16), (((1,), (1,)), ((), ())),
                           preferred_element_type=F32)


def _mm_tn(a, b):
    return lax.dot_general(a.astype(BF16), b.astype(BF16), (((0,), (0,)), ((), ())),
                           preferred_element_type=F32)


def _mm_split(m01, x):
    hi = x.astype(BF16)
    lo = (x - hi.astype(F32)).astype(BF16)
    return (jnp.dot(m01, hi, preferred_element_type=F32)
            + jnp.dot(m01, lo, preferred_element_type=F32))


def _mm_split_r(x, m01):
    hi = x.astype(BF16)
    lo = (x - hi.astype(F32)).astype(BF16)
    return (jnp.dot(hi, m01, preferred_element_type=F32)
            + jnp.dot(lo, m01, preferred_element_type=F32))


def _head_ones(n, head, scale):
    r = lax.broadcasted_iota(jnp.int32, (n, n), 0) // head
    c = lax.broadcasted_iota(jnp.int32, (n, n), 1) // head
    return jnp.where(r == c, scale, 0.0).astype(BF16)


def _w_in_layout_kernel(wt_ref, o_ref, *, moves, pad):
    o_ref[pad[0]:pad[1], :] = jnp.zeros((pad[1] - pad[0], o_ref.shape[1]), BF16)
    for dst, src, width in moves:
        o_ref[dst:dst + width, :] = wt_ref[src:src + width, :].astype(BF16)


def _w_in_layout(w_in_layers, layer, cols=256):
    _, d, n = w_in_layers.shape
    wt_layers = jnp.swapaxes(w_in_layers, 1, 2)
    o3 = 3 * RW_DIM
    o4 = o3 + DECAY_LORA
    o5 = o4 + ICLR_LORA
    o6 = o5 + GATE_LORA
    oq = o6 + ATT_Q_DIM
    ok = oq + ATT_KV_DIM
    ov = ok + ATT_KV_DIM
    moves = ((C_R, 0, o3), (C_Q, o6, ATT_Q_DIM), (C_GRW, ov, 2 * D_MODEL),
             (C_LORA, o3, DECAY_LORA), (C_LORA + LORA_W, o4, ICLR_LORA),
             (C_LORA + LORA_W + LORA_A, o5, GATE_LORA), (C_KA, oq, ATT_KV_DIM), (C_VA, ok, ATT_KV_DIM))
    return pl.pallas_call(
        functools.partial(_w_in_layout_kernel, moves=moves, pad=(C_LORA, C_LORA + LORA_W + LORA_A)),
        grid=(d // cols,),
        in_specs=[pl.BlockSpec((None, n, cols), lambda i: (layer, 0, i))],
        out_specs=pl.BlockSpec((Z_COLS, cols), lambda i: (0, i)),
        out_shape=jax.ShapeDtypeStruct((Z_COLS, d), BF16),
        compiler_params=_cparams(("parallel",)),
        name="w_in_layout",
    )(wt_layers)


def _in_proj_kernel(x_ref, g_ref, wt_ref, z_ref, h_ref):
    @pl.when(pl.program_id(1) == 0)
    def _():
        xf = x_ref[...]
        y = xf * lax.rsqrt(jnp.mean(xf * xf, axis=-1, keepdims=True) + RMS_EPS)
        h_ref[...] = (y * g_ref[...]).astype(BF16)

    z_ref[...] = _mm_nt(h_ref[...], wt_ref[...]).astype(BF16)


def _in_proj(x2, g1, w_in_t, tm=1024, tn=1024):
    m, d = x2.shape
    n = w_in_t.shape[0]
    return pl.pallas_call(
        _in_proj_kernel,
        grid=(m // tm, n // tn),
        in_specs=[pl.BlockSpec((tm, d), lambda i, j: (i, 0)),
                  pl.BlockSpec((1, d), lambda i, j: (0, 0)),
                  pl.BlockSpec((tn, d), lambda i, j: (j, 0))],
        out_specs=pl.BlockSpec((tm, tn), lambda i, j: (i, j)),
        out_shape=jax.ShapeDtypeStruct((m, n), BF16),
        scratch_shapes=[pltpu.VMEM((tm, d), BF16)],
        compiler_params=_cparams(("parallel", "arbitrary")),
        name="in_proj",
    )(x2, g1, w_in_t)


def _rwkv_prep_kernel(zr_ref, zk_ref, zv_ref, zl_ref, pr_ref, pk_ref, pv_ref, pli_ref,
                      mur_ref, muk_ref, muv_ref, mul_ref,
                      w0_ref, a0_ref, kk_ref, ka_ref, rk_ref, w2_ref, a2_ref, g2_ref,
                      at_ref, rt_ref, bt_ref, kt_ref, bg_ref, kg_ref, v_ref, g_ref, bonus_ref,
                      gT_ref, gm_ref, *, tm, first):
    keep = jnp.where(first, 0.0, 1.0)
    nc = tm // CHUNK
    last = slice(BF16_SUBLANES - 1, BF16_SUBLANES)

    def shifted(z, prev_last, mu):
        z = z.astype(F32)
        row = lax.broadcasted_iota(jnp.int32, z.shape, 0)
        zprev = jnp.where(row == 0, prev_last.astype(F32) * keep, pltpu.roll(z, 1, 0))
        return z + (zprev - z) * mu

    lo = shifted(zl_ref[...], pli_ref[last, :], mul_ref[...])
    tanh_wd = jnp.tanh(lo[:, :LORA_W]).astype(BF16)
    ad = lo[:, LORA_W:LORA_W + LORA_A].astype(BF16)
    sig_gd = jax.nn.sigmoid(lo[:, LORA_W + LORA_A:]).astype(BF16)

    ti = lax.broadcasted_iota(jnp.int32, (tm, tm), 0)
    si = lax.broadcasted_iota(jnp.int32, (tm, tm), 1)
    tri = jnp.where(((ti // CHUNK) == (si // CHUNK)) & (si <= ti), 1.0, 0.0).astype(BF16)
    ones_h = _head_ones(PAIR, RW_HEAD, 1.0)
    w_lora = jnp.dot(tanh_wd, w2_ref[...], preferred_element_type=F32)
    a_lora = jnp.dot(ad, a2_ref[...], preferred_element_type=F32)
    g_ref[...] = jnp.dot(sig_gd, g2_ref[...], preferred_element_type=F32).astype(BF16)
    yield

    stash = []
    for p in range(N_PAIRS):
        cs = slice(p * PAIR, (p + 1) * PAIR)
        r = shifted(zr_ref[:, cs], pr_ref[last, cs], mur_ref[:, cs])
        k = shifted(zk_ref[:, cs], pk_ref[last, cs], muk_ref[:, cs])
        v = shifted(zv_ref[:, cs], pv_ref[last, cs], muv_ref[:, cs])

        wpre = w0_ref[:, cs] + w_lora[:, cs]
        w = jnp.minimum(wpre, 0.0) - jnp.log1p(jnp.exp(-jnp.abs(wpre))) - 0.5
        lw = -jnp.exp(w)
        a = jax.nn.sigmoid(a0_ref[:, cs] + a_lora[:, cs])

        kk = k * kk_ref[:, cs]
        kk = kk * lax.rsqrt(jnp.maximum(_mm(kk * kk, ones_h), 1e-24))
        k2 = k * (1.0 + (a - 1.0) * ka_ref[:, cs])
        bonus_ref[:, cs] = (_mm(r * k2 * rk_ref[:, cs], ones_h) * v).astype(BF16)
        v_ref[:, cs] = v.astype(BF16)
        stash.append((r, k2, -kk, kk * a, lw))
        yield

    c_all = _mm_split(tri, jnp.concatenate([st[4] for st in stash], axis=1))
    yield

    for p in range(N_PAIRS):
        cs = slice(p * PAIR, (p + 1) * PAIR)
        r, k2, ka_vec, kb_vec, lw = stash[p]
        c = c_all[:, cs]
        c3 = c.reshape(nc, CHUNK, PAIR)
        c_mid = c3[:, CHUNK // 2 - 1:CHUNK // 2, :]
        c_end = c3[:, CHUNK - 1:CHUNK, :]
        cm = jnp.broadcast_to(c_mid, c3.shape).reshape(tm, PAIR)
        cT = jnp.broadcast_to(c_end, c3.shape).reshape(tm, PAIR)

        e_in = jnp.exp(c - cm)
        e_out = jnp.exp(cm - c)
        e_end = jnp.exp(cT - c)
        at_ref[:, cs] = (ka_vec * jnp.exp(c - lw - cm)).astype(BF16)
        rt_ref[:, cs] = (r * e_in).astype(BF16)
        bt_ref[:, cs] = (kb_vec * e_out).astype(BF16)
        kt_ref[:, cs] = (k2 * e_out).astype(BF16)
        bg_ref[:, cs] = (kb_vec * e_end).astype(BF16)
        kg_ref[:, cs] = (k2 * e_end).astype(BF16)
        gT_ref[:, :, cs] = jnp.exp(c_end)
        gm_ref[:, :, cs] = jnp.exp(c_mid)
        yield


def _rwkv_kernel(*refs, tb, nb):
    prep_in, (gnw_ref, gnb_ref, o_ref), scratch = refs[:20], refs[20:23], refs[23:]
    operands, s_ref = scratch[:11], scratch[11]
    j = pl.program_id(1)
    prep = functools.partial(_rwkv_prep_kernel, *prep_in, *operands, tm=tb, first=j == 0)
    chunk = functools.partial(_rwkv_chunk_kernel, *operands, gnw_ref, gnb_ref, o_ref, s_ref,
                              chunks=tb // CHUNK, pairs=N_PAIRS, first=j == 1)

    @pl.when(j == 0)
    def _():
        for _ in prep():
            pass

    @pl.when((j > 0) & (j < nb))
    def _():
        chunk(filler=prep())

    @pl.when(j == nb)
    def _():
        chunk()


def _rwkv(z, batch, seq, mu_r, mu_k, mu_v, mu_l, w0, a0, k_k, k_a, r_k, w2p, a2p, g2, gn_w, gn_b, tb=128):
    m = z.shape[0]
    nb = seq // tb
    pb = tb // BF16_SUBLANES
    blk = lambda b, j: b * nb + jnp.minimum(j, nb - 1)

    def cur(col0, width):
        return pl.BlockSpec((tb, width), lambda b, j, c0=col0 // width: (blk(b, j), c0))

    def prev(col0, width):
        return pl.BlockSpec((BF16_SUBLANES, width),
                            lambda b, j, c0=col0 // width: (jnp.maximum(blk(b, j) * pb - 1, 0), c0))

    def whole(rows, width):
        return pl.BlockSpec((rows, width), lambda b, j: (0, 0))

    in_specs = [cur(C_R, RW_DIM), cur(C_K, RW_DIM), cur(C_V, RW_DIM), cur(C_LORA, LORA_COLS),
                prev(C_R, RW_DIM), prev(C_K, RW_DIM), prev(C_V, RW_DIM), prev(C_LORA, LORA_COLS),
                whole(1, RW_DIM), whole(1, RW_DIM), whole(1, RW_DIM), whole(1, LORA_COLS),
                whole(1, RW_DIM), whole(1, RW_DIM), whole(1, RW_DIM), whole(1, RW_DIM), whole(1, RW_DIM),
                whole(LORA_W, RW_DIM), whole(LORA_A, RW_DIM), whole(GATE_LORA, RW_DIM),
                whole(1, RW_DIM), whole(1, RW_DIM)]
    big = pltpu.VMEM((tb, RW_DIM), BF16)
    per_chunk = pltpu.VMEM((tb // CHUNK, 1, RW_DIM), F32)
    return pl.pallas_call(
        functools.partial(_rwkv_kernel, tb=tb, nb=nb),
        grid=(batch, nb + 1),
        in_specs=in_specs,
        out_specs=pl.BlockSpec((tb, RW_DIM), lambda b, j: (b * nb + jnp.maximum(j - 1, 0), 0)),
        out_shape=jax.ShapeDtypeStruct((m, RW_DIM), BF16),
        scratch_shapes=[big] * 9 + [per_chunk] * 2 + [pltpu.VMEM((N_PAIRS, PAIR, PAIR), F32)],
        compiler_params=_cparams(("parallel", "arbitrary")),
        name="rwkv",
    )(z, z, z, z, z, z, z, z, mu_r, mu_k, mu_v, mu_l, w0, a0, k_k, k_a, r_k, w2p, a2p, g2, gn_w, gn_b)


def _rwkv_chunk_kernel(at_ref, rt_ref, bt_ref, kt_ref, bg_ref, kg_ref, v_ref, g_ref, bonus_ref,
                       gT_ref, gm_ref, gnw_ref, gnb_ref, o_ref, s_ref, *, chunks, pairs, first,
                       filler=None):
    @pl.when(first)
    def _():
        s_ref[...] = jnp.zeros_like(s_ref)

    T = CHUNK
    lane = lax.broadcasted_iota(jnp.int32, (T, PAIR), 1)
    head0 = lane < RW_HEAD
    ri = lax.broadcasted_iota(jnp.int32, (2 * T, 2 * T), 0)
    ci = lax.broadcasted_iota(jnp.int32, (2 * T, 2 * T), 1)
    same = (ri // T) == (ci // T)
    strict = same & ((ri % T) > (ci % T))
    incl = same & ((ri % T) >= (ci % T))
    eye = jnp.where(ri == ci, 1.0, 0.0)
    own = (ri // T) == (ci // RW_HEAD)

    items = [(p, c) for c in range(chunks) for p in range(pairs)]
    rows = lambda c: slice(c * T, (c + 1) * T)
    cols = lambda p: slice(p * PAIR, (p + 1) * PAIR)

    def stack(ref):
        out = []
        for p, c in items:
            x = ref[rows(c), cols(p)]
            zero = jnp.zeros_like(x)
            out.append(jnp.concatenate([jnp.where(head0, x, zero), jnp.where(head0, zero, x)], axis=0))
        return out

    each = lambda f, *ls: [f(*xs) for xs in zip(*ls)]
    La, Lr, Rb, Rk = stack(at_ref), stack(rt_ref), stack(bt_ref), stack(kt_ref)
    Rbg, Rkg, Vs = stack(bg_ref), stack(kg_ref), stack(v_ref)
    gm_row = [gm_ref[c][:, cols(p)] for p, c in items]
    gT_row = [gT_ref[c][:, cols(p)] for p, c in items]
    gate = [g_ref[rows(c), cols(p)] for p, c in items]
    bonus = [bonus_ref[rows(c), cols(p)] for p, c in items]
    filler = iter(()) if filler is None else filler
    tick = lambda: next(filler, None)

    AA = each(lambda la, lr, rb, rk: _mm_nt(jnp.concatenate([la, lr], axis=0),
                                            jnp.concatenate([rb, rk], axis=0)), La, Lr, Rb, Rk)
    tick()
    N = [jnp.where(strict, aa[:2 * T, :2 * T], 0.0) for aa in AA]
    Aak = [jnp.where(strict, aa[:2 * T, 2 * T:], 0.0) for aa in AA]
    Arb = [jnp.where(incl, aa[2 * T:, :2 * T], 0.0) for aa in AA]
    Ark = [jnp.where(incl, aa[2 * T:, 2 * T:], 0.0) for aa in AA]
    def live_rows(x, t0):
        return x if t0 == 0 else jnp.concatenate([x[t0:T, :], x[T + t0:, :]], axis=0)

    def all_rows(y, t0):
        if t0 == 0:
            return y
        zero = jnp.zeros((t0, y.shape[1]), y.dtype)
        return jnp.concatenate([zero, y[:T - t0, :], zero, y[T - t0:, :]], axis=0)

    W = each(lambda n: eye + n, N)
    P = N
    span = 2
    while span < T:
        t0 = span if span % BF16_SUBLANES == 0 else 0
        P = each(lambda p_: all_rows(_mm(live_rows(p_, t0), p_), t0), P)
        tick()
        W = each(lambda w, p_: w + all_rows(_mm(live_rows(w, t0), p_), t0), W, P)
        tick()
        span *= 2
    AkV = each(_mm, Aak, Vs)
    tick()
    X = each(lambda w, la, akv: _mm(w, jnp.concatenate([la, akv.astype(BF16)], axis=1)).astype(BF16),
             W, La, AkV)
    tick()
    Z = each(lambda arb, ark, x, vs:
             _mm(jnp.concatenate([arb.astype(BF16), ark.astype(BF16)], axis=1),
                 jnp.concatenate([x, jnp.concatenate([jnp.zeros_like(vs), vs], axis=1)], axis=0)),
             Arb, Ark, X, Vs)
    tick()
    Q = each(lambda z, lr, gm: (z[:, :PAIR] + lr.astype(F32)) * gm, Z, Lr, gm_row)
    Y0 = [z[:, PAIR:] for z in Z]
    tick()
    Mbd = each(lambda rbg, x, gm: _mm_tn(rbg, x[:, :PAIR]) * gm, Rbg, X, gm_row)
    NcT = each(lambda x, vs, rbg, rkg: _mm_tn(jnp.concatenate([x[:, PAIR:], vs], axis=0),
                                              jnp.concatenate([rbg, rkg], axis=0)), X, Vs, Rbg, Rkg)
    for _ in filler:
        pass

    S = [s_ref[p] for p in range(pairs)]
    for c in range(chunks):
        for p in range(pairs):
            i = c * pairs + p
            Ys = _mm_nt(Q[i], S[p]) + Y0[i]
            S[p] = S[p] * gT_row[i] + _mm_nt(S[p], Mbd[i]) + NcT[i]
            mu = jnp.sum(Ys, axis=-1, keepdims=True) * (1.0 / RW_HEAD)
            d = jnp.where(own, Ys - mu, 0.0)
            var = jnp.sum(d * d, axis=-1, keepdims=True) * (1.0 / RW_HEAD)
            dn = d * lax.rsqrt(var + GN_EPS)
            yn = (dn[:T, :] + dn[T:, :]) * gnw_ref[:, cols(p)] + gnb_ref[:, cols(p)]
            out = (yn + bonus[i].astype(F32)) * gate[i].astype(F32)
            o_ref[rows(c), cols(p)] = out.astype(BF16)
    for p in range(pairs):
        s_ref[p] = S[p]


def _swa_kernel(sink_ref, q_ref, kc_ref, kp_ref, vc_ref, vp_ref, qg_ref, kg_ref, o_ref, *,
                blocks_per_seq, slopes):
    first = (pl.program_id(0) % blocks_per_seq) == 0
    mean_h = _head_ones(LANES, ATT_HEAD, 1.0 / ATT_HEAD)

    def head_rms(x, gain):
        parts = []
        for b in range(x.shape[1] // LANES):
            xb = x[:, b * LANES:(b + 1) * LANES]
            ms = _mm_split_r(xb * xb, mean_h)
            parts.append(xb * lax.rsqrt(ms + RMS_EPS))
        return jnp.concatenate(parts, axis=1) * gain

    q = head_rms(q_ref[...].astype(F32), qg_ref[...]) * (ATT_HEAD ** -0.5 * LOG2E)
    q = q.astype(BF16)
    kcat = jnp.concatenate([kp_ref[...], kc_ref[...]], axis=0).astype(F32)
    kcat = head_rms(kcat, kg_ref[...]).astype(BF16)
    vcat = jnp.concatenate([vp_ref[...], vc_ref[...]], axis=0)

    qi = lax.broadcasted_iota(jnp.int32, (BLOCK, 2 * BLOCK), 0)
    kj = lax.broadcasted_iota(jnp.int32, (BLOCK, 2 * BLOCK), 1)
    dist_i = BLOCK + qi - kj
    first_key = jnp.where(first, BLOCK, 0)
    valid = (dist_i >= 0) & (dist_i < WINDOW) & (kj >= first_key)
    neg_dist = jnp.where(valid, -dist_i.astype(F32), NEG_BIG)

    outs = []
    for h in range(ATT_HEADS):
        j = h // ATT_GROUP
        qh = q[:, h * ATT_HEAD:(h + 1) * ATT_HEAD]
        kh = kcat[:, j * ATT_HEAD:(j + 1) * ATT_HEAD]
        vh = vcat[:, j * ATT_HEAD:(j + 1) * ATT_HEAD]
        s = _mm_nt(qh, kh) + (slopes[h] * LOG2E) * neg_dist
        sink = sink_ref[h] * LOG2E
        mx = jnp.maximum(jnp.max(s, axis=-1, keepdims=True), sink)
        p = jnp.exp2(s - mx)
        denom = jnp.sum(p, axis=-1, keepdims=True) + jnp.exp2(sink - mx)
        outs.append(_mm(p, vh) * (1.0 / denom))
    o_ref[...] = jnp.concatenate(outs, axis=1).astype(BF16)


def _swa(z, sinks, q_gain_t, k_gain_t, seq):
    m = z.shape[0]
    nblk = m // BLOCK
    bps = seq // BLOCK
    slopes = tuple(float(s) for s in
                   np.exp2(-8.0 * np.arange(1, ATT_HEADS + 1, dtype=np.float32) / ATT_HEADS).astype(np.float32))
    kv_cur = lambda c0: pl.BlockSpec((BLOCK, ATT_KV_DIM), lambda n, c=c0 // ATT_KV_DIM: (n, c))
    kv_prev = lambda c0: pl.BlockSpec((BLOCK, ATT_KV_DIM),
                                      lambda n, c=c0 // ATT_KV_DIM: (jnp.maximum(n - 1, 0), c))
    return pl.pallas_call(
        functools.partial(_swa_kernel, blocks_per_seq=bps, slopes=slopes),
        grid=(nblk,),
        in_specs=[pl.BlockSpec(memory_space=pltpu.SMEM),
                  pl.BlockSpec((BLOCK, ATT_Q_DIM), lambda n: (n, C_Q // ATT_Q_DIM)),
                  kv_cur(C_KA), kv_prev(C_KA), kv_cur(C_VA), kv_prev(C_VA),
                  pl.BlockSpec((1, ATT_Q_DIM), lambda n: (0, 0)),
                  pl.BlockSpec((1, ATT_KV_DIM), lambda n: (0, 0))],
        out_specs=pl.BlockSpec((BLOCK, ATT_Q_DIM), lambda n: (n, 0)),
        out_shape=jax.ShapeDtypeStruct((m, ATT_Q_DIM), BF16),
        compiler_params=_cparams(("parallel",)),
        name="swa",
    )(sinks, z, z, z, z, z, q_gain_t, k_gain_t)


def _mix_out_kernel(orw_ref, oatt_ref, wb1_ref, wb2_ref, zg_ref, wo_ref, x_ref, g_ref, x1_ref, h2_ref):
    p_rw = jnp.dot(orw_ref[...], wb1_ref[...], preferred_element_type=F32)
    p_att = jnp.dot(oatt_ref[...], wb2_ref[...], preferred_element_type=F32)
    g_rw = jax.nn.sigmoid(zg_ref[:, :D_MODEL].astype(F32))
    g_att = jax.nn.sigmoid(zg_ref[:, D_MODEL:].astype(F32))
    mix = (g_rw * p_rw + g_att * p_att).astype(BF16)
    x1 = x_ref[...] + jnp.dot(mix, wo_ref[...], preferred_element_type=F32)
    x1_ref[...] = x1
    y = x1 * lax.rsqrt(jnp.mean(x1 * x1, axis=-1, keepdims=True) + RMS_EPS)
    h2_ref[...] = (y * g_ref[...]).astype(BF16)


def _mix_out(o_rw, o_att, wb1, wb2, z, w_out, x2, g2, tm=256):
    m = x2.shape[0]
    const = lambda rows, cols: pl.BlockSpec((rows, cols), lambda i: (0, 0))
    rows = lambda cols, c0=0: pl.BlockSpec((tm, cols), lambda i: (i, c0))
    return pl.pallas_call(
        _mix_out_kernel,
        grid=(m // tm,),
        in_specs=[rows(RW_DIM), rows(ATT_Q_DIM), const(RW_DIM, D_MODEL), const(ATT_Q_DIM, D_MODEL),
                  rows(2 * D_MODEL, C_GRW // (2 * D_MODEL)), const(D_MODEL, D_MODEL),
                  rows(D_MODEL), const(1, D_MODEL)],
        out_specs=[rows(D_MODEL), rows(D_MODEL)],
        out_shape=[jax.ShapeDtypeStruct((m, D_MODEL), F32),
                   jax.ShapeDtypeStruct((m, D_MODEL), BF16)],
        compiler_params=_cparams(("parallel",)),
        name="mix_out",
    )(o_rw, o_att, wb1, wb2, z, w_out, x2, g2)


def _ffn_up_kernel(h_ref, wv_ref, wg_ref, cwv_ref, cwg_ref, cbv_ref, cbg_ref, wd_ref, a_ref, wdb_ref,
                   wvb_ref, wgb_ref, cv_ref, cg_ref, *, tm, sub, tiles_per_seq):
    halo = 8
    wdb_ref[...] = wd_ref[...].astype(BF16)

    @pl.when(pl.program_id(1) == 0)
    def _():
        wvb_ref[...] = wv_ref[...].astype(BF16)
        wgb_ref[...] = wg_ref[...].astype(BF16)

    @pl.when((pl.program_id(1) % tiles_per_seq) == 0)
    def _():
        cv_ref[...] = jnp.zeros_like(cv_ref)
        cg_ref[...] = jnp.zeros_like(cg_ref)

    row = lax.broadcasted_iota(jnp.int32, (sub, a_ref.shape[1]), 0)

    def conv(u, carry_ref, cw_ref, cb_ref):
        p1 = carry_ref[halo - 1:halo, :]
        p2 = carry_ref[halo - 2:halo - 1, :]
        prev1 = jnp.where(row == 0, p1, pltpu.roll(u, 1, 0))
        prev2 = jnp.where(row == 0, p2, jnp.where(row == 1, p1, pltpu.roll(u, 2, 0)))
        carry_ref[...] = u[sub - halo:sub, :]
        return prev2 * cw_ref[0:1, :] + prev1 * cw_ref[1:2, :] + u * cw_ref[2:3, :] + cb_ref[...]

    for s in range(tm // sub):
        h = h_ref[s * sub:(s + 1) * sub, :]
        val = conv(jnp.dot(h, wvb_ref[...], preferred_element_type=F32), cv_ref, cwv_ref, cbv_ref)
        gate = conv(jnp.dot(h, wgb_ref[...], preferred_element_type=F32), cg_ref, cwg_ref, cbg_ref)
        a_ref[s * sub:(s + 1) * sub, :] = (gate * jax.nn.sigmoid(gate) * val).astype(BF16)


def _ffn_up(h2, w_up, conv_w, conv_b, w_down, seq, tm=1024, tn=512, sub=512):
    m = h2.shape[0]
    nj = D_FF // tn
    ni = m // tm
    wd_rows = w_down.shape[0] // (nj * ni)
    wd_spec = pl.BlockSpec((wd_rows, D_MODEL), lambda j, i: (j * ni + i, 0))
    return pl.pallas_call(
        functools.partial(_ffn_up_kernel, tm=tm, sub=sub, tiles_per_seq=seq // tm),
        grid=(nj, ni),
        in_specs=[pl.BlockSpec((tm, D_MODEL), lambda j, i: (i, 0)),
                  pl.BlockSpec((D_MODEL, tn), lambda j, i: (0, j)),
                  pl.BlockSpec((D_MODEL, tn), lambda j, i: (0, nj + j)),
                  pl.BlockSpec((3, tn), lambda j, i: (0, j)),
                  pl.BlockSpec((3, tn), lambda j, i: (0, nj + j)),
                  pl.BlockSpec((1, tn), lambda j, i: (0, j)),
                  pl.BlockSpec((1, tn), lambda j, i: (0, nj + j)),
                  wd_spec],
        out_specs=[pl.BlockSpec((tm, tn), lambda j, i: (i, j)), wd_spec],
        out_shape=[jax.ShapeDtypeStruct((m, D_FF), BF16), jax.ShapeDtypeStruct(w_down.shape, BF16)],
        scratch_shapes=[pltpu.VMEM((D_MODEL, tn), BF16), pltpu.VMEM((D_MODEL, tn), BF16),
                        pltpu.VMEM((8, tn), F32), pltpu.VMEM((8, tn), F32)],
        compiler_params=_cparams(("parallel", "arbitrary")),
        name="ffn_up",
    )(h2, w_up, w_up, conv_w, conv_w, conv_b, conv_b, w_down)


def _ffn_down_kernel(a_ref, w_ref, x1_ref, o_ref):
    o_ref[...] = x1_ref[...] + jnp.dot(a_ref[...], w_ref[...], preferred_element_type=F32)


def _ffn_down(act, w_down_b, x1, tm=512, tn=1024):
    m = act.shape[0]
    return pl.pallas_call(
        _ffn_down_kernel,
        grid=(D_MODEL // tn, m // tm),
        in_specs=[pl.BlockSpec((tm, D_FF), lambda j, i: (i, 0)),
                  pl.BlockSpec((D_FF, tn), lambda j, i: (0, j)),
                  pl.BlockSpec((tm, tn), lambda j, i: (i, j))],
        out_specs=pl.BlockSpec((tm, tn), lambda j, i: (i, j)),
        out_shape=jax.ShapeDtypeStruct((m, D_MODEL), F32),
        compiler_params=_cparams(("parallel", "arbitrary")),
        name="ffn_down",
    )(act, w_down_b, x1)


def _pad_cols(w, n):
    return jnp.pad(w, ((0, 0), (0, n - w.shape[1])))


def _pad_rows(w, n):
    return jnp.pad(w, ((0, n - w.shape[0]), (0, 0)))


def _layer(x2, batch, seq, w_in_p, norm1_g, rw_mu, rw_w0, rw_w2, rw_a0, rw_a2, rw_g2, rw_k_k, rw_k_a,
           rw_r_k, rw_gn_w, rw_gn_b, q_norm_g, k_norm_g, attn_sinks, w_branch, w_out,
           norm2_g, w_up, conv_w, conv_b, w_down):
    row = lambda v: v.reshape(1, -1).astype(F32)
    o3 = 3 * RW_DIM
    o4 = o3 + DECAY_LORA
    o5 = o4 + ICLR_LORA
    o6 = o5 + GATE_LORA
    mu = rw_mu.reshape(1, -1)
    mu_l = jnp.concatenate([_pad_cols(mu[:, o3:o4], LORA_W), _pad_cols(mu[:, o4:o5], LORA_A),
                            mu[:, o5:o6]], axis=1)
    w2p = _pad_rows(rw_w2, LORA_W).astype(BF16)
    a2p = _pad_rows(rw_a2, LORA_A).astype(BF16)

    z = _in_proj(x2, row(norm1_g), w_in_p)
    o_rw = _rwkv(z, batch, seq, mu[:, :RW_DIM], mu[:, RW_DIM:2 * RW_DIM], mu[:, 2 * RW_DIM:o3], mu_l,
                 row(rw_w0), row(rw_a0), row(rw_k_k), row(rw_k_a), row(rw_r_k),
                 w2p, a2p, rw_g2.astype(BF16), row(rw_gn_w), row(rw_gn_b))
    o_att = _swa(z, attn_sinks.astype(F32), jnp.tile(row(q_norm_g), (1, ATT_HEADS)),
                 jnp.tile(row(k_norm_g), (1, ATT_KV_HEADS)), seq)
    x1, h2 = _mix_out(o_rw, o_att, w_branch[:RW_DIM].astype(BF16), w_branch[RW_DIM:].astype(BF16), z,
                      w_out.astype(BF16), x2, row(norm2_g))
    act, w_down_b = _ffn_up(h2, w_up, conv_w.astype(F32), row(conv_b), w_down, seq)
    return _ffn_down(act, w_down_b, x1)


def kernel(x, norm1_g, w_in, rw_mu, rw_w0, rw_w2, rw_a0, rw_a2, rw_g2, rw_k_k, rw_k_a, rw_r_k,
           rw_gn_w, rw_gn_b, q_norm_g, k_norm_g, attn_sinks, w_branch, w_out, norm2_g, w_up,
           conv_w, conv_b, w_down):
    batch, seq, d = x.shape
    x2 = x.reshape(batch * seq, d)
    params = (norm1_g, rw_mu, rw_w0, rw_w2, rw_a0, rw_a2, rw_g2, rw_k_k, rw_k_a, rw_r_k,
              rw_gn_w, rw_gn_b, q_norm_g, k_norm_g, attn_sinks, w_branch, w_out, norm2_g, w_up,
              conv_w, conv_b, w_down)
    for layer in range(norm1_g.shape[0]):
        x2 = _layer(x2, batch, seq, _w_in_layout(w_in, layer), *(p[layer] for p in params))
    return x2.reshape(batch, seq, d)
```

```python
import functools

import jax
import jax.numpy as jnp
import numpy as np
from jax import lax
from jax.experimental import pallas as pl
from jax.experimental.pallas import tpu as pltpu

F32 = jnp.float32
BF16 = jnp.bfloat16

LANES = 128
BF16_SUBLANES = 16
VMEM_LIMIT = 56 * 1024 * 1024

D_MODEL = 2048
RW_HEADS = 16
RW_HEAD = 64
RW_DIM = RW_HEADS * RW_HEAD
DECAY_LORA = 96
ICLR_LORA = 96
GATE_LORA = 256
GN_EPS = 64e-5
ATT_HEADS = 16
ATT_KV_HEADS = 4
ATT_GROUP = ATT_HEADS // ATT_KV_HEADS
ATT_HEAD = 64
ATT_Q_DIM = ATT_HEADS * ATT_HEAD
ATT_KV_DIM = ATT_KV_HEADS * ATT_HEAD
WINDOW = 128
BLOCK = 128
D_FF = 5632
RMS_EPS = 1e-6
NEG_BIG = -1e30
LOG2E = 1.4426950408889634

C_R = 0
C_K = RW_DIM
C_V = 2 * RW_DIM
C_Q = 3 * RW_DIM
C_GRW = C_Q + ATT_Q_DIM
C_GATT = C_GRW + D_MODEL
C_LORA = C_GATT + D_MODEL
LORA_W = 128
LORA_A = 128
LORA_COLS = LORA_W + LORA_A + GATE_LORA
C_KA = C_LORA + LORA_COLS
C_VA = C_KA + ATT_KV_DIM
Z_COLS = C_VA + ATT_KV_DIM

CHUNK = 64
PAIR = 2 * RW_HEAD
N_PAIRS = RW_DIM // PAIR


def _cparams(sem):
    return pltpu.CompilerParams(dimension_semantics=sem, vmem_limit_bytes=VMEM_LIMIT)


def _mm(a, b):
    return jnp.dot(a.astype(BF16), b.astype(BF16), preferred_element_type=F32)


def _mm_nt(a, b):
    return lax.dot_general(a.astype(BF16), b.astype(BF16), (((1,), (1,)), ((), ())),
                           preferred_element_type=F32)


def _mm_tn(a, b):
    return lax.dot_general(a.astype(BF16), b.astype(BF16), (((0,), (0,)), ((), ())),
                           preferred_element_type=F32)


def _mm_split(m01, x):
    hi = x.astype(BF16)
    lo = (x - hi.astype(F32)).astype(BF16)
    return (jnp.dot(m01, hi, preferred_element_type=F32)
            + jnp.dot(m01, lo, preferred_element_type=F32))


def _mm_split_r(x, m01):
    hi = x.astype(BF16)
    lo = (x - hi.astype(F32)).astype(BF16)
    return (jnp.dot(hi, m01, preferred_element_type=F32)
            + jnp.dot(lo, m01, preferred_element_type=F32))


def _head_ones(n, head, scale):
    r = lax.broadcasted_iota(jnp.int32, (n, n), 0) // head
    c = lax.broadcasted_iota(jnp.int32, (n, n), 1) // head
    return jnp.where(r == c, scale, 0.0).astype(BF16)


def _w_in_layout_kernel(wt_ref, o_ref, *, moves, pad):
    o_ref[pad[0]:pad[1], :] = jnp.zeros((pad[1] - pad[0], o_ref.shape[1]), BF16)
    for dst, src, width in moves:
        o_ref[dst:dst + width, :] = wt_ref[src:src + width, :].astype(BF16)


def _w_in_layout(w_in_layers, layer, cols=256):
    _, d, n = w_in_layers.shape
    wt_layers = jnp.swapaxes(w_in_layers, 1, 2)
    o3 = 3 * RW_DIM
    o4 = o3 + DECAY_LORA
    o5 = o4 + ICLR_LORA
    o6 = o5 + GATE_LORA
    oq = o6 + ATT_Q_DIM
    ok = oq + ATT_KV_DIM
    ov = ok + ATT_KV_DIM
    moves = ((C_R, 0, o3), (C_Q, o6, ATT_Q_DIM), (C_GRW, ov, 2 * D_MODEL),
             (C_LORA, o3, DECAY_LORA), (C_LORA + LORA_W, o4, ICLR_LORA),
             (C_LORA + LORA_W + LORA_A, o5, GATE_LORA), (C_KA, oq, ATT_KV_DIM), (C_VA, ok, ATT_KV_DIM))
    return pl.pallas_call(
        functools.partial(_w_in_layout_kernel, moves=moves, pad=(C_LORA, C_LORA + LORA_W + LORA_A)),
        grid=(d // cols,),
        in_specs=[pl.BlockSpec((None, n, cols), lambda i: (layer, 0, i))],
        out_specs=pl.BlockSpec((Z_COLS, cols), lambda i: (0, i)),
        out_shape=jax.ShapeDtypeStruct((Z_COLS, d), BF16),
        compiler_params=_cparams(("parallel",)),
        name="w_in_layout",
    )(wt_layers)


def _in_proj_kernel(x_ref, g_ref, wt_ref, *refs, n_side):
    side_in, z_ref, side_out, h_ref = refs[:n_side], refs[n_side], refs[n_side + 1:-1], refs[-1]

    @pl.when(pl.program_id(1) == 0)
    def _():
        xf = x_ref[...]
        y = xf * lax.rsqrt(jnp.mean(xf * xf, axis=-1, keepdims=True) + RMS_EPS)
        h_ref[...] = (y * g_ref[...]).astype(BF16)

    z_ref[...] = _mm_nt(h_ref[...], wt_ref[...]).astype(BF16)
    for w_ref, wb_ref in zip(side_in, side_out):
        wb_ref[...] = w_ref[...].astype(BF16)


def _in_proj(x2, g1, w_in_t, side_weights, tm=1024, tn=1024, side_rows=32):
    m, d = x2.shape
    n = w_in_t.shape[0]
    nj = n // tn
    for w in side_weights:
        assert w.shape[0] // side_rows <= (m // tm) * nj, "not enough grid steps to cast this weight"
    side_specs = [pl.BlockSpec((side_rows, w.shape[1]),
                               lambda i, j, last=w.shape[0] // side_rows - 1: (jnp.minimum(i * nj + j, last), 0))
                  for w in side_weights]
    outs = pl.pallas_call(
        functools.partial(_in_proj_kernel, n_side=len(side_weights)),
        grid=(m // tm, nj),
        in_specs=[pl.BlockSpec((tm, d), lambda i, j: (i, 0)),
                  pl.BlockSpec((1, d), lambda i, j: (0, 0)),
                  pl.BlockSpec((tn, d), lambda i, j: (j, 0))] + side_specs,
        out_specs=[pl.BlockSpec((tm, tn), lambda i, j: (i, j))] + side_specs,
        out_shape=[jax.ShapeDtypeStruct((m, n), BF16)]
        + [jax.ShapeDtypeStruct(w.shape, BF16) for w in side_weights],
        scratch_shapes=[pltpu.VMEM((tm, d), BF16)],
        compiler_params=_cparams(("parallel", "arbitrary")),
        name="in_proj",
    )(x2, g1, w_in_t, *side_weights)
    return outs[0], outs[1:]


def _rwkv_prep_kernel(zr_ref, zk_ref, zv_ref, zl_ref, pr_ref, pk_ref, pv_ref, pli_ref,
                      mur_ref, muk_ref, muv_ref, mul_ref,
                      w0_ref, a0_ref, kk_ref, ka_ref, rk_ref, w2_ref, a2_ref, g2_ref,
                      at_ref, rt_ref, bt_ref, kt_ref, bg_ref, kg_ref, v_ref, g_ref, bonus_ref,
                      gT_ref, gm_ref, *, tm, first):
    keep = jnp.where(first, 0.0, 1.0)
    nc = tm // CHUNK
    last = slice(BF16_SUBLANES - 1, BF16_SUBLANES)

    def shifted(z, prev_last, mu):
        z = z.astype(F32)
        row = lax.broadcasted_iota(jnp.int32, z.shape, 0)
        zprev = jnp.where(row == 0, prev_last.astype(F32) * keep, pltpu.roll(z, 1, 0))
        return z + (zprev - z) * mu

    lo = shifted(zl_ref[...], pli_ref[last, :], mul_ref[...])
    tanh_wd = jnp.tanh(lo[:, :LORA_W]).astype(BF16)
    ad = lo[:, LORA_W:LORA_W + LORA_A].astype(BF16)
    sig_gd = jax.nn.sigmoid(lo[:, LORA_W + LORA_A:]).astype(BF16)

    ti = lax.broadcasted_iota(jnp.int32, (tm, tm), 0)
    si = lax.broadcasted_iota(jnp.int32, (tm, tm), 1)
    tri = jnp.where(((ti // CHUNK) == (si // CHUNK)) & (si <= ti), 1.0, 0.0).astype(BF16)
    ones_h = _head_ones(PAIR, RW_HEAD, 1.0)
    w_lora = jnp.dot(tanh_wd, w2_ref[...], preferred_element_type=F32)
    a_lora = jnp.dot(ad, a2_ref[...], preferred_element_type=F32)
    g_ref[...] = jnp.dot(sig_gd, g2_ref[...], preferred_element_type=F32).astype(BF16)
    yield

    stash = []
    for p in range(N_PAIRS):
        cs = slice(p * PAIR, (p + 1) * PAIR)
        r = shifted(zr_ref[:, cs], pr_ref[last, cs], mur_ref[:, cs])
        k = shifted(zk_ref[:, cs], pk_ref[last, cs], muk_ref[:, cs])
        v = shifted(zv_ref[:, cs], pv_ref[last, cs], muv_ref[:, cs])

        wpre = w0_ref[:, cs] + w_lora[:, cs]
        w = jnp.minimum(wpre, 0.0) - jnp.log1p(jnp.exp(-jnp.abs(wpre))) - 0.5
        lw = -jnp.exp(w)
        a = jax.nn.sigmoid(a0_ref[:, cs] + a_lora[:, cs])

        kk = k * kk_ref[:, cs]
        kk = kk * lax.rsqrt(jnp.maximum(_mm(kk * kk, ones_h), 1e-24))
        k2 = k * (1.0 + (a - 1.0) * ka_ref[:, cs])
        bonus_ref[:, cs] = (_mm(r * k2 * rk_ref[:, cs], ones_h) * v).astype(BF16)
        v_ref[:, cs] = v.astype(BF16)
        stash.append((r, k2, -kk, kk * a, lw))
        yield

    c_all = _mm_split(tri, jnp.concatenate([st[4] for st in stash], axis=1))
    yield

    for p in range(N_PAIRS):
        cs = slice(p * PAIR, (p + 1) * PAIR)
        r, k2, ka_vec, kb_vec, lw = stash[p]
        c = c_all[:, cs]
        c3 = c.reshape(nc, CHUNK, PAIR)
        c_mid = c3[:, CHUNK // 2 - 1:CHUNK // 2, :]
        c_end = c3[:, CHUNK - 1:CHUNK, :]
        cm = jnp.broadcast_to(c_mid, c3.shape).reshape(tm, PAIR)
        cT = jnp.broadcast_to(c_end, c3.shape).reshape(tm, PAIR)

        e_in = jnp.exp(c - cm)
        e_out = jnp.exp(cm - c)
        e_end = jnp.exp(cT - c)
        at_ref[:, cs] = (ka_vec * jnp.exp(c - lw - cm)).astype(BF16)
        rt_ref[:, cs] = (r * e_in).astype(BF16)
        bt_ref[:, cs] = (kb_vec * e_out).astype(BF16)
        kt_ref[:, cs] = (k2 * e_out).astype(BF16)
        bg_ref[:, cs] = (kb_vec * e_end).astype(BF16)
        kg_ref[:, cs] = (k2 * e_end).astype(BF16)
        gT_ref[:, :, cs] = jnp.exp(c_end)
        gm_ref[:, :, cs] = jnp.exp(c_mid)
        yield


def _rwkv_kernel(*refs, tb, nb):
    prep_in, (gnw_ref, gnb_ref, o_ref), scratch = refs[:20], refs[20:23], refs[23:]
    operands, s_ref = scratch[:11], scratch[11]
    j = pl.program_id(1)
    prep = functools.partial(_rwkv_prep_kernel, *prep_in, *operands, tm=tb, first=j == 0)
    chunk = functools.partial(_rwkv_chunk_kernel, *operands, gnw_ref, gnb_ref, o_ref, s_ref,
                              chunks=tb // CHUNK, pairs=N_PAIRS, first=j == 1)

    @pl.when(j == 0)
    def _():
        for _ in prep():
            pass

    @pl.when((j > 0) & (j < nb))
    def _():
        chunk(filler=prep())

    @pl.when(j == nb)
    def _():
        chunk()


def _rwkv(z, batch, seq, mu_r, mu_k, mu_v, mu_l, w0, a0, k_k, k_a, r_k, w2p, a2p, g2, gn_w, gn_b, tb=128):
    m = z.shape[0]
    nb = seq // tb
    pb = tb // BF16_SUBLANES
    blk = lambda b, j: b * nb + jnp.minimum(j, nb - 1)

    def cur(col0, width):
        return pl.BlockSpec((tb, width), lambda b, j, c0=col0 // width: (blk(b, j), c0))

    def prev(col0, width):
        return pl.BlockSpec((BF16_SUBLANES, width),
                            lambda b, j, c0=col0 // width: (jnp.maximum(blk(b, j) * pb - 1, 0), c0))

    def whole(rows, width):
        return pl.BlockSpec((rows, width), lambda b, j: (0, 0))

    in_specs = [cur(C_R, RW_DIM), cur(C_K, RW_DIM), cur(C_V, RW_DIM), cur(C_LORA, LORA_COLS),
                prev(C_R, RW_DIM), prev(C_K, RW_DIM), prev(C_V, RW_DIM), prev(C_LORA, LORA_COLS),
                whole(1, RW_DIM), whole(1, RW_DIM), whole(1, RW_DIM), whole(1, LORA_COLS),
                whole(1, RW_DIM), whole(1, RW_DIM), whole(1, RW_DIM), whole(1, RW_DIM), whole(1, RW_DIM),
                whole(LORA_W, RW_DIM), whole(LORA_A, RW_DIM), whole(GATE_LORA, RW_DIM),
                whole(1, RW_DIM), whole(1, RW_DIM)]
    big = pltpu.VMEM((tb, RW_DIM), BF16)
    per_chunk = pltpu.VMEM((tb // CHUNK, 1, RW_DIM), F32)
    return pl.pallas_call(
        functools.partial(_rwkv_kernel, tb=tb, nb=nb),
        grid=(batch, nb + 1),
        in_specs=in_specs,
        out_specs=pl.BlockSpec((tb, RW_DIM), lambda b, j: (b * nb + jnp.maximum(j - 1, 0), 0)),
        out_shape=jax.ShapeDtypeStruct((m, RW_DIM), BF16),
        scratch_shapes=[big] * 9 + [per_chunk] * 2 + [pltpu.VMEM((N_PAIRS, PAIR, PAIR), F32)],
        compiler_params=_cparams(("parallel", "arbitrary")),
        name="rwkv",
    )(z, z, z, z, z, z, z, z, mu_r, mu_k, mu_v, mu_l, w0, a0, k_k, k_a, r_k, w2p, a2p, g2, gn_w, gn_b)


def _rwkv_chunk_kernel(at_ref, rt_ref, bt_ref, kt_ref, bg_ref, kg_ref, v_ref, g_ref, bonus_ref,
                       gT_ref, gm_ref, gnw_ref, gnb_ref, o_ref, s_ref, *, chunks, pairs, first,
                       filler=None):
    @pl.when(first)
    def _():
        s_ref[...] = jnp.zeros_like(s_ref)

    T = CHUNK
    lane = lax.broadcasted_iota(jnp.int32, (T, PAIR), 1)
    head0 = lane < RW_HEAD
    ri = lax.broadcasted_iota(jnp.int32, (2 * T, 2 * T), 0)
    ci = lax.broadcasted_iota(jnp.int32, (2 * T, 2 * T), 1)
    same = (ri // T) == (ci // T)
    strict = same & ((ri % T) > (ci % T))
    incl = same & ((ri % T) >= (ci % T))
    eye = jnp.where(ri == ci, 1.0, 0.0)
    own = (ri // T) == (ci // RW_HEAD)

    items = [(p, c) for c in range(chunks) for p in range(pairs)]
    rows = lambda c: slice(c * T, (c + 1) * T)
    cols = lambda p: slice(p * PAIR, (p + 1) * PAIR)

    def stack(ref):
        out = []
        for p, c in items:
            x = ref[rows(c), cols(p)]
            zero = jnp.zeros_like(x)
            out.append(jnp.concatenate([jnp.where(head0, x, zero), jnp.where(head0, zero, x)], axis=0))
        return out

    each = lambda f, *ls: [f(*xs) for xs in zip(*ls)]
    La, Lr, Rb, Rk = stack(at_ref), stack(rt_ref), stack(bt_ref), stack(kt_ref)
    Rbg, Rkg, Vs = stack(bg_ref), stack(kg_ref), stack(v_ref)
    gm_row = [gm_ref[c][:, cols(p)] for p, c in items]
    gT_row = [gT_ref[c][:, cols(p)] for p, c in items]
    gate = [g_ref[rows(c), cols(p)] for p, c in items]
    bonus = [bonus_ref[rows(c), cols(p)] for p, c in items]
    filler = iter(()) if filler is None else filler
    tick = lambda: next(filler, None)

    AA = each(lambda la, lr, rb, rk: _mm_nt(jnp.concatenate([la, lr], axis=0),
                                            jnp.concatenate([rb, rk], axis=0)), La, Lr, Rb, Rk)
    tick()
    N = [jnp.where(strict, aa[:2 * T, :2 * T], 0.0) for aa in AA]
    Aak = [jnp.where(strict, aa[:2 * T, 2 * T:], 0.0) for aa in AA]
    Arb = [jnp.where(incl, aa[2 * T:, :2 * T], 0.0) for aa in AA]
    Ark = [jnp.where(incl, aa[2 * T:, 2 * T:], 0.0) for aa in AA]
    def live_rows(x, t0):
        return x if t0 == 0 else jnp.concatenate([x[t0:T, :], x[T + t0:, :]], axis=0)

    def all_rows(y, t0):
        if t0 == 0:
            return y
        zero = jnp.zeros((t0, y.shape[1]), y.dtype)
        return jnp.concatenate([zero, y[:T - t0, :], zero, y[T - t0:, :]], axis=0)

    W = each(lambda n: eye + n, N)
    P = N
    span = 2
    while span < T:
        t0 = span if span % BF16_SUBLANES == 0 else 0
        P = each(lambda p_: all_rows(_mm(live_rows(p_, t0), p_), t0), P)
        tick()
        W = each(lambda w, p_: w + all_rows(_mm(live_rows(w, t0), p_), t0), W, P)
        tick()
        span *= 2
    AkV = each(_mm, Aak, Vs)
    tick()
    X = each(lambda w, la, akv: _mm(w, jnp.concatenate([la, akv.astype(BF16)], axis=1)).astype(BF16),
             W, La, AkV)
    tick()
    Z = each(lambda arb, ark, x, vs:
             _mm(jnp.concatenate([arb.astype(BF16), ark.astype(BF16)], axis=1),
                 jnp.concatenate([x, jnp.concatenate([jnp.zeros_like(vs), vs], axis=1)], axis=0)),
             Arb, Ark, X, Vs)
    tick()
    Q = each(lambda z, lr, gm: (z[:, :PAIR] + lr.astype(F32)) * gm, Z, Lr, gm_row)
    Y0 = [z[:, PAIR:] for z in Z]
    tick()
    Mbd = each(lambda rbg, x, gm: _mm_tn(rbg, x[:, :PAIR]) * gm, Rbg, X, gm_row)
    NcT = each(lambda x, vs, rbg, rkg: _mm_tn(jnp.concatenate([x[:, PAIR:], vs], axis=0),
                                              jnp.concatenate([rbg, rkg], axis=0)), X, Vs, Rbg, Rkg)
    for _ in filler:
        pass

    S = [s_ref[p] for p in range(pairs)]
    for c in range(chunks):
        for p in range(pairs):
            i = c * pairs + p
            Ys = _mm_nt(Q[i], S[p]) + Y0[i]
            S[p] = S[p] * gT_row[i] + _mm_nt(S[p], Mbd[i]) + NcT[i]
            mu = jnp.sum(Ys, axis=-1, keepdims=True) * (1.0 / RW_HEAD)
            d = jnp.where(own, Ys - mu, 0.0)
            var = jnp.sum(d * d, axis=-1, keepdims=True) * (1.0 / RW_HEAD)
            dn = d * lax.rsqrt(var + GN_EPS)
            yn = (dn[:T, :] + dn[T:, :]) * gnw_ref[:, cols(p)] + gnb_ref[:, cols(p)]
            out = (yn + bonus[i].astype(F32)) * gate[i].astype(F32)
            o_ref[rows(c), cols(p)] = out.astype(BF16)
    for p in range(pairs):
        s_ref[p] = S[p]


def _swa_kernel(sink_ref, q_ref, kc_ref, kp_ref, vc_ref, vp_ref, qg_ref, kg_ref, o_ref, *,
                blocks_per_seq, slopes):
    first = (pl.program_id(0) % blocks_per_seq) == 0
    mean_h = _head_ones(LANES, ATT_HEAD, 1.0 / ATT_HEAD)

    def head_rms(x, gain):
        parts = []
        for b in range(x.shape[1] // LANES):
            xb = x[:, b * LANES:(b + 1) * LANES]
            ms = _mm(xb * xb, mean_h)
            parts.append(xb * lax.rsqrt(ms + RMS_EPS))
        return jnp.concatenate(parts, axis=1) * gain

    q = head_rms(q_ref[...].astype(F32), qg_ref[...]) * (ATT_HEAD ** -0.5 * LOG2E)
    q = q.astype(BF16)
    kcat = jnp.concatenate([kp_ref[...], kc_ref[...]], axis=0).astype(F32)
    kcat = head_rms(kcat, kg_ref[...]).astype(BF16)
    vcat = jnp.concatenate([vp_ref[...], vc_ref[...]], axis=0)

    qi = lax.broadcasted_iota(jnp.int32, (BLOCK, 2 * BLOCK), 0)
    kj = lax.broadcasted_iota(jnp.int32, (BLOCK, 2 * BLOCK), 1)
    dist_i = BLOCK + qi - kj
    first_key = jnp.where(first, BLOCK, 0)
    valid = (dist_i >= 0) & (dist_i < WINDOW) & (kj >= first_key)
    neg_dist = jnp.where(valid, -dist_i.astype(F32), NEG_BIG)

    lane_q = lax.broadcasted_iota(jnp.int32, (BLOCK, LANES), 1)
    lane_kv = lax.broadcasted_iota(jnp.int32, (2 * BLOCK, LANES), 1)
    zero_q = jnp.zeros((BLOCK, LANES), BF16)
    one_kv = jnp.ones((2 * BLOCK, LANES), BF16)
    k_blocks = [kcat[:, b * LANES:(b + 1) * LANES] for b in range(ATT_KV_DIM // LANES)]
    k_rolled = [pltpu.roll(kb, ATT_HEAD, 1) for kb in k_blocks]
    v_aug = [jnp.where((lane_kv // ATT_HEAD) == (j % 2),
                       vcat[:, (j // 2) * LANES:(j // 2 + 1) * LANES], one_kv) for j in range(ATT_KV_HEADS)]
    heads = range(ATT_HEADS)
    kv_of = [h // ATT_GROUP for h in heads]
    aligned = [(h % 2) == (kv_of[h] % 2) for h in heads]
    s = [_mm_nt(jnp.where((lane_q // ATT_HEAD) == (h % 2), q[:, (h // 2) * LANES:(h // 2 + 1) * LANES], zero_q),
                k_blocks[kv_of[h] // 2] if aligned[h] else k_rolled[kv_of[h] // 2])
         + (slopes[h] * LOG2E) * neg_dist for h in heads]
    sink = [sink_ref[h] * LOG2E for h in heads]
    mx = [jnp.maximum(jnp.max(s[h], axis=-1, keepdims=True), sink[h]) for h in heads]
    p = [jnp.exp2(s[h] - mx[h]) for h in heads]
    pv = [_mm(p[h], v_aug[kv_of[h]]) for h in heads]
    pv_sw = [pltpu.roll(pv[h], ATT_HEAD, 1) for h in heads]
    out = []
    for h in heads:
        num, den = (pv[h], pv_sw[h]) if aligned[h] else (pv_sw[h], pv[h])
        out.append(num * (1.0 / (den + jnp.exp2(sink[h] - mx[h]))))
    o_ref[...] = jnp.concatenate(
        [jnp.where((lane_q // ATT_HEAD) == 0, out[2 * m], out[2 * m + 1]) for m in range(ATT_HEADS // 2)],
        axis=1).astype(BF16)


def _swa(z, sinks, q_gain_t, k_gain_t, seq):
    m = z.shape[0]
    nblk = m // BLOCK
    bps = seq // BLOCK
    slopes = tuple(float(s) for s in
                   np.exp2(-8.0 * np.arange(1, ATT_HEADS + 1, dtype=np.float32) / ATT_HEADS).astype(np.float32))
    kv_cur = lambda c0: pl.BlockSpec((BLOCK, ATT_KV_DIM), lambda n, c=c0 // ATT_KV_DIM: (n, c))
    kv_prev = lambda c0: pl.BlockSpec((BLOCK, ATT_KV_DIM),
                                      lambda n, c=c0 // ATT_KV_DIM: (jnp.maximum(n - 1, 0), c))
    return pl.pallas_call(
        functools.partial(_swa_kernel, blocks_per_seq=bps, slopes=slopes),
        grid=(nblk,),
        in_specs=[pl.BlockSpec(memory_space=pltpu.SMEM),
                  pl.BlockSpec((BLOCK, ATT_Q_DIM), lambda n: (n, C_Q // ATT_Q_DIM)),
                  kv_cur(C_KA), kv_prev(C_KA), kv_cur(C_VA), kv_prev(C_VA),
                  pl.BlockSpec((1, ATT_Q_DIM), lambda n: (0, 0)),
                  pl.BlockSpec((1, ATT_KV_DIM), lambda n: (0, 0))],
        out_specs=pl.BlockSpec((BLOCK, ATT_Q_DIM), lambda n: (n, 0)),
        out_shape=jax.ShapeDtypeStruct((m, ATT_Q_DIM), BF16),
        compiler_params=_cparams(("parallel",)),
        name="swa",
    )(sinks, z, z, z, z, z, q_gain_t, k_gain_t)


def _mix_out_kernel(orw_ref, oatt_ref, wb1_ref, wb2_ref, zg_ref, wo_ref, x_ref, g_ref, x1_ref, h2_ref):
    p_rw = jnp.dot(orw_ref[...], wb1_ref[...], preferred_element_type=F32)
    p_att = jnp.dot(oatt_ref[...], wb2_ref[...], preferred_element_type=F32)
    g_rw = jax.nn.sigmoid(zg_ref[:, :D_MODEL].astype(F32))
    g_att = jax.nn.sigmoid(zg_ref[:, D_MODEL:].astype(F32))
    mix = (g_rw * p_rw + g_att * p_att).astype(BF16)
    x1 = x_ref[...] + jnp.dot(mix, wo_ref[...], preferred_element_type=F32)
    x1_ref[...] = x1
    y = x1 * lax.rsqrt(jnp.mean(x1 * x1, axis=-1, keepdims=True) + RMS_EPS)
    h2_ref[...] = (y * g_ref[...]).astype(BF16)


def _mix_out(o_rw, o_att, w_branch_b, z, w_out_b, x2, g2, tm=256):
    m = x2.shape[0]
    const = lambda rows, cols, r0=0: pl.BlockSpec((rows, cols), lambda i: (r0, 0))
    rows = lambda cols, c0=0: pl.BlockSpec((tm, cols), lambda i: (i, c0))
    return pl.pallas_call(
        _mix_out_kernel,
        grid=(m // tm,),
        in_specs=[rows(RW_DIM), rows(ATT_Q_DIM), const(RW_DIM, D_MODEL), const(ATT_Q_DIM, D_MODEL, 1),
                  rows(2 * D_MODEL, C_GRW // (2 * D_MODEL)), const(D_MODEL, D_MODEL),
                  rows(D_MODEL), const(1, D_MODEL)],
        out_specs=[rows(D_MODEL), rows(D_MODEL)],
        out_shape=[jax.ShapeDtypeStruct((m, D_MODEL), F32),
                   jax.ShapeDtypeStruct((m, D_MODEL), BF16)],
        compiler_params=_cparams(("parallel",)),
        name="mix_out",
    )(o_rw, o_att, w_branch_b, w_branch_b, z, w_out_b, x2, g2)


def _ffn_up_kernel(h_ref, wvb_ref, wgb_ref, cwv_ref, cwg_ref, cbv_ref, cbg_ref, wd_ref, a_ref, wdb_ref,
                   cv_ref, cg_ref, *, tm, sub, tiles_per_seq):
    halo = 8
    wdb_ref[...] = wd_ref[...].astype(BF16)

    @pl.when((pl.program_id(1) % tiles_per_seq) == 0)
    def _():
        cv_ref[...] = jnp.zeros_like(cv_ref)
        cg_ref[...] = jnp.zeros_like(cg_ref)

    row = lax.broadcasted_iota(jnp.int32, (sub, a_ref.shape[1]), 0)

    def conv(u, carry_ref, cw_ref, cb_ref):
        p1 = carry_ref[halo - 1:halo, :]
        p2 = carry_ref[halo - 2:halo - 1, :]
        prev1 = jnp.where(row == 0, p1, pltpu.roll(u, 1, 0))
        prev2 = jnp.where(row == 0, p2, jnp.where(row == 1, p1, pltpu.roll(u, 2, 0)))
        carry_ref[...] = u[sub - halo:sub, :]
        return prev2 * cw_ref[0:1, :] + prev1 * cw_ref[1:2, :] + u * cw_ref[2:3, :] + cb_ref[...]

    for s in range(tm // sub):
        h = h_ref[s * sub:(s + 1) * sub, :]
        val = conv(jnp.dot(h, wvb_ref[...], preferred_element_type=F32), cv_ref, cwv_ref, cbv_ref)
        gate = conv(jnp.dot(h, wgb_ref[...], preferred_element_type=F32), cg_ref, cwg_ref, cbg_ref)
        a_ref[s * sub:(s + 1) * sub, :] = (gate * jax.nn.sigmoid(gate) * val).astype(BF16)


def _ffn_up(h2, w_up_b, conv_w, conv_b, w_down, seq, tm=1024, tn=512, sub=256):
    m = h2.shape[0]
    nj = D_FF // tn
    ni = m // tm
    wd_rows = w_down.shape[0] // (nj * ni)
    wd_spec = pl.BlockSpec((wd_rows, D_MODEL), lambda j, i: (j * ni + i, 0))
    return pl.pallas_call(
        functools.partial(_ffn_up_kernel, tm=tm, sub=sub, tiles_per_seq=seq // tm),
        grid=(nj, ni),
        in_specs=[pl.BlockSpec((tm, D_MODEL), lambda j, i: (i, 0)),
                  pl.BlockSpec((D_MODEL, tn), lambda j, i: (0, j)),
                  pl.BlockSpec((D_MODEL, tn), lambda j, i: (0, nj + j)),
                  pl.BlockSpec((3, tn), lambda j, i: (0, j)),
                  pl.BlockSpec((3, tn), lambda j, i: (0, nj + j)),
                  pl.BlockSpec((1, tn), lambda j, i: (0, j)),
                  pl.BlockSpec((1, tn), lambda j, i: (0, nj + j)),
                  wd_spec],
        out_specs=[pl.BlockSpec((tm, tn), lambda j, i: (i, j)), wd_spec],
        out_shape=[jax.ShapeDtypeStruct((m, D_FF), BF16), jax.ShapeDtypeStruct(w_down.shape, BF16)],
        scratch_shapes=[pltpu.VMEM((8, tn), F32), pltpu.VMEM((8, tn), F32)],
        compiler_params=_cparams(("parallel", "arbitrary")),
        name="ffn_up",
    )(h2, w_up_b, w_up_b, conv_w, conv_w, conv_b, conv_b, w_down)


def _ffn_down_kernel(a_ref, w_ref, x1_ref, o_ref):
    o_ref[...] = x1_ref[...] + jnp.dot(a_ref[...], w_ref[...], preferred_element_type=F32)


def _ffn_down(act, w_down_b, x1, tm=512, tn=1024):
    m = act.shape[0]
    return pl.pallas_call(
        _ffn_down_kernel,
        grid=(D_MODEL // tn, m // tm),
        in_specs=[pl.BlockSpec((tm, D_FF), lambda j, i: (i, 0)),
                  pl.BlockSpec((D_FF, tn), lambda j, i: (0, j)),
                  pl.BlockSpec((tm, tn), lambda j, i: (i, j))],
        out_specs=pl.BlockSpec((tm, tn), lambda j, i: (i, j)),
        out_shape=jax.ShapeDtypeStruct((m, D_MODEL), F32),
        compiler_params=_cparams(("parallel", "arbitrary")),
        name="ffn_down",
    )(act, w_down_b, x1)


def _pad_cols(w, n):
    return jnp.pad(w, ((0, 0), (0, n - w.shape[1])))


def _pad_rows(w, n):
    return jnp.pad(w, ((0, n - w.shape[0]), (0, 0)))


def _layer(x2, batch, seq, w_in_p, norm1_g, rw_mu, rw_w0, rw_w2, rw_a0, rw_a2, rw_g2, rw_k_k, rw_k_a,
           rw_r_k, rw_gn_w, rw_gn_b, q_norm_g, k_norm_g, attn_sinks, w_branch, w_out,
           norm2_g, w_up, conv_w, conv_b, w_down):
    row = lambda v: v.reshape(1, -1).astype(F32)
    o3 = 3 * RW_DIM
    o4 = o3 + DECAY_LORA
    o5 = o4 + ICLR_LORA
    o6 = o5 + GATE_LORA
    mu = rw_mu.reshape(1, -1)
    mu_l = jnp.concatenate([_pad_cols(mu[:, o3:o4], LORA_W), _pad_cols(mu[:, o4:o5], LORA_A),
                            mu[:, o5:o6]], axis=1)
    w2p = _pad_rows(rw_w2, LORA_W).astype(BF16)
    a2p = _pad_rows(rw_a2, LORA_A).astype(BF16)

    z, (w_branch_b, w_out_b, w_up_b) = _in_proj(x2, row(norm1_g), w_in_p, (w_branch, w_out, w_up))
    o_rw = _rwkv(z, batch, seq, mu[:, :RW_DIM], mu[:, RW_DIM:2 * RW_DIM], mu[:, 2 * RW_DIM:o3], mu_l,
                 row(rw_w0), row(rw_a0), row(rw_k_k), row(rw_k_a), row(rw_r_k),
                 w2p, a2p, rw_g2.astype(BF16), row(rw_gn_w), row(rw_gn_b))
    o_att = _swa(z, attn_sinks.astype(F32), jnp.tile(row(q_norm_g), (1, ATT_HEADS)),
                 jnp.tile(row(k_norm_g), (1, ATT_KV_HEADS)), seq)
    x1, h2 = _mix_out(o_rw, o_att, w_branch_b, z, w_out_b, x2, row(norm2_g))
    act, w_down_b = _ffn_up(h2, w_up_b, conv_w.astype(F32), row(conv_b), w_down, seq)
    return _ffn_down(act, w_down_b, x1)


def kernel(x, norm1_g, w_in, rw_mu, rw_w0, rw_w2, rw_a0, rw_a2, rw_g2, rw_k_k, rw_k_a, rw_r_k,
           rw_gn_w, rw_gn_b, q_norm_g, k_norm_g, attn_sinks, w_branch, w_out, norm2_g, w_up,
           conv_w, conv_b, w_down):
    batch, seq, d = x.shape
    x2 = x.reshape(batch * seq, d)
    params = (norm1_g, rw_mu, rw_w0, rw_w2, rw_a0, rw_a2, rw_g2, rw_k_k, rw_k_a, rw_r_k,
              rw_gn_w, rw_gn_b, q_norm_g, k_norm_g, attn_sinks, w_branch, w_out, norm2_g, w_up,
              conv_w, conv_b, w_down)
    for layer in range(norm1_g.shape[0]):
        x2 = _layer(x2, batch, seq, _w_in_layout(w_in, layer), *(p[layer] for p in params))
    return x2.reshape(batch, seq, d)
```

```python
import functools

import jax
import jax.numpy as jnp
import numpy as np
from jax import lax
from jax.experimental import pallas as pl
from jax.experimental.pallas import tpu as pltpu

F32 = jnp.float32
BF16 = jnp.bfloat16

LANES = 128
BF16_SUBLANES = 16
VMEM_LIMIT = 56 * 1024 * 1024

D_MODEL = 2048
RW_HEADS = 16
RW_HEAD = 64
RW_DIM = RW_HEADS * RW_HEAD
DECAY_LORA = 96
ICLR_LORA = 96
GATE_LORA = 256
GN_EPS = 64e-5
ATT_HEADS = 16
ATT_KV_HEADS = 4
ATT_GROUP = ATT_HEADS // ATT_KV_HEADS
ATT_HEAD = 64
ATT_Q_DIM = ATT_HEADS * ATT_HEAD
ATT_KV_DIM = ATT_KV_HEADS * ATT_HEAD
WINDOW = 128
BLOCK = 128
D_FF = 5632
RMS_EPS = 1e-6
NEG_BIG = -1e30
LOG2E = 1.4426950408889634

C_R = 0
C_K = RW_DIM
C_V = 2 * RW_DIM
C_Q = 3 * RW_DIM
C_GRW = C_Q + ATT_Q_DIM
C_GATT = C_GRW + D_MODEL
C_LORA = C_GATT + D_MODEL
LORA_W = 128
LORA_A = 128
LORA_COLS = LORA_W + LORA_A + GATE_LORA
C_KA = C_LORA + LORA_COLS
C_VA = C_KA + ATT_KV_DIM
Z_COLS = C_VA + ATT_KV_DIM

CHUNK = 64
PAIR = 2 * RW_HEAD
N_PAIRS = RW_DIM // PAIR


def _cparams(sem):
    return pltpu.CompilerParams(dimension_semantics=sem, vmem_limit_bytes=VMEM_LIMIT)


def _mm(a, b):
    return jnp.dot(a.astype(BF16), b.astype(BF16), preferred_element_type=F32)


def _mm_nt(a, b):
    return lax.dot_general(a.astype(BF16), b.astype(BF16), (((1,), (1,)), ((), ())),
                           preferred_element_type=F32)


def _mm_tn(a, b):
    return lax.dot_general(a.astype(BF16), b.astype(BF16), (((0,), (0,)), ((), ())),
                           preferred_element_type=F32)


def _mm_split(m01, x):
    hi = x.astype(BF16)
    lo = (x - hi.astype(F32)).astype(BF16)
    return (jnp.dot(m01, hi, preferred_element_type=F32)
            + jnp.dot(m01, lo, preferred_element_type=F32))


def _mm_split_r(x, m01):
    hi = x.astype(BF16)
    lo = (x - hi.astype(F32)).astype(BF16)
    return (jnp.dot(hi, m01, preferred_element_type=F32)
            + jnp.dot(lo, m01, preferred_element_type=F32))


def _head_ones(n, head, scale):
    r = lax.broadcasted_iota(jnp.int32, (n, n), 0) // head
    c = lax.broadcasted_iota(jnp.int32, (n, n), 1) // head
    return jnp.where(r == c, scale, 0.0).astype(BF16)


def _w_in_layout_kernel(wt_ref, o_ref, *, moves, pad):
    o_ref[pad[0]:pad[1], :] = jnp.zeros((pad[1] - pad[0], o_ref.shape[1]), BF16)
    for dst, src, width in moves:
        o_ref[dst:dst + width, :] = wt_ref[src:src + width, :].astype(BF16)


def _w_in_layout(w_in_layers, layer, cols=256):
    _, d, n = w_in_layers.shape
    wt_layers = jnp.swapaxes(w_in_layers, 1, 2)
    o3 = 3 * RW_DIM
    o4 = o3 + DECAY_LORA
    o5 = o4 + ICLR_LORA
    o6 = o5 + GATE_LORA
    oq = o6 + ATT_Q_DIM
    ok = oq + ATT_KV_DIM
    ov = ok + ATT_KV_DIM
    moves = ((C_R, 0, o3), (C_Q, o6, ATT_Q_DIM), (C_GRW, ov, 2 * D_MODEL),
             (C_LORA, o3, DECAY_LORA), (C_LORA + LORA_W, o4, ICLR_LORA),
             (C_LORA + LORA_W + LORA_A, o5, GATE_LORA), (C_KA, oq, ATT_KV_DIM), (C_VA, ok, ATT_KV_DIM))
    return pl.pallas_call(
        functools.partial(_w_in_layout_kernel, moves=moves, pad=(C_LORA, C_LORA + LORA_W + LORA_A)),
        grid=(d // cols,),
        in_specs=[pl.BlockSpec((None, n, cols), lambda i: (layer, 0, i))],
        out_specs=pl.BlockSpec((Z_COLS, cols), lambda i: (0, i)),
        out_shape=jax.ShapeDtypeStruct((Z_COLS, d), BF16),
        compiler_params=_cparams(("parallel",)),
        name="w_in_layout",
    )(wt_layers)


def _in_proj_kernel(x_ref, g_ref, wt_ref, *refs, n_side):
    side_in, z_ref, side_out, h_ref = refs[:n_side], refs[n_side], refs[n_side + 1:-1], refs[-1]

    @pl.when(pl.program_id(1) == 0)
    def _():
        xf = x_ref[...]
        y = xf * lax.rsqrt(jnp.mean(xf * xf, axis=-1, keepdims=True) + RMS_EPS)
        h_ref[...] = (y * g_ref[...]).astype(BF16)

    z_ref[...] = _mm_nt(h_ref[...], wt_ref[...]).astype(BF16)
    for w_ref, wb_ref in zip(side_in, side_out):
        wb_ref[...] = w_ref[...].astype(BF16)


def _in_proj(x2, g1, w_in_t, side_weights, tm=1024, tn=1024, side_rows=32):
    m, d = x2.shape
    n = w_in_t.shape[0]
    nj = n // tn
    for w in side_weights:
        assert w.shape[0] // side_rows <= (m // tm) * nj, "not enough grid steps to cast this weight"
    side_specs = [pl.BlockSpec((side_rows, w.shape[1]),
                               lambda i, j, last=w.shape[0] // side_rows - 1: (jnp.minimum(i * nj + j, last), 0))
                  for w in side_weights]
    outs = pl.pallas_call(
        functools.partial(_in_proj_kernel, n_side=len(side_weights)),
        grid=(m // tm, nj),
        in_specs=[pl.BlockSpec((tm, d), lambda i, j: (i, 0)),
                  pl.BlockSpec((1, d), lambda i, j: (0, 0)),
                  pl.BlockSpec((tn, d), lambda i, j: (j, 0))] + side_specs,
        out_specs=[pl.BlockSpec((tm, tn), lambda i, j: (i, j))] + side_specs,
        out_shape=[jax.ShapeDtypeStruct((m, n), BF16)]
        + [jax.ShapeDtypeStruct(w.shape, BF16) for w in side_weights],
        scratch_shapes=[pltpu.VMEM((tm, d), BF16)],
        compiler_params=_cparams(("parallel", "arbitrary")),
        name="in_proj",
    )(x2, g1, w_in_t, *side_weights)
    return outs[0], outs[1:]


def _rwkv_prep_kernel(zr_ref, zk_ref, zv_ref, zl_ref, pr_ref, pk_ref, pv_ref, pli_ref,
                      mur_ref, muk_ref, muv_ref, mul_ref,
                      w0_ref, a0_ref, kk_ref, ka_ref, rk_ref, w2_ref, a2_ref, g2_ref,
                      at_ref, rt_ref, bt_ref, kt_ref, bg_ref, kg_ref, v_ref, g_ref, bonus_ref,
                      gT_ref, gm_ref, *, tm, first):
    keep = jnp.where(first, 0.0, 1.0)
    nc = tm // CHUNK
    last = slice(BF16_SUBLANES - 1, BF16_SUBLANES)

    def shifted(z, prev_last, mu):
        z = z.astype(F32)
        row = lax.broadcasted_iota(jnp.int32, z.shape, 0)
        zprev = jnp.where(row == 0, prev_last.astype(F32) * keep, pltpu.roll(z, 1, 0))
        return z + (zprev - z) * mu

    lo = shifted(zl_ref[...], pli_ref[last, :], mul_ref[...])
    tanh_wd = jnp.tanh(lo[:, :LORA_W]).astype(BF16)
    ad = lo[:, LORA_W:LORA_W + LORA_A].astype(BF16)
    sig_gd = jax.nn.sigmoid(lo[:, LORA_W + LORA_A:]).astype(BF16)

    ti = lax.broadcasted_iota(jnp.int32, (tm, tm), 0)
    si = lax.broadcasted_iota(jnp.int32, (tm, tm), 1)
    tri = jnp.where(((ti // CHUNK) == (si // CHUNK)) & (si <= ti), 1.0, 0.0).astype(BF16)
    ones_h = _head_ones(PAIR, RW_HEAD, 1.0)
    w_lora = jnp.dot(tanh_wd, w2_ref[...], preferred_element_type=F32)
    a_lora = jnp.dot(ad, a2_ref[...], preferred_element_type=F32)
    g_ref[...] = jnp.dot(sig_gd, g2_ref[...], preferred_element_type=F32).astype(BF16)
    yield

    stash = []
    for p in range(N_PAIRS):
        cs = slice(p * PAIR, (p + 1) * PAIR)
        r = shifted(zr_ref[:, cs], pr_ref[last, cs], mur_ref[:, cs])
        k = shifted(zk_ref[:, cs], pk_ref[last, cs], muk_ref[:, cs])
        v = shifted(zv_ref[:, cs], pv_ref[last, cs], muv_ref[:, cs])

        wpre = w0_ref[:, cs] + w_lora[:, cs]
        w = jnp.minimum(wpre, 0.0) - jnp.log1p(jnp.exp(-jnp.abs(wpre))) - 0.5
        lw = -jnp.exp(w)
        a = jax.nn.sigmoid(a0_ref[:, cs] + a_lora[:, cs])

        kk = k * kk_ref[:, cs]
        kk = kk * lax.rsqrt(jnp.maximum(_mm(kk * kk, ones_h), 1e-24))
        k2 = k * (1.0 + (a - 1.0) * ka_ref[:, cs])
        bonus_ref[:, cs] = (_mm(r * k2 * rk_ref[:, cs], ones_h) * v).astype(BF16)
        v_ref[:, cs] = v.astype(BF16)
        stash.append((r, k2, -kk, kk * a, lw))
        yield

    c_all = _mm_split(tri, jnp.concatenate([st[4] for st in stash], axis=1))
    yield

    for p in range(N_PAIRS):
        cs = slice(p * PAIR, (p + 1) * PAIR)
        r, k2, ka_vec, kb_vec, lw = stash[p]
        c = c_all[:, cs]
        c3 = c.reshape(nc, CHUNK, PAIR)
        c_mid = c3[:, CHUNK // 2 - 1:CHUNK // 2, :]
        c_end = c3[:, CHUNK - 1:CHUNK, :]
        cm = jnp.broadcast_to(c_mid, c3.shape).reshape(tm, PAIR)
        cT = jnp.broadcast_to(c_end, c3.shape).reshape(tm, PAIR)

        e_in = jnp.exp(c - cm)
        e_out = jnp.exp(cm - c)
        e_end = jnp.exp(cT - c)
        at_ref[:, cs] = (ka_vec * jnp.exp(c - lw - cm)).astype(BF16)
        rt_ref[:, cs] = (r * e_in).astype(BF16)
        bt_ref[:, cs] = (kb_vec * e_out).astype(BF16)
        kt_ref[:, cs] = (k2 * e_out).astype(BF16)
        bg_ref[:, cs] = (kb_vec * e_end).astype(BF16)
        kg_ref[:, cs] = (k2 * e_end).astype(BF16)
        gT_ref[:, :, cs] = jnp.exp(c_end)
        gm_ref[:, :, cs] = jnp.exp(c_mid)
        yield


def _rwkv_kernel(*refs, tb, nb):
    prep_in, (gnw_ref, gnb_ref, o_ref), scratch = refs[:20], refs[20:23], refs[23:]
    operands, s_ref = scratch[:11], scratch[11]
    j = pl.program_id(1)
    prep = functools.partial(_rwkv_prep_kernel, *prep_in, *operands, tm=tb, first=j == 0)
    chunk = functools.partial(_rwkv_chunk_kernel, *operands, gnw_ref, gnb_ref, o_ref, s_ref,
                              chunks=tb // CHUNK, pairs=N_PAIRS, first=j == 1)

    @pl.when(j == 0)
    def _():
        for _ in prep():
            pass

    @pl.when((j > 0) & (j < nb))
    def _():
        chunk(filler=prep())

    @pl.when(j == nb)
    def _():
        chunk()


def _rwkv(z, batch, seq, mu_r, mu_k, mu_v, mu_l, w0, a0, k_k, k_a, r_k, w2p, a2p, g2, gn_w, gn_b, tb=256):
    m = z.shape[0]
    nb = seq // tb
    pb = tb // BF16_SUBLANES
    blk = lambda b, j: b * nb + jnp.minimum(j, nb - 1)

    def cur(col0, width):
        return pl.BlockSpec((tb, width), lambda b, j, c0=col0 // width: (blk(b, j), c0))

    def prev(col0, width):
        return pl.BlockSpec((BF16_SUBLANES, width),
                            lambda b, j, c0=col0 // width: (jnp.maximum(blk(b, j) * pb - 1, 0), c0))

    def whole(rows, width):
        return pl.BlockSpec((rows, width), lambda b, j: (0, 0))

    in_specs = [cur(C_R, RW_DIM), cur(C_K, RW_DIM), cur(C_V, RW_DIM), cur(C_LORA, LORA_COLS),
                prev(C_R, RW_DIM), prev(C_K, RW_DIM), prev(C_V, RW_DIM), prev(C_LORA, LORA_COLS),
                whole(1, RW_DIM), whole(1, RW_DIM), whole(1, RW_DIM), whole(1, LORA_COLS),
                whole(1, RW_DIM), whole(1, RW_DIM), whole(1, RW_DIM), whole(1, RW_DIM), whole(1, RW_DIM),
                whole(LORA_W, RW_DIM), whole(LORA_A, RW_DIM), whole(GATE_LORA, RW_DIM),
                whole(1, RW_DIM), whole(1, RW_DIM)]
    big = pltpu.VMEM((tb, RW_DIM), BF16)
    per_chunk = pltpu.VMEM((tb // CHUNK, 1, RW_DIM), F32)
    return pl.pallas_call(
        functools.partial(_rwkv_kernel, tb=tb, nb=nb),
        grid=(batch, nb + 1),
        in_specs=in_specs,
        out_specs=pl.BlockSpec((tb, RW_DIM), lambda b, j: (b * nb + jnp.maximum(j - 1, 0), 0)),
        out_shape=jax.ShapeDtypeStruct((m, RW_DIM), BF16),
        scratch_shapes=[big] * 9 + [per_chunk] * 2 + [pltpu.VMEM((N_PAIRS, PAIR, PAIR), F32)],
        compiler_params=_cparams(("parallel", "arbitrary")),
        name="rwkv",
    )(z, z, z, z, z, z, z, z, mu_r, mu_k, mu_v, mu_l, w0, a0, k_k, k_a, r_k, w2p, a2p, g2, gn_w, gn_b)


def _rwkv_chunk_kernel(at_ref, rt_ref, bt_ref, kt_ref, bg_ref, kg_ref, v_ref, g_ref, bonus_ref,
                       gT_ref, gm_ref, gnw_ref, gnb_ref, o_ref, s_ref, *, chunks, pairs, first,
                       filler=None):
    @pl.when(first)
    def _():
        s_ref[...] = jnp.zeros_like(s_ref)

    T = CHUNK
    lane = lax.broadcasted_iota(jnp.int32, (T, PAIR), 1)
    head0 = lane < RW_HEAD
    ri = lax.broadcasted_iota(jnp.int32, (2 * T, 2 * T), 0)
    ci = lax.broadcasted_iota(jnp.int32, (2 * T, 2 * T), 1)
    same = (ri // T) == (ci // T)
    strict = same & ((ri % T) > (ci % T))
    incl = same & ((ri % T) >= (ci % T))
    eye = jnp.where(ri == ci, 1.0, 0.0)
    own = (ri // T) == (ci // RW_HEAD)

    items = [(p, c) for c in range(chunks) for p in range(pairs)]
    rows = lambda c: slice(c * T, (c + 1) * T)
    cols = lambda p: slice(p * PAIR, (p + 1) * PAIR)

    def stack(ref):
        out = []
        for p, c in items:
            x = ref[rows(c), cols(p)]
            zero = jnp.zeros_like(x)
            out.append(jnp.concatenate([jnp.where(head0, x, zero), jnp.where(head0, zero, x)], axis=0))
        return out

    each = lambda f, *ls: [f(*xs) for xs in zip(*ls)]
    La, Lr, Rb, Rk = stack(at_ref), stack(rt_ref), stack(bt_ref), stack(kt_ref)
    Rbg, Rkg, Vs = stack(bg_ref), stack(kg_ref), stack(v_ref)
    gm_row = [gm_ref[c][:, cols(p)] for p, c in items]
    gT_row = [gT_ref[c][:, cols(p)] for p, c in items]
    gate = [g_ref[rows(c), cols(p)] for p, c in items]
    bonus = [bonus_ref[rows(c), cols(p)] for p, c in items]
    filler = iter(()) if filler is None else filler
    tick = lambda: next(filler, None)

    AA = each(lambda la, lr, rb, rk: _mm_nt(jnp.concatenate([la, lr], axis=0),
                                            jnp.concatenate([rb, rk], axis=0)), La, Lr, Rb, Rk)
    tick()
    N = [jnp.where(strict, aa[:2 * T, :2 * T], 0.0) for aa in AA]
    Aak = [jnp.where(strict, aa[:2 * T, 2 * T:], 0.0) for aa in AA]
    Arb = [jnp.where(incl, aa[2 * T:, :2 * T], 0.0) for aa in AA]
    Ark = [jnp.where(incl, aa[2 * T:, 2 * T:], 0.0) for aa in AA]
    def live_rows(x, t0):
        return x if t0 == 0 else jnp.concatenate([x[t0:T, :], x[T + t0:, :]], axis=0)

    def all_rows(y, t0):
        if t0 == 0:
            return y
        zero = jnp.zeros((t0, y.shape[1]), y.dtype)
        return jnp.concatenate([zero, y[:T - t0, :], zero, y[T - t0:, :]], axis=0)

    W = each(lambda n: eye + n, N)
    P = N
    span = 2
    while span < T:
        t0 = span if span % BF16_SUBLANES == 0 else 0
        P = each(lambda p_: all_rows(_mm(live_rows(p_, t0), p_), t0), P)
        tick()
        W = each(lambda w, p_: w + all_rows(_mm(live_rows(w, t0), p_), t0), W, P)
        tick()
        span *= 2
    AkV = each(_mm, Aak, Vs)
    tick()
    X = each(lambda w, la, akv: _mm(w, jnp.concatenate([la, akv.astype(BF16)], axis=1)).astype(BF16),
             W, La, AkV)
    tick()
    Z = each(lambda arb, ark, x, vs:
             _mm(jnp.concatenate([arb.astype(BF16), ark.astype(BF16)], axis=1),
                 jnp.concatenate([x, jnp.concatenate([jnp.zeros_like(vs), vs], axis=1)], axis=0)),
             Arb, Ark, X, Vs)
    tick()
    Q = each(lambda z, lr, gm: (z[:, :PAIR] + lr.astype(F32)) * gm, Z, Lr, gm_row)
    Y0 = [z[:, PAIR:] for z in Z]
    tick()
    Mbd = each(lambda rbg, x, gm: _mm_tn(rbg, x[:, :PAIR]) * gm, Rbg, X, gm_row)
    NcT = each(lambda x, vs, rbg, rkg: _mm_tn(jnp.concatenate([x[:, PAIR:], vs], axis=0),
                                              jnp.concatenate([rbg, rkg], axis=0)), X, Vs, Rbg, Rkg)
    for _ in filler:
        pass

    S = [s_ref[p] for p in range(pairs)]
    for c in range(chunks):
        for p in range(pairs):
            i = c * pairs + p
            Ys = _mm_nt(Q[i], S[p]) + Y0[i]
            S[p] = S[p] * gT_row[i] + _mm_nt(S[p], Mbd[i]) + NcT[i]
            mu = jnp.sum(Ys, axis=-1, keepdims=True) * (1.0 / RW_HEAD)
            d = jnp.where(own, Ys - mu, 0.0)
            var = jnp.sum(d * d, axis=-1, keepdims=True) * (1.0 / RW_HEAD)
            dn = d * lax.rsqrt(var + GN_EPS)
            yn = (dn[:T, :] + dn[T:, :]) * gnw_ref[:, cols(p)] + gnb_ref[:, cols(p)]
            out = (yn + bonus[i].astype(F32)) * gate[i].astype(F32)
            o_ref[rows(c), cols(p)] = out.astype(BF16)
    for p in range(pairs):
        s_ref[p] = S[p]


def _swa_kernel(sink_ref, q_ref, kc_ref, kp_ref, vc_ref, vp_ref, qg_ref, kg_ref, o_ref, *,
                blocks_per_seq, slopes):
    first = (pl.program_id(0) % blocks_per_seq) == 0
    mean_h = _head_ones(LANES, ATT_HEAD, 1.0 / ATT_HEAD)

    def head_rms(x, gain):
        parts = []
        for b in range(x.shape[1] // LANES):
            xb = x[:, b * LANES:(b + 1) * LANES]
            ms = _mm(xb * xb, mean_h)
            parts.append(xb * lax.rsqrt(ms + RMS_EPS))
        return jnp.concatenate(parts, axis=1) * gain

    q = head_rms(q_ref[...].astype(F32), qg_ref[...]) * (ATT_HEAD ** -0.5 * LOG2E)
    q = q.astype(BF16)
    kcat = jnp.concatenate([kp_ref[...], kc_ref[...]], axis=0).astype(F32)
    kcat = head_rms(kcat, kg_ref[...]).astype(BF16)
    vcat = jnp.concatenate([vp_ref[...], vc_ref[...]], axis=0)

    qi = lax.broadcasted_iota(jnp.int32, (BLOCK, 2 * BLOCK), 0)
    kj = lax.broadcasted_iota(jnp.int32, (BLOCK, 2 * BLOCK), 1)
    dist_i = BLOCK + qi - kj
    first_key = jnp.where(first, BLOCK, 0)
    valid = (dist_i >= 0) & (dist_i < WINDOW) & (kj >= first_key)
    neg_dist = jnp.where(valid, -dist_i.astype(F32), NEG_BIG)

    lane_q = lax.broadcasted_iota(jnp.int32, (BLOCK, LANES), 1)
    lane_kv = lax.broadcasted_iota(jnp.int32, (2 * BLOCK, LANES), 1)
    zero_q = jnp.zeros((BLOCK, LANES), BF16)
    one_kv = jnp.ones((2 * BLOCK, LANES), BF16)
    k_blocks = [kcat[:, b * LANES:(b + 1) * LANES] for b in range(ATT_KV_DIM // LANES)]
    k_rolled = [pltpu.roll(kb, ATT_HEAD, 1) for kb in k_blocks]
    v_aug = [jnp.where((lane_kv // ATT_HEAD) == (j % 2),
                       vcat[:, (j // 2) * LANES:(j // 2 + 1) * LANES], one_kv) for j in range(ATT_KV_HEADS)]
    heads = range(ATT_HEADS)
    kv_of = [h // ATT_GROUP for h in heads]
    aligned = [(h % 2) == (kv_of[h] % 2) for h in heads]
    s = [_mm_nt(jnp.where((lane_q // ATT_HEAD) == (h % 2), q[:, (h // 2) * LANES:(h // 2 + 1) * LANES], zero_q),
                k_blocks[kv_of[h] // 2] if aligned[h] else k_rolled[kv_of[h] // 2])
         + (slopes[h] * LOG2E) * neg_dist for h in heads]
    sink = [sink_ref[h] * LOG2E for h in heads]
    mx = [jnp.maximum(jnp.max(s[h], axis=-1, keepdims=True), sink[h]) for h in heads]
    p = [jnp.exp2(s[h] - mx[h]) for h in heads]
    pv = [_mm(p[h], v_aug[kv_of[h]]) for h in heads]
    pv_sw = [pltpu.roll(pv[h], ATT_HEAD, 1) for h in heads]
    out = []
    for h in heads:
        num, den = (pv[h], pv_sw[h]) if aligned[h] else (pv_sw[h], pv[h])
        out.append(num * (1.0 / (den + jnp.exp2(sink[h] - mx[h]))))
    o_ref[...] = jnp.concatenate(
        [jnp.where((lane_q // ATT_HEAD) == 0, out[2 * m], out[2 * m + 1]) for m in range(ATT_HEADS // 2)],
        axis=1).astype(BF16)


def _swa(z, sinks, q_gain_t, k_gain_t, seq):
    m = z.shape[0]
    nblk = m // BLOCK
    bps = seq // BLOCK
    slopes = tuple(float(s) for s in
                   np.exp2(-8.0 * np.arange(1, ATT_HEADS + 1, dtype=np.float32) / ATT_HEADS).astype(np.float32))
    kv_cur = lambda c0: pl.BlockSpec((BLOCK, ATT_KV_DIM), lambda n, c=c0 // ATT_KV_DIM: (n, c))
    kv_prev = lambda c0: pl.BlockSpec((BLOCK, ATT_KV_DIM),
                                      lambda n, c=c0 // ATT_KV_DIM: (jnp.maximum(n - 1, 0), c))
    return pl.pallas_call(
        functools.partial(_swa_kernel, blocks_per_seq=bps, slopes=slopes),
        grid=(nblk,),
        in_specs=[pl.BlockSpec(memory_space=pltpu.SMEM),
                  pl.BlockSpec((BLOCK, ATT_Q_DIM), lambda n: (n, C_Q // ATT_Q_DIM)),
                  kv_cur(C_KA), kv_prev(C_KA), kv_cur(C_VA), kv_prev(C_VA),
                  pl.BlockSpec((1, ATT_Q_DIM), lambda n: (0, 0)),
                  pl.BlockSpec((1, ATT_KV_DIM), lambda n: (0, 0))],
        out_specs=pl.BlockSpec((BLOCK, ATT_Q_DIM), lambda n: (n, 0)),
        out_shape=jax.ShapeDtypeStruct((m, ATT_Q_DIM), BF16),
        compiler_params=_cparams(("parallel",)),
        name="swa",
    )(sinks, z, z, z, z, z, q_gain_t, k_gain_t)


def _mix_out_kernel(orw_ref, oatt_ref, wb1_ref, wb2_ref, zg_ref, wo_ref, x_ref, g_ref, x1_ref, h2_ref):
    p_rw = jnp.dot(orw_ref[...], wb1_ref[...], preferred_element_type=F32)
    p_att = jnp.dot(oatt_ref[...], wb2_ref[...], preferred_element_type=F32)
    g_rw = jax.nn.sigmoid(zg_ref[:, :D_MODEL].astype(F32))
    g_att = jax.nn.sigmoid(zg_ref[:, D_MODEL:].astype(F32))
    mix = (g_rw * p_rw + g_att * p_att).astype(BF16)
    x1 = x_ref[...] + jnp.dot(mix, wo_ref[...], preferred_element_type=F32)
    x1_ref[...] = x1
    y = x1 * lax.rsqrt(jnp.mean(x1 * x1, axis=-1, keepdims=True) + RMS_EPS)
    h2_ref[...] = (y * g_ref[...]).astype(BF16)


def _mix_out(o_rw, o_att, w_branch_b, z, w_out_b, x2, g2, tm=512):
    m = x2.shape[0]
    const = lambda rows, cols, r0=0: pl.BlockSpec((rows, cols), lambda i: (r0, 0),
                                                  pipeline_mode=pl.Buffered(1))
    rows = lambda cols, c0=0: pl.BlockSpec((tm, cols), lambda i: (i, c0))
    return pl.pallas_call(
        _mix_out_kernel,
        grid=(m // tm,),
        in_specs=[rows(RW_DIM), rows(ATT_Q_DIM), const(RW_DIM, D_MODEL), const(ATT_Q_DIM, D_MODEL, 1),
                  rows(2 * D_MODEL, C_GRW // (2 * D_MODEL)), const(D_MODEL, D_MODEL),
                  rows(D_MODEL), const(1, D_MODEL)],
        out_specs=[rows(D_MODEL), rows(D_MODEL)],
        out_shape=[jax.ShapeDtypeStruct((m, D_MODEL), F32),
                   jax.ShapeDtypeStruct((m, D_MODEL), BF16)],
        compiler_params=_cparams(("parallel",)),
        name="mix_out",
    )(o_rw, o_att, w_branch_b, w_branch_b, z, w_out_b, x2, g2)


def _ffn_up_kernel(h_ref, wvb_ref, wgb_ref, cwv_ref, cwg_ref, cbv_ref, cbg_ref, wd_ref, a_ref, wdb_ref,
                   cv_ref, cg_ref, *, tm, sub, tiles_per_seq):
    halo = 8
    wdb_ref[...] = wd_ref[...].astype(BF16)

    @pl.when((pl.program_id(1) % tiles_per_seq) == 0)
    def _():
        cv_ref[...] = jnp.zeros_like(cv_ref)
        cg_ref[...] = jnp.zeros_like(cg_ref)

    row = lax.broadcasted_iota(jnp.int32, (halo, a_ref.shape[1]), 0)

    def conv(u, carry_ref, cw_ref, cb_ref):
        p1 = carry_ref[halo - 1:halo, :]
        p2 = carry_ref[halo - 2:halo - 1, :]
        r1 = pltpu.roll(u, 1, 0)
        r2 = pltpu.roll(u, 2, 0)
        prev1 = jnp.concatenate([jnp.where(row == 0, p1, r1[:halo, :]), r1[halo:, :]], axis=0)
        prev2 = jnp.concatenate(
            [jnp.where(row == 0, p2, jnp.where(row == 1, p1, r2[:halo, :])), r2[halo:, :]], axis=0)
        carry_ref[...] = u[sub - halo:sub, :]
        return prev2 * cw_ref[0:1, :] + prev1 * cw_ref[1:2, :] + u * cw_ref[2:3, :] + cb_ref[...]

    for s in range(tm // sub):
        h = h_ref[s * sub:(s + 1) * sub, :]
        val = conv(jnp.dot(h, wvb_ref[...], preferred_element_type=F32), cv_ref, cwv_ref, cbv_ref)
        gate = conv(jnp.dot(h, wgb_ref[...], preferred_element_type=F32), cg_ref, cwg_ref, cbg_ref)
        a_ref[s * sub:(s + 1) * sub, :] = (gate * jax.nn.sigmoid(gate) * val).astype(BF16)


def _ffn_up(h2, w_up_b, conv_w, conv_b, w_down, seq, tm=1024, tn=512, sub=256):
    m = h2.shape[0]
    nj = D_FF // tn
    ni = m // tm
    wd_rows = w_down.shape[0] // (nj * ni)
    wd_spec = pl.BlockSpec((wd_rows, D_MODEL), lambda j, i: (j * ni + i, 0))
    return pl.pallas_call(
        functools.partial(_ffn_up_kernel, tm=tm, sub=sub, tiles_per_seq=seq // tm),
        grid=(nj, ni),
        in_specs=[pl.BlockSpec((tm, D_MODEL), lambda j, i: (i, 0)),
                  pl.BlockSpec((D_MODEL, tn), lambda j, i: (0, j)),
                  pl.BlockSpec((D_MODEL, tn), lambda j, i: (0, nj + j)),
                  pl.BlockSpec((3, tn), lambda j, i: (0, j)),
                  pl.BlockSpec((3, tn), lambda j, i: (0, nj + j)),
                  pl.BlockSpec((1, tn), lambda j, i: (0, j)),
                  pl.BlockSpec((1, tn), lambda j, i: (0, nj + j)),
                  wd_spec],
        out_specs=[pl.BlockSpec((tm, tn), lambda j, i: (i, j)), wd_spec],
        out_shape=[jax.ShapeDtypeStruct((m, D_FF), BF16), jax.ShapeDtypeStruct(w_down.shape, BF16)],
        scratch_shapes=[pltpu.VMEM((8, tn), F32), pltpu.VMEM((8, tn), F32)],
        compiler_params=_cparams(("parallel", "arbitrary")),
        name="ffn_up",
    )(h2, w_up_b, w_up_b, conv_w, conv_w, conv_b, conv_b, w_down)


def _ffn_down_kernel(a_ref, w_ref, x1_ref, o_ref):
    o_ref[...] = x1_ref[...] + jnp.dot(a_ref[...], w_ref[...], preferred_element_type=F32)


def _ffn_down(act, w_down_b, x1, tm=512, tn=1024):
    m = act.shape[0]
    return pl.pallas_call(
        _ffn_down_kernel,
        grid=(D_MODEL // tn, m // tm),
        in_specs=[pl.BlockSpec((tm, D_FF), lambda j, i: (i, 0)),
                  pl.BlockSpec((D_FF, tn), lambda j, i: (0, j)),
                  pl.BlockSpec((tm, tn), lambda j, i: (i, j))],
        out_specs=pl.BlockSpec((tm, tn), lambda j, i: (i, j)),
        out_shape=jax.ShapeDtypeStruct((m, D_MODEL), F32),
        compiler_params=_cparams(("parallel", "arbitrary")),
        name="ffn_down",
    )(act, w_down_b, x1)


def _pad_cols(w, n):
    return jnp.pad(w, ((0, 0), (0, n - w.shape[1])))


def _pad_rows(w, n):
    return jnp.pad(w, ((0, n - w.shape[0]), (0, 0)))


def _layer(x2, batch, seq, w_in_p, norm1_g, rw_mu, rw_w0, rw_w2, rw_a0, rw_a2, rw_g2, rw_k_k, rw_k_a,
           rw_r_k, rw_gn_w, rw_gn_b, q_norm_g, k_norm_g, attn_sinks, w_branch, w_out,
           norm2_g, w_up, conv_w, conv_b, w_down):
    row = lambda v: v.reshape(1, -1).astype(F32)
    o3 = 3 * RW_DIM
    o4 = o3 + DECAY_LORA
    o5 = o4 + ICLR_LORA
    o6 = o5 + GATE_LORA
    mu = rw_mu.reshape(1, -1)
    mu_l = jnp.concatenate([_pad_cols(mu[:, o3:o4], LORA_W), _pad_cols(mu[:, o4:o5], LORA_A),
                            mu[:, o5:o6]], axis=1)
    w2p = _pad_rows(rw_w2, LORA_W).astype(BF16)
    a2p = _pad_rows(rw_a2, LORA_A).astype(BF16)

    z, (w_branch_b, w_out_b, w_up_b) = _in_proj(x2, row(norm1_g), w_in_p, (w_branch, w_out, w_up))
    o_rw = _rwkv(z, batch, seq, mu[:, :RW_DIM], mu[:, RW_DIM:2 * RW_DIM], mu[:, 2 * RW_DIM:o3], mu_l,
                 row(rw_w0), row(rw_a0), row(rw_k_k), row(rw_k_a), row(rw_r_k),
                 w2p, a2p, rw_g2.astype(BF16), row(rw_gn_w), row(rw_gn_b))
    o_att = _swa(z, attn_sinks.astype(F32), jnp.tile(row(q_norm_g), (1, ATT_HEADS)),
                 jnp.tile(row(k_norm_g), (1, ATT_KV_HEADS)), seq)
    x1, h2 = _mix_out(o_rw, o_att, w_branch_b, z, w_out_b, x2, row(norm2_g))
    act, w_down_b = _ffn_up(h2, w_up_b, conv_w.astype(F32), row(conv_b), w_down, seq)
    return _ffn_down(act, w_down_b, x1)


def kernel(x, norm1_g, w_in, rw_mu, rw_w0, rw_w2, rw_a0, rw_a2, rw_g2, rw_k_k, rw_k_a, rw_r_k,
           rw_gn_w, rw_gn_b, q_norm_g, k_norm_g, attn_sinks, w_branch, w_out, norm2_g, w_up,
           conv_w, conv_b, w_down):
    batch, seq, d = x.shape
    x2 = x.reshape(batch * seq, d)
    params = (norm1_g, rw_mu, rw_w0, rw_w2, rw_a0, rw_a2, rw_g2, rw_k_k, rw_k_a, rw_r_k,
              rw_gn_w, rw_gn_b, q_norm_g, k_norm_g, attn_sinks, w_branch, w_out, norm2_g, w_up,
              conv_w, conv_b, w_down)
    for layer in range(norm1_g.shape[0]):
        x2 = _layer(x2, batch, seq, _w_in_layout(w_in, layer), *(p[layer] for p in params))
    return x2.reshape(batch, seq, d)
```

```python
import functools

import jax
import jax.numpy as jnp
import numpy as np
from jax import lax
from jax.experimental import pallas as pl
from jax.experimental.pallas import tpu as pltpu

F32 = jnp.float32
BF16 = jnp.bfloat16

LANES = 128
BF16_SUBLANES = 16
VMEM_LIMIT = 56 * 1024 * 1024

D_MODEL = 2048
RW_HEADS = 16
RW_HEAD = 64
RW_DIM = RW_HEADS * RW_HEAD
DECAY_LORA = 96
ICLR_LORA = 96
GATE_LORA = 256
GN_EPS = 64e-5
ATT_HEADS = 16
ATT_KV_HEADS = 4
ATT_GROUP = ATT_HEADS // ATT_KV_HEADS
ATT_HEAD = 64
ATT_Q_DIM = ATT_HEADS * ATT_HEAD
ATT_KV_DIM = ATT_KV_HEADS * ATT_HEAD
WINDOW = 128
BLOCK = 128
D_FF = 5632
RMS_EPS = 1e-6
NEG_BIG = -1e30
LOG2E = 1.4426950408889634

C_R = 0
C_K = RW_DIM
C_V = 2 * RW_DIM
C_Q = 3 * RW_DIM
C_GRW = C_Q + ATT_Q_DIM
C_GATT = C_GRW + D_MODEL
C_LORA = C_GATT + D_MODEL
LORA_W = 128
LORA_A = 128
LORA_COLS = LORA_W + LORA_A + GATE_LORA
C_KA = C_LORA + LORA_COLS
C_VA = C_KA + ATT_KV_DIM
Z_COLS = C_VA + ATT_KV_DIM

CHUNK = 64
PAIR = 2 * RW_HEAD
N_PAIRS = RW_DIM // PAIR


def _cparams(sem):
    return pltpu.CompilerParams(dimension_semantics=sem, vmem_limit_bytes=VMEM_LIMIT)


def _mm(a, b):
    return jnp.dot(a.astype(BF16), b.astype(BF16), preferred_element_type=F32)


def _mm_nt(a, b):
    return lax.dot_general(a.astype(BF16), b.astype(BF16), (((1,), (1,)), ((), ())),
                           preferred_element_type=F32)


def _mm_tn(a, b):
    return lax.dot_general(a.astype(BF16), b.astype(BF16), (((0,), (0,)), ((), ())),
                           preferred_element_type=F32)


def _mm_split(m01, x):
    hi = x.astype(BF16)
    lo = (x - hi.astype(F32)).astype(BF16)
    return (jnp.dot(m01, hi, preferred_element_type=F32)
            + jnp.dot(m01, lo, preferred_element_type=F32))


def _mm_split_r(x, m01):
    hi = x.astype(BF16)
    lo = (x - hi.astype(F32)).astype(BF16)
    return (jnp.dot(hi, m01, preferred_element_type=F32)
            + jnp.dot(lo, m01, preferred_element_type=F32))


def _head_ones(n, head, scale):
    r = lax.broadcasted_iota(jnp.int32, (n, n), 0) // head
    c = lax.broadcasted_iota(jnp.int32, (n, n), 1) // head
    return jnp.where(r == c, scale, 0.0).astype(BF16)


def _w_in_layout_kernel(wt_ref, o_ref, *, moves, pad):
    o_ref[pad[0]:pad[1], :] = jnp.zeros((pad[1] - pad[0], o_ref.shape[1]), BF16)
    for dst, src, width in moves:
        o_ref[dst:dst + width, :] = wt_ref[src:src + width, :].astype(BF16)


def _w_in_layout(w_in_layers, layer, cols=256):
    _, d, n = w_in_layers.shape
    wt_layers = jnp.swapaxes(w_in_layers, 1, 2)
    o3 = 3 * RW_DIM
    o4 = o3 + DECAY_LORA
    o5 = o4 + ICLR_LORA
    o6 = o5 + GATE_LORA
    oq = o6 + ATT_Q_DIM
    ok = oq + ATT_KV_DIM
    ov = ok + ATT_KV_DIM
    moves = ((C_R, 0, o3), (C_Q, o6, ATT_Q_DIM), (C_GRW, ov, 2 * D_MODEL),
             (C_LORA, o3, DECAY_LORA), (C_LORA + LORA_W, o4, ICLR_LORA),
             (C_LORA + LORA_W + LORA_A, o5, GATE_LORA), (C_KA, oq, ATT_KV_DIM), (C_VA, ok, ATT_KV_DIM))
    return pl.pallas_call(
        functools.partial(_w_in_layout_kernel, moves=moves, pad=(C_LORA, C_LORA + LORA_W + LORA_A)),
        grid=(d // cols,),
        in_specs=[pl.BlockSpec((None, n, cols), lambda i: (layer, 0, i))],
        out_specs=pl.BlockSpec((Z_COLS, cols), lambda i: (0, i)),
        out_shape=jax.ShapeDtypeStruct((Z_COLS, d), BF16),
        compiler_params=_cparams(("parallel",)),
        name="w_in_layout",
    )(wt_layers)


def _in_proj_kernel(x_ref, g_ref, wt_ref, *refs, n_side):
    side_in, z_ref, side_out, h_ref = refs[:n_side], refs[n_side], refs[n_side + 1:-1], refs[-1]

    @pl.when(pl.program_id(1) == 0)
    def _():
        xf = x_ref[...]
        y = xf * lax.rsqrt(jnp.mean(xf * xf, axis=-1, keepdims=True) + RMS_EPS)
        h_ref[...] = (y * g_ref[...]).astype(BF16)

    z_ref[...] = _mm_nt(h_ref[...], wt_ref[...]).astype(BF16)
    for w_ref, wb_ref in zip(side_in, side_out):
        wb_ref[...] = w_ref[...].astype(BF16)


def _in_proj(x2, g1, w_in_t, side_weights, tm=1024, tn=1536, side_rows=64):
    m, d = x2.shape
    n = w_in_t.shape[0]
    nj = n // tn
    for w in side_weights:
        assert w.shape[0] // side_rows <= (m // tm) * nj, "not enough grid steps to cast this weight"
    side_specs = [pl.BlockSpec((side_rows, w.shape[1]),
                               lambda i, j, last=w.shape[0] // side_rows - 1: (jnp.minimum(i * nj + j, last), 0))
                  for w in side_weights]
    outs = pl.pallas_call(
        functools.partial(_in_proj_kernel, n_side=len(side_weights)),
        grid=(m // tm, nj),
        in_specs=[pl.BlockSpec((tm, d), lambda i, j: (i, 0)),
                  pl.BlockSpec((1, d), lambda i, j: (0, 0)),
                  pl.BlockSpec((tn, d), lambda i, j: (j, 0))] + side_specs,
        out_specs=[pl.BlockSpec((tm, tn), lambda i, j: (i, j))] + side_specs,
        out_shape=[jax.ShapeDtypeStruct((m, n), BF16)]
        + [jax.ShapeDtypeStruct(w.shape, BF16) for w in side_weights],
        scratch_shapes=[pltpu.VMEM((tm, d), BF16)],
        compiler_params=_cparams(("arbitrary", "arbitrary")),
        name="in_proj",
    )(x2, g1, w_in_t, *side_weights)
    return outs[0], outs[1:]


def _rwkv_prep_kernel(zr_ref, zk_ref, zv_ref, zl_ref, pr_ref, pk_ref, pv_ref, pli_ref,
                      mur_ref, muk_ref, muv_ref, mul_ref,
                      w0_ref, a0_ref, kk_ref, ka_ref, rk_ref, w2_ref, a2_ref, g2_ref,
                      at_ref, rt_ref, bt_ref, kt_ref, bg_ref, kg_ref, v_ref, g_ref, bonus_ref,
                      gT_ref, gm_ref, *, tm, first):
    keep = jnp.where(first, 0.0, 1.0)
    nc = tm // CHUNK
    last = slice(BF16_SUBLANES - 1, BF16_SUBLANES)

    def shifted(z, prev_last, mu):
        z = z.astype(F32)
        row = lax.broadcasted_iota(jnp.int32, z.shape, 0)
        zprev = jnp.where(row == 0, prev_last.astype(F32) * keep, pltpu.roll(z, 1, 0))
        return z + (zprev - z) * mu

    lo = shifted(zl_ref[...], pli_ref[last, :], mul_ref[...])
    tanh_wd = jnp.tanh(lo[:, :LORA_W]).astype(BF16)
    ad = lo[:, LORA_W:LORA_W + LORA_A].astype(BF16)
    sig_gd = jax.nn.sigmoid(lo[:, LORA_W + LORA_A:]).astype(BF16)

    ti = lax.broadcasted_iota(jnp.int32, (tm, tm), 0)
    si = lax.broadcasted_iota(jnp.int32, (tm, tm), 1)
    tri = jnp.where(((ti // CHUNK) == (si // CHUNK)) & (si <= ti), 1.0, 0.0).astype(BF16)
    ones_h = _head_ones(PAIR, RW_HEAD, 1.0)
    w_lora = jnp.dot(tanh_wd, w2_ref[...], preferred_element_type=F32)
    a_lora = jnp.dot(ad, a2_ref[...], preferred_element_type=F32)
    g_ref[...] = jnp.dot(sig_gd, g2_ref[...], preferred_element_type=F32).astype(BF16)
    yield

    stash = []
    for p in range(N_PAIRS):
        cs = slice(p * PAIR, (p + 1) * PAIR)
        r = shifted(zr_ref[:, cs], pr_ref[last, cs], mur_ref[:, cs])
        k = shifted(zk_ref[:, cs], pk_ref[last, cs], muk_ref[:, cs])
        v = shifted(zv_ref[:, cs], pv_ref[last, cs], muv_ref[:, cs])

        wpre = w0_ref[:, cs] + w_lora[:, cs]
        w = jnp.minimum(wpre, 0.0) - jnp.log1p(jnp.exp(-jnp.abs(wpre))) - 0.5
        lw = -jnp.exp(w)
        a = jax.nn.sigmoid(a0_ref[:, cs] + a_lora[:, cs])

        kk = k * kk_ref[:, cs]
        kk = kk * lax.rsqrt(jnp.maximum(_mm(kk * kk, ones_h), 1e-24))
        k2 = k * (1.0 + (a - 1.0) * ka_ref[:, cs])
        bonus_ref[:, cs] = (_mm(r * k2 * rk_ref[:, cs], ones_h) * v).astype(BF16)
        v_ref[:, cs] = v.astype(BF16)
        stash.append((r, k2, -kk, kk * a, lw))
        yield

    c_all = _mm_split(tri, jnp.concatenate([st[4] for st in stash], axis=1))
    yield

    for p in range(N_PAIRS):
        cs = slice(p * PAIR, (p + 1) * PAIR)
        r, k2, ka_vec, kb_vec, lw = stash[p]
        c = c_all[:, cs]
        c3 = c.reshape(nc, CHUNK, PAIR)
        c_mid = c3[:, CHUNK // 2 - 1:CHUNK // 2, :]
        c_end = c3[:, CHUNK - 1:CHUNK, :]
        cm = jnp.broadcast_to(c_mid, c3.shape).reshape(tm, PAIR)
        cT = jnp.broadcast_to(c_end, c3.shape).reshape(tm, PAIR)

        e_in = jnp.exp(c - cm)
        e_out = jnp.exp(cm - c)
        e_end = jnp.exp(cT - c)
        at_ref[:, cs] = (ka_vec * jnp.exp(c - lw - cm)).astype(BF16)
        rt_ref[:, cs] = (r * e_in).astype(BF16)
        bt_ref[:, cs] = (kb_vec * e_out).astype(BF16)
        kt_ref[:, cs] = (k2 * e_out).astype(BF16)
        bg_ref[:, cs] = (kb_vec * e_end).astype(BF16)
        kg_ref[:, cs] = (k2 * e_end).astype(BF16)
        gT_ref[:, :, cs] = jnp.exp(c_end)
        gm_ref[:, :, cs] = jnp.exp(c_mid)
        yield


def _rwkv_kernel(*refs, tb, nb):
    prep_in, (gnw_ref, gnb_ref, o_ref), scratch = refs[:20], refs[20:23], refs[23:]
    operands, s_ref = scratch[:11], scratch[11]
    j = pl.program_id(1)
    prep = functools.partial(_rwkv_prep_kernel, *prep_in, *operands, tm=tb, first=j == 0)
    chunk = functools.partial(_rwkv_chunk_kernel, *operands, gnw_ref, gnb_ref, o_ref, s_ref,
                              chunks=tb // CHUNK, pairs=N_PAIRS, first=j == 1)

    @pl.when(j == 0)
    def _():
        for _ in prep():
            pass

    @pl.when((j > 0) & (j < nb))
    def _():
        chunk(filler=prep())

    @pl.when(j == nb)
    def _():
        chunk()


def _rwkv(z, batch, seq, mu_r, mu_k, mu_v, mu_l, w0, a0, k_k, k_a, r_k, w2p, a2p, g2, gn_w, gn_b, tb=128):
    m = z.shape[0]
    nb = seq // tb
    pb = tb // BF16_SUBLANES
    blk = lambda b, j: b * nb + jnp.minimum(j, nb - 1)

    def cur(col0, width):
        return pl.BlockSpec((tb, width), lambda b, j, c0=col0 // width: (blk(b, j), c0))

    def prev(col0, width):
        return pl.BlockSpec((BF16_SUBLANES, width),
                            lambda b, j, c0=col0 // width: (jnp.maximum(blk(b, j) * pb - 1, 0), c0))

    def whole(rows, width):
        return pl.BlockSpec((rows, width), lambda b, j: (0, 0))

    in_specs = [cur(C_R, RW_DIM), cur(C_K, RW_DIM), cur(C_V, RW_DIM), cur(C_LORA, LORA_COLS),
                prev(C_R, RW_DIM), prev(C_K, RW_DIM), prev(C_V, RW_DIM), prev(C_LORA, LORA_COLS),
                whole(1, RW_DIM), whole(1, RW_DIM), whole(1, RW_DIM), whole(1, LORA_COLS),
                whole(1, RW_DIM), whole(1, RW_DIM), whole(1, RW_DIM), whole(1, RW_DIM), whole(1, RW_DIM),
                whole(LORA_W, RW_DIM), whole(LORA_A, RW_DIM), whole(GATE_LORA, RW_DIM),
                whole(1, RW_DIM), whole(1, RW_DIM)]
    big = pltpu.VMEM((tb, RW_DIM), BF16)
    per_chunk = pltpu.VMEM((tb // CHUNK, 1, RW_DIM), F32)
    return pl.pallas_call(
        functools.partial(_rwkv_kernel, tb=tb, nb=nb),
        grid=(batch, nb + 1),
        in_specs=in_specs,
        out_specs=pl.BlockSpec((tb, RW_DIM), lambda b, j: (b * nb + jnp.maximum(j - 1, 0), 0)),
        out_shape=jax.ShapeDtypeStruct((m, RW_DIM), BF16),
        scratch_shapes=[big] * 9 + [per_chunk] * 2 + [pltpu.VMEM((N_PAIRS, PAIR, PAIR), F32)],
        compiler_params=_cparams(("parallel", "arbitrary")),
        name="rwkv",
    )(z, z, z, z, z, z, z, z, mu_r, mu_k, mu_v, mu_l, w0, a0, k_k, k_a, r_k, w2p, a2p, g2, gn_w, gn_b)


def _rwkv_chunk_kernel(at_ref, rt_ref, bt_ref, kt_ref, bg_ref, kg_ref, v_ref, g_ref, bonus_ref,
                       gT_ref, gm_ref, gnw_ref, gnb_ref, o_ref, s_ref, *, chunks, pairs, first,
                       filler=None):
    @pl.when(first)
    def _():
        s_ref[...] = jnp.zeros_like(s_ref)

    T = CHUNK
    lane = lax.broadcasted_iota(jnp.int32, (T, PAIR), 1)
    head0 = lane < RW_HEAD
    ri = lax.broadcasted_iota(jnp.int32, (2 * T, 2 * T), 0)
    ci = lax.broadcasted_iota(jnp.int32, (2 * T, 2 * T), 1)
    same = (ri // T) == (ci // T)
    strict = same & ((ri % T) > (ci % T))
    incl = same & ((ri % T) >= (ci % T))
    eye = jnp.where(ri == ci, 1.0, 0.0)
    own = (ri // T) == (ci // RW_HEAD)

    items = [(p, c) for c in range(chunks) for p in range(pairs)]
    rows = lambda c: slice(c * T, (c + 1) * T)
    cols = lambda p: slice(p * PAIR, (p + 1) * PAIR)

    def stack(ref):
        out = []
        for p, c in items:
            x = ref[rows(c), cols(p)]
            zero = jnp.zeros_like(x)
            out.append(jnp.concatenate([jnp.where(head0, x, zero), jnp.where(head0, zero, x)], axis=0))
        return out

    each = lambda f, *ls: [f(*xs) for xs in zip(*ls)]
    La, Lr, Rb, Rk = stack(at_ref), stack(rt_ref), stack(bt_ref), stack(kt_ref)
    Rbg, Rkg, Vs = stack(bg_ref), stack(kg_ref), stack(v_ref)
    gm_row = [gm_ref[c][:, cols(p)] for p, c in items]
    gT_row = [gT_ref[c][:, cols(p)] for p, c in items]
    gate = [g_ref[rows(c), cols(p)] for p, c in items]
    bonus = [bonus_ref[rows(c), cols(p)] for p, c in items]
    filler = iter(()) if filler is None else filler
    tick = lambda: next(filler, None)

    AA = each(lambda la, lr, rb, rk: _mm_nt(jnp.concatenate([la, lr], axis=0),
                                            jnp.concatenate([rb, rk], axis=0)), La, Lr, Rb, Rk)
    tick()
    N = [jnp.where(strict, aa[:2 * T, :2 * T], 0.0) for aa in AA]
    Aak = [jnp.where(strict, aa[:2 * T, 2 * T:], 0.0) for aa in AA]
    Arb = [jnp.where(incl, aa[2 * T:, :2 * T], 0.0) for aa in AA]
    Ark = [jnp.where(incl, aa[2 * T:, 2 * T:], 0.0) for aa in AA]
    def live_rows(x, t0):
        return x if t0 == 0 else jnp.concatenate([x[t0:T, :], x[T + t0:, :]], axis=0)

    def all_rows(y, t0):
        if t0 == 0:
            return y
        zero = jnp.zeros((t0, y.shape[1]), y.dtype)
        return jnp.concatenate([zero, y[:T - t0, :], zero, y[T - t0:, :]], axis=0)

    W = each(lambda n: eye + n, N)
    P = N
    span = 2
    while span < T:
        t0 = span if span % BF16_SUBLANES == 0 else 0
        P = each(lambda p_: all_rows(_mm(live_rows(p_, t0), p_), t0), P)
        tick()
        W = each(lambda w, p_: w + all_rows(_mm(live_rows(w, t0), p_), t0), W, P)
        tick()
        span *= 2
    AkV = each(_mm, Aak, Vs)
    tick()
    X = each(lambda w, la, akv: _mm(w, jnp.concatenate([la, akv.astype(BF16)], axis=1)).astype(BF16),
             W, La, AkV)
    tick()
    Z = each(lambda arb, ark, x, vs:
             _mm(jnp.concatenate([arb.astype(BF16), ark.astype(BF16)], axis=1),
                 jnp.concatenate([x, jnp.concatenate([jnp.zeros_like(vs), vs], axis=1)], axis=0)),
             Arb, Ark, X, Vs)
    tick()
    Q = each(lambda z, lr, gm: (z[:, :PAIR] + lr.astype(F32)) * gm, Z, Lr, gm_row)
    Y0 = [z[:, PAIR:] for z in Z]
    tick()
    Mbd = each(lambda rbg, x, gm: _mm_tn(rbg, x[:, :PAIR]) * gm, Rbg, X, gm_row)
    NcT = each(lambda x, vs, rbg, rkg: _mm_tn(jnp.concatenate([x[:, PAIR:], vs], axis=0),
                                              jnp.concatenate([rbg, rkg], axis=0)), X, Vs, Rbg, Rkg)
    for _ in filler:
        pass

    S = [s_ref[p] for p in range(pairs)]
    for c in range(chunks):
        for p in range(pairs):
            i = c * pairs + p
            Ys = _mm_nt(Q[i], S[p]) + Y0[i]
            S[p] = S[p] * gT_row[i] + _mm_nt(S[p], Mbd[i]) + NcT[i]
            mu = jnp.sum(Ys, axis=-1, keepdims=True) * (1.0 / RW_HEAD)
            d = jnp.where(own, Ys - mu, 0.0)
            var = jnp.sum(d * d, axis=-1, keepdims=True) * (1.0 / RW_HEAD)
            dn = d * lax.rsqrt(var + GN_EPS)
            yn = (dn[:T, :] + dn[T:, :]) * gnw_ref[:, cols(p)] + gnb_ref[:, cols(p)]
            out = (yn + bonus[i].astype(F32)) * gate[i].astype(F32)
            o_ref[rows(c), cols(p)] = out.astype(BF16)
    for p in range(pairs):
        s_ref[p] = S[p]


def _swa_kernel(sink_ref, q_ref, kc_ref, kp_ref, vc_ref, vp_ref, qg_ref, kg_ref, o_ref, *,
                steps_per_seq, slopes, qb):
    first = (pl.program_id(0) % steps_per_seq) == 0
    mean_h = _head_ones(LANES, ATT_HEAD, 1.0 / ATT_HEAD)

    def head_rms(x, gain):
        parts = []
        for b in range(x.shape[1] // LANES):
            xb = x[:, b * LANES:(b + 1) * LANES]
            ms = _mm(xb * xb, mean_h)
            parts.append(xb * lax.rsqrt(ms + RMS_EPS))
        return jnp.concatenate(parts, axis=1) * gain

    q = head_rms(q_ref[...].astype(F32), qg_ref[...]) * (ATT_HEAD ** -0.5 * LOG2E)
    q = q.astype(BF16)
    kcat = jnp.concatenate([kp_ref[...], kc_ref[...]], axis=0).astype(F32)
    kcat = head_rms(kcat, kg_ref[...]).astype(BF16)
    vcat = jnp.concatenate([vp_ref[...], vc_ref[...]], axis=0)

    qi = lax.broadcasted_iota(jnp.int32, (BLOCK, 2 * BLOCK), 0)
    kj = lax.broadcasted_iota(jnp.int32, (BLOCK, 2 * BLOCK), 1)
    dist_i = BLOCK + qi - kj
    in_window = (dist_i >= 0) & (dist_i < WINDOW)
    first_key = jnp.where(first, BLOCK, 0)
    neg_dist_rest = jnp.where(in_window, -dist_i.astype(F32), NEG_BIG)
    neg_dist_first = jnp.where(in_window & (kj >= first_key), -dist_i.astype(F32), NEG_BIG)
    neg_dist = [neg_dist_first] + [neg_dist_rest] * (qb - 1)

    lane_q = lax.broadcasted_iota(jnp.int32, (BLOCK, LANES), 1)
    lane_kv = lax.broadcasted_iota(jnp.int32, (2 * BLOCK, LANES), 1)
    zero_q = jnp.zeros((BLOCK, LANES), BF16)
    one_kv = jnp.ones((2 * BLOCK, LANES), BF16)
    kv_lane_blocks = range(ATT_KV_DIM // LANES)
    k_all = [kcat[:, b * LANES:(b + 1) * LANES] for b in kv_lane_blocks]
    k_all_rolled = [pltpu.roll(kb, ATT_HEAD, 1) for kb in k_all]
    window = lambda x, t: x[t * BLOCK:(t + 2) * BLOCK, :]
    v_aug = [[jnp.where((lane_kv // ATT_HEAD) == (j % 2),
                        window(vcat[:, (j // 2) * LANES:(j // 2 + 1) * LANES], t), one_kv)
              for j in range(ATT_KV_HEADS)] for t in range(qb)]
    items = [(t, h) for t in range(qb) for h in range(ATT_HEADS)]
    kv_of = lambda h: h // ATT_GROUP
    aligned = lambda h: (h % 2) == (kv_of(h) % 2)

    def scores(t, h):
        qh = jnp.where((lane_q // ATT_HEAD) == (h % 2),
                       q[t * BLOCK:(t + 1) * BLOCK, (h // 2) * LANES:(h // 2 + 1) * LANES], zero_q)
        kh = window((k_all if aligned(h) else k_all_rolled)[kv_of(h) // 2], t)
        return _mm_nt(qh, kh) + (slopes[h] * LOG2E) * neg_dist[t]

    s = [scores(t, h) for t, h in items]
    sink = [sink_ref[h] * LOG2E for _, h in items]
    mx = [jnp.maximum(jnp.max(s_, axis=-1, keepdims=True), sk) for s_, sk in zip(s, sink)]
    p = [jnp.exp2(s_ - m_) for s_, m_ in zip(s, mx)]
    pv = [_mm(p_, v_aug[t][kv_of(h)]) for p_, (t, h) in zip(p, items)]
    pv_sw = [pltpu.roll(x, ATT_HEAD, 1) for x in pv]
    out = []
    for i, (t, h) in enumerate(items):
        num, den = (pv[i], pv_sw[i]) if aligned(h) else (pv_sw[i], pv[i])
        out.append(num * (1.0 / (den + jnp.exp2(sink[i] - mx[i]))))
    for t in range(qb):
        o = out[t * ATT_HEADS:(t + 1) * ATT_HEADS]
        o_ref[t * BLOCK:(t + 1) * BLOCK, :] = jnp.concatenate(
            [jnp.where((lane_q // ATT_HEAD) == 0, o[2 * m], o[2 * m + 1]) for m in range(ATT_HEADS // 2)],
            axis=1).astype(BF16)


def _swa(z, sinks, q_gain_t, k_gain_t, seq, qb=4):
    m = z.shape[0]
    rows = qb * BLOCK
    slopes = tuple(float(s) for s in
                   np.exp2(-8.0 * np.arange(1, ATT_HEADS + 1, dtype=np.float32) / ATT_HEADS).astype(np.float32))
    kv_cur = lambda c0: pl.BlockSpec((rows, ATT_KV_DIM), lambda n, c=c0 // ATT_KV_DIM: (n, c))
    kv_prev = lambda c0: pl.BlockSpec((BLOCK, ATT_KV_DIM),
                                      lambda n, c=c0 // ATT_KV_DIM: (jnp.maximum(n * qb - 1, 0), c))
    return pl.pallas_call(
        functools.partial(_swa_kernel, steps_per_seq=seq // rows, slopes=slopes, qb=qb),
        grid=(m // rows,),
        in_specs=[pl.BlockSpec(memory_space=pltpu.SMEM),
                  pl.BlockSpec((rows, ATT_Q_DIM), lambda n: (n, C_Q // ATT_Q_DIM)),
                  kv_cur(C_KA), kv_prev(C_KA), kv_cur(C_VA), kv_prev(C_VA),
                  pl.BlockSpec((1, ATT_Q_DIM), lambda n: (0, 0)),
                  pl.BlockSpec((1, ATT_KV_DIM), lambda n: (0, 0))],
        out_specs=pl.BlockSpec((rows, ATT_Q_DIM), lambda n: (n, 0)),
        out_shape=jax.ShapeDtypeStruct((m, ATT_Q_DIM), BF16),
        compiler_params=_cparams(("parallel",)),
        name="swa",
    )(sinks, z, z, z, z, z, q_gain_t, k_gain_t)


def _mix_out_kernel(orw_ref, oatt_ref, wb1_ref, wb2_ref, zg_ref, wo_ref, x_ref, g_ref, x1_ref, h2_ref):
    p_rw = jnp.dot(orw_ref[...], wb1_ref[...], preferred_element_type=F32)
    p_att = jnp.dot(oatt_ref[...], wb2_ref[...], preferred_element_type=F32)
    g_rw = jax.nn.sigmoid(zg_ref[:, :D_MODEL].astype(F32))
    g_att = jax.nn.sigmoid(zg_ref[:, D_MODEL:].astype(F32))
    mix = (g_rw * p_rw + g_att * p_att).astype(BF16)
    x1 = x_ref[...] + jnp.dot(mix, wo_ref[...], preferred_element_type=F32)
    x1_ref[...] = x1
    y = x1 * lax.rsqrt(jnp.mean(x1 * x1, axis=-1, keepdims=True) + RMS_EPS)
    h2_ref[...] = (y * g_ref[...]).astype(BF16)


def _mix_out(o_rw, o_att, w_branch_b, z, w_out_b, x2, g2, tm=512):
    m = x2.shape[0]
    const = lambda rows, cols, r0=0: pl.BlockSpec((rows, cols), lambda i: (r0, 0),
                                                  pipeline_mode=pl.Buffered(1))
    rows = lambda cols, c0=0: pl.BlockSpec((tm, cols), lambda i: (i, c0))
    return pl.pallas_call(
        _mix_out_kernel,
        grid=(m // tm,),
        in_specs=[rows(RW_DIM), rows(ATT_Q_DIM), const(RW_DIM, D_MODEL), const(ATT_Q_DIM, D_MODEL, 1),
                  rows(2 * D_MODEL, C_GRW // (2 * D_MODEL)), const(D_MODEL, D_MODEL),
                  rows(D_MODEL), const(1, D_MODEL)],
        out_specs=[rows(D_MODEL), rows(D_MODEL)],
        out_shape=[jax.ShapeDtypeStruct((m, D_MODEL), F32),
                   jax.ShapeDtypeStruct((m, D_MODEL), BF16)],
        compiler_params=_cparams(("parallel",)),
        name="mix_out",
    )(o_rw, o_att, w_branch_b, w_branch_b, z, w_out_b, x2, g2)


def _ffn_up_kernel(h_ref, wvb_ref, wgb_ref, cwv_ref, cwg_ref, cbv_ref, cbg_ref, wd_ref, a_ref, wdb_ref,
                   cv_ref, cg_ref, *, tm, sub, tiles_per_seq):
    halo = 8
    wdb_ref[...] = wd_ref[...].astype(BF16)

    @pl.when((pl.program_id(1) % tiles_per_seq) == 0)
    def _():
        cv_ref[...] = jnp.zeros_like(cv_ref)
        cg_ref[...] = jnp.zeros_like(cg_ref)

    row = lax.broadcasted_iota(jnp.int32, (halo, a_ref.shape[1]), 0)

    def conv(u, carry_ref, cw_ref, cb_ref):
        p1 = carry_ref[halo - 1:halo, :]
        p2 = carry_ref[halo - 2:halo - 1, :]
        r1 = pltpu.roll(u, 1, 0)
        r2 = pltpu.roll(u, 2, 0)
        prev1 = jnp.concatenate([jnp.where(row == 0, p1, r1[:halo, :]), r1[halo:, :]], axis=0)
        prev2 = jnp.concatenate(
            [jnp.where(row == 0, p2, jnp.where(row == 1, p1, r2[:halo, :])), r2[halo:, :]], axis=0)
        carry_ref[...] = u[sub - halo:sub, :]
        return prev2 * cw_ref[0:1, :] + prev1 * cw_ref[1:2, :] + u * cw_ref[2:3, :] + cb_ref[...]

    for s in range(tm // sub):
        h = h_ref[s * sub:(s + 1) * sub, :]
        val = conv(jnp.dot(h, wvb_ref[...], preferred_element_type=F32), cv_ref, cwv_ref, cbv_ref)
        gate = conv(jnp.dot(h, wgb_ref[...], preferred_element_type=F32), cg_ref, cwg_ref, cbg_ref)
        a_ref[s * sub:(s + 1) * sub, :] = (gate * jax.nn.sigmoid(gate) * val).astype(BF16)


def _ffn_up(h2, w_up_b, conv_w, conv_b, w_down, seq, tm=1024, tn=512, sub=256):
    m = h2.shape[0]
    nj = D_FF // tn
    ni = m // tm
    wd_rows = w_down.shape[0] // (nj * ni)
    wd_spec = pl.BlockSpec((wd_rows, D_MODEL), lambda j, i: (j * ni + i, 0))
    return pl.pallas_call(
        functools.partial(_ffn_up_kernel, tm=tm, sub=sub, tiles_per_seq=seq // tm),
        grid=(nj, ni),
        in_specs=[pl.BlockSpec((tm, D_MODEL), lambda j, i: (i, 0)),
                  pl.BlockSpec((D_MODEL, tn), lambda j, i: (0, j)),
                  pl.BlockSpec((D_MODEL, tn), lambda j, i: (0, nj + j)),
                  pl.BlockSpec((3, tn), lambda j, i: (0, j)),
                  pl.BlockSpec((3, tn), lambda j, i: (0, nj + j)),
                  pl.BlockSpec((1, tn), lambda j, i: (0, j)),
                  pl.BlockSpec((1, tn), lambda j, i: (0, nj + j)),
                  wd_spec],
        out_specs=[pl.BlockSpec((tm, tn), lambda j, i: (i, j)), wd_spec],
        out_shape=[jax.ShapeDtypeStruct((m, D_FF), BF16), jax.ShapeDtypeStruct(w_down.shape, BF16)],
        scratch_shapes=[pltpu.VMEM((8, tn), F32), pltpu.VMEM((8, tn), F32)],
        compiler_params=_cparams(("parallel", "arbitrary")),
        name="ffn_up",
    )(h2, w_up_b, w_up_b, conv_w, conv_w, conv_b, conv_b, w_down)


def _ffn_down_kernel(a_ref, w_ref, x1_ref, o_ref):
    o_ref[...] = x1_ref[...] + jnp.dot(a_ref[...], w_ref[...], preferred_element_type=F32)


def _ffn_down(act, w_down_b, x1, tm=512, tn=1024):
    m = act.shape[0]
    return pl.pallas_call(
        _ffn_down_kernel,
        grid=(D_MODEL // tn, m // tm),
        in_specs=[pl.BlockSpec((tm, D_FF), lambda j, i: (i, 0)),
                  pl.BlockSpec((D_FF, tn), lambda j, i: (0, j)),
                  pl.BlockSpec((tm, tn), lambda j, i: (i, j))],
        out_specs=pl.BlockSpec((tm, tn), lambda j, i: (i, j)),
        out_shape=jax.ShapeDtypeStruct((m, D_MODEL), F32),
        compiler_params=_cparams(("parallel", "arbitrary")),
        name="ffn_down",
    )(act, w_down_b, x1)


def _pad_cols(w, n):
    return jnp.pad(w, ((0, 0), (0, n - w.shape[1])))


def _pad_rows(w, n):
    return jnp.pad(w, ((0, n - w.shape[0]), (0, 0)))


def _layer(x2, batch, seq, w_in_p, norm1_g, rw_mu, rw_w0, rw_w2, rw_a0, rw_a2, rw_g2, rw_k_k, rw_k_a,
           rw_r_k, rw_gn_w, rw_gn_b, q_norm_g, k_norm_g, attn_sinks, w_branch, w_out,
           norm2_g, w_up, conv_w, conv_b, w_down):
    row = lambda v: v.reshape(1, -1).astype(F32)
    o3 = 3 * RW_DIM
    o4 = o3 + DECAY_LORA
    o5 = o4 + ICLR_LORA
    o6 = o5 + GATE_LORA
    mu = rw_mu.reshape(1, -1)
    mu_l = jnp.concatenate([_pad_cols(mu[:, o3:o4], LORA_W), _pad_cols(mu[:, o4:o5], LORA_A),
                            mu[:, o5:o6]], axis=1)
    w2p = _pad_rows(rw_w2, LORA_W).astype(BF16)
    a2p = _pad_rows(rw_a2, LORA_A).astype(BF16)

    z, (w_branch_b, w_out_b, w_up_b) = _in_proj(x2, row(norm1_g), w_in_p, (w_branch, w_out, w_up))
    o_rw = _rwkv(z, batch, seq, mu[:, :RW_DIM], mu[:, RW_DIM:2 * RW_DIM], mu[:, 2 * RW_DIM:o3], mu_l,
                 row(rw_w0), row(rw_a0), row(rw_k_k), row(rw_k_a), row(rw_r_k),
                 w2p, a2p, rw_g2.astype(BF16), row(rw_gn_w), row(rw_gn_b))
    o_att = _swa(z, attn_sinks.astype(F32), jnp.tile(row(q_norm_g), (1, ATT_HEADS)),
                 jnp.tile(row(k_norm_g), (1, ATT_KV_HEADS)), seq)
    x1, h2 = _mix_out(o_rw, o_att, w_branch_b, z, w_out_b, x2, row(norm2_g))
    act, w_down_b = _ffn_up(h2, w_up_b, conv_w.astype(F32), row(conv_b), w_down, seq)
    return _ffn_down(act, w_down_b, x1)


def kernel(x, norm1_g, w_in, rw_mu, rw_w0, rw_w2, rw_a0, rw_a2, rw_g2, rw_k_k, rw_k_a, rw_r_k,
           rw_gn_w, rw_gn_b, q_norm_g, k_norm_g, attn_sinks, w_branch, w_out, norm2_g, w_up,
           conv_w, conv_b, w_down):
    batch, seq, d = x.shape
    x2 = x.reshape(batch * seq, d)
    params = (norm1_g, rw_mu, rw_w0, rw_w2, rw_a0, rw_a2, rw_g2, rw_k_k, rw_k_a, rw_r_k,
              rw_gn_w, rw_gn_b, q_norm_g, k_norm_g, attn_sinks, w_branch, w_out, norm2_g, w_up,
              conv_w, conv_b, w_down)
    for layer in range(norm1_g.shape[0]):
        x2 = _layer(x2, batch, seq, _w_in_layout(w_in, layer), *(p[layer] for p in params))
    return x2.reshape(batch, seq, d)
```

```python
import functools

import jax
import jax.numpy as jnp
import numpy as np
from jax import lax
from jax.experimental import pallas as pl
from jax.experimental.pallas import tpu as pltpu

F32 = jnp.float32
BF16 = jnp.bfloat16

LANES = 128
BF16_SUBLANES = 16
VMEM_LIMIT = 56 * 1024 * 1024

D_MODEL = 2048
RW_HEADS = 16
RW_HEAD = 64
RW_DIM = RW_HEADS * RW_HEAD
DECAY_LORA = 96
ICLR_LORA = 96
GATE_LORA = 256
GN_EPS = 64e-5
ATT_HEADS = 16
ATT_KV_HEADS = 4
ATT_GROUP = ATT_HEADS // ATT_KV_HEADS
ATT_HEAD = 64
ATT_Q_DIM = ATT_HEADS * ATT_HEAD
ATT_KV_DIM = ATT_KV_HEADS * ATT_HEAD
WINDOW = 128
BLOCK = 128
D_FF = 5632
RMS_EPS = 1e-6
NEG_BIG = -1e30
LOG2E = 1.4426950408889634

C_R = 0
C_K = RW_DIM
C_V = 2 * RW_DIM
C_Q = 3 * RW_DIM
C_GRW = C_Q + ATT_Q_DIM
C_GATT = C_GRW + D_MODEL
C_LORA = C_GATT + D_MODEL
LORA_W = 128
LORA_A = 128
LORA_COLS = LORA_W + LORA_A + GATE_LORA
C_KA = C_LORA + LORA_COLS
C_VA = C_KA + ATT_KV_DIM
Z_COLS = C_VA + ATT_KV_DIM

CHUNK = 64
PAIR = 2 * RW_HEAD
N_PAIRS = RW_DIM // PAIR


def _cparams(sem):
    return pltpu.CompilerParams(dimension_semantics=sem, vmem_limit_bytes=VMEM_LIMIT)


def _mm(a, b):
    return jnp.dot(a.astype(BF16), b.astype(BF16), preferred_element_type=F32)


def _mm_nt(a, b):
    return lax.dot_general(a.astype(BF16), b.astype(BF16), (((1,), (1,)), ((), ())),
                           preferred_element_type=F32)


def _mm_tn(a, b):
    return lax.dot_general(a.astype(BF16), b.astype(BF16), (((0,), (0,)), ((), ())),
                           preferred_element_type=F32)


def _mm_split(m01, x):
    hi = x.astype(BF16)
    lo = (x - hi.astype(F32)).astype(BF16)
    return (jnp.dot(m01, hi, preferred_element_type=F32)
            + jnp.dot(m01, lo, preferred_element_type=F32))


def _mm_split_r(x, m01):
    hi = x.astype(BF16)
    lo = (x - hi.astype(F32)).astype(BF16)
    return (jnp.dot(hi, m01, preferred_element_type=F32)
            + jnp.dot(lo, m01, preferred_element_type=F32))


def _head_ones(n, head, scale):
    r = lax.broadcasted_iota(jnp.int32, (n, n), 0) // head
    c = lax.broadcasted_iota(jnp.int32, (n, n), 1) // head
    return jnp.where(r == c, scale, 0.0).astype(BF16)


def _w_in_layout_kernel(wt_ref, o_ref, *, moves, pad):
    o_ref[pad[0]:pad[1], :] = jnp.zeros((pad[1] - pad[0], o_ref.shape[1]), BF16)
    for dst, src, width in moves:
        o_ref[dst:dst + width, :] = wt_ref[src:src + width, :].astype(BF16)


def _w_in_layout(w_in_layers, layer, cols=256):
    _, d, n = w_in_layers.shape
    wt_layers = jnp.swapaxes(w_in_layers, 1, 2)
    o3 = 3 * RW_DIM
    o4 = o3 + DECAY_LORA
    o5 = o4 + ICLR_LORA
    o6 = o5 + GATE_LORA
    oq = o6 + ATT_Q_DIM
    ok = oq + ATT_KV_DIM
    ov = ok + ATT_KV_DIM
    moves = ((C_R, 0, o3), (C_Q, o6, ATT_Q_DIM), (C_GRW, ov, 2 * D_MODEL),
             (C_LORA, o3, DECAY_LORA), (C_LORA + LORA_W, o4, ICLR_LORA),
             (C_LORA + LORA_W + LORA_A, o5, GATE_LORA), (C_KA, oq, ATT_KV_DIM), (C_VA, ok, ATT_KV_DIM))
    return pl.pallas_call(
        functools.partial(_w_in_layout_kernel, moves=moves, pad=(C_LORA, C_LORA + LORA_W + LORA_A)),
        grid=(d // cols,),
        in_specs=[pl.BlockSpec((None, n, cols), lambda i: (layer, 0, i))],
        out_specs=pl.BlockSpec((Z_COLS, cols), lambda i: (0, i)),
        out_shape=jax.ShapeDtypeStruct((Z_COLS, d), BF16),
        compiler_params=_cparams(("parallel",)),
        name="w_in_layout",
    )(wt_layers)


def _in_proj_kernel(x_ref, g_ref, wt_ref, *refs, n_side):
    side_in, z_ref, side_out, h_ref = refs[:n_side], refs[n_side], refs[n_side + 1:-1], refs[-1]

    @pl.when(pl.program_id(1) == 0)
    def _():
        xf = x_ref[...]
        y = xf * lax.rsqrt(jnp.mean(xf * xf, axis=-1, keepdims=True) + RMS_EPS)
        h_ref[...] = (y * g_ref[...]).astype(BF16)

    z_ref[...] = _mm_nt(h_ref[...], wt_ref[...]).astype(BF16)
    for w_ref, wb_ref in zip(side_in, side_out):
        wb_ref[...] = w_ref[...].astype(BF16)


def _in_proj(x2, g1, w_in_t, side_weights, tm=1024, tn=1536, side_rows=64):
    m, d = x2.shape
    n = w_in_t.shape[0]
    nj = n // tn
    for w in side_weights:
        assert w.shape[0] // side_rows <= (m // tm) * nj, "not enough grid steps to cast this weight"
    side_specs = [pl.BlockSpec((side_rows, w.shape[1]),
                               lambda i, j, last=w.shape[0] // side_rows - 1: (jnp.minimum(i * nj + j, last), 0))
                  for w in side_weights]
    outs = pl.pallas_call(
        functools.partial(_in_proj_kernel, n_side=len(side_weights)),
        grid=(m // tm, nj),
        in_specs=[pl.BlockSpec((tm, d), lambda i, j: (i, 0)),
                  pl.BlockSpec((1, d), lambda i, j: (0, 0)),
                  pl.BlockSpec((tn, d), lambda i, j: (j, 0))] + side_specs,
        out_specs=[pl.BlockSpec((tm, tn), lambda i, j: (i, j))] + side_specs,
        out_shape=[jax.ShapeDtypeStruct((m, n), BF16)]
        + [jax.ShapeDtypeStruct(w.shape, BF16) for w in side_weights],
        scratch_shapes=[pltpu.VMEM((tm, d), BF16)],
        compiler_params=_cparams(("arbitrary", "arbitrary")),
        name="in_proj",
    )(x2, g1, w_in_t, *side_weights)
    return outs[0], outs[1:]


def _rwkv_prep_kernel(zr_ref, zk_ref, zv_ref, zl_ref, pr_ref, pk_ref, pv_ref, pli_ref,
                      mur_ref, muk_ref, muv_ref, mul_ref,
                      w0_ref, a0_ref, kk_ref, ka_ref, rk_ref, w2_ref, a2_ref, g2_ref,
                      at_ref, rt_ref, bt_ref, kt_ref, bg_ref, kg_ref, v_ref, g_ref, bonus_ref,
                      gT_ref, gm_ref, *, tm, first):
    keep = jnp.where(first, 0.0, 1.0)
    nc = tm // CHUNK
    last = slice(BF16_SUBLANES - 1, BF16_SUBLANES)

    def shifted(z, prev_last, mu):
        z = z.astype(F32)
        row = lax.broadcasted_iota(jnp.int32, z.shape, 0)
        zprev = jnp.where(row == 0, prev_last.astype(F32) * keep, pltpu.roll(z, 1, 0))
        return z + (zprev - z) * mu

    lo = shifted(zl_ref[...], pli_ref[last, :], mul_ref[...])
    tanh_wd = jnp.tanh(lo[:, :LORA_W]).astype(BF16)
    ad = lo[:, LORA_W:LORA_W + LORA_A].astype(BF16)
    sig_gd = jax.nn.sigmoid(lo[:, LORA_W + LORA_A:]).astype(BF16)

    ti = lax.broadcasted_iota(jnp.int32, (tm, tm), 0)
    si = lax.broadcasted_iota(jnp.int32, (tm, tm), 1)
    tri = jnp.where(((ti // CHUNK) == (si // CHUNK)) & (si <= ti), 1.0, 0.0).astype(BF16)
    ones_h = _head_ones(PAIR, RW_HEAD, 1.0)
    w_lora = jnp.dot(tanh_wd, w2_ref[...], preferred_element_type=F32)
    a_lora = jnp.dot(ad, a2_ref[...], preferred_element_type=F32)
    g_ref[...] = jnp.dot(sig_gd, g2_ref[...], preferred_element_type=F32).astype(BF16)
    yield

    stash = []
    for p in range(N_PAIRS):
        cs = slice(p * PAIR, (p + 1) * PAIR)
        r = shifted(zr_ref[:, cs], pr_ref[last, cs], mur_ref[:, cs])
        k = shifted(zk_ref[:, cs], pk_ref[last, cs], muk_ref[:, cs])
        v = shifted(zv_ref[:, cs], pv_ref[last, cs], muv_ref[:, cs])

        wpre = w0_ref[:, cs] + w_lora[:, cs]
        w = jnp.minimum(wpre, 0.0) - jnp.log1p(jnp.exp(-jnp.abs(wpre))) - 0.5
        lw = -jnp.exp(w)
        a = jax.nn.sigmoid(a0_ref[:, cs] + a_lora[:, cs])

        kk = k * kk_ref[:, cs]
        kk = kk * lax.rsqrt(jnp.maximum(_mm(kk * kk, ones_h), 1e-24))
        k2 = k * (1.0 + (a - 1.0) * ka_ref[:, cs])
        bonus_ref[:, cs] = (_mm(r * k2 * rk_ref[:, cs], ones_h) * v).astype(BF16)
        v_ref[:, cs] = v.astype(BF16)
        stash.append((r, k2, -kk, kk * a, lw))
        yield

    c_all = _mm_split(tri, jnp.concatenate([st[4] for st in stash], axis=1))
    yield

    for p in range(N_PAIRS):
        cs = slice(p * PAIR, (p + 1) * PAIR)
        r, k2, ka_vec, kb_vec, lw = stash[p]
        c = c_all[:, cs]
        c3 = c.reshape(nc, CHUNK, PAIR)
        c_mid = c3[:, CHUNK // 2 - 1:CHUNK // 2, :]
        c_end = c3[:, CHUNK - 1:CHUNK, :]
        cm = jnp.broadcast_to(c_mid, c3.shape).reshape(tm, PAIR)
        cT = jnp.broadcast_to(c_end, c3.shape).reshape(tm, PAIR)

        e_in = jnp.exp(c - cm)
        e_out = jnp.exp(cm - c)
        e_end = jnp.exp(cT - c)
        at_ref[:, cs] = (ka_vec * jnp.exp(c - lw - cm)).astype(BF16)
        rt_ref[:, cs] = (r * e_in).astype(BF16)
        bt_ref[:, cs] = (kb_vec * e_out).astype(BF16)
        kt_ref[:, cs] = (k2 * e_out).astype(BF16)
        bg_ref[:, cs] = (kb_vec * e_end).astype(BF16)
        kg_ref[:, cs] = (k2 * e_end).astype(BF16)
        gT_ref[:, :, cs] = jnp.exp(c_end)
        gm_ref[:, :, cs] = jnp.exp(c_mid)
        yield


def _rwkv_kernel(*refs, tb, nb):
    prep_in, (gnw_ref, gnb_ref, o_ref), scratch = refs[:20], refs[20:23], refs[23:]
    operands, s_ref = scratch[:11], scratch[11]
    j = pl.program_id(1)
    prep = functools.partial(_rwkv_prep_kernel, *prep_in, *operands, tm=tb, first=j == 0)
    chunk = functools.partial(_rwkv_chunk_kernel, *operands, gnw_ref, gnb_ref, o_ref, s_ref,
                              chunks=tb // CHUNK, pairs=N_PAIRS, first=j == 1)

    @pl.when(j == 0)
    def _():
        for _ in prep():
            pass

    @pl.when((j > 0) & (j < nb))
    def _():
        chunk(filler=prep())

    @pl.when(j == nb)
    def _():
        chunk()


def _rwkv(z, batch, seq, mu_r, mu_k, mu_v, mu_l, w0, a0, k_k, k_a, r_k, w2p, a2p, g2, gn_w, gn_b, tb=128):
    m = z.shape[0]
    nb = seq // tb
    pb = tb // BF16_SUBLANES
    blk = lambda b, j: b * nb + jnp.minimum(j, nb - 1)

    def cur(col0, width):
        return pl.BlockSpec((tb, width), lambda b, j, c0=col0 // width: (blk(b, j), c0))

    def prev(col0, width):
        return pl.BlockSpec((BF16_SUBLANES, width),
                            lambda b, j, c0=col0 // width: (jnp.maximum(blk(b, j) * pb - 1, 0), c0))

    def whole(rows, width):
        return pl.BlockSpec((rows, width), lambda b, j: (0, 0))

    in_specs = [cur(C_R, RW_DIM), cur(C_K, RW_DIM), cur(C_V, RW_DIM), cur(C_LORA, LORA_COLS),
                prev(C_R, RW_DIM), prev(C_K, RW_DIM), prev(C_V, RW_DIM), prev(C_LORA, LORA_COLS),
                whole(1, RW_DIM), whole(1, RW_DIM), whole(1, RW_DIM), whole(1, LORA_COLS),
                whole(1, RW_DIM), whole(1, RW_DIM), whole(1, RW_DIM), whole(1, RW_DIM), whole(1, RW_DIM),
                whole(LORA_W, RW_DIM), whole(LORA_A, RW_DIM), whole(GATE_LORA, RW_DIM),
                whole(1, RW_DIM), whole(1, RW_DIM)]
    big = pltpu.VMEM((tb, RW_DIM), BF16)
    per_chunk = pltpu.VMEM((tb // CHUNK, 1, RW_DIM), F32)
    return pl.pallas_call(
        functools.partial(_rwkv_kernel, tb=tb, nb=nb),
        grid=(batch, nb + 1),
        in_specs=in_specs,
        out_specs=pl.BlockSpec((tb, RW_DIM), lambda b, j: (b * nb + jnp.maximum(j - 1, 0), 0)),
        out_shape=jax.ShapeDtypeStruct((m, RW_DIM), BF16),
        scratch_shapes=[big] * 9 + [per_chunk] * 2 + [pltpu.VMEM((N_PAIRS, PAIR, PAIR), F32)],
        compiler_params=_cparams(("parallel", "arbitrary")),
        name="rwkv",
    )(z, z, z, z, z, z, z, z, mu_r, mu_k, mu_v, mu_l, w0, a0, k_k, k_a, r_k, w2p, a2p, g2, gn_w, gn_b)


def _rwkv_chunk_kernel(at_ref, rt_ref, bt_ref, kt_ref, bg_ref, kg_ref, v_ref, g_ref, bonus_ref,
                       gT_ref, gm_ref, gnw_ref, gnb_ref, o_ref, s_ref, *, chunks, pairs, first,
                       filler=None):
    @pl.when(first)
    def _():
        s_ref[...] = jnp.zeros_like(s_ref)

    T = CHUNK
    lane = lax.broadcasted_iota(jnp.int32, (T, PAIR), 1)
    head0 = lane < RW_HEAD
    ri = lax.broadcasted_iota(jnp.int32, (2 * T, 2 * T), 0)
    ci = lax.broadcasted_iota(jnp.int32, (2 * T, 2 * T), 1)
    same = (ri // T) == (ci // T)
    strict = same & ((ri % T) > (ci % T))
    incl = same & ((ri % T) >= (ci % T))
    eye = jnp.where(ri == ci, 1.0, 0.0)
    own = (ri // T) == (ci // RW_HEAD)

    items = [(p, c) for c in range(chunks) for p in range(pairs)]
    rows = lambda c: slice(c * T, (c + 1) * T)
    cols = lambda p: slice(p * PAIR, (p + 1) * PAIR)

    def stack(ref):
        out = []
        for p, c in items:
            x = ref[rows(c), cols(p)]
            zero = jnp.zeros_like(x)
            out.append(jnp.concatenate([jnp.where(head0, x, zero), jnp.where(head0, zero, x)], axis=0))
        return out

    each = lambda f, *ls: [f(*xs) for xs in zip(*ls)]
    La, Lr, Rb, Rk = stack(at_ref), stack(rt_ref), stack(bt_ref), stack(kt_ref)
    Rbg, Rkg, Vs = stack(bg_ref), stack(kg_ref), stack(v_ref)
    gm_row = [gm_ref[c][:, cols(p)] for p, c in items]
    gT_row = [gT_ref[c][:, cols(p)] for p, c in items]
    gate = [g_ref[rows(c), cols(p)] for p, c in items]
    bonus = [bonus_ref[rows(c), cols(p)] for p, c in items]
    filler = iter(()) if filler is None else filler
    tick = lambda: next(filler, None)

    AA = each(lambda la, lr, rb, rk: _mm_nt(jnp.concatenate([la, lr], axis=0),
                                            jnp.concatenate([rb, rk], axis=0)), La, Lr, Rb, Rk)
    tick()
    N = [jnp.where(strict, aa[:2 * T, :2 * T], 0.0) for aa in AA]
    Aak = [jnp.where(strict, aa[:2 * T, 2 * T:], 0.0) for aa in AA]
    Arb = [jnp.where(incl, aa[2 * T:, :2 * T], 0.0) for aa in AA]
    Ark = [jnp.where(incl, aa[2 * T:, 2 * T:], 0.0) for aa in AA]
    def live_rows(x, t0):
        return x if t0 == 0 else jnp.concatenate([x[t0:T, :], x[T + t0:, :]], axis=0)

    def all_rows(y, t0):
        if t0 == 0:
            return y
        zero = jnp.zeros((t0, y.shape[1]), y.dtype)
        return jnp.concatenate([zero, y[:T - t0, :], zero, y[T - t0:, :]], axis=0)

    W = each(lambda n: eye + n, N)
    P = N
    span = 2
    while span < T:
        t0 = span if span % BF16_SUBLANES == 0 else 0
        P = each(lambda p_: all_rows(_mm(live_rows(p_, t0), p_), t0), P)
        tick()
        W = each(lambda w, p_: w + all_rows(_mm(live_rows(w, t0), p_), t0), W, P)
        tick()
        span *= 2
    AkV = each(_mm, Aak, Vs)
    tick()
    X = each(lambda w, la, akv: _mm(w, jnp.concatenate([la, akv.astype(BF16)], axis=1)).astype(BF16),
             W, La, AkV)
    tick()
    Z = each(lambda arb, ark, x, vs:
             _mm(jnp.concatenate([arb.astype(BF16), ark.astype(BF16)], axis=1),
                 jnp.concatenate([x, jnp.concatenate([jnp.zeros_like(vs), vs], axis=1)], axis=0)),
             Arb, Ark, X, Vs)
    tick()
    Q = each(lambda z, lr, gm: (z[:, :PAIR] + lr.astype(F32)) * gm, Z, Lr, gm_row)
    Y0 = [z[:, PAIR:] for z in Z]
    tick()
    Mbd = each(lambda rbg, x, gm: _mm_tn(rbg, x[:, :PAIR]) * gm, Rbg, X, gm_row)
    NcT = each(lambda x, vs, rbg, rkg: _mm_tn(jnp.concatenate([x[:, PAIR:], vs], axis=0),
                                              jnp.concatenate([rbg, rkg], axis=0)), X, Vs, Rbg, Rkg)
    for _ in filler:
        pass

    S = [s_ref[p] for p in range(pairs)]
    for c in range(chunks):
        for p in range(pairs):
            i = c * pairs + p
            Ys = _mm_nt(Q[i], S[p]) + Y0[i]
            S[p] = S[p] * gT_row[i] + _mm_nt(S[p], Mbd[i]) + NcT[i]
            mu = jnp.sum(Ys, axis=-1, keepdims=True) * (1.0 / RW_HEAD)
            d = jnp.where(own, Ys - mu, 0.0)
            var = jnp.sum(d * d, axis=-1, keepdims=True) * (1.0 / RW_HEAD)
            dn = d * lax.rsqrt(var + GN_EPS)
            yn = (dn[:T, :] + dn[T:, :]) * gnw_ref[:, cols(p)] + gnb_ref[:, cols(p)]
            out = (yn + bonus[i].astype(F32)) * gate[i].astype(F32)
            o_ref[rows(c), cols(p)] = out.astype(BF16)
    for p in range(pairs):
        s_ref[p] = S[p]


def _swa_kernel(sink_ref, q_ref, kc_ref, kp_ref, vc_ref, vp_ref, qg_ref, kg_ref, o_ref, *,
                steps_per_seq, slopes, qb):
    first = (pl.program_id(0) % steps_per_seq) == 0
    mean_h = _head_ones(LANES, ATT_HEAD, 1.0 / ATT_HEAD)

    def head_rms(x, gain):
        parts = []
        for b in range(x.shape[1] // LANES):
            xb = x[:, b * LANES:(b + 1) * LANES]
            ms = _mm(xb * xb, mean_h)
            parts.append(xb * lax.rsqrt(ms + RMS_EPS))
        return jnp.concatenate(parts, axis=1) * gain

    q = head_rms(q_ref[...].astype(F32), qg_ref[...]) * (ATT_HEAD ** -0.5 * LOG2E)
    q = q.astype(BF16)
    kcat = jnp.concatenate([kp_ref[...], kc_ref[...]], axis=0).astype(F32)
    kcat = head_rms(kcat, kg_ref[...]).astype(BF16)
    vcat = jnp.concatenate([vp_ref[...], vc_ref[...]], axis=0)

    qi = lax.broadcasted_iota(jnp.int32, (BLOCK, 2 * BLOCK), 0)
    kj = lax.broadcasted_iota(jnp.int32, (BLOCK, 2 * BLOCK), 1)
    dist_i = BLOCK + qi - kj
    in_window = (dist_i >= 0) & (dist_i < WINDOW)
    first_key = jnp.where(first, BLOCK, 0)
    neg_dist_rest = jnp.where(in_window, -dist_i.astype(F32), NEG_BIG)
    neg_dist_first = jnp.where(in_window & (kj >= first_key), -dist_i.astype(F32), NEG_BIG)
    neg_dist = [neg_dist_first] + [neg_dist_rest] * (qb - 1)

    lane_q = lax.broadcasted_iota(jnp.int32, (BLOCK, LANES), 1)
    lane_kv = lax.broadcasted_iota(jnp.int32, (2 * BLOCK, LANES), 1)
    zero_q = jnp.zeros((BLOCK, LANES), BF16)
    one_kv = jnp.ones((2 * BLOCK, LANES), BF16)
    kv_lane_blocks = range(ATT_KV_DIM // LANES)
    k_all = [kcat[:, b * LANES:(b + 1) * LANES] for b in kv_lane_blocks]
    k_all_rolled = [pltpu.roll(kb, ATT_HEAD, 1) for kb in k_all]
    window = lambda x, t: x[t * BLOCK:(t + 2) * BLOCK, :]
    v_aug = [[jnp.where((lane_kv // ATT_HEAD) == (j % 2),
                        window(vcat[:, (j // 2) * LANES:(j // 2 + 1) * LANES], t), one_kv)
              for j in range(ATT_KV_HEADS)] for t in range(qb)]
    items = [(t, h) for t in range(qb) for h in range(ATT_HEADS)]
    kv_of = lambda h: h // ATT_GROUP
    aligned = lambda h: (h % 2) == (kv_of(h) % 2)

    def scores(t, h):
        qh = jnp.where((lane_q // ATT_HEAD) == (h % 2),
                       q[t * BLOCK:(t + 1) * BLOCK, (h // 2) * LANES:(h // 2 + 1) * LANES], zero_q)
        kh = window((k_all if aligned(h) else k_all_rolled)[kv_of(h) // 2], t)
        return _mm_nt(qh, kh) + (slopes[h] * LOG2E) * neg_dist[t]

    s = [scores(t, h) for t, h in items]
    sink = [sink_ref[h] * LOG2E for _, h in items]
    mx = [jnp.maximum(jnp.max(s_, axis=-1, keepdims=True), sk) for s_, sk in zip(s, sink)]
    p = [jnp.exp2(s_ - m_) for s_, m_ in zip(s, mx)]
    pv = [_mm(p_, v_aug[t][kv_of(h)]) for p_, (t, h) in zip(p, items)]
    pv_sw = [pltpu.roll(x, ATT_HEAD, 1) for x in pv]
    out = []
    for i, (t, h) in enumerate(items):
        num, den = (pv[i], pv_sw[i]) if aligned(h) else (pv_sw[i], pv[i])
        out.append(num * (1.0 / (den + jnp.exp2(sink[i] - mx[i]))))
    for t in range(qb):
        o = out[t * ATT_HEADS:(t + 1) * ATT_HEADS]
        o_ref[t * BLOCK:(t + 1) * BLOCK, :] = jnp.concatenate(
            [jnp.where((lane_q // ATT_HEAD) == 0, o[2 * m], o[2 * m + 1]) for m in range(ATT_HEADS // 2)],
            axis=1).astype(BF16)


def _swa(z, sinks, q_gain_t, k_gain_t, seq, qb=8):
    m = z.shape[0]
    rows = qb * BLOCK
    slopes = tuple(float(s) for s in
                   np.exp2(-8.0 * np.arange(1, ATT_HEADS + 1, dtype=np.float32) / ATT_HEADS).astype(np.float32))
    kv_cur = lambda c0: pl.BlockSpec((rows, ATT_KV_DIM), lambda n, c=c0 // ATT_KV_DIM: (n, c))
    kv_prev = lambda c0: pl.BlockSpec((BLOCK, ATT_KV_DIM),
                                      lambda n, c=c0 // ATT_KV_DIM: (jnp.maximum(n * qb - 1, 0), c))
    return pl.pallas_call(
        functools.partial(_swa_kernel, steps_per_seq=seq // rows, slopes=slopes, qb=qb),
        grid=(m // rows,),
        in_specs=[pl.BlockSpec(memory_space=pltpu.SMEM),
                  pl.BlockSpec((rows, ATT_Q_DIM), lambda n: (n, C_Q // ATT_Q_DIM)),
                  kv_cur(C_KA), kv_prev(C_KA), kv_cur(C_VA), kv_prev(C_VA),
                  pl.BlockSpec((1, ATT_Q_DIM), lambda n: (0, 0)),
                  pl.BlockSpec((1, ATT_KV_DIM), lambda n: (0, 0))],
        out_specs=pl.BlockSpec((rows, ATT_Q_DIM), lambda n: (n, 0)),
        out_shape=jax.ShapeDtypeStruct((m, ATT_Q_DIM), BF16),
        compiler_params=_cparams(("parallel",)),
        name="swa",
    )(sinks, z, z, z, z, z, q_gain_t, k_gain_t)


def _mix_out_kernel(orw_ref, oatt_ref, wb1_ref, wb2_ref, zg_ref, wo_ref, x_ref, g_ref, x1_ref, h2_ref):
    p_rw = jnp.dot(orw_ref[...], wb1_ref[...], preferred_element_type=F32)
    p_att = jnp.dot(oatt_ref[...], wb2_ref[...], preferred_element_type=F32)
    g_rw = jax.nn.sigmoid(zg_ref[:, :D_MODEL].astype(F32))
    g_att = jax.nn.sigmoid(zg_ref[:, D_MODEL:].astype(F32))
    mix = (g_rw * p_rw + g_att * p_att).astype(BF16)
    x1 = x_ref[...] + jnp.dot(mix, wo_ref[...], preferred_element_type=F32)
    x1_ref[...] = x1
    y = x1 * lax.rsqrt(jnp.mean(x1 * x1, axis=-1, keepdims=True) + RMS_EPS)
    h2_ref[...] = (y * g_ref[...]).astype(BF16)


def _mix_out(o_rw, o_att, w_branch_b, z, w_out_b, x2, g2, tm=512):
    m = x2.shape[0]
    const = lambda rows, cols, r0=0: pl.BlockSpec((rows, cols), lambda i: (r0, 0),
                                                  pipeline_mode=pl.Buffered(1))
    rows = lambda cols, c0=0: pl.BlockSpec((tm, cols), lambda i: (i, c0))
    return pl.pallas_call(
        _mix_out_kernel,
        grid=(m // tm,),
        in_specs=[rows(RW_DIM), rows(ATT_Q_DIM), const(RW_DIM, D_MODEL), const(ATT_Q_DIM, D_MODEL, 1),
                  rows(2 * D_MODEL, C_GRW // (2 * D_MODEL)), const(D_MODEL, D_MODEL),
                  rows(D_MODEL), const(1, D_MODEL)],
        out_specs=[rows(D_MODEL), rows(D_MODEL)],
        out_shape=[jax.ShapeDtypeStruct((m, D_MODEL), F32),
                   jax.ShapeDtypeStruct((m, D_MODEL), BF16)],
        compiler_params=_cparams(("parallel",)),
        name="mix_out",
    )(o_rw, o_att, w_branch_b, w_branch_b, z, w_out_b, x2, g2)


def _ffn_up_kernel(h_ref, wvb_ref, wgb_ref, cwv_ref, cwg_ref, cbv_ref, cbg_ref, wd_ref, a_ref, wdb_ref,
                   cv_ref, cg_ref, *, tm, sub, tiles_per_seq):
    halo = 8
    wdb_ref[...] = wd_ref[...].astype(BF16)

    @pl.when((pl.program_id(1) % tiles_per_seq) == 0)
    def _():
        cv_ref[...] = jnp.zeros_like(cv_ref)
        cg_ref[...] = jnp.zeros_like(cg_ref)

    row = lax.broadcasted_iota(jnp.int32, (halo, a_ref.shape[1]), 0)

    def conv(u, carry_ref, cw_ref, cb_ref):
        p1 = carry_ref[halo - 1:halo, :]
        p2 = carry_ref[halo - 2:halo - 1, :]
        r1 = pltpu.roll(u, 1, 0)
        r2 = pltpu.roll(u, 2, 0)
        prev1 = jnp.concatenate([jnp.where(row == 0, p1, r1[:halo, :]), r1[halo:, :]], axis=0)
        prev2 = jnp.concatenate(
            [jnp.where(row == 0, p2, jnp.where(row == 1, p1, r2[:halo, :])), r2[halo:, :]], axis=0)
        carry_ref[...] = u[sub - halo:sub, :]
        return prev2 * cw_ref[0:1, :] + prev1 * cw_ref[1:2, :] + u * cw_ref[2:3, :] + cb_ref[...]

    for s in range(tm // sub):
        h = h_ref[s * sub:(s + 1) * sub, :]
        val = conv(jnp.dot(h, wvb_ref[...], preferred_element_type=F32), cv_ref, cwv_ref, cbv_ref)
        gate = conv(jnp.dot(h, wgb_ref[...], preferred_element_type=F32), cg_ref, cwg_ref, cbg_ref)
        a_ref[s * sub:(s + 1) * sub, :] = (gate * jax.nn.sigmoid(gate) * val).astype(BF16)


def _ffn_up(h2, w_up_b, conv_w, conv_b, w_down, seq, tm=2048, tn=512, sub=256):
    m = h2.shape[0]
    nj = D_FF // tn
    ni = m // tm
    wd_rows = w_down.shape[0] // (nj * ni)
    wd_spec = pl.BlockSpec((wd_rows, D_MODEL), lambda j, i: (j * ni + i, 0))
    return pl.pallas_call(
        functools.partial(_ffn_up_kernel, tm=tm, sub=sub, tiles_per_seq=seq // tm),
        grid=(nj, ni),
        in_specs=[pl.BlockSpec((tm, D_MODEL), lambda j, i: (i, 0)),
                  pl.BlockSpec((D_MODEL, tn), lambda j, i: (0, j)),
                  pl.BlockSpec((D_MODEL, tn), lambda j, i: (0, nj + j)),
                  pl.BlockSpec((3, tn), lambda j, i: (0, j)),
                  pl.BlockSpec((3, tn), lambda j, i: (0, nj + j)),
                  pl.BlockSpec((1, tn), lambda j, i: (0, j)),
                  pl.BlockSpec((1, tn), lambda j, i: (0, nj + j)),
                  wd_spec],
        out_specs=[pl.BlockSpec((tm, tn), lambda j, i: (i, j)), wd_spec],
        out_shape=[jax.ShapeDtypeStruct((m, D_FF), BF16), jax.ShapeDtypeStruct(w_down.shape, BF16)],
        scratch_shapes=[pltpu.VMEM((8, tn), F32), pltpu.VMEM((8, tn), F32)],
        compiler_params=_cparams(("parallel", "arbitrary")),
        name="ffn_up",
    )(h2, w_up_b, w_up_b, conv_w, conv_w, conv_b, conv_b, w_down)


def _ffn_down_kernel(a_ref, w_ref, x1_ref, o_ref):
    o_ref[...] = x1_ref[...] + jnp.dot(a_ref[...], w_ref[...], preferred_element_type=F32)


def _ffn_down(act, w_down_b, x1, tm=512, tn=1024):
    m = act.shape[0]
    return pl.pallas_call(
        _ffn_down_kernel,
        grid=(D_MODEL // tn, m // tm),
        in_specs=[pl.BlockSpec((tm, D_FF), lambda j, i: (i, 0)),
                  pl.BlockSpec((D_FF, tn), lambda j, i: (0, j)),
                  pl.BlockSpec((tm, tn), lambda j, i: (i, j))],
        out_specs=pl.BlockSpec((tm, tn), lambda j, i: (i, j)),
        out_shape=jax.ShapeDtypeStruct((m, D_MODEL), F32),
        compiler_params=_cparams(("parallel", "arbitrary")),
        name="ffn_down",
    )(act, w_down_b, x1)


def _pad_cols(w, n):
    return jnp.pad(w, ((0, 0), (0, n - w.shape[1])))


def _pad_rows(w, n):
    return jnp.pad(w, ((0, n - w.shape[0]), (0, 0)))


def _layer(x2, batch, seq, w_in_p, norm1_g, rw_mu, rw_w0, rw_w2, rw_a0, rw_a2, rw_g2, rw_k_k, rw_k_a,
           rw_r_k, rw_gn_w, rw_gn_b, q_norm_g, k_norm_g, attn_sinks, w_branch, w_out,
           norm2_g, w_up, conv_w, conv_b, w_down):
    row = lambda v: v.reshape(1, -1).astype(F32)
    o3 = 3 * RW_DIM
    o4 = o3 + DECAY_LORA
    o5 = o4 + ICLR_LORA
    o6 = o5 + GATE_LORA
    mu = rw_mu.reshape(1, -1)
    mu_l = jnp.concatenate([_pad_cols(mu[:, o3:o4], LORA_W), _pad_cols(mu[:, o4:o5], LORA_A),
                            mu[:, o5:o6]], axis=1)
    w2p = _pad_rows(rw_w2, LORA_W).astype(BF16)
    a2p = _pad_rows(rw_a2, LORA_A).astype(BF16)

    z, (w_branch_b, w_out_b, w_up_b) = _in_proj(x2, row(norm1_g), w_in_p, (w_branch, w_out, w_up))
    o_rw = _rwkv(z, batch, seq, mu[:, :RW_DIM], mu[:, RW_DIM:2 * RW_DIM], mu[:, 2 * RW_DIM:o3], mu_l,
                 row(rw_w0), row(rw_a0), row(rw_k_k), row(rw_k_a), row(rw_r_k),
                 w2p, a2p, rw_g2.astype(BF16), row(rw_gn_w), row(rw_gn_b))
    o_att = _swa(z, attn_sinks.astype(F32), jnp.tile(row(q_norm_g), (1, ATT_HEADS)),
                 jnp.tile(row(k_norm_g), (1, ATT_KV_HEADS)), seq)
    x1, h2 = _mix_out(o_rw, o_att, w_branch_b, z, w_out_b, x2, row(norm2_g))
    act, w_down_b = _ffn_up(h2, w_up_b, conv_w.astype(F32), row(conv_b), w_down, seq)
    return _ffn_down(act, w_down_b, x1)


def kernel(x, norm1_g, w_in, rw_mu, rw_w0, rw_w2, rw_a0, rw_a2, rw_g2, rw_k_k, rw_k_a, rw_r_k,
           rw_gn_w, rw_gn_b, q_norm_g, k_norm_g, attn_sinks, w_branch, w_out, norm2_g, w_up,
           conv_w, conv_b, w_down):
    batch, seq, d = x.shape
    x2 = x.reshape(batch * seq, d)
    params = (norm1_g, rw_mu, rw_w0, rw_w2, rw_a0, rw_a2, rw_g2, rw_k_k, rw_k_a, rw_r_k,
              rw_gn_w, rw_gn_b, q_norm_g, k_norm_g, attn_sinks, w_branch, w_out, norm2_g, w_up,
              conv_w, conv_b, w_down)
    for layer in range(norm1_g.shape[0]):
        x2 = _layer(x2, batch, seq, _w_in_layout(w_in, layer), *(p[layer] for p in params))
    return x2.reshape(batch, seq, d)
```

```python
import functools

import jax
import jax.numpy as jnp
import numpy as np
from jax import lax
from jax.experimental import pallas as pl
from jax.experimental.pallas import tpu as pltpu

F32 = jnp.float32
BF16 = jnp.bfloat16

LANES = 128
BF16_SUBLANES = 16
VMEM_LIMIT = 56 * 1024 * 1024

D_MODEL = 2048
RW_HEADS = 16
RW_HEAD = 64
RW_DIM = RW_HEADS * RW_HEAD
DECAY_LORA = 96
ICLR_LORA = 96
GATE_LORA = 256
GN_EPS = 64e-5
ATT_HEADS = 16
ATT_KV_HEADS = 4
ATT_GROUP = ATT_HEADS // ATT_KV_HEADS
ATT_HEAD = 64
ATT_Q_DIM = ATT_HEADS * ATT_HEAD
ATT_KV_DIM = ATT_KV_HEADS * ATT_HEAD
WINDOW = 128
BLOCK = 128
D_FF = 5632
RMS_EPS = 1e-6
NEG_BIG = -1e30
LOG2E = 1.4426950408889634

C_R = 0
C_K = RW_DIM
C_V = 2 * RW_DIM
C_Q = 3 * RW_DIM
C_GRW = C_Q + ATT_Q_DIM
C_GATT = C_GRW + D_MODEL
C_LORA = C_GATT + D_MODEL
LORA_W = 128
LORA_A = 128
LORA_COLS = LORA_W + LORA_A + GATE_LORA
C_KA = C_LORA + LORA_COLS
C_VA = C_KA + ATT_KV_DIM
Z_COLS = C_VA + ATT_KV_DIM

CHUNK = 64
PAIR = 2 * RW_HEAD
N_PAIRS = RW_DIM // PAIR


def _cparams(sem):
    return pltpu.CompilerParams(dimension_semantics=sem, vmem_limit_bytes=VMEM_LIMIT)


def _mm(a, b):
    return jnp.dot(a.astype(BF16), b.astype(BF16), preferred_element_type=F32)


def _mm_nt(a, b):
    return lax.dot_general(a.astype(BF16), b.astype(BF16), (((1,), (1,)), ((), ())),
                           preferred_element_type=F32)


def _mm_tn(a, b):
    return lax.dot_general(a.astype(BF16), b.astype(BF16), (((0,), (0,)), ((), ())),
                           preferred_element_type=F32)


def _mm_split(m01, x):
    hi = x.astype(BF16)
    lo = (x - hi.astype(F32)).astype(BF16)
    return (jnp.dot(m01, hi, preferred_element_type=F32)
            + jnp.dot(m01, lo, preferred_element_type=F32))


def _head_ones(n, head, scale):
    r = lax.broadcasted_iota(jnp.int32, (n, n), 0) // head
    c = lax.broadcasted_iota(jnp.int32, (n, n), 1) // head
    return jnp.where(r == c, scale, 0.0).astype(BF16)


def _w_in_layout_kernel(wt_ref, o_ref, *, moves, pad):
    o_ref[pad[0]:pad[1], :] = jnp.zeros((pad[1] - pad[0], o_ref.shape[1]), BF16)
    for dst, src, width in moves:
        o_ref[dst:dst + width, :] = wt_ref[src:src + width, :].astype(BF16)


def _w_in_layout(w_in_layers, layer, cols=256):
    _, d, n = w_in_layers.shape
    wt_layers = jnp.swapaxes(w_in_layers, 1, 2)
    o3 = 3 * RW_DIM
    o4 = o3 + DECAY_LORA
    o5 = o4 + ICLR_LORA
    o6 = o5 + GATE_LORA
    oq = o6 + ATT_Q_DIM
    ok = oq + ATT_KV_DIM
    ov = ok + ATT_KV_DIM
    moves = ((C_R, 0, o3), (C_Q, o6, ATT_Q_DIM), (C_GRW, ov, 2 * D_MODEL),
             (C_LORA, o3, DECAY_LORA), (C_LORA + LORA_W, o4, ICLR_LORA),
             (C_LORA + LORA_W + LORA_A, o5, GATE_LORA), (C_KA, oq, ATT_KV_DIM), (C_VA, ok, ATT_KV_DIM))
    return pl.pallas_call(
        functools.partial(_w_in_layout_kernel, moves=moves, pad=(C_LORA, C_LORA + LORA_W + LORA_A)),
        grid=(d // cols,),
        in_specs=[pl.BlockSpec((None, n, cols), lambda i: (layer, 0, i))],
        out_specs=pl.BlockSpec((Z_COLS, cols), lambda i: (0, i)),
        out_shape=jax.ShapeDtypeStruct((Z_COLS, d), BF16),
        compiler_params=_cparams(("parallel",)),
        name="w_in_layout",
    )(wt_layers)


def _in_proj_kernel(x_ref, g_ref, wt_ref, *refs, n_side):
    side_in, z_ref, side_out, h_ref = refs[:n_side], refs[n_side], refs[n_side + 1:-1], refs[-1]

    @pl.when(pl.program_id(1) == 0)
    def _():
        xf = x_ref[...]
        y = xf * lax.rsqrt(jnp.mean(xf * xf, axis=-1, keepdims=True) + RMS_EPS)
        h_ref[...] = (y * g_ref[...]).astype(BF16)

    z_ref[...] = _mm_nt(h_ref[...], wt_ref[...]).astype(BF16)
    for w_ref, wb_ref in zip(side_in, side_out):
        wb_ref[...] = w_ref[...].astype(BF16)


def _in_proj(x2, g1, w_in_t, side_weights, tm=1024, tn=1536, side_rows=64):
    m, d = x2.shape
    n = w_in_t.shape[0]
    nj = n // tn
    for w in side_weights:
        assert w.shape[0] // side_rows <= (m // tm) * nj, "not enough grid steps to cast this weight"
    side_specs = [pl.BlockSpec((side_rows, w.shape[1]),
                               lambda i, j, last=w.shape[0] // side_rows - 1: (jnp.minimum(i * nj + j, last), 0))
                  for w in side_weights]
    outs = pl.pallas_call(
        functools.partial(_in_proj_kernel, n_side=len(side_weights)),
        grid=(m // tm, nj),
        in_specs=[pl.BlockSpec((tm, d), lambda i, j: (i, 0)),
                  pl.BlockSpec((1, d), lambda i, j: (0, 0)),
                  pl.BlockSpec((tn, d), lambda i, j: (j, 0))] + side_specs,
        out_specs=[pl.BlockSpec((tm, tn), lambda i, j: (i, j))] + side_specs,
        out_shape=[jax.ShapeDtypeStruct((m, n), BF16)]
        + [jax.ShapeDtypeStruct(w.shape, BF16) for w in side_weights],
        scratch_shapes=[pltpu.VMEM((tm, d), BF16)],
        compiler_params=_cparams(("arbitrary", "arbitrary")),
        name="in_proj",
    )(x2, g1, w_in_t, *side_weights)
    return outs[0], outs[1:]


def _rwkv_prep_kernel(zr_ref, zk_ref, zv_ref, zl_ref, pr_ref, pk_ref, pv_ref, pli_ref,
                      mur_ref, muk_ref, muv_ref, mul_ref,
                      w0_ref, a0_ref, kk_ref, ka_ref, rk_ref, w2_ref, a2_ref, g2_ref,
                      at_ref, rt_ref, bt_ref, kt_ref, bg_ref, kg_ref, v_ref, g_ref, bonus_ref,
                      gT_ref, gm_ref, *, tm, first):
    keep = jnp.where(first, 0.0, 1.0)
    nc = tm // CHUNK
    last = slice(BF16_SUBLANES - 1, BF16_SUBLANES)

    def shifted(z, prev_last, mu):
        z = z.astype(F32)
        row = lax.broadcasted_iota(jnp.int32, z.shape, 0)
        zprev = jnp.where(row == 0, prev_last.astype(F32) * keep, pltpu.roll(z, 1, 0))
        return z + (zprev - z) * mu

    lo = shifted(zl_ref[...], pli_ref[last, :], mul_ref[...])
    tanh_wd = jnp.tanh(lo[:, :LORA_W]).astype(BF16)
    ad = lo[:, LORA_W:LORA_W + LORA_A].astype(BF16)
    sig_gd = jax.nn.sigmoid(lo[:, LORA_W + LORA_A:]).astype(BF16)

    ti = lax.broadcasted_iota(jnp.int32, (tm, tm), 0)
    si = lax.broadcasted_iota(jnp.int32, (tm, tm), 1)
    tri = jnp.where(((ti // CHUNK) == (si // CHUNK)) & (si <= ti), 1.0, 0.0).astype(BF16)
    ones_h = _head_ones(PAIR, RW_HEAD, 1.0)
    w_lora = jnp.dot(tanh_wd, w2_ref[...], preferred_element_type=F32)
    a_lora = jnp.dot(ad, a2_ref[...], preferred_element_type=F32)
    g_ref[...] = jnp.dot(sig_gd, g2_ref[...], preferred_element_type=F32).astype(BF16)
    yield

    stash = []
    for p in range(N_PAIRS):
        cs = slice(p * PAIR, (p + 1) * PAIR)
        r = shifted(zr_ref[:, cs], pr_ref[last, cs], mur_ref[:, cs])
        k = shifted(zk_ref[:, cs], pk_ref[last, cs], muk_ref[:, cs])
        v = shifted(zv_ref[:, cs], pv_ref[last, cs], muv_ref[:, cs])

        wpre = w0_ref[:, cs] + w_lora[:, cs]
        w = jnp.minimum(wpre, 0.0) - jnp.log1p(jnp.exp(-jnp.abs(wpre))) - 0.5
        lw = -jnp.exp(w)
        a = jax.nn.sigmoid(a0_ref[:, cs] + a_lora[:, cs])

        kk = k * kk_ref[:, cs]
        kk = kk * lax.rsqrt(jnp.maximum(_mm(kk * kk, ones_h), 1e-24))
        k2 = k * (1.0 + (a - 1.0) * ka_ref[:, cs])
        bonus_ref[:, cs] = (_mm(r * k2 * rk_ref[:, cs], ones_h) * v).astype(BF16)
        v_ref[:, cs] = v.astype(BF16)
        stash.append((r, k2, -kk, kk * a, lw))
        yield

    c_all = _mm_split(tri, jnp.concatenate([st[4] for st in stash], axis=1))
    yield

    for p in range(N_PAIRS):
        cs = slice(p * PAIR, (p + 1) * PAIR)
        r, k2, ka_vec, kb_vec, lw = stash[p]
        c = c_all[:, cs]
        c3 = c.reshape(nc, CHUNK, PAIR)
        c_mid = c3[:, CHUNK // 2 - 1:CHUNK // 2, :]
        c_end = c3[:, CHUNK - 1:CHUNK, :]
        cm = jnp.broadcast_to(c_mid, c3.shape).reshape(tm, PAIR)
        cT = jnp.broadcast_to(c_end, c3.shape).reshape(tm, PAIR)

        e_in = jnp.exp(c - cm)
        e_out = jnp.exp(cm - c)
        e_end = jnp.exp(cT - c)
        at_ref[:, cs] = (ka_vec * jnp.exp(c - lw - cm)).astype(BF16)
        rt_ref[:, cs] = (r * e_in).astype(BF16)
        bt_ref[:, cs] = (kb_vec * e_out).astype(BF16)
        kt_ref[:, cs] = (k2 * e_out).astype(BF16)
        bg_ref[:, cs] = (kb_vec * e_end).astype(BF16)
        kg_ref[:, cs] = (k2 * e_end).astype(BF16)
        gT_ref[:, :, cs] = jnp.exp(c_end)
        gm_ref[:, :, cs] = jnp.exp(c_mid)
        yield


def _rwkv_kernel(*refs, tb, nb):
    prep_in, (gnw_ref, gnb_ref, o_ref), scratch = refs[:20], refs[20:23], refs[23:]
    operands, s_ref = scratch[:11], scratch[11]
    j = pl.program_id(1)
    prep = functools.partial(_rwkv_prep_kernel, *prep_in, *operands, tm=tb, first=j == 0)
    chunk = functools.partial(_rwkv_chunk_kernel, *operands, gnw_ref, gnb_ref, o_ref, s_ref,
                              chunks=tb // CHUNK, pairs=N_PAIRS, first=j == 1)

    @pl.when(j == 0)
    def _():
        for _ in prep():
            pass

    @pl.when((j > 0) & (j < nb))
    def _():
        chunk(filler=prep())

    @pl.when(j == nb)
    def _():
        chunk()


def _rwkv(z, batch, seq, mu_r, mu_k, mu_v, mu_l, w0, a0, k_k, k_a, r_k, w2p, a2p, g2, gn_w, gn_b, tb=128):
    m = z.shape[0]
    nb = seq // tb
    pb = tb // BF16_SUBLANES
    blk = lambda b, j: b * nb + jnp.minimum(j, nb - 1)

    def cur(col0, width):
        return pl.BlockSpec((tb, width), lambda b, j, c0=col0 // width: (blk(b, j), c0))

    def prev(col0, width):
        return pl.BlockSpec((BF16_SUBLANES, width),
                            lambda b, j, c0=col0 // width: (jnp.maximum(blk(b, j) * pb - 1, 0), c0))

    def whole(rows, width):
        return pl.BlockSpec((rows, width), lambda b, j: (0, 0))

    in_specs = [cur(C_R, RW_DIM), cur(C_K, RW_DIM), cur(C_V, RW_DIM), cur(C_LORA, LORA_COLS),
                prev(C_R, RW_DIM), prev(C_K, RW_DIM), prev(C_V, RW_DIM), prev(C_LORA, LORA_COLS),
                whole(1, RW_DIM), whole(1, RW_DIM), whole(1, RW_DIM), whole(1, LORA_COLS),
                whole(1, RW_DIM), whole(1, RW_DIM), whole(1, RW_DIM), whole(1, RW_DIM), whole(1, RW_DIM),
                whole(LORA_W, RW_DIM), whole(LORA_A, RW_DIM), whole(GATE_LORA, RW_DIM),
                whole(1, RW_DIM), whole(1, RW_DIM)]
    big = pltpu.VMEM((tb, RW_DIM), BF16)
    per_chunk = pltpu.VMEM((tb // CHUNK, 1, RW_DIM), F32)
    return pl.pallas_call(
        functools.partial(_rwkv_kernel, tb=tb, nb=nb),
        grid=(batch, nb + 1),
        in_specs=in_specs,
        out_specs=pl.BlockSpec((tb, RW_DIM), lambda b, j: (b * nb + jnp.maximum(j - 1, 0), 0)),
        out_shape=jax.ShapeDtypeStruct((m, RW_DIM), BF16),
        scratch_shapes=[big] * 9 + [per_chunk] * 2 + [pltpu.VMEM((N_PAIRS, PAIR, PAIR), F32)],
        compiler_params=_cparams(("parallel", "arbitrary")),
        name="rwkv",
    )(z, z, z, z, z, z, z, z, mu_r, mu_k, mu_v, mu_l, w0, a0, k_k, k_a, r_k, w2p, a2p, g2, gn_w, gn_b)


def _rwkv_chunk_kernel(at_ref, rt_ref, bt_ref, kt_ref, bg_ref, kg_ref, v_ref, g_ref, bonus_ref,
                       gT_ref, gm_ref, gnw_ref, gnb_ref, o_ref, s_ref, *, chunks, pairs, first,
                       filler=None):
    @pl.when(first)
    def _():
        s_ref[...] = jnp.zeros_like(s_ref)

    T = CHUNK
    lane = lax.broadcasted_iota(jnp.int32, (T, PAIR), 1)
    head0 = lane < RW_HEAD
    ri = lax.broadcasted_iota(jnp.int32, (2 * T, 2 * T), 0)
    ci = lax.broadcasted_iota(jnp.int32, (2 * T, 2 * T), 1)
    same = (ri // T) == (ci // T)
    strict = same & ((ri % T) > (ci % T))
    incl = same & ((ri % T) >= (ci % T))
    eye = jnp.where(ri == ci, 1.0, 0.0)
    own = (ri // T) == (ci // RW_HEAD)

    items = [(p, c) for c in range(chunks) for p in range(pairs)]
    rows = lambda c: slice(c * T, (c + 1) * T)
    cols = lambda p: slice(p * PAIR, (p + 1) * PAIR)

    def stack(ref):
        out = []
        for p, c in items:
            x = ref[rows(c), cols(p)]
            zero = jnp.zeros_like(x)
            out.append(jnp.concatenate([jnp.where(head0, x, zero), jnp.where(head0, zero, x)], axis=0))
        return out

    each = lambda f, *ls: [f(*xs) for xs in zip(*ls)]
    La, Lr, Rb, Rk = stack(at_ref), stack(rt_ref), stack(bt_ref), stack(kt_ref)
    Rbg, Rkg, Vs = stack(bg_ref), stack(kg_ref), stack(v_ref)
    gm_row = [gm_ref[c][:, cols(p)] for p, c in items]
    gT_row = [gT_ref[c][:, cols(p)] for p, c in items]
    gate = [g_ref[rows(c), cols(p)] for p, c in items]
    bonus = [bonus_ref[rows(c), cols(p)] for p, c in items]
    filler = iter(()) if filler is None else filler
    tick = lambda: next(filler, None)

    AA = each(lambda la, lr, rb, rk: _mm_nt(jnp.concatenate([la, lr], axis=0),
                                            jnp.concatenate([rb, rk], axis=0)), La, Lr, Rb, Rk)
    tick()
    N = [jnp.where(strict, aa[:2 * T, :2 * T], 0.0) for aa in AA]
    Aak = [jnp.where(strict, aa[:2 * T, 2 * T:], 0.0) for aa in AA]
    Arb = [jnp.where(incl, aa[2 * T:, :2 * T], 0.0) for aa in AA]
    Ark = [jnp.where(incl, aa[2 * T:, 2 * T:], 0.0) for aa in AA]
    def live_rows(x, t0):
        return x if t0 == 0 else jnp.concatenate([x[t0:T, :], x[T + t0:, :]], axis=0)

    def all_rows(y, t0):
        if t0 == 0:
            return y
        zero = jnp.zeros((t0, y.shape[1]), y.dtype)
        return jnp.concatenate([zero, y[:T - t0, :], zero, y[T - t0:, :]], axis=0)

    def skip(span):
        return span if span % BF16_SUBLANES == 0 else 0

    def blockdiag(a, b):
        zero = jnp.zeros_like(a)
        return jnp.concatenate([jnp.concatenate([a, zero], axis=1),
                                jnp.concatenate([zero, b], axis=1)], axis=0)

    groups = [(i, i + 1) for i in range(0, len(items), 2)]
    W = each(lambda n: eye + n, N)
    P = [n.astype(BF16) for n in N]
    for a, b in groups:
        sq = _mm(jnp.concatenate([P[a], P[b]], axis=1), blockdiag(P[a], P[b]))
        P[a], P[b] = sq[:, :PAIR].astype(BF16), sq[:, PAIR:].astype(BF16)
    tick()
    span = 2
    while span < T:
        last = 2 * span >= T
        t0, t1 = skip(span), skip(2 * span)
        for a, b in groups:
            lhs = jnp.concatenate([live_rows(W[a].astype(BF16), t0), live_rows(W[b].astype(BF16), t0)], axis=1)
            if not last:
                lhs_p = jnp.concatenate([live_rows(P[a], t1), live_rows(P[b], t1)], axis=1)
                lhs = jnp.concatenate([lhs_p, lhs], axis=0)
            res = _mm(lhs, blockdiag(P[a], P[b]))
            if not last:
                sq = all_rows(res[:lhs_p.shape[0], :], t1)
                P[a], P[b] = sq[:, :PAIR].astype(BF16), sq[:, PAIR:].astype(BF16)
                res = res[lhs_p.shape[0]:, :]
            wp = all_rows(res, t0)
            W[a] = W[a] + wp[:, :PAIR]
            W[b] = W[b] + wp[:, PAIR:]
        tick()
        span *= 2
    AkV = each(_mm, Aak, Vs)
    tick()
    X = each(lambda w, la, akv: _mm(w, jnp.concatenate([la, akv.astype(BF16)], axis=1)).astype(BF16),
             W, La, AkV)
    tick()
    Z = each(lambda arb, ark, x, vs:
             _mm(jnp.concatenate([arb.astype(BF16), ark.astype(BF16)], axis=1),
                 jnp.concatenate([x, jnp.concatenate([jnp.zeros_like(vs), vs], axis=1)], axis=0)),
             Arb, Ark, X, Vs)
    tick()
    Q = each(lambda z, lr, gm: (z[:, :PAIR] + lr.astype(F32)) * gm, Z, Lr, gm_row)
    Y0 = [z[:, PAIR:] for z in Z]
    tick()
    Mbd = each(lambda rbg, x, gm: _mm_tn(rbg, x[:, :PAIR]) * gm, Rbg, X, gm_row)
    NcT = each(lambda x, vs, rbg, rkg: _mm_tn(jnp.concatenate([x[:, PAIR:], vs], axis=0),
                                              jnp.concatenate([rbg, rkg], axis=0)), X, Vs, Rbg, Rkg)
    for _ in filler:
        pass

    S = [s_ref[p] for p in range(pairs)]
    for c in range(chunks):
        for p in range(pairs):
            i = c * pairs + p
            Ys = _mm_nt(Q[i], S[p]) + Y0[i]
            S[p] = S[p] * gT_row[i] + _mm_nt(S[p], Mbd[i]) + NcT[i]
            mu = jnp.sum(Ys, axis=-1, keepdims=True) * (1.0 / RW_HEAD)
            d = jnp.where(own, Ys - mu, 0.0)
            var = jnp.sum(d * d, axis=-1, keepdims=True) * (1.0 / RW_HEAD)
            dn = d * lax.rsqrt(var + GN_EPS)
            yn = (dn[:T, :] + dn[T:, :]) * gnw_ref[:, cols(p)] + gnb_ref[:, cols(p)]
            out = (yn + bonus[i].astype(F32)) * gate[i].astype(F32)
            o_ref[rows(c), cols(p)] = out.astype(BF16)
    for p in range(pairs):
        s_ref[p] = S[p]


def _swa_kernel(sink_ref, q_ref, kc_ref, kp_ref, vc_ref, vp_ref, qg_ref, kg_ref, o_ref, *,
                steps_per_seq, slopes, qb):
    first = (pl.program_id(0) % steps_per_seq) == 0
    mean_h = _head_ones(LANES, ATT_HEAD, 1.0 / ATT_HEAD)

    def head_rms(x, gain):
        parts = []
        for b in range(x.shape[1] // LANES):
            xb = x[:, b * LANES:(b + 1) * LANES]
            ms = _mm(xb * xb, mean_h)
            parts.append(xb * lax.rsqrt(ms + RMS_EPS))
        return jnp.concatenate(parts, axis=1) * gain

    q = head_rms(q_ref[...].astype(F32), qg_ref[...]) * (ATT_HEAD ** -0.5 * LOG2E)
    q = q.astype(BF16)
    kcat = jnp.concatenate([kp_ref[...], kc_ref[...]], axis=0).astype(F32)
    kcat = head_rms(kcat, kg_ref[...]).astype(BF16)
    vcat = jnp.concatenate([vp_ref[...], vc_ref[...]], axis=0)

    qi = lax.broadcasted_iota(jnp.int32, (BLOCK, 2 * BLOCK), 0)
    kj = lax.broadcasted_iota(jnp.int32, (BLOCK, 2 * BLOCK), 1)
    dist_i = BLOCK + qi - kj
    in_window = (dist_i >= 0) & (dist_i < WINDOW)
    first_key = jnp.where(first, BLOCK, 0)
    neg_dist_rest = jnp.where(in_window, -dist_i.astype(F32), NEG_BIG)
    neg_dist_first = jnp.where(in_window & (kj >= first_key), -dist_i.astype(F32), NEG_BIG)
    neg_dist = [neg_dist_first] + [neg_dist_rest] * (qb - 1)

    lane_q = lax.broadcasted_iota(jnp.int32, (BLOCK, LANES), 1)
    lane_kv = lax.broadcasted_iota(jnp.int32, (2 * BLOCK, LANES), 1)
    zero_q = jnp.zeros((BLOCK, LANES), BF16)
    one_kv = jnp.ones((2 * BLOCK, LANES), BF16)
    kv_lane_blocks = range(ATT_KV_DIM // LANES)
    k_all = [kcat[:, b * LANES:(b + 1) * LANES] for b in kv_lane_blocks]
    k_all_rolled = [pltpu.roll(kb, ATT_HEAD, 1) for kb in k_all]
    window = lambda x, t: x[t * BLOCK:(t + 2) * BLOCK, :]
    v_aug = [[jnp.where((lane_kv // ATT_HEAD) == (j % 2),
                        window(vcat[:, (j // 2) * LANES:(j // 2 + 1) * LANES], t), one_kv)
              for j in range(ATT_KV_HEADS)] for t in range(qb)]
    items = [(t, h) for t in range(qb) for h in range(ATT_HEADS)]
    kv_of = lambda h: h // ATT_GROUP
    aligned = lambda h: (h % 2) == (kv_of(h) % 2)

    def scores(t, h):
        qh = jnp.where((lane_q // ATT_HEAD) == (h % 2),
                       q[t * BLOCK:(t + 1) * BLOCK, (h // 2) * LANES:(h // 2 + 1) * LANES], zero_q)
        kh = window((k_all if aligned(h) else k_all_rolled)[kv_of(h) // 2], t)
        return _mm_nt(qh, kh) + (slopes[h] * LOG2E) * neg_dist[t]

    s = [scores(t, h) for t, h in items]
    sink = [sink_ref[h] * LOG2E for _, h in items]
    mx = [jnp.maximum(jnp.max(s_, axis=-1, keepdims=True), sk) for s_, sk in zip(s, sink)]
    p = [jnp.exp2(s_ - m_) for s_, m_ in zip(s, mx)]
    pv = [_mm(p_, v_aug[t][kv_of(h)]) for p_, (t, h) in zip(p, items)]
    pv_sw = [pltpu.roll(x, ATT_HEAD, 1) for x in pv]
    out = []
    for i, (t, h) in enumerate(items):
        num, den = (pv[i], pv_sw[i]) if aligned(h) else (pv_sw[i], pv[i])
        out.append(num * (1.0 / (den + jnp.exp2(sink[i] - mx[i]))))
    for t in range(qb):
        o = out[t * ATT_HEADS:(t + 1) * ATT_HEADS]
        o_ref[t * BLOCK:(t + 1) * BLOCK, :] = jnp.concatenate(
            [jnp.where((lane_q // ATT_HEAD) == 0, o[2 * m], o[2 * m + 1]) for m in range(ATT_HEADS // 2)],
            axis=1).astype(BF16)


def _swa(z, sinks, q_gain_t, k_gain_t, seq, qb=8):
    m = z.shape[0]
    rows = qb * BLOCK
    slopes = tuple(float(s) for s in
                   np.exp2(-8.0 * np.arange(1, ATT_HEADS + 1, dtype=np.float32) / ATT_HEADS).astype(np.float32))
    kv_cur = lambda c0: pl.BlockSpec((rows, ATT_KV_DIM), lambda n, c=c0 // ATT_KV_DIM: (n, c))
    kv_prev = lambda c0: pl.BlockSpec((BLOCK, ATT_KV_DIM),
                                      lambda n, c=c0 // ATT_KV_DIM: (jnp.maximum(n * qb - 1, 0), c))
    return pl.pallas_call(
        functools.partial(_swa_kernel, steps_per_seq=seq // rows, slopes=slopes, qb=qb),
        grid=(m // rows,),
        in_specs=[pl.BlockSpec(memory_space=pltpu.SMEM),
                  pl.BlockSpec((rows, ATT_Q_DIM), lambda n: (n, C_Q // ATT_Q_DIM)),
                  kv_cur(C_KA), kv_prev(C_KA), kv_cur(C_VA), kv_prev(C_VA),
                  pl.BlockSpec((1, ATT_Q_DIM), lambda n: (0, 0)),
                  pl.BlockSpec((1, ATT_KV_DIM), lambda n: (0, 0))],
        out_specs=pl.BlockSpec((rows, ATT_Q_DIM), lambda n: (n, 0)),
        out_shape=jax.ShapeDtypeStruct((m, ATT_Q_DIM), BF16),
        compiler_params=_cparams(("parallel",)),
        name="swa",
    )(sinks, z, z, z, z, z, q_gain_t, k_gain_t)


def _mix_out_kernel(orw_ref, oatt_ref, wb1_ref, wb2_ref, zg_ref, wo_ref, x_ref, g_ref, x1_ref, h2_ref):
    p_rw = jnp.dot(orw_ref[...], wb1_ref[...], preferred_element_type=F32)
    p_att = jnp.dot(oatt_ref[...], wb2_ref[...], preferred_element_type=F32)
    g_rw = jax.nn.sigmoid(zg_ref[:, :D_MODEL].astype(F32))
    g_att = jax.nn.sigmoid(zg_ref[:, D_MODEL:].astype(F32))
    mix = (g_rw * p_rw + g_att * p_att).astype(BF16)
    x1 = x_ref[...] + jnp.dot(mix, wo_ref[...], preferred_element_type=F32)
    x1_ref[...] = x1
    y = x1 * lax.rsqrt(jnp.mean(x1 * x1, axis=-1, keepdims=True) + RMS_EPS)
    h2_ref[...] = (y * g_ref[...]).astype(BF16)


def _mix_out(o_rw, o_att, w_branch_b, z, w_out_b, x2, g2, tm=512):
    m = x2.shape[0]
    const = lambda rows, cols, r0=0: pl.BlockSpec((rows, cols), lambda i: (r0, 0),
                                                  pipeline_mode=pl.Buffered(1))
    rows = lambda cols, c0=0: pl.BlockSpec((tm, cols), lambda i: (i, c0))
    return pl.pallas_call(
        _mix_out_kernel,
        grid=(m // tm,),
        in_specs=[rows(RW_DIM), rows(ATT_Q_DIM), const(RW_DIM, D_MODEL), const(ATT_Q_DIM, D_MODEL, 1),
                  rows(2 * D_MODEL, C_GRW // (2 * D_MODEL)), const(D_MODEL, D_MODEL),
                  rows(D_MODEL), const(1, D_MODEL)],
        out_specs=[rows(D_MODEL), rows(D_MODEL)],
        out_shape=[jax.ShapeDtypeStruct((m, D_MODEL), F32),
                   jax.ShapeDtypeStruct((m, D_MODEL), BF16)],
        compiler_params=_cparams(("parallel",)),
        name="mix_out",
    )(o_rw, o_att, w_branch_b, w_branch_b, z, w_out_b, x2, g2)


def _ffn_up_kernel(h_ref, wvb_ref, wgb_ref, cwv_ref, cwg_ref, cbv_ref, cbg_ref, wd_ref, a_ref, wdb_ref,
                   cv_ref, cg_ref, *, tm, sub, tiles_per_seq):
    halo = 8
    wdb_ref[...] = wd_ref[...].astype(BF16)

    @pl.when((pl.program_id(1) % tiles_per_seq) == 0)
    def _():
        cv_ref[...] = jnp.zeros_like(cv_ref)
        cg_ref[...] = jnp.zeros_like(cg_ref)

    row = lax.broadcasted_iota(jnp.int32, (halo, a_ref.shape[1]), 0)

    def conv(u, carry_ref, cw_ref, cb_ref):
        p1 = carry_ref[halo - 1:halo, :]
        p2 = carry_ref[halo - 2:halo - 1, :]
        r1 = pltpu.roll(u, 1, 0)
        r2 = pltpu.roll(u, 2, 0)
        prev1 = jnp.concatenate([jnp.where(row == 0, p1, r1[:halo, :]), r1[halo:, :]], axis=0)
        prev2 = jnp.concatenate(
            [jnp.where(row == 0, p2, jnp.where(row == 1, p1, r2[:halo, :])), r2[halo:, :]], axis=0)
        carry_ref[...] = u[sub - halo:sub, :]
        return prev2 * cw_ref[0:1, :] + prev1 * cw_ref[1:2, :] + u * cw_ref[2:3, :] + cb_ref[...]

    for s in range(tm // sub):
        h = h_ref[s * sub:(s + 1) * sub, :]
        val = conv(jnp.dot(h, wvb_ref[...], preferred_element_type=F32), cv_ref, cwv_ref, cbv_ref)
        gate = conv(jnp.dot(h, wgb_ref[...], preferred_element_type=F32), cg_ref, cwg_ref, cbg_ref)
        a_ref[s * sub:(s + 1) * sub, :] = (gate * jax.nn.sigmoid(gate) * val).astype(BF16)


def _ffn_up(h2, w_up_b, conv_w, conv_b, w_down, seq, tm=2048, tn=512, sub=256):
    m = h2.shape[0]
    nj = D_FF // tn
    ni = m // tm
    wd_rows = w_down.shape[0] // (nj * ni)
    wd_spec = pl.BlockSpec((wd_rows, D_MODEL), lambda j, i: (j * ni + i, 0))
    return pl.pallas_call(
        functools.partial(_ffn_up_kernel, tm=tm, sub=sub, tiles_per_seq=seq // tm),
        grid=(nj, ni),
        in_specs=[pl.BlockSpec((tm, D_MODEL), lambda j, i: (i, 0)),
                  pl.BlockSpec((D_MODEL, tn), lambda j, i: (0, j)),
                  pl.BlockSpec((D_MODEL, tn), lambda j, i: (0, nj + j)),
                  pl.BlockSpec((3, tn), lambda j, i: (0, j)),
                  pl.BlockSpec((3, tn), lambda j, i: (0, nj + j)),
                  pl.BlockSpec((1, tn), lambda j, i: (0, j)),
                  pl.BlockSpec((1, tn), lambda j, i: (0, nj + j)),
                  wd_spec],
        out_specs=[pl.BlockSpec((tm, tn), lambda j, i: (i, j)), wd_spec],
        out_shape=[jax.ShapeDtypeStruct((m, D_FF), BF16), jax.ShapeDtypeStruct(w_down.shape, BF16)],
        scratch_shapes=[pltpu.VMEM((8, tn), F32), pltpu.VMEM((8, tn), F32)],
        compiler_params=_cparams(("parallel", "arbitrary")),
        name="ffn_up",
    )(h2, w_up_b, w_up_b, conv_w, conv_w, conv_b, conv_b, w_down)


def _ffn_down_kernel(a_ref, w_ref, x1_ref, o_ref):
    o_ref[...] = x1_ref[...] + jnp.dot(a_ref[...], w_ref[...], preferred_element_type=F32)


def _ffn_down(act, w_down_b, x1, tm=512, tn=1024):
    m = act.shape[0]
    return pl.pallas_call(
        _ffn_down_kernel,
        grid=(D_MODEL // tn, m // tm),
        in_specs=[pl.BlockSpec((tm, D_FF), lambda j, i: (i, 0)),
                  pl.BlockSpec((D_FF, tn), lambda j, i: (0, j)),
                  pl.BlockSpec((tm, tn), lambda j, i: (i, j))],
        out_specs=pl.BlockSpec((tm, tn), lambda j, i: (i, j)),
        out_shape=jax.ShapeDtypeStruct((m, D_MODEL), F32),
        compiler_params=_cparams(("parallel", "arbitrary")),
        name="ffn_down",
    )(act, w_down_b, x1)


def _pad_cols(w, n):
    return jnp.pad(w, ((0, 0), (0, n - w.shape[1])))


def _pad_rows(w, n):
    return jnp.pad(w, ((0, n - w.shape[0]), (0, 0)))


def _layer(x2, batch, seq, w_in_p, norm1_g, rw_mu, rw_w0, rw_w2, rw_a0, rw_a2, rw_g2, rw_k_k, rw_k_a,
           rw_r_k, rw_gn_w, rw_gn_b, q_norm_g, k_norm_g, attn_sinks, w_branch, w_out,
           norm2_g, w_up, conv_w, conv_b, w_down):
    row = lambda v: v.reshape(1, -1).astype(F32)
    o3 = 3 * RW_DIM
    o4 = o3 + DECAY_LORA
    o5 = o4 + ICLR_LORA
    o6 = o5 + GATE_LORA
    mu = rw_mu.reshape(1, -1)
    mu_l = jnp.concatenate([_pad_cols(mu[:, o3:o4], LORA_W), _pad_cols(mu[:, o4:o5], LORA_A),
                            mu[:, o5:o6]], axis=1)
    w2p = _pad_rows(rw_w2, LORA_W).astype(BF16)
    a2p = _pad_rows(rw_a2, LORA_A).astype(BF16)

    z, (w_branch_b, w_out_b, w_up_b) = _in_proj(x2, row(norm1_g), w_in_p, (w_branch, w_out, w_up))
    o_rw = _rwkv(z, batch, seq, mu[:, :RW_DIM], mu[:, RW_DIM:2 * RW_DIM], mu[:, 2 * RW_DIM:o3], mu_l,
                 row(rw_w0), row(rw_a0), row(rw_k_k), row(rw_k_a), row(rw_r_k),
                 w2p, a2p, rw_g2.astype(BF16), row(rw_gn_w), row(rw_gn_b))
    o_att = _swa(z, attn_sinks.astype(F32), jnp.tile(row(q_norm_g), (1, ATT_HEADS)),
                 jnp.tile(row(k_norm_g), (1, ATT_KV_HEADS)), seq)
    x1, h2 = _mix_out(o_rw, o_att, w_branch_b, z, w_out_b, x2, row(norm2_g))
    act, w_down_b = _ffn_up(h2, w_up_b, conv_w.astype(F32), row(conv_b), w_down, seq)
    return _ffn_down(act, w_down_b, x1)


def kernel(x, norm1_g, w_in, rw_mu, rw_w0, rw_w2, rw_a0, rw_a2, rw_g2, rw_k_k, rw_k_a, rw_r_k,
           rw_gn_w, rw_gn_b, q_norm_g, k_norm_g, attn_sinks, w_branch, w_out, norm2_g, w_up,
           conv_w, conv_b, w_down):
    batch, seq, d = x.shape
    x2 = x.reshape(batch * seq, d)
    params = (norm1_g, rw_mu, rw_w0, rw_w2, rw_a0, rw_a2, rw_g2, rw_k_k, rw_k_a, rw_r_k,
              rw_gn_w, rw_gn_b, q_norm_g, k_norm_g, attn_sinks, w_branch, w_out, norm2_g, w_up,
              conv_w, conv_b, w_down)
    for layer in range(norm1_g.shape[0]):
        x2 = _layer(x2, batch, seq, _w_in_layout(w_in, layer), *(p[layer] for p in params))
    return x2.reshape(batch, seq, d)
```

```python
import functools

import jax
import jax.numpy as jnp
import numpy as np
from jax import lax
from jax.experimental import pallas as pl
from jax.experimental.pallas import tpu as pltpu

F32 = jnp.float32
BF16 = jnp.bfloat16

LANES = 128
BF16_SUBLANES = 16
VMEM_LIMIT = 56 * 1024 * 1024

D_MODEL = 2048
RW_HEADS = 16
RW_HEAD = 64
RW_DIM = RW_HEADS * RW_HEAD
DECAY_LORA = 96
ICLR_LORA = 96
GATE_LORA = 256
GN_EPS = 64e-5
ATT_HEADS = 16
ATT_KV_HEADS = 4
ATT_GROUP = ATT_HEADS // ATT_KV_HEADS
ATT_HEAD = 64
ATT_Q_DIM = ATT_HEADS * ATT_HEAD
ATT_KV_DIM = ATT_KV_HEADS * ATT_HEAD
WINDOW = 128
BLOCK = 128
D_FF = 5632
RMS_EPS = 1e-6
NEG_BIG = -1e30
LOG2E = 1.4426950408889634

C_R = 0
C_K = RW_DIM
C_V = 2 * RW_DIM
C_Q = 3 * RW_DIM
C_GRW = C_Q + ATT_Q_DIM
C_GATT = C_GRW + D_MODEL
C_LORA = C_GATT + D_MODEL
LORA_W = 128
LORA_A = 128
LORA_COLS = LORA_W + LORA_A + GATE_LORA
C_KA = C_LORA + LORA_COLS
C_VA = C_KA + ATT_KV_DIM
Z_COLS = C_VA + ATT_KV_DIM

CHUNK = 64
PAIR = 2 * RW_HEAD
N_PAIRS = RW_DIM // PAIR


def _cparams(sem):
    return pltpu.CompilerParams(dimension_semantics=sem, vmem_limit_bytes=VMEM_LIMIT)


def _mm(a, b):
    return jnp.dot(a.astype(BF16), b.astype(BF16), preferred_element_type=F32)


def _mm_nt(a, b):
    return lax.dot_general(a.astype(BF16), b.astype(BF16), (((1,), (1,)), ((), ())),
                           preferred_element_type=F32)


def _mm_tn(a, b):
    return lax.dot_general(a.astype(BF16), b.astype(BF16), (((0,), (0,)), ((), ())),
                           preferred_element_type=F32)


def _mm_split(m01, x):
    hi = x.astype(BF16)
    lo = (x - hi.astype(F32)).astype(BF16)
    return (jnp.dot(m01, hi, preferred_element_type=F32)
            + jnp.dot(m01, lo, preferred_element_type=F32))


def _head_ones(n, head, scale):
    r = lax.broadcasted_iota(jnp.int32, (n, n), 0) // head
    c = lax.broadcasted_iota(jnp.int32, (n, n), 1) // head
    return jnp.where(r == c, scale, 0.0).astype(BF16)


def _w_in_layout_kernel(wt_ref, o_ref, *, moves, pad):
    o_ref[pad[0]:pad[1], :] = jnp.zeros((pad[1] - pad[0], o_ref.shape[1]), BF16)
    for dst, src, width in moves:
        o_ref[dst:dst + width, :] = wt_ref[src:src + width, :].astype(BF16)


def _w_in_layout(w_in_layers, layer, cols=256):
    _, d, n = w_in_layers.shape
    wt_layers = jnp.swapaxes(w_in_layers, 1, 2)
    o3 = 3 * RW_DIM
    o4 = o3 + DECAY_LORA
    o5 = o4 + ICLR_LORA
    o6 = o5 + GATE_LORA
    oq = o6 + ATT_Q_DIM
    ok = oq + ATT_KV_DIM
    ov = ok + ATT_KV_DIM
    moves = ((C_R, 0, o3), (C_Q, o6, ATT_Q_DIM), (C_GRW, ov, 2 * D_MODEL),
             (C_LORA, o3, DECAY_LORA), (C_LORA + LORA_W, o4, ICLR_LORA),
             (C_LORA + LORA_W + LORA_A, o5, GATE_LORA), (C_KA, oq, ATT_KV_DIM), (C_VA, ok, ATT_KV_DIM))
    return pl.pallas_call(
        functools.partial(_w_in_layout_kernel, moves=moves, pad=(C_LORA, C_LORA + LORA_W + LORA_A)),
        grid=(d // cols,),
        in_specs=[pl.BlockSpec((None, n, cols), lambda i: (layer, 0, i))],
        out_specs=pl.BlockSpec((Z_COLS, cols), lambda i: (0, i)),
        out_shape=jax.ShapeDtypeStruct((Z_COLS, d), BF16),
        compiler_params=_cparams(("parallel",)),
        name="w_in_layout",
    )(wt_layers)


def _in_proj_kernel(x_ref, g_ref, wt_ref, *refs, n_side):
    side_in, z_ref, side_out, h_ref = refs[:n_side], refs[n_side], refs[n_side + 1:-1], refs[-1]

    @pl.when(pl.program_id(1) == 0)
    def _():
        xf = x_ref[...]
        y = xf * lax.rsqrt(jnp.mean(xf * xf, axis=-1, keepdims=True) + RMS_EPS)
        h_ref[...] = (y * g_ref[...]).astype(BF16)

    z_ref[...] = _mm_nt(h_ref[...], wt_ref[...]).astype(BF16)
    for w_ref, wb_ref in zip(side_in, side_out):
        wb_ref[...] = w_ref[...].astype(BF16)


def _in_proj(x2, g1, w_in_t, side_weights, tm=1024, tn=1536, side_rows=64):
    m, d = x2.shape
    n = w_in_t.shape[0]
    nj = n // tn
    for w in side_weights:
        assert w.shape[0] // side_rows <= (m // tm) * nj, "not enough grid steps to cast this weight"
    side_specs = [pl.BlockSpec((side_rows, w.shape[1]),
                               lambda i, j, last=w.shape[0] // side_rows - 1: (jnp.minimum(i * nj + j, last), 0))
                  for w in side_weights]
    outs = pl.pallas_call(
        functools.partial(_in_proj_kernel, n_side=len(side_weights)),
        grid=(m // tm, nj),
        in_specs=[pl.BlockSpec((tm, d), lambda i, j: (i, 0)),
                  pl.BlockSpec((1, d), lambda i, j: (0, 0)),
                  pl.BlockSpec((tn, d), lambda i, j: (j, 0))] + side_specs,
        out_specs=[pl.BlockSpec((tm, tn), lambda i, j: (i, j))] + side_specs,
        out_shape=[jax.ShapeDtypeStruct((m, n), BF16)]
        + [jax.ShapeDtypeStruct(w.shape, BF16) for w in side_weights],
        scratch_shapes=[pltpu.VMEM((tm, d), BF16)],
        compiler_params=_cparams(("arbitrary", "arbitrary")),
        name="in_proj",
    )(x2, g1, w_in_t, *side_weights)
    return outs[0], outs[1:]


def _rwkv_prep_kernel(zr_ref, zk_ref, zv_ref, zl_ref, pr_ref, pk_ref, pv_ref, pli_ref,
                      mur_ref, muk_ref, muv_ref, mul_ref,
                      w0_ref, a0_ref, kk_ref, ka_ref, rk_ref, w2_ref, a2_ref, g2_ref,
                      at_ref, rt_ref, bt_ref, kt_ref, bg_ref, kg_ref, v_ref, g_ref, bonus_ref,
                      gT_ref, gm_ref, *, tm, first):
    keep = jnp.where(first, 0.0, 1.0)
    nc = tm // CHUNK
    last = slice(BF16_SUBLANES - 1, BF16_SUBLANES)

    def shifted(z, prev_last, mu):
        z = z.astype(F32)
        row = lax.broadcasted_iota(jnp.int32, z.shape, 0)
        zprev = jnp.where(row == 0, prev_last.astype(F32) * keep, pltpu.roll(z, 1, 0))
        return z + (zprev - z) * mu

    lo = shifted(zl_ref[...], pli_ref[last, :], mul_ref[...])
    tanh_wd = jnp.tanh(lo[:, :LORA_W]).astype(BF16)
    ad = lo[:, LORA_W:LORA_W + LORA_A].astype(BF16)
    sig_gd = jax.nn.sigmoid(lo[:, LORA_W + LORA_A:]).astype(BF16)

    ti = lax.broadcasted_iota(jnp.int32, (tm, tm), 0)
    si = lax.broadcasted_iota(jnp.int32, (tm, tm), 1)
    tri = jnp.where(((ti // CHUNK) == (si // CHUNK)) & (si <= ti), 1.0, 0.0).astype(BF16)
    ones_h = _head_ones(PAIR, RW_HEAD, 1.0)
    w_lora = jnp.dot(tanh_wd, w2_ref[...], preferred_element_type=F32)
    a_lora = jnp.dot(ad, a2_ref[...], preferred_element_type=F32)
    g_ref[...] = jnp.dot(sig_gd, g2_ref[...], preferred_element_type=F32).astype(BF16)
    yield

    stash = []
    for p in range(N_PAIRS):
        cs = slice(p * PAIR, (p + 1) * PAIR)
        r = shifted(zr_ref[:, cs], pr_ref[last, cs], mur_ref[:, cs])
        k = shifted(zk_ref[:, cs], pk_ref[last, cs], muk_ref[:, cs])
        v = shifted(zv_ref[:, cs], pv_ref[last, cs], muv_ref[:, cs])

        wpre = w0_ref[:, cs] + w_lora[:, cs]
        w = jnp.minimum(wpre, 0.0) - jnp.log1p(jnp.exp(-jnp.abs(wpre))) - 0.5
        lw = -jnp.exp(w)
        a = jax.nn.sigmoid(a0_ref[:, cs] + a_lora[:, cs])

        kk = k * kk_ref[:, cs]
        kk = kk * lax.rsqrt(jnp.maximum(_mm(kk * kk, ones_h), 1e-24))
        k2 = k * (1.0 + (a - 1.0) * ka_ref[:, cs])
        bonus_ref[:, cs] = (_mm(r * k2 * rk_ref[:, cs], ones_h) * v).astype(BF16)
        v_ref[:, cs] = v.astype(BF16)
        stash.append((r, k2, -kk, kk * a, lw))
        yield

    c_all = _mm_split(tri, jnp.concatenate([st[4] for st in stash], axis=1))
    yield

    for p in range(N_PAIRS):
        cs = slice(p * PAIR, (p + 1) * PAIR)
        r, k2, ka_vec, kb_vec, lw = stash[p]
        c = c_all[:, cs]
        c3 = c.reshape(nc, CHUNK, PAIR)
        c_mid = c3[:, CHUNK // 2 - 1:CHUNK // 2, :]
        c_end = c3[:, CHUNK - 1:CHUNK, :]
        cm = jnp.broadcast_to(c_mid, c3.shape).reshape(tm, PAIR)
        cT = jnp.broadcast_to(c_end, c3.shape).reshape(tm, PAIR)

        e_in = jnp.exp(c - cm)
        e_out = jnp.exp(cm - c)
        e_end = jnp.exp(cT - c)
        at_ref[:, cs] = (ka_vec * jnp.exp(c - lw - cm)).astype(BF16)
        rt_ref[:, cs] = (r * e_in).astype(BF16)
        bt_ref[:, cs] = (kb_vec * e_out).astype(BF16)
        kt_ref[:, cs] = (k2 * e_out).astype(BF16)
        bg_ref[:, cs] = (kb_vec * e_end).astype(BF16)
        kg_ref[:, cs] = (k2 * e_end).astype(BF16)
        gT_ref[:, :, cs] = jnp.exp(c_end)
        gm_ref[:, :, cs] = jnp.exp(c_mid)
        yield


def _rwkv_kernel(*refs, tb, nb):
    prep_in, (gnw_ref, gnb_ref, o_ref), scratch = refs[:20], refs[20:23], refs[23:]
    operands, s_ref = scratch[:11], scratch[11]
    j = pl.program_id(1)
    prep = functools.partial(_rwkv_prep_kernel, *prep_in, *operands, tm=tb, first=j == 0)
    chunk = functools.partial(_rwkv_chunk_kernel, *operands, gnw_ref, gnb_ref, o_ref, s_ref,
                              chunks=tb // CHUNK, pairs=N_PAIRS, first=j == 1)

    @pl.when(j == 0)
    def _():
        for _ in prep():
            pass

    @pl.when((j > 0) & (j < nb))
    def _():
        chunk(filler=prep())

    @pl.when(j == nb)
    def _():
        chunk()


def _rwkv(z, batch, seq, mu_r, mu_k, mu_v, mu_l, w0, a0, k_k, k_a, r_k, w2p, a2p, g2, gn_w, gn_b, tb=128):
    m = z.shape[0]
    nb = seq // tb
    pb = tb // BF16_SUBLANES
    blk = lambda b, j: b * nb + jnp.minimum(j, nb - 1)

    def cur(col0, width):
        return pl.BlockSpec((tb, width), lambda b, j, c0=col0 // width: (blk(b, j), c0))

    def prev(col0, width):
        return pl.BlockSpec((BF16_SUBLANES, width),
                            lambda b, j, c0=col0 // width: (jnp.maximum(blk(b, j) * pb - 1, 0), c0))

    def whole(rows, width):
        return pl.BlockSpec((rows, width), lambda b, j: (0, 0))

    in_specs = [cur(C_R, RW_DIM), cur(C_K, RW_DIM), cur(C_V, RW_DIM), cur(C_LORA, LORA_COLS),
                prev(C_R, RW_DIM), prev(C_K, RW_DIM), prev(C_V, RW_DIM), prev(C_LORA, LORA_COLS),
                whole(1, RW_DIM), whole(1, RW_DIM), whole(1, RW_DIM), whole(1, LORA_COLS),
                whole(1, RW_DIM), whole(1, RW_DIM), whole(1, RW_DIM), whole(1, RW_DIM), whole(1, RW_DIM),
                whole(LORA_W, RW_DIM), whole(LORA_A, RW_DIM), whole(GATE_LORA, RW_DIM),
                whole(1, RW_DIM), whole(1, RW_DIM)]
    big = pltpu.VMEM((tb, RW_DIM), BF16)
    per_chunk = pltpu.VMEM((tb // CHUNK, 1, RW_DIM), F32)
    return pl.pallas_call(
        functools.partial(_rwkv_kernel, tb=tb, nb=nb),
        grid=(batch, nb + 1),
        in_specs=in_specs,
        out_specs=pl.BlockSpec((tb, RW_DIM), lambda b, j: (b * nb + jnp.maximum(j - 1, 0), 0)),
        out_shape=jax.ShapeDtypeStruct((m, RW_DIM), BF16),
        scratch_shapes=[big] * 9 + [per_chunk] * 2 + [pltpu.VMEM((N_PAIRS, PAIR, PAIR), F32)],
        compiler_params=_cparams(("parallel", "arbitrary")),
        name="rwkv",
    )(z, z, z, z, z, z, z, z, mu_r, mu_k, mu_v, mu_l, w0, a0, k_k, k_a, r_k, w2p, a2p, g2, gn_w, gn_b)


def _rwkv_chunk_kernel(at_ref, rt_ref, bt_ref, kt_ref, bg_ref, kg_ref, v_ref, g_ref, bonus_ref,
                       gT_ref, gm_ref, gnw_ref, gnb_ref, o_ref, s_ref, *, chunks, pairs, first,
                       filler=None):
    @pl.when(first)
    def _():
        s_ref[...] = jnp.zeros_like(s_ref)

    T = CHUNK
    lane = lax.broadcasted_iota(jnp.int32, (T, PAIR), 1)
    head0 = lane < RW_HEAD
    ri = lax.broadcasted_iota(jnp.int32, (2 * T, 2 * T), 0)
    ci = lax.broadcasted_iota(jnp.int32, (2 * T, 2 * T), 1)
    same = (ri // T) == (ci // T)
    strict = same & ((ri % T) > (ci % T))
    incl = same & ((ri % T) >= (ci % T))
    eye = jnp.where(ri == ci, 1.0, 0.0)
    own = (ri // T) == (ci // RW_HEAD)

    items = [(p, c) for c in range(chunks) for p in range(pairs)]
    rows = lambda c: slice(c * T, (c + 1) * T)
    cols = lambda p: slice(p * PAIR, (p + 1) * PAIR)

    def stack(ref):
        out = []
        for p, c in items:
            x = ref[rows(c), cols(p)]
            zero = jnp.zeros_like(x)
            out.append(jnp.concatenate([jnp.where(head0, x, zero), jnp.where(head0, zero, x)], axis=0))
        return out

    each = lambda f, *ls: [f(*xs) for xs in zip(*ls)]
    La, Lr, Rb, Rk = stack(at_ref), stack(rt_ref), stack(bt_ref), stack(kt_ref)
    Rbg, Rkg, Vs = stack(bg_ref), stack(kg_ref), stack(v_ref)
    gm_row = [gm_ref[c][:, cols(p)] for p, c in items]
    gT_row = [gT_ref[c][:, cols(p)] for p, c in items]
    gate = [g_ref[rows(c), cols(p)] for p, c in items]
    bonus = [bonus_ref[rows(c), cols(p)] for p, c in items]
    filler = iter(()) if filler is None else filler
    tick = lambda: next(filler, None)

    AA = each(lambda la, lr, rb, rk: _mm_nt(jnp.concatenate([la, lr], axis=0),
                                            jnp.concatenate([rb, rk], axis=0)), La, Lr, Rb, Rk)
    tick()
    N = [jnp.where(strict, aa[:2 * T, :2 * T], 0.0) for aa in AA]
    Aak = [jnp.where(strict, aa[:2 * T, 2 * T:], 0.0) for aa in AA]
    Arb = [jnp.where(incl, aa[2 * T:, :2 * T], 0.0) for aa in AA]
    Ark = [jnp.where(incl, aa[2 * T:, 2 * T:], 0.0) for aa in AA]
    def live_rows(x, t0):
        return x if t0 == 0 else jnp.concatenate([x[t0:T, :], x[T + t0:, :]], axis=0)

    def all_rows(y, t0):
        if t0 == 0:
            return y
        zero = jnp.zeros((t0, y.shape[1]), y.dtype)
        return jnp.concatenate([zero, y[:T - t0, :], zero, y[T - t0:, :]], axis=0)

    def skip(span):
        return span if span % BF16_SUBLANES == 0 else 0

    def blockdiag(a, b):
        zero = jnp.zeros_like(a)
        return jnp.concatenate([jnp.concatenate([a, zero], axis=1),
                                jnp.concatenate([zero, b], axis=1)], axis=0)

    groups = [(i, i + 1) for i in range(0, len(items), 2)]
    W = each(lambda n: eye + n, N)
    P = [n.astype(BF16) for n in N]
    for a, b in groups:
        sq = _mm(jnp.concatenate([P[a], P[b]], axis=1), blockdiag(P[a], P[b]))
        P[a], P[b] = sq[:, :PAIR].astype(BF16), sq[:, PAIR:].astype(BF16)
    tick()
    span = 2
    while span < T:
        last = 2 * span >= T
        t0, t1 = skip(span), skip(2 * span)
        for a, b in groups:
            lhs = jnp.concatenate([live_rows(W[a].astype(BF16), t0), live_rows(W[b].astype(BF16), t0)], axis=1)
            if not last:
                lhs_p = jnp.concatenate([live_rows(P[a], t1), live_rows(P[b], t1)], axis=1)
                lhs = jnp.concatenate([lhs_p, lhs], axis=0)
            res = _mm(lhs, blockdiag(P[a], P[b]))
            if not last:
                sq = all_rows(res[:lhs_p.shape[0], :], t1)
                P[a], P[b] = sq[:, :PAIR].astype(BF16), sq[:, PAIR:].astype(BF16)
                res = res[lhs_p.shape[0]:, :]
            wp = all_rows(res, t0)
            W[a] = W[a] + wp[:, :PAIR]
            W[b] = W[b] + wp[:, PAIR:]
        tick()
        span *= 2
    AkV = each(_mm, Aak, Vs)
    tick()
    X = each(lambda w, la, akv: _mm(w, jnp.concatenate([la, akv.astype(BF16)], axis=1)).astype(BF16),
             W, La, AkV)
    tick()
    Z = each(lambda arb, ark, x, vs:
             _mm(jnp.concatenate([arb.astype(BF16), ark.astype(BF16)], axis=1),
                 jnp.concatenate([x, jnp.concatenate([jnp.zeros_like(vs), vs], axis=1)], axis=0)),
             Arb, Ark, X, Vs)
    tick()
    Q = each(lambda z, lr, gm: (z[:, :PAIR] + lr.astype(F32)) * gm, Z, Lr, gm_row)
    Y0 = [z[:, PAIR:] for z in Z]
    tick()
    Mbd = each(lambda rbg, x, gm: _mm_tn(rbg, x[:, :PAIR]) * gm, Rbg, X, gm_row)
    NcT = each(lambda x, vs, rbg, rkg: _mm_tn(jnp.concatenate([x[:, PAIR:], vs], axis=0),
                                              jnp.concatenate([rbg, rkg], axis=0)), X, Vs, Rbg, Rkg)
    for _ in filler:
        pass

    S = [s_ref[p] for p in range(pairs)]
    for c in range(chunks):
        for p in range(pairs):
            i = c * pairs + p
            Ys = _mm_nt(Q[i], S[p]) + Y0[i]
            S[p] = S[p] * gT_row[i] + _mm_nt(S[p], Mbd[i]) + NcT[i]
            mu = jnp.sum(Ys, axis=-1, keepdims=True) * (1.0 / RW_HEAD)
            d = jnp.where(own, Ys - mu, 0.0)
            var = jnp.sum(d * d, axis=-1, keepdims=True) * (1.0 / RW_HEAD)
            dn = d * lax.rsqrt(var + GN_EPS)
            yn = (dn[:T, :] + dn[T:, :]) * gnw_ref[:, cols(p)] + gnb_ref[:, cols(p)]
            out = (yn + bonus[i].astype(F32)) * gate[i].astype(F32)
            o_ref[rows(c), cols(p)] = out.astype(BF16)
    for p in range(pairs):
        s_ref[p] = S[p]


def _swa_kernel(sink_ref, q_ref, kc_ref, kp_ref, vc_ref, vp_ref, qg_ref, kg_ref, o_ref, *,
                steps_per_seq, slopes, qb):
    first = (pl.program_id(0) % steps_per_seq) == 0
    mean_h = _head_ones(LANES, ATT_HEAD, 1.0 / ATT_HEAD)

    def head_rms(x, gain):
        parts = []
        for b in range(x.shape[1] // LANES):
            xb = x[:, b * LANES:(b + 1) * LANES]
            ms = _mm(xb * xb, mean_h)
            parts.append(xb * lax.rsqrt(ms + RMS_EPS))
        return jnp.concatenate(parts, axis=1) * gain

    q = head_rms(q_ref[...].astype(F32), qg_ref[...]) * (ATT_HEAD ** -0.5 * LOG2E)
    q = q.astype(BF16)
    kcat = jnp.concatenate([kp_ref[...], kc_ref[...]], axis=0).astype(F32)
    kcat = head_rms(kcat, kg_ref[...]).astype(BF16)
    vcat = jnp.concatenate([vp_ref[...], vc_ref[...]], axis=0)

    qi = lax.broadcasted_iota(jnp.int32, (BLOCK, 2 * BLOCK), 0)
    kj = lax.broadcasted_iota(jnp.int32, (BLOCK, 2 * BLOCK), 1)
    dist_i = BLOCK + qi - kj
    in_window = (dist_i >= 0) & (dist_i < WINDOW)
    first_key = jnp.where(first, BLOCK, 0)
    neg_dist_rest = jnp.where(in_window, -dist_i.astype(F32), NEG_BIG)
    neg_dist_first = jnp.where(in_window & (kj >= first_key), -dist_i.astype(F32), NEG_BIG)
    neg_dist = [neg_dist_first] + [neg_dist_rest] * (qb - 1)

    lane_q = lax.broadcasted_iota(jnp.int32, (BLOCK, LANES), 1)
    lane_kv = lax.broadcasted_iota(jnp.int32, (2 * BLOCK, LANES), 1)
    zero_q = jnp.zeros((BLOCK, LANES), BF16)
    one_kv = jnp.ones((2 * BLOCK, LANES), BF16)
    kv_lane_blocks = range(ATT_KV_DIM // LANES)
    k_all = [kcat[:, b * LANES:(b + 1) * LANES] for b in kv_lane_blocks]
    k_all_rolled = [pltpu.roll(kb, ATT_HEAD, 1) for kb in k_all]
    window = lambda x, t: x[t * BLOCK:(t + 2) * BLOCK, :]
    v_aug = [[jnp.where((lane_kv // ATT_HEAD) == (j % 2),
                        window(vcat[:, (j // 2) * LANES:(j // 2 + 1) * LANES], t), one_kv)
              for j in range(ATT_KV_HEADS)] for t in range(qb)]
    items = [(t, h) for t in range(qb) for h in range(ATT_HEADS)]
    kv_of = lambda h: h // ATT_GROUP
    aligned = lambda h: (h % 2) == (kv_of(h) % 2)

    def scores(t, h):
        qh = jnp.where((lane_q // ATT_HEAD) == (h % 2),
                       q[t * BLOCK:(t + 1) * BLOCK, (h // 2) * LANES:(h // 2 + 1) * LANES], zero_q)
        kh = window((k_all if aligned(h) else k_all_rolled)[kv_of(h) // 2], t)
        return _mm_nt(qh, kh) + (slopes[h] * LOG2E) * neg_dist[t]

    s = [scores(t, h) for t, h in items]
    sink = [sink_ref[h] * LOG2E for _, h in items]
    mx = [jnp.maximum(jnp.max(s_, axis=-1, keepdims=True), sk) for s_, sk in zip(s, sink)]
    p = [jnp.exp2(s_ - m_) for s_, m_ in zip(s, mx)]
    pv = [_mm(p_, v_aug[t][kv_of(h)]) for p_, (t, h) in zip(p, items)]
    pv_sw = [pltpu.roll(x, ATT_HEAD, 1) for x in pv]
    out = []
    for i, (t, h) in enumerate(items):
        num, den = (pv[i], pv_sw[i]) if aligned(h) else (pv_sw[i], pv[i])
        out.append(num * (1.0 / (den + jnp.exp2(sink[i] - mx[i]))))
    for t in range(qb):
        o = out[t * ATT_HEADS:(t + 1) * ATT_HEADS]
        o_ref[t * BLOCK:(t + 1) * BLOCK, :] = jnp.concatenate(
            [jnp.where((lane_q // ATT_HEAD) == 0, o[2 * m], o[2 * m + 1]) for m in range(ATT_HEADS // 2)],
            axis=1).astype(BF16)


def _swa(z, sinks, q_gain_t, k_gain_t, seq, qb=8):
    m = z.shape[0]
    rows = qb * BLOCK
    slopes = tuple(float(s) for s in
                   np.exp2(-8.0 * np.arange(1, ATT_HEADS + 1, dtype=np.float32) / ATT_HEADS).astype(np.float32))
    kv_cur = lambda c0: pl.BlockSpec((rows, ATT_KV_DIM), lambda n, c=c0 // ATT_KV_DIM: (n, c))
    kv_prev = lambda c0: pl.BlockSpec((BLOCK, ATT_KV_DIM),
                                      lambda n, c=c0 // ATT_KV_DIM: (jnp.maximum(n * qb - 1, 0), c))
    return pl.pallas_call(
        functools.partial(_swa_kernel, steps_per_seq=seq // rows, slopes=slopes, qb=qb),
        grid=(m // rows,),
        in_specs=[pl.BlockSpec(memory_space=pltpu.SMEM),
                  pl.BlockSpec((rows, ATT_Q_DIM), lambda n: (n, C_Q // ATT_Q_DIM)),
                  kv_cur(C_KA), kv_prev(C_KA), kv_cur(C_VA), kv_prev(C_VA),
                  pl.BlockSpec((1, ATT_Q_DIM), lambda n: (0, 0)),
                  pl.BlockSpec((1, ATT_KV_DIM), lambda n: (0, 0))],
        out_specs=pl.BlockSpec((rows, ATT_Q_DIM), lambda n: (n, 0)),
        out_shape=jax.ShapeDtypeStruct((m, ATT_Q_DIM), BF16),
        compiler_params=_cparams(("parallel",)),
        name="swa",
    )(sinks, z, z, z, z, z, q_gain_t, k_gain_t)


def _mix_out_kernel(orw_ref, oatt_ref, wb1_ref, wb2_ref, zg_ref, wo_ref, x_ref, g_ref, x1_ref, h2_ref):
    p_rw = jnp.dot(orw_ref[...], wb1_ref[...], preferred_element_type=F32)
    p_att = jnp.dot(oatt_ref[...], wb2_ref[...], preferred_element_type=F32)
    g_rw = jax.nn.sigmoid(zg_ref[:, :D_MODEL].astype(F32))
    g_att = jax.nn.sigmoid(zg_ref[:, D_MODEL:].astype(F32))
    mix = (g_rw * p_rw + g_att * p_att).astype(BF16)
    x1 = x_ref[...] + jnp.dot(mix, wo_ref[...], preferred_element_type=F32)
    x1_ref[...] = x1
    y = x1 * lax.rsqrt(jnp.mean(x1 * x1, axis=-1, keepdims=True) + RMS_EPS)
    h2_ref[...] = (y * g_ref[...]).astype(BF16)


def _mix_out(o_rw, o_att, w_branch_b, z, w_out_b, x2, g2, tm=512):
    m = x2.shape[0]
    const = lambda rows, cols, r0=0: pl.BlockSpec((rows, cols), lambda i: (r0, 0),
                                                  pipeline_mode=pl.Buffered(1))
    rows = lambda cols, c0=0: pl.BlockSpec((tm, cols), lambda i: (i, c0))
    return pl.pallas_call(
        _mix_out_kernel,
        grid=(m // tm,),
        in_specs=[rows(RW_DIM), rows(ATT_Q_DIM), const(RW_DIM, D_MODEL), const(ATT_Q_DIM, D_MODEL, 1),
                  rows(2 * D_MODEL, C_GRW // (2 * D_MODEL)), const(D_MODEL, D_MODEL),
                  rows(D_MODEL), const(1, D_MODEL)],
        out_specs=[rows(D_MODEL), rows(D_MODEL)],
        out_shape=[jax.ShapeDtypeStruct((m, D_MODEL), F32),
                   jax.ShapeDtypeStruct((m, D_MODEL), BF16)],
        compiler_params=_cparams(("parallel",)),
        name="mix_out",
    )(o_rw, o_att, w_branch_b, w_branch_b, z, w_out_b, x2, g2)


def _ffn_up_kernel(h_ref, wvb_ref, wgb_ref, cwv_ref, cwg_ref, cbv_ref, cbg_ref, wd_ref, a_ref, wdb_ref,
                   cv_ref, cg_ref, *, tm, sub, tiles_per_seq):
    halo = 8
    wdb_ref[...] = wd_ref[...].astype(BF16)

    @pl.when((pl.program_id(1) % tiles_per_seq) == 0)
    def _():
        cv_ref[...] = jnp.zeros_like(cv_ref)
        cg_ref[...] = jnp.zeros_like(cg_ref)

    row = lax.broadcasted_iota(jnp.int32, (halo, a_ref.shape[1]), 0)

    def conv(u, carry_ref, cw_ref, cb_ref):
        p1 = carry_ref[halo - 1:halo, :]
        p2 = carry_ref[halo - 2:halo - 1, :]
        r1 = pltpu.roll(u, 1, 0)
        r2 = pltpu.roll(u, 2, 0)
        prev1 = jnp.concatenate([jnp.where(row == 0, p1, r1[:halo, :]), r1[halo:, :]], axis=0)
        prev2 = jnp.concatenate(
            [jnp.where(row == 0, p2, jnp.where(row == 1, p1, r2[:halo, :])), r2[halo:, :]], axis=0)
        carry_ref[...] = u[sub - halo:sub, :]
        return prev2 * cw_ref[0:1, :] + prev1 * cw_ref[1:2, :] + u * cw_ref[2:3, :] + cb_ref[...]

    for s in range(tm // sub):
        h = h_ref[s * sub:(s + 1) * sub, :]
        val = conv(jnp.dot(h, wvb_ref[...], preferred_element_type=F32), cv_ref, cwv_ref, cbv_ref)
        gate = conv(jnp.dot(h, wgb_ref[...], preferred_element_type=F32), cg_ref, cwg_ref, cbg_ref)
        a_ref[s * sub:(s + 1) * sub, :] = (gate * jax.nn.sigmoid(gate) * val).astype(BF16)


def _ffn_up(h2, w_up_b, conv_w, conv_b, w_down, seq, tm=4096, tn=512, sub=256):
    m = h2.shape[0]
    nj = D_FF // tn
    ni = m // tm
    wd_rows = w_down.shape[0] // (nj * ni)
    wd_spec = pl.BlockSpec((wd_rows, D_MODEL), lambda j, i: (j * ni + i, 0))
    return pl.pallas_call(
        functools.partial(_ffn_up_kernel, tm=tm, sub=sub, tiles_per_seq=seq // tm),
        grid=(nj, ni),
        in_specs=[pl.BlockSpec((tm, D_MODEL), lambda j, i: (i, 0)),
                  pl.BlockSpec((D_MODEL, tn), lambda j, i: (0, j)),
                  pl.BlockSpec((D_MODEL, tn), lambda j, i: (0, nj + j)),
                  pl.BlockSpec((3, tn), lambda j, i: (0, j)),
                  pl.BlockSpec((3, tn), lambda j, i: (0, nj + j)),
                  pl.BlockSpec((1, tn), lambda j, i: (0, j)),
                  pl.BlockSpec((1, tn), lambda j, i: (0, nj + j)),
                  wd_spec],
        out_specs=[pl.BlockSpec((tm, tn), lambda j, i: (i, j)), wd_spec],
        out_shape=[jax.ShapeDtypeStruct((m, D_FF), BF16), jax.ShapeDtypeStruct(w_down.shape, BF16)],
        scratch_shapes=[pltpu.VMEM((8, tn), F32), pltpu.VMEM((8, tn), F32)],
        compiler_params=_cparams(("parallel", "arbitrary")),
        name="ffn_up",
    )(h2, w_up_b, w_up_b, conv_w, conv_w, conv_b, conv_b, w_down)


def _ffn_down_kernel(a_ref, w_ref, x1_ref, o_ref):
    o_ref[...] = x1_ref[...] + jnp.dot(a_ref[...], w_ref[...], preferred_element_type=F32)


def _ffn_down(act, w_down_b, x1, tm=512, tn=1024):
    m = act.shape[0]
    return pl.pallas_call(
        _ffn_down_kernel,
        grid=(D_MODEL // tn, m // tm),
        in_specs=[pl.BlockSpec((tm, D_FF), lambda j, i: (i, 0)),
                  pl.BlockSpec((D_FF, tn), lambda j, i: (0, j)),
                  pl.BlockSpec((tm, tn), lambda j, i: (i, j))],
        out_specs=pl.BlockSpec((tm, tn), lambda j, i: (i, j)),
        out_shape=jax.ShapeDtypeStruct((m, D_MODEL), F32),
        compiler_params=_cparams(("parallel", "arbitrary")),
        name="ffn_down",
    )(act, w_down_b, x1)


def _pad_cols(w, n):
    return jnp.pad(w, ((0, 0), (0, n - w.shape[1])))


def _pad_rows(w, n):
    return jnp.pad(w, ((0, n - w.shape[0]), (0, 0)))


def _layer(x2, batch, seq, w_in_p, norm1_g, rw_mu, rw_w0, rw_w2, rw_a0, rw_a2, rw_g2, rw_k_k, rw_k_a,
           rw_r_k, rw_gn_w, rw_gn_b, q_norm_g, k_norm_g, attn_sinks, w_branch, w_out,
           norm2_g, w_up, conv_w, conv_b, w_down):
    row = lambda v: v.reshape(1, -1).astype(F32)
    o3 = 3 * RW_DIM
    o4 = o3 + DECAY_LORA
    o5 = o4 + ICLR_LORA
    o6 = o5 + GATE_LORA
    mu = rw_mu.reshape(1, -1)
    mu_l = jnp.concatenate([_pad_cols(mu[:, o3:o4], LORA_W), _pad_cols(mu[:, o4:o5], LORA_A),
                            mu[:, o5:o6]], axis=1)
    w2p = _pad_rows(rw_w2, LORA_W).astype(BF16)
    a2p = _pad_rows(rw_a2, LORA_A).astype(BF16)

    z, (w_branch_b, w_out_b, w_up_b) = _in_proj(x2, row(norm1_g), w_in_p, (w_branch, w_out, w_up))
    o_rw = _rwkv(z, batch, seq, mu[:, :RW_DIM], mu[:, RW_DIM:2 * RW_DIM], mu[:, 2 * RW_DIM:o3], mu_l,
                 row(rw_w0), row(rw_a0), row(rw_k_k), row(rw_k_a), row(rw_r_k),
                 w2p, a2p, rw_g2.astype(BF16), row(rw_gn_w), row(rw_gn_b))
    o_att = _swa(z, attn_sinks.astype(F32), jnp.tile(row(q_norm_g), (1, ATT_HEADS)),
                 jnp.tile(row(k_norm_g), (1, ATT_KV_HEADS)), seq)
    x1, h2 = _mix_out(o_rw, o_att, w_branch_b, z, w_out_b, x2, row(norm2_g))
    act, w_down_b = _ffn_up(h2, w_up_b, conv_w.astype(F32), row(conv_b), w_down, seq)
    return _ffn_down(act, w_down_b, x1)


def kernel(x, norm1_g, w_in, rw_mu, rw_w0, rw_w2, rw_a0, rw_a2, rw_g2, rw_k_k, rw_k_a, rw_r_k,
           rw_gn_w, rw_gn_b, q_norm_g, k_norm_g, attn_sinks, w_branch, w_out, norm2_g, w_up,
           conv_w, conv_b, w_down):
    batch, seq, d = x.shape
    x2 = x.reshape(batch * seq, d)
    params = (norm1_g, rw_mu, rw_w0, rw_w2, rw_a0, rw_a2, rw_g2, rw_k_k, rw_k_a, rw_r_k,
              rw_gn_w, rw_gn_b, q_norm_g, k_norm_g, attn_sinks, w_branch, w_out, norm2_g, w_up,
              conv_w, conv_b, w_down)
    for layer in range(norm1_g.shape[0]):
        x2 = _layer(x2, batch, seq, _w_in_layout(w_in, layer), *(p[layer] for p in params))
    return x2.reshape(batch, seq, d)
```

```python
import functools

import jax
import jax.numpy as jnp
import numpy as np
from jax import lax
from jax.experimental import pallas as pl
from jax.experimental.pallas import tpu as pltpu

F32 = jnp.float32
BF16 = jnp.bfloat16

LANES = 128
BF16_SUBLANES = 16
VMEM_LIMIT = 56 * 1024 * 1024

D_MODEL = 2048
RW_HEADS = 16
RW_HEAD = 64
RW_DIM = RW_HEADS * RW_HEAD
DECAY_LORA = 96
ICLR_LORA = 96
GATE_LORA = 256
GN_EPS = 64e-5
ATT_HEADS = 16
ATT_KV_HEADS = 4
ATT_GROUP = ATT_HEADS // ATT_KV_HEADS
ATT_HEAD = 64
ATT_Q_DIM = ATT_HEADS * ATT_HEAD
ATT_KV_DIM = ATT_KV_HEADS * ATT_HEAD
WINDOW = 128
BLOCK = 128
D_FF = 5632
RMS_EPS = 1e-6
NEG_BIG = -1e30
LOG2E = 1.4426950408889634

C_R = 0
C_K = RW_DIM
C_V = 2 * RW_DIM
C_Q = 3 * RW_DIM
C_GRW = C_Q + ATT_Q_DIM
C_GATT = C_GRW + D_MODEL
C_LORA = C_GATT + D_MODEL
LORA_W = 128
LORA_A = 128
LORA_COLS = LORA_W + LORA_A + GATE_LORA
C_KA = C_LORA + LORA_COLS
C_VA = C_KA + ATT_KV_DIM
Z_COLS = C_VA + ATT_KV_DIM

CHUNK = 64
PAIR = 2 * RW_HEAD
N_PAIRS = RW_DIM // PAIR


def _cparams(sem):
    return pltpu.CompilerParams(dimension_semantics=sem, vmem_limit_bytes=VMEM_LIMIT)


def _mm(a, b):
    return jnp.dot(a.astype(BF16), b.astype(BF16), preferred_element_type=F32)


def _mm_nt(a, b):
    return lax.dot_general(a.astype(BF16), b.astype(BF16), (((1,), (1,)), ((), ())),
                           preferred_element_type=F32)


def _mm_tn(a, b):
    return lax.dot_general(a.astype(BF16), b.astype(BF16), (((0,), (0,)), ((), ())),
                           preferred_element_type=F32)


def _mm_split(m01, x):
    hi = x.astype(BF16)
    lo = (x - hi.astype(F32)).astype(BF16)
    return (jnp.dot(m01, hi, preferred_element_type=F32)
            + jnp.dot(m01, lo, preferred_element_type=F32))


def _head_ones(n, head, scale):
    r = lax.broadcasted_iota(jnp.int32, (n, n), 0) // head
    c = lax.broadcasted_iota(jnp.int32, (n, n), 1) // head
    return jnp.where(r == c, scale, 0.0).astype(BF16)


def _w_in_layout_kernel(wt_ref, o_ref, *, moves, pad):
    o_ref[pad[0]:pad[1], :] = jnp.zeros((pad[1] - pad[0], o_ref.shape[1]), BF16)
    for dst, src, width in moves:
        o_ref[dst:dst + width, :] = wt_ref[src:src + width, :].astype(BF16)


def _w_in_layout(w_in_layers, layer, cols=256):
    _, d, n = w_in_layers.shape
    wt_layers = jnp.swapaxes(w_in_layers, 1, 2)
    o3 = 3 * RW_DIM
    o4 = o3 + DECAY_LORA
    o5 = o4 + ICLR_LORA
    o6 = o5 + GATE_LORA
    oq = o6 + ATT_Q_DIM
    ok = oq + ATT_KV_DIM
    ov = ok + ATT_KV_DIM
    moves = ((C_R, 0, o3), (C_Q, o6, ATT_Q_DIM), (C_GRW, ov, 2 * D_MODEL),
             (C_LORA, o3, DECAY_LORA), (C_LORA + LORA_W, o4, ICLR_LORA),
             (C_LORA + LORA_W + LORA_A, o5, GATE_LORA), (C_KA, oq, ATT_KV_DIM), (C_VA, ok, ATT_KV_DIM))
    return pl.pallas_call(
        functools.partial(_w_in_layout_kernel, moves=moves, pad=(C_LORA, C_LORA + LORA_W + LORA_A)),
        grid=(d // cols,),
        in_specs=[pl.BlockSpec((None, n, cols), lambda i: (layer, 0, i))],
        out_specs=pl.BlockSpec((Z_COLS, cols), lambda i: (0, i)),
        out_shape=jax.ShapeDtypeStruct((Z_COLS, d), BF16),
        compiler_params=_cparams(("parallel",)),
        name="w_in_layout",
    )(wt_layers)


def _in_proj_kernel(x_ref, g_ref, wt_ref, *refs, n_side):
    side_in, z_ref, side_out, h_ref = refs[:n_side], refs[n_side], refs[n_side + 1:-1], refs[-1]

    @pl.when(pl.program_id(1) == 0)
    def _():
        xf = x_ref[...]
        y = xf * lax.rsqrt(jnp.mean(xf * xf, axis=-1, keepdims=True) + RMS_EPS)
        h_ref[...] = (y * g_ref[...]).astype(BF16)

    z_ref[...] = _mm_nt(h_ref[...], wt_ref[...]).astype(BF16)
    for w_ref, wb_ref in zip(side_in, side_out):
        wb_ref[...] = w_ref[...].astype(BF16)


def _in_proj(x2, g1, w_in_t, side_weights, tm=1024, tn=1536, side_rows=64):
    m, d = x2.shape
    n = w_in_t.shape[0]
    nj = n // tn
    for w in side_weights:
        assert w.shape[0] // side_rows <= (m // tm) * nj, "not enough grid steps to cast this weight"
    side_specs = [pl.BlockSpec((side_rows, w.shape[1]),
                               lambda i, j, last=w.shape[0] // side_rows - 1: (jnp.minimum(i * nj + j, last), 0))
                  for w in side_weights]
    outs = pl.pallas_call(
        functools.partial(_in_proj_kernel, n_side=len(side_weights)),
        grid=(m // tm, nj),
        in_specs=[pl.BlockSpec((tm, d), lambda i, j: (i, 0)),
                  pl.BlockSpec((1, d), lambda i, j: (0, 0)),
                  pl.BlockSpec((tn, d), lambda i, j: (j, 0))] + side_specs,
        out_specs=[pl.BlockSpec((tm, tn), lambda i, j: (i, j))] + side_specs,
        out_shape=[jax.ShapeDtypeStruct((m, n), BF16)]
        + [jax.ShapeDtypeStruct(w.shape, BF16) for w in side_weights],
        scratch_shapes=[pltpu.VMEM((tm, d), BF16)],
        compiler_params=_cparams(("arbitrary", "arbitrary")),
        name="in_proj",
    )(x2, g1, w_in_t, *side_weights)
    return outs[0], outs[1:]


def _rwkv_prep_kernel(zr_ref, zk_ref, zv_ref, zl_ref, pr_ref, pk_ref, pv_ref, pli_ref,
                      mur_ref, muk_ref, muv_ref, mul_ref,
                      w0_ref, a0_ref, kk_ref, ka_ref, rk_ref, w2_ref, a2_ref, g2_ref,
                      at_ref, rt_ref, bt_ref, kt_ref, bg_ref, kg_ref, v_ref, g_ref, bonus_ref,
                      gT_ref, gm_ref, *, tm, first):
    keep = jnp.where(first, 0.0, 1.0)
    nc = tm // CHUNK
    last = slice(BF16_SUBLANES - 1, BF16_SUBLANES)

    def shifted(z, prev_last, mu):
        z = z.astype(F32)
        row = lax.broadcasted_iota(jnp.int32, z.shape, 0)
        zprev = jnp.where(row == 0, prev_last.astype(F32) * keep, pltpu.roll(z, 1, 0))
        return z + (zprev - z) * mu

    lo = shifted(zl_ref[...], pli_ref[last, :], mul_ref[...])
    tanh_wd = jnp.tanh(lo[:, :LORA_W]).astype(BF16)
    ad = lo[:, LORA_W:LORA_W + LORA_A].astype(BF16)
    sig_gd = jax.nn.sigmoid(lo[:, LORA_W + LORA_A:]).astype(BF16)

    ti = lax.broadcasted_iota(jnp.int32, (tm, tm), 0)
    si = lax.broadcasted_iota(jnp.int32, (tm, tm), 1)
    tri = jnp.where(((ti // CHUNK) == (si // CHUNK)) & (si <= ti), 1.0, 0.0).astype(BF16)
    ones_h = _head_ones(PAIR, RW_HEAD, 1.0)
    w_lora = jnp.dot(tanh_wd, w2_ref[...], preferred_element_type=F32)
    a_lora = jnp.dot(ad, a2_ref[...], preferred_element_type=F32)
    g_ref[...] = jnp.dot(sig_gd, g2_ref[...], preferred_element_type=F32).astype(BF16)
    yield

    stash = []
    for p in range(N_PAIRS):
        cs = slice(p * PAIR, (p + 1) * PAIR)
        r = shifted(zr_ref[:, cs], pr_ref[last, cs], mur_ref[:, cs])
        k = shifted(zk_ref[:, cs], pk_ref[last, cs], muk_ref[:, cs])
        v = shifted(zv_ref[:, cs], pv_ref[last, cs], muv_ref[:, cs])

        wpre = w0_ref[:, cs] + w_lora[:, cs]
        w = jnp.minimum(wpre, 0.0) - jnp.log1p(jnp.exp(-jnp.abs(wpre))) - 0.5
        lw = -jnp.exp(w)
        a = jax.nn.sigmoid(a0_ref[:, cs] + a_lora[:, cs])

        kk = k * kk_ref[:, cs]
        kk = kk * lax.rsqrt(jnp.maximum(_mm(kk * kk, ones_h), 1e-24))
        k2 = k * (1.0 + (a - 1.0) * ka_ref[:, cs])
        bonus_ref[:, cs] = (_mm(r * k2 * rk_ref[:, cs], ones_h) * v).astype(BF16)
        v_ref[:, cs] = v.astype(BF16)
        stash.append((r, k2, -kk, kk * a, lw))
        yield

    c_all = _mm_split(tri, jnp.concatenate([st[4] for st in stash], axis=1))
    yield

    for p in range(N_PAIRS):
        cs = slice(p * PAIR, (p + 1) * PAIR)
        r, k2, ka_vec, kb_vec, lw = stash[p]
        c = c_all[:, cs]
        c3 = c.reshape(nc, CHUNK, PAIR)
        c_mid = c3[:, CHUNK // 2 - 1:CHUNK // 2, :]
        c_end = c3[:, CHUNK - 1:CHUNK, :]
        cm = jnp.broadcast_to(c_mid, c3.shape).reshape(tm, PAIR)
        cT = jnp.broadcast_to(c_end, c3.shape).reshape(tm, PAIR)

        e_in = jnp.exp(c - cm)
        e_out = jnp.exp(cm - c)
        e_end = jnp.exp(cT - c)
        at_ref[:, cs] = (ka_vec * jnp.exp(c - lw - cm)).astype(BF16)
        rt_ref[:, cs] = (r * e_in).astype(BF16)
        bt_ref[:, cs] = (kb_vec * e_out).astype(BF16)
        kt_ref[:, cs] = (k2 * e_out).astype(BF16)
        bg_ref[:, cs] = (kb_vec * e_end).astype(BF16)
        kg_ref[:, cs] = (k2 * e_end).astype(BF16)
        gT_ref[:, :, cs] = jnp.exp(c_end)
        gm_ref[:, :, cs] = jnp.exp(c_mid)
        yield


def _rwkv_kernel(*refs, tb, nb):
    prep_in, (gnw_ref, gnb_ref, o_ref), scratch = refs[:20], refs[20:23], refs[23:]
    operands, s_ref = scratch[:11], scratch[11]
    j = pl.program_id(1)
    prep = functools.partial(_rwkv_prep_kernel, *prep_in, *operands, tm=tb, first=j == 0)
    chunk = functools.partial(_rwkv_chunk_kernel, *operands, gnw_ref, gnb_ref, o_ref, s_ref,
                              chunks=tb // CHUNK, pairs=N_PAIRS, first=j == 1)

    @pl.when(j == 0)
    def _():
        for _ in prep():
            pass

    @pl.when((j > 0) & (j < nb))
    def _():
        chunk(filler=prep())

    @pl.when(j == nb)
    def _():
        chunk()


def _rwkv(z, batch, seq, mu_r, mu_k, mu_v, mu_l, w0, a0, k_k, k_a, r_k, w2p, a2p, g2, gn_w, gn_b, tb=128):
    m = z.shape[0]
    nb = seq // tb
    pb = tb // BF16_SUBLANES
    blk = lambda b, j: b * nb + jnp.minimum(j, nb - 1)

    def cur(col0, width):
        return pl.BlockSpec((tb, width), lambda b, j, c0=col0 // width: (blk(b, j), c0))

    def prev(col0, width):
        return pl.BlockSpec((BF16_SUBLANES, width),
                            lambda b, j, c0=col0 // width: (jnp.maximum(blk(b, j) * pb - 1, 0), c0))

    def whole(rows, width):
        return pl.BlockSpec((rows, width), lambda b, j: (0, 0))

    in_specs = [cur(C_R, RW_DIM), cur(C_K, RW_DIM), cur(C_V, RW_DIM), cur(C_LORA, LORA_COLS),
                prev(C_R, RW_DIM), prev(C_K, RW_DIM), prev(C_V, RW_DIM), prev(C_LORA, LORA_COLS),
                whole(1, RW_DIM), whole(1, RW_DIM), whole(1, RW_DIM), whole(1, LORA_COLS),
                whole(1, RW_DIM), whole(1, RW_DIM), whole(1, RW_DIM), whole(1, RW_DIM), whole(1, RW_DIM),
                whole(LORA_W, RW_DIM), whole(LORA_A, RW_DIM), whole(GATE_LORA, RW_DIM),
                whole(1, RW_DIM), whole(1, RW_DIM)]
    big = pltpu.VMEM((tb, RW_DIM), BF16)
    per_chunk = pltpu.VMEM((tb // CHUNK, 1, RW_DIM), F32)
    return pl.pallas_call(
        functools.partial(_rwkv_kernel, tb=tb, nb=nb),
        grid=(batch, nb + 1),
        in_specs=in_specs,
        out_specs=pl.BlockSpec((tb, RW_DIM), lambda b, j: (b * nb + jnp.maximum(j - 1, 0), 0)),
        out_shape=jax.ShapeDtypeStruct((m, RW_DIM), BF16),
        scratch_shapes=[big] * 9 + [per_chunk] * 2 + [pltpu.VMEM((N_PAIRS, PAIR, PAIR), F32)],
        compiler_params=_cparams(("parallel", "arbitrary")),
        name="rwkv",
    )(z, z, z, z, z, z, z, z, mu_r, mu_k, mu_v, mu_l, w0, a0, k_k, k_a, r_k, w2p, a2p, g2, gn_w, gn_b)


def _rwkv_chunk_kernel(at_ref, rt_ref, bt_ref, kt_ref, bg_ref, kg_ref, v_ref, g_ref, bonus_ref,
                       gT_ref, gm_ref, gnw_ref, gnb_ref, o_ref, s_ref, *, chunks, pairs, first,
                       filler=None):
    @pl.when(first)
    def _():
        s_ref[...] = jnp.zeros_like(s_ref)

    T = CHUNK
    lane = lax.broadcasted_iota(jnp.int32, (T, PAIR), 1)
    head0 = lane < RW_HEAD
    ri = lax.broadcasted_iota(jnp.int32, (2 * T, 2 * T), 0)
    ci = lax.broadcasted_iota(jnp.int32, (2 * T, 2 * T), 1)
    same = (ri // T) == (ci // T)
    strict = same & ((ri % T) > (ci % T))
    incl = same & ((ri % T) >= (ci % T))
    eye = jnp.where(ri == ci, 1.0, 0.0)
    own = (ri // T) == (ci // RW_HEAD)

    items = [(p, c) for c in range(chunks) for p in range(pairs)]
    rows = lambda c: slice(c * T, (c + 1) * T)
    cols = lambda p: slice(p * PAIR, (p + 1) * PAIR)

    def stack(ref):
        out = []
        for p, c in items:
            x = ref[rows(c), cols(p)]
            zero = jnp.zeros_like(x)
            out.append(jnp.concatenate([jnp.where(head0, x, zero), jnp.where(head0, zero, x)], axis=0))
        return out

    each = lambda f, *ls: [f(*xs) for xs in zip(*ls)]
    La, Lr, Rb, Rk = stack(at_ref), stack(rt_ref), stack(bt_ref), stack(kt_ref)
    Rbg, Rkg, Vs = stack(bg_ref), stack(kg_ref), stack(v_ref)
    gm_row = [gm_ref[c][:, cols(p)] for p, c in items]
    gT_row = [gT_ref[c][:, cols(p)] for p, c in items]
    gate = [g_ref[rows(c), cols(p)] for p, c in items]
    bonus = [bonus_ref[rows(c), cols(p)] for p, c in items]
    filler = iter(()) if filler is None else filler
    tick = lambda: next(filler, None)

    AA = each(lambda la, lr, rb, rk: _mm_nt(jnp.concatenate([la, lr], axis=0),
                                            jnp.concatenate([rb, rk], axis=0)), La, Lr, Rb, Rk)
    tick()
    N = [jnp.where(strict, aa[:2 * T, :2 * T], 0.0) for aa in AA]
    Aak = [jnp.where(strict, aa[:2 * T, 2 * T:], 0.0) for aa in AA]
    Arb = [jnp.where(incl, aa[2 * T:, :2 * T], 0.0) for aa in AA]
    Ark = [jnp.where(incl, aa[2 * T:, 2 * T:], 0.0) for aa in AA]
    def live_rows(x, t0):
        return x if t0 == 0 else jnp.concatenate([x[t0:T, :], x[T + t0:, :]], axis=0)

    def all_rows(y, t0):
        if t0 == 0:
            return y
        zero = jnp.zeros((t0, y.shape[1]), y.dtype)
        return jnp.concatenate([zero, y[:T - t0, :], zero, y[T - t0:, :]], axis=0)

    def skip(span):
        return span if span % BF16_SUBLANES == 0 else 0

    def blockdiag(a, b):
        zero = jnp.zeros_like(a)
        return jnp.concatenate([jnp.concatenate([a, zero], axis=1),
                                jnp.concatenate([zero, b], axis=1)], axis=0)

    groups = [(i, i + 1) for i in range(0, len(items), 2)]
    W = each(lambda n: eye + n, N)
    P = [n.astype(BF16) for n in N]
    for a, b in groups:
        sq = _mm(jnp.concatenate([P[a], P[b]], axis=1), blockdiag(P[a], P[b]))
        P[a], P[b] = sq[:, :PAIR].astype(BF16), sq[:, PAIR:].astype(BF16)
    tick()
    span = 2
    while span < T:
        last = 2 * span >= T
        t0, t1 = skip(span), skip(2 * span)
        for a, b in groups:
            lhs = jnp.concatenate([live_rows(W[a].astype(BF16), t0), live_rows(W[b].astype(BF16), t0)], axis=1)
            if not last:
                lhs_p = jnp.concatenate([live_rows(P[a], t1), live_rows(P[b], t1)], axis=1)
                lhs = jnp.concatenate([lhs_p, lhs], axis=0)
            res = _mm(lhs, blockdiag(P[a], P[b]))
            if not last:
                sq = all_rows(res[:lhs_p.shape[0], :], t1)
                P[a], P[b] = sq[:, :PAIR].astype(BF16), sq[:, PAIR:].astype(BF16)
                res = res[lhs_p.shape[0]:, :]
            wp = all_rows(res, t0)
            W[a] = W[a] + wp[:, :PAIR]
            W[b] = W[b] + wp[:, PAIR:]
        tick()
        span *= 2
    AkV = each(_mm, Aak, Vs)
    tick()
    X = each(lambda w, la, akv: _mm(w, jnp.concatenate([la, akv.astype(BF16)], axis=1)).astype(BF16),
             W, La, AkV)
    tick()
    Z = each(lambda arb, ark, x, vs:
             _mm(jnp.concatenate([arb.astype(BF16), ark.astype(BF16)], axis=1),
                 jnp.concatenate([x, jnp.concatenate([jnp.zeros_like(vs), vs], axis=1)], axis=0)),
             Arb, Ark, X, Vs)
    tick()
    Q = each(lambda z, lr, gm: (z[:, :PAIR] + lr.astype(F32)) * gm, Z, Lr, gm_row)
    Y0 = [z[:, PAIR:] for z in Z]
    tick()
    Mbd = each(lambda rbg, x, gm: _mm_tn(rbg, x[:, :PAIR]) * gm, Rbg, X, gm_row)
    NcT = each(lambda x, vs, rbg, rkg: _mm_tn(jnp.concatenate([x[:, PAIR:], vs], axis=0),
                                              jnp.concatenate([rbg, rkg], axis=0)), X, Vs, Rbg, Rkg)
    for _ in filler:
        pass

    S = [s_ref[p] for p in range(pairs)]
    for c in range(chunks):
        for p in range(pairs):
            i = c * pairs + p
            Ys = _mm_nt(Q[i], S[p]) + Y0[i]
            S[p] = S[p] * gT_row[i] + _mm_nt(S[p], Mbd[i]) + NcT[i]
            mu = jnp.sum(Ys, axis=-1, keepdims=True) * (1.0 / RW_HEAD)
            d = jnp.where(own, Ys - mu, 0.0)
            var = jnp.sum(d * d, axis=-1, keepdims=True) * (1.0 / RW_HEAD)
            dn = d * lax.rsqrt(var + GN_EPS)
            yn = (dn[:T, :] + dn[T:, :]) * gnw_ref[:, cols(p)] + gnb_ref[:, cols(p)]
            out = (yn + bonus[i].astype(F32)) * gate[i].astype(F32)
            o_ref[rows(c), cols(p)] = out.astype(BF16)
    for p in range(pairs):
        s_ref[p] = S[p]


def _swa_kernel(sink_ref, q_ref, kc_ref, kp_ref, vc_ref, vp_ref, qg_ref, kg_ref, o_ref, *,
                steps_per_seq, slopes, qb):
    first = (pl.program_id(0) % steps_per_seq) == 0
    mean_h = _head_ones(LANES, ATT_HEAD, 1.0 / ATT_HEAD)

    def head_rms(x, gain):
        parts = []
        for b in range(x.shape[1] // LANES):
            xb = x[:, b * LANES:(b + 1) * LANES]
            ms = _mm(xb * xb, mean_h)
            parts.append(xb * lax.rsqrt(ms + RMS_EPS))
        return jnp.concatenate(parts, axis=1) * gain

    q = head_rms(q_ref[...].astype(F32), qg_ref[...]) * (ATT_HEAD ** -0.5 * LOG2E)
    q = q.astype(BF16)
    kcat = jnp.concatenate([kp_ref[...], kc_ref[...]], axis=0).astype(F32)
    kcat = head_rms(kcat, kg_ref[...]).astype(BF16)
    vcat = jnp.concatenate([vp_ref[...], vc_ref[...]], axis=0)

    qi = lax.broadcasted_iota(jnp.int32, (BLOCK, 2 * BLOCK), 0)
    kj = lax.broadcasted_iota(jnp.int32, (BLOCK, 2 * BLOCK), 1)
    dist_i = BLOCK + qi - kj
    in_window = (dist_i >= 0) & (dist_i < WINDOW)
    first_key = jnp.where(first, BLOCK, 0)
    neg_dist_rest = jnp.where(in_window, -dist_i.astype(F32), NEG_BIG)
    neg_dist_first = jnp.where(in_window & (kj >= first_key), -dist_i.astype(F32), NEG_BIG)
    neg_dist = [neg_dist_first] + [neg_dist_rest] * (qb - 1)

    lane_q = lax.broadcasted_iota(jnp.int32, (BLOCK, LANES), 1)
    lane_kv = lax.broadcasted_iota(jnp.int32, (2 * BLOCK, LANES), 1)
    zero_q = jnp.zeros((BLOCK, LANES), BF16)
    one_kv = jnp.ones((2 * BLOCK, LANES), BF16)
    kv_lane_blocks = range(ATT_KV_DIM // LANES)
    k_all = [kcat[:, b * LANES:(b + 1) * LANES] for b in kv_lane_blocks]
    k_all_rolled = [pltpu.roll(kb, ATT_HEAD, 1) for kb in k_all]
    window = lambda x, t: x[t * BLOCK:(t + 2) * BLOCK, :]
    v_aug = [[jnp.where((lane_kv // ATT_HEAD) == (j % 2),
                        window(vcat[:, (j // 2) * LANES:(j // 2 + 1) * LANES], t), one_kv)
              for j in range(ATT_KV_HEADS)] for t in range(qb)]
    items = [(t, h) for t in range(qb) for h in range(ATT_HEADS)]
    kv_of = lambda h: h // ATT_GROUP
    aligned = lambda h: (h % 2) == (kv_of(h) % 2)

    def scores(t, h):
        qh = jnp.where((lane_q // ATT_HEAD) == (h % 2),
                       q[t * BLOCK:(t + 1) * BLOCK, (h // 2) * LANES:(h // 2 + 1) * LANES], zero_q)
        kh = window((k_all if aligned(h) else k_all_rolled)[kv_of(h) // 2], t)
        return _mm_nt(qh, kh) + (slopes[h] * LOG2E) * neg_dist[t]

    s = [scores(t, h) for t, h in items]
    sink = [sink_ref[h] * LOG2E for _, h in items]
    mx = [jnp.maximum(jnp.max(s_, axis=-1, keepdims=True), sk) for s_, sk in zip(s, sink)]
    p = [jnp.exp2(s_ - m_) for s_, m_ in zip(s, mx)]
    pv = [_mm(p_, v_aug[t][kv_of(h)]) for p_, (t, h) in zip(p, items)]
    pv_sw = [pltpu.roll(x, ATT_HEAD, 1) for x in pv]
    out = []
    for i, (t, h) in enumerate(items):
        num, den = (pv[i], pv_sw[i]) if aligned(h) else (pv_sw[i], pv[i])
        out.append(num * (1.0 / (den + jnp.exp2(sink[i] - mx[i]))))
    for t in range(qb):
        o = out[t * ATT_HEADS:(t + 1) * ATT_HEADS]
        o_ref[t * BLOCK:(t + 1) * BLOCK, :] = jnp.concatenate(
            [jnp.where((lane_q // ATT_HEAD) == 0, o[2 * m], o[2 * m + 1]) for m in range(ATT_HEADS // 2)],
            axis=1).astype(BF16)


def _swa(z, sinks, q_gain_t, k_gain_t, seq, qb=8):
    m = z.shape[0]
    rows = qb * BLOCK
    slopes = tuple(float(s) for s in
                   np.exp2(-8.0 * np.arange(1, ATT_HEADS + 1, dtype=np.float32) / ATT_HEADS).astype(np.float32))
    kv_cur = lambda c0: pl.BlockSpec((rows, ATT_KV_DIM), lambda n, c=c0 // ATT_KV_DIM: (n, c))
    kv_prev = lambda c0: pl.BlockSpec((BLOCK, ATT_KV_DIM),
                                      lambda n, c=c0 // ATT_KV_DIM: (jnp.maximum(n * qb - 1, 0), c))
    return pl.pallas_call(
        functools.partial(_swa_kernel, steps_per_seq=seq // rows, slopes=slopes, qb=qb),
        grid=(m // rows,),
        in_specs=[pl.BlockSpec(memory_space=pltpu.SMEM),
                  pl.BlockSpec((rows, ATT_Q_DIM), lambda n: (n, C_Q // ATT_Q_DIM)),
                  kv_cur(C_KA), kv_prev(C_KA), kv_cur(C_VA), kv_prev(C_VA),
                  pl.BlockSpec((1, ATT_Q_DIM), lambda n: (0, 0)),
                  pl.BlockSpec((1, ATT_KV_DIM), lambda n: (0, 0))],
        out_specs=pl.BlockSpec((rows, ATT_Q_DIM), lambda n: (n, 0)),
        out_shape=jax.ShapeDtypeStruct((m, ATT_Q_DIM), BF16),
        compiler_params=_cparams(("parallel",)),
        name="swa",
    )(sinks, z, z, z, z, z, q_gain_t, k_gain_t)


def _mix_out_kernel(orw_ref, oatt_ref, wb1_ref, wb2_ref, zg_ref, wo_ref, x_ref, g_ref, x1_ref, h2_ref):
    p_rw = jnp.dot(orw_ref[...], wb1_ref[...], preferred_element_type=F32)
    p_att = jnp.dot(oatt_ref[...], wb2_ref[...], preferred_element_type=F32)
    g_rw = jax.nn.sigmoid(zg_ref[:, :D_MODEL].astype(F32))
    g_att = jax.nn.sigmoid(zg_ref[:, D_MODEL:].astype(F32))
    mix = (g_rw * p_rw + g_att * p_att).astype(BF16)
    x1 = x_ref[...] + jnp.dot(mix, wo_ref[...], preferred_element_type=F32)
    x1_ref[...] = x1
    y = x1 * lax.rsqrt(jnp.mean(x1 * x1, axis=-1, keepdims=True) + RMS_EPS)
    h2_ref[...] = (y * g_ref[...]).astype(BF16)


def _mix_out(o_rw, o_att, w_branch_b, z, w_out_b, x2, g2, tm=512):
    m = x2.shape[0]
    const = lambda rows, cols, r0=0: pl.BlockSpec((rows, cols), lambda i: (r0, 0),
                                                  pipeline_mode=pl.Buffered(1))
    rows = lambda cols, c0=0: pl.BlockSpec((tm, cols), lambda i: (i, c0))
    return pl.pallas_call(
        _mix_out_kernel,
        grid=(m // tm,),
        in_specs=[rows(RW_DIM), rows(ATT_Q_DIM), const(RW_DIM, D_MODEL), const(ATT_Q_DIM, D_MODEL, 1),
                  rows(2 * D_MODEL, C_GRW // (2 * D_MODEL)), const(D_MODEL, D_MODEL),
                  rows(D_MODEL), const(1, D_MODEL)],
        out_specs=[rows(D_MODEL), rows(D_MODEL)],
        out_shape=[jax.ShapeDtypeStruct((m, D_MODEL), F32),
                   jax.ShapeDtypeStruct((m, D_MODEL), BF16)],
        compiler_params=_cparams(("parallel",)),
        name="mix_out",
    )(o_rw, o_att, w_branch_b, w_branch_b, z, w_out_b, x2, g2)


def _ffn_up_kernel(h_ref, wvb_ref, wgb_ref, cwv_ref, cwg_ref, cbv_ref, cbg_ref, wd_ref, a_ref, wdb_ref,
                   cv_ref, cg_ref, *, tm, sub, tiles_per_seq, col_split=2):
    halo = 8
    wdb_ref[...] = wd_ref[...].astype(BF16)

    @pl.when((pl.program_id(1) % tiles_per_seq) == 0)
    def _():
        cv_ref[...] = jnp.zeros_like(cv_ref)
        cg_ref[...] = jnp.zeros_like(cg_ref)

    cw = a_ref.shape[1] // col_split
    row = lax.broadcasted_iota(jnp.int32, (halo, cw), 0)

    def conv(u, carry_ref, cw_ref, cb_ref, cs):
        p1 = carry_ref[halo - 1:halo, cs]
        p2 = carry_ref[halo - 2:halo - 1, cs]
        r1 = pltpu.roll(u, 1, 0)
        r2 = pltpu.roll(u, 2, 0)
        prev1 = jnp.concatenate([jnp.where(row == 0, p1, r1[:halo, :]), r1[halo:, :]], axis=0)
        prev2 = jnp.concatenate(
            [jnp.where(row == 0, p2, jnp.where(row == 1, p1, r2[:halo, :])), r2[halo:, :]], axis=0)
        carry_ref[:, cs] = u[sub - halo:sub, :]
        return prev2 * cw_ref[0:1, cs] + prev1 * cw_ref[1:2, cs] + u * cw_ref[2:3, cs] + cb_ref[:, cs]

    for s in range(tm // sub):
        h = h_ref[s * sub:(s + 1) * sub, :]
        for c in range(col_split):
            cs = slice(c * cw, (c + 1) * cw)
            val = conv(jnp.dot(h, wvb_ref[:, cs], preferred_element_type=F32), cv_ref, cwv_ref, cbv_ref, cs)
            gate = conv(jnp.dot(h, wgb_ref[:, cs], preferred_element_type=F32), cg_ref, cwg_ref, cbg_ref, cs)
            a_ref[s * sub:(s + 1) * sub, cs] = (gate * jax.nn.sigmoid(gate) * val).astype(BF16)


def _ffn_up(h2, w_up_b, conv_w, conv_b, w_down, seq, tm=2048, tn=512, sub=256):
    m = h2.shape[0]
    nj = D_FF // tn
    ni = m // tm
    wd_rows = w_down.shape[0] // (nj * ni)
    wd_spec = pl.BlockSpec((wd_rows, D_MODEL), lambda j, i: (j * ni + i, 0))
    return pl.pallas_call(
        functools.partial(_ffn_up_kernel, tm=tm, sub=sub, tiles_per_seq=seq // tm),
        grid=(nj, ni),
        in_specs=[pl.BlockSpec((tm, D_MODEL), lambda j, i: (i, 0)),
                  pl.BlockSpec((D_MODEL, tn), lambda j, i: (0, j)),
                  pl.BlockSpec((D_MODEL, tn), lambda j, i: (0, nj + j)),
                  pl.BlockSpec((3, tn), lambda j, i: (0, j)),
                  pl.BlockSpec((3, tn), lambda j, i: (0, nj + j)),
                  pl.BlockSpec((1, tn), lambda j, i: (0, j)),
                  pl.BlockSpec((1, tn), lambda j, i: (0, nj + j)),
                  wd_spec],
        out_specs=[pl.BlockSpec((tm, tn), lambda j, i: (i, j)), wd_spec],
        out_shape=[jax.ShapeDtypeStruct((m, D_FF), BF16), jax.ShapeDtypeStruct(w_down.shape, BF16)],
        scratch_shapes=[pltpu.VMEM((8, tn), F32), pltpu.VMEM((8, tn), F32)],
        compiler_params=_cparams(("parallel", "arbitrary")),
        name="ffn_up",
    )(h2, w_up_b, w_up_b, conv_w, conv_w, conv_b, conv_b, w_down)


def _ffn_down_kernel(a_ref, w_ref, x1_ref, o_ref):
    o_ref[...] = x1_ref[...] + jnp.dot(a_ref[...], w_ref[...], preferred_element_type=F32)


def _ffn_down(act, w_down_b, x1, tm=512, tn=1024):
    m = act.shape[0]
    return pl.pallas_call(
        _ffn_down_kernel,
        grid=(D_MODEL // tn, m // tm),
        in_specs=[pl.BlockSpec((tm, D_FF), lambda j, i: (i, 0)),
                  pl.BlockSpec((D_FF, tn), lambda j, i: (0, j)),
                  pl.BlockSpec((tm, tn), lambda j, i: (i, j))],
        out_specs=pl.BlockSpec((tm, tn), lambda j, i: (i, j)),
        out_shape=jax.ShapeDtypeStruct((m, D_MODEL), F32),
        compiler_params=_cparams(("parallel", "arbitrary")),
        name="ffn_down",
    )(act, w_down_b, x1)


def _pad_cols(w, n):
    return jnp.pad(w, ((0, 0), (0, n - w.shape[1])))


def _pad_rows(w, n):
    return jnp.pad(w, ((0, n - w.shape[0]), (0, 0)))


def _layer(x2, batch, seq, w_in_p, norm1_g, rw_mu, rw_w0, rw_w2, rw_a0, rw_a2, rw_g2, rw_k_k, rw_k_a,
           rw_r_k, rw_gn_w, rw_gn_b, q_norm_g, k_norm_g, attn_sinks, w_branch, w_out,
           norm2_g, w_up, conv_w, conv_b, w_down):
    row = lambda v: v.reshape(1, -1).astype(F32)
    o3 = 3 * RW_DIM
    o4 = o3 + DECAY_LORA
    o5 = o4 + ICLR_LORA
    o6 = o5 + GATE_LORA
    mu = rw_mu.reshape(1, -1)
    mu_l = jnp.concatenate([_pad_cols(mu[:, o3:o4], LORA_W), _pad_cols(mu[:, o4:o5], LORA_A),
                            mu[:, o5:o6]], axis=1)
    w2p = _pad_rows(rw_w2, LORA_W).astype(BF16)
    a2p = _pad_rows(rw_a2, LORA_A).astype(BF16)

    z, (w_branch_b, w_out_b, w_up_b) = _in_proj(x2, row(norm1_g), w_in_p, (w_branch, w_out, w_up))
    o_rw = _rwkv(z, batch, seq, mu[:, :RW_DIM], mu[:, RW_DIM:2 * RW_DIM], mu[:, 2 * RW_DIM:o3], mu_l,
                 row(rw_w0), row(rw_a0), row(rw_k_k), row(rw_k_a), row(rw_r_k),
                 w2p, a2p, rw_g2.astype(BF16), row(rw_gn_w), row(rw_gn_b))
    o_att = _swa(z, attn_sinks.astype(F32), jnp.tile(row(q_norm_g), (1, ATT_HEADS)),
                 jnp.tile(row(k_norm_g), (1, ATT_KV_HEADS)), seq)
    x1, h2 = _mix_out(o_rw, o_att, w_branch_b, z, w_out_b, x2, row(norm2_g))
    act, w_down_b = _ffn_up(h2, w_up_b, conv_w.astype(F32), row(conv_b), w_down, seq)
    return _ffn_down(act, w_down_b, x1)


def kernel(x, norm1_g, w_in, rw_mu, rw_w0, rw_w2, rw_a0, rw_a2, rw_g2, rw_k_k, rw_k_a, rw_r_k,
           rw_gn_w, rw_gn_b, q_norm_g, k_norm_g, attn_sinks, w_branch, w_out, norm2_g, w_up,
           conv_w, conv_b, w_down):
    batch, seq, d = x.shape
    x2 = x.reshape(batch * seq, d)
    params = (norm1_g, rw_mu, rw_w0, rw_w2, rw_a0, rw_a2, rw_g2, rw_k_k, rw_k_a, rw_r_k,
              rw_gn_w, rw_gn_b, q_norm_g, k_norm_g, attn_sinks, w_branch, w_out, norm2_g, w_up,
              conv_w, conv_b, w_down)
    for layer in range(norm1_g.shape[0]):
        x2 = _layer(x2, batch, seq, _w_in_layout(w_in, layer), *(p[layer] for p in params))
    return x2.reshape(batch, seq, d)
```

```python
import functools

import jax
import jax.numpy as jnp
import numpy as np
from jax import lax
from jax.experimental import pallas as pl
from jax.experimental.pallas import tpu as pltpu

F32 = jnp.float32
BF16 = jnp.bfloat16

LANES = 128
BF16_SUBLANES = 16
VMEM_LIMIT = 56 * 1024 * 1024

D_MODEL = 2048
RW_HEADS = 16
RW_HEAD = 64
RW_DIM = RW_HEADS * RW_HEAD
DECAY_LORA = 96
ICLR_LORA = 96
GATE_LORA = 256
GN_EPS = 64e-5
ATT_HEADS = 16
ATT_KV_HEADS = 4
ATT_GROUP = ATT_HEADS // ATT_KV_HEADS
ATT_HEAD = 64
ATT_Q_DIM = ATT_HEADS * ATT_HEAD
ATT_KV_DIM = ATT_KV_HEADS * ATT_HEAD
WINDOW = 128
BLOCK = 128
D_FF = 5632
RMS_EPS = 1e-6
NEG_BIG = -1e30
LOG2E = 1.4426950408889634

C_R = 0
C_K = RW_DIM
C_V = 2 * RW_DIM
C_Q = 3 * RW_DIM
C_GRW = C_Q + ATT_Q_DIM
C_GATT = C_GRW + D_MODEL
C_LORA = C_GATT + D_MODEL
LORA_W = 128
LORA_A = 128
LORA_COLS = LORA_W + LORA_A + GATE_LORA
C_KA = C_LORA + LORA_COLS
C_VA = C_KA + ATT_KV_DIM
Z_COLS = C_VA + ATT_KV_DIM

CHUNK = 64
PAIR = 2 * RW_HEAD
N_PAIRS = RW_DIM // PAIR


def _cparams(sem):
    return pltpu.CompilerParams(dimension_semantics=sem, vmem_limit_bytes=VMEM_LIMIT)


def _mm(a, b):
    return jnp.dot(a.astype(BF16), b.astype(BF16), preferred_element_type=F32)


def _mm_nt(a, b):
    return lax.dot_general(a.astype(BF16), b.astype(BF16), (((1,), (1,)), ((), ())),
                           preferred_element_type=F32)


def _mm_tn(a, b):
    return lax.dot_general(a.astype(BF16), b.astype(BF16), (((0,), (0,)), ((), ())),
                           preferred_element_type=F32)


def _mm_split(m01, x):
    hi = x.astype(BF16)
    lo = (x - hi.astype(F32)).astype(BF16)
    return (jnp.dot(m01, hi, preferred_element_type=F32)
            + jnp.dot(m01, lo, preferred_element_type=F32))


def _head_ones(n, head, scale):
    r = lax.broadcasted_iota(jnp.int32, (n, n), 0) // head
    c = lax.broadcasted_iota(jnp.int32, (n, n), 1) // head
    return jnp.where(r == c, scale, 0.0).astype(BF16)


def _w_in_layout_kernel(wt_ref, o_ref, *, moves, pad):
    o_ref[pad[0]:pad[1], :] = jnp.zeros((pad[1] - pad[0], o_ref.shape[1]), BF16)
    for dst, src, width in moves:
        o_ref[dst:dst + width, :] = wt_ref[src:src + width, :].astype(BF16)


def _w_in_layout(w_in_layers, layer, cols=256):
    _, d, n = w_in_layers.shape
    wt_layers = jnp.swapaxes(w_in_layers, 1, 2)
    o3 = 3 * RW_DIM
    o4 = o3 + DECAY_LORA
    o5 = o4 + ICLR_LORA
    o6 = o5 + GATE_LORA
    oq = o6 + ATT_Q_DIM
    ok = oq + ATT_KV_DIM
    ov = ok + ATT_KV_DIM
    moves = ((C_R, 0, o3), (C_Q, o6, ATT_Q_DIM), (C_GRW, ov, 2 * D_MODEL),
             (C_LORA, o3, DECAY_LORA), (C_LORA + LORA_W, o4, ICLR_LORA),
             (C_LORA + LORA_W + LORA_A, o5, GATE_LORA), (C_KA, oq, ATT_KV_DIM), (C_VA, ok, ATT_KV_DIM))
    return pl.pallas_call(
        functools.partial(_w_in_layout_kernel, moves=moves, pad=(C_LORA, C_LORA + LORA_W + LORA_A)),
        grid=(d // cols,),
        in_specs=[pl.BlockSpec((None, n, cols), lambda i: (layer, 0, i))],
        out_specs=pl.BlockSpec((Z_COLS, cols), lambda i: (0, i)),
        out_shape=jax.ShapeDtypeStruct((Z_COLS, d), BF16),
        compiler_params=_cparams(("parallel",)),
        name="w_in_layout",
    )(wt_layers)


def _in_proj_kernel(x_ref, g_ref, wt_ref, *refs, n_side):
    side_in, z_ref, side_out, h_ref = refs[:n_side], refs[n_side], refs[n_side + 1:-1], refs[-1]

    @pl.when(pl.program_id(1) == 0)
    def _():
        xf = x_ref[...]
        y = xf * lax.rsqrt(jnp.mean(xf * xf, axis=-1, keepdims=True) + RMS_EPS)
        h_ref[...] = (y * g_ref[...]).astype(BF16)

    z_ref[...] = _mm_nt(h_ref[...], wt_ref[...]).astype(BF16)
    for w_ref, wb_ref in zip(side_in, side_out):
        wb_ref[...] = w_ref[...].astype(BF16)


def _in_proj(x2, g1, w_in_t, side_weights, tm=1024, tn=1536, side_rows=64):
    m, d = x2.shape
    n = w_in_t.shape[0]
    nj = n // tn
    for w in side_weights:
        assert w.shape[0] // side_rows <= (m // tm) * nj, "not enough grid steps to cast this weight"
    side_specs = [pl.BlockSpec((side_rows, w.shape[1]),
                               lambda i, j, last=w.shape[0] // side_rows - 1: (jnp.minimum(i * nj + j, last), 0))
                  for w in side_weights]
    outs = pl.pallas_call(
        functools.partial(_in_proj_kernel, n_side=len(side_weights)),
        grid=(m // tm, nj),
        in_specs=[pl.BlockSpec((tm, d), lambda i, j: (i, 0)),
                  pl.BlockSpec((1, d), lambda i, j: (0, 0)),
                  pl.BlockSpec((tn, d), lambda i, j: (j, 0))] + side_specs,
        out_specs=[pl.BlockSpec((tm, tn), lambda i, j: (i, j))] + side_specs,
        out_shape=[jax.ShapeDtypeStruct((m, n), BF16)]
        + [jax.ShapeDtypeStruct(w.shape, BF16) for w in side_weights],
        scratch_shapes=[pltpu.VMEM((tm, d), BF16)],
        compiler_params=_cparams(("arbitrary", "arbitrary")),
        name="in_proj",
    )(x2, g1, w_in_t, *side_weights)
    return outs[0], outs[1:]


def _rwkv_prep_kernel(zr_ref, zk_ref, zv_ref, zl_ref, pr_ref, pk_ref, pv_ref, pli_ref,
                      mur_ref, muk_ref, muv_ref, mul_ref,
                      w0_ref, a0_ref, kk_ref, ka_ref, rk_ref, w2_ref, a2_ref, g2_ref,
                      at_ref, rt_ref, bt_ref, kt_ref, bg_ref, kg_ref, v_ref, g_ref, bonus_ref,
                      gT_ref, gm_ref, *, tm, first):
    keep = jnp.where(first, 0.0, 1.0)
    nc = tm // CHUNK
    last = slice(BF16_SUBLANES - 1, BF16_SUBLANES)

    def shifted(z, prev_last, mu):
        z = z.astype(F32)
        row = lax.broadcasted_iota(jnp.int32, z.shape, 0)
        zprev = jnp.where(row == 0, prev_last.astype(F32) * keep, pltpu.roll(z, 1, 0))
        return z + (zprev - z) * mu

    lo = shifted(zl_ref[...], pli_ref[last, :], mul_ref[...])
    tanh_wd = jnp.tanh(lo[:, :LORA_W]).astype(BF16)
    ad = lo[:, LORA_W:LORA_W + LORA_A].astype(BF16)
    sig_gd = jax.nn.sigmoid(lo[:, LORA_W + LORA_A:]).astype(BF16)

    ti = lax.broadcasted_iota(jnp.int32, (tm, tm), 0)
    si = lax.broadcasted_iota(jnp.int32, (tm, tm), 1)
    tri = jnp.where(((ti // CHUNK) == (si // CHUNK)) & (si <= ti), 1.0, 0.0).astype(BF16)
    ones_h = _head_ones(PAIR, RW_HEAD, 1.0)
    w_lora = jnp.dot(tanh_wd, w2_ref[...], preferred_element_type=F32)
    a_lora = jnp.dot(ad, a2_ref[...], preferred_element_type=F32)
    g_ref[...] = jnp.dot(sig_gd, g2_ref[...], preferred_element_type=F32).astype(BF16)
    yield

    stash = []
    for p in range(N_PAIRS):
        cs = slice(p * PAIR, (p + 1) * PAIR)
        r = shifted(zr_ref[:, cs], pr_ref[last, cs], mur_ref[:, cs])
        k = shifted(zk_ref[:, cs], pk_ref[last, cs], muk_ref[:, cs])
        v = shifted(zv_ref[:, cs], pv_ref[last, cs], muv_ref[:, cs])

        wpre = w0_ref[:, cs] + w_lora[:, cs]
        w = jnp.minimum(wpre, 0.0) - jnp.log1p(jnp.exp(-jnp.abs(wpre))) - 0.5
        lw = -jnp.exp(w)
        a = jax.nn.sigmoid(a0_ref[:, cs] + a_lora[:, cs])

        kk = k * kk_ref[:, cs]
        kk = kk * lax.rsqrt(jnp.maximum(_mm(kk * kk, ones_h), 1e-24))
        k2 = k * (1.0 + (a - 1.0) * ka_ref[:, cs])
        bonus_ref[:, cs] = (_mm(r * k2 * rk_ref[:, cs], ones_h) * v).astype(BF16)
        v_ref[:, cs] = v.astype(BF16)
        stash.append((r, k2, -kk, kk * a, lw))
        yield

    c_all = _mm_split(tri, jnp.concatenate([st[4] for st in stash], axis=1))
    yield

    for p in range(N_PAIRS):
        cs = slice(p * PAIR, (p + 1) * PAIR)
        r, k2, ka_vec, kb_vec, lw = stash[p]
        c = c_all[:, cs]
        c3 = c.reshape(nc, CHUNK, PAIR)
        c_mid = c3[:, CHUNK // 2 - 1:CHUNK // 2, :]
        c_end = c3[:, CHUNK - 1:CHUNK, :]
        cm = jnp.broadcast_to(c_mid, c3.shape).reshape(tm, PAIR)
        cT = jnp.broadcast_to(c_end, c3.shape).reshape(tm, PAIR)

        e_in = jnp.exp(c - cm)
        e_out = jnp.exp(cm - c)
        e_end = jnp.exp(cT - c)
        at_ref[:, cs] = (ka_vec * jnp.exp(c - lw - cm)).astype(BF16)
        rt_ref[:, cs] = (r * e_in).astype(BF16)
        bt_ref[:, cs] = (kb_vec * e_out).astype(BF16)
        kt_ref[:, cs] = (k2 * e_out).astype(BF16)
        bg_ref[:, cs] = (kb_vec * e_end).astype(BF16)
        kg_ref[:, cs] = (k2 * e_end).astype(BF16)
        gT_ref[:, :, cs] = jnp.exp(c_end)
        gm_ref[:, :, cs] = jnp.exp(c_mid)
        yield


def _rwkv_kernel(*refs, tb, nb, n_side):
    prep_in, (gnw_ref, gnb_ref), refs = refs[:20], refs[20:22], refs[22:]
    side_in, o_ref, side_out, scratch = refs[:n_side], refs[n_side], refs[n_side + 1:2 * n_side + 1], refs[2 * n_side + 1:]
    operands, s_ref = scratch[:11], scratch[11]
    for w_ref, wb_ref in zip(side_in, side_out):
        wb_ref[...] = w_ref[...].astype(BF16)
    j = pl.program_id(1)
    prep = functools.partial(_rwkv_prep_kernel, *prep_in, *operands, tm=tb, first=j == 0)
    chunk = functools.partial(_rwkv_chunk_kernel, *operands, gnw_ref, gnb_ref, o_ref, s_ref,
                              chunks=tb // CHUNK, pairs=N_PAIRS, first=j == 1)

    @pl.when(j == 0)
    def _():
        for _ in prep():
            pass

    @pl.when((j > 0) & (j < nb))
    def _():
        chunk(filler=prep())

    @pl.when(j == nb)
    def _():
        chunk()


def _rwkv(z, batch, seq, mu_r, mu_k, mu_v, mu_l, w0, a0, k_k, k_a, r_k, w2p, a2p, g2, gn_w, gn_b,
          side_weights, tb=128, side_rows=32):
    m = z.shape[0]
    nb = seq // tb
    for w in side_weights:
        assert w.shape[0] // side_rows <= batch * (nb + 1), "not enough grid steps to cast this weight"
    side_specs = [pl.BlockSpec((side_rows, w.shape[1]),
                               lambda b, j, last=w.shape[0] // side_rows - 1:
                               (jnp.minimum(b * (nb + 1) + j, last), 0)) for w in side_weights]
    pb = tb // BF16_SUBLANES
    blk = lambda b, j: b * nb + jnp.minimum(j, nb - 1)

    def cur(col0, width):
        return pl.BlockSpec((tb, width), lambda b, j, c0=col0 // width: (blk(b, j), c0))

    def prev(col0, width):
        return pl.BlockSpec((BF16_SUBLANES, width),
                            lambda b, j, c0=col0 // width: (jnp.maximum(blk(b, j) * pb - 1, 0), c0))

    def whole(rows, width):
        return pl.BlockSpec((rows, width), lambda b, j: (0, 0))

    in_specs = [cur(C_R, RW_DIM), cur(C_K, RW_DIM), cur(C_V, RW_DIM), cur(C_LORA, LORA_COLS),
                prev(C_R, RW_DIM), prev(C_K, RW_DIM), prev(C_V, RW_DIM), prev(C_LORA, LORA_COLS),
                whole(1, RW_DIM), whole(1, RW_DIM), whole(1, RW_DIM), whole(1, LORA_COLS),
                whole(1, RW_DIM), whole(1, RW_DIM), whole(1, RW_DIM), whole(1, RW_DIM), whole(1, RW_DIM),
                whole(LORA_W, RW_DIM), whole(LORA_A, RW_DIM), whole(GATE_LORA, RW_DIM),
                whole(1, RW_DIM), whole(1, RW_DIM)]
    big = pltpu.VMEM((tb, RW_DIM), BF16)
    per_chunk = pltpu.VMEM((tb // CHUNK, 1, RW_DIM), F32)
    outs = pl.pallas_call(
        functools.partial(_rwkv_kernel, tb=tb, nb=nb, n_side=len(side_weights)),
        grid=(batch, nb + 1),
        in_specs=in_specs + side_specs,
        out_specs=[pl.BlockSpec((tb, RW_DIM), lambda b, j: (b * nb + jnp.maximum(j - 1, 0), 0))] + side_specs,
        out_shape=[jax.ShapeDtypeStruct((m, RW_DIM), BF16)]
        + [jax.ShapeDtypeStruct(w.shape, BF16) for w in side_weights],
        scratch_shapes=[big] * 9 + [per_chunk] * 2 + [pltpu.VMEM((N_PAIRS, PAIR, PAIR), F32)],
        compiler_params=_cparams(("arbitrary", "arbitrary")),
        name="rwkv",
    )(z, z, z, z, z, z, z, z, mu_r, mu_k, mu_v, mu_l, w0, a0, k_k, k_a, r_k, w2p, a2p, g2, gn_w, gn_b,
      *side_weights)
    return outs[0], outs[1:]


def _rwkv_chunk_kernel(at_ref, rt_ref, bt_ref, kt_ref, bg_ref, kg_ref, v_ref, g_ref, bonus_ref,
                       gT_ref, gm_ref, gnw_ref, gnb_ref, o_ref, s_ref, *, chunks, pairs, first,
                       filler=None):
    @pl.when(first)
    def _():
        s_ref[...] = jnp.zeros_like(s_ref)

    T = CHUNK
    lane = lax.broadcasted_iota(jnp.int32, (T, PAIR), 1)
    head0 = lane < RW_HEAD
    ri = lax.broadcasted_iota(jnp.int32, (2 * T, 2 * T), 0)
    ci = lax.broadcasted_iota(jnp.int32, (2 * T, 2 * T), 1)
    same = (ri // T) == (ci // T)
    strict = same & ((ri % T) > (ci % T))
    incl = same & ((ri % T) >= (ci % T))
    eye = jnp.where(ri == ci, 1.0, 0.0)
    own = (ri // T) == (ci // RW_HEAD)

    items = [(p, c) for c in range(chunks) for p in range(pairs)]
    rows = lambda c: slice(c * T, (c + 1) * T)
    cols = lambda p: slice(p * PAIR, (p + 1) * PAIR)

    def stack(ref):
        out = []
        for p, c in items:
            x = ref[rows(c), cols(p)]
            zero = jnp.zeros_like(x)
            out.append(jnp.concatenate([jnp.where(head0, x, zero), jnp.where(head0, zero, x)], axis=0))
        return out

    each = lambda f, *ls: [f(*xs) for xs in zip(*ls)]
    La, Lr, Rb, Rk = stack(at_ref), stack(rt_ref), stack(bt_ref), stack(kt_ref)
    Rbg, Rkg, Vs = stack(bg_ref), stack(kg_ref), stack(v_ref)
    gm_row = [gm_ref[c][:, cols(p)] for p, c in items]
    gT_row = [gT_ref[c][:, cols(p)] for p, c in items]
    gate = [g_ref[rows(c), cols(p)] for p, c in items]
    bonus = [bonus_ref[rows(c), cols(p)] for p, c in items]
    filler = iter(()) if filler is None else filler
    tick = lambda: next(filler, None)

    AA = each(lambda la, lr, rb, rk: _mm_nt(jnp.concatenate([la, lr], axis=0),
                                            jnp.concatenate([rb, rk], axis=0)), La, Lr, Rb, Rk)
    tick()
    N = [jnp.where(strict, aa[:2 * T, :2 * T], 0.0) for aa in AA]
    Aak = [jnp.where(strict, aa[:2 * T, 2 * T:], 0.0) for aa in AA]
    Arb = [jnp.where(incl, aa[2 * T:, :2 * T], 0.0) for aa in AA]
    Ark = [jnp.where(incl, aa[2 * T:, 2 * T:], 0.0) for aa in AA]
    def live_rows(x, t0):
        return x if t0 == 0 else jnp.concatenate([x[t0:T, :], x[T + t0:, :]], axis=0)

    def all_rows(y, t0):
        if t0 == 0:
            return y
        zero = jnp.zeros((t0, y.shape[1]), y.dtype)
        return jnp.concatenate([zero, y[:T - t0, :], zero, y[T - t0:, :]], axis=0)

    def skip(span):
        return span if span % BF16_SUBLANES == 0 else 0

    def blockdiag(a, b):
        zero = jnp.zeros_like(a)
        return jnp.concatenate([jnp.concatenate([a, zero], axis=1),
                                jnp.concatenate([zero, b], axis=1)], axis=0)

    groups = [(i, i + 1) for i in range(0, len(items), 2)]
    W = each(lambda n: eye + n, N)
    P = [n.astype(BF16) for n in N]
    for a, b in groups:
        sq = _mm(jnp.concatenate([P[a], P[b]], axis=1), blockdiag(P[a], P[b]))
        P[a], P[b] = sq[:, :PAIR].astype(BF16), sq[:, PAIR:].astype(BF16)
    tick()
    span = 2
    while span < T:
        last = 2 * span >= T
        t0, t1 = skip(span), skip(2 * span)
        for a, b in groups:
            lhs = jnp.concatenate([live_rows(W[a].astype(BF16), t0), live_rows(W[b].astype(BF16), t0)], axis=1)
            if not last:
                lhs_p = jnp.concatenate([live_rows(P[a], t1), live_rows(P[b], t1)], axis=1)
                lhs = jnp.concatenate([lhs_p, lhs], axis=0)
            res = _mm(lhs, blockdiag(P[a], P[b]))
            if not last:
                sq = all_rows(res[:lhs_p.shape[0], :], t1)
                P[a], P[b] = sq[:, :PAIR].astype(BF16), sq[:, PAIR:].astype(BF16)
                res = res[lhs_p.shape[0]:, :]
            wp = all_rows(res, t0)
            W[a] = W[a] + wp[:, :PAIR]
            W[b] = W[b] + wp[:, PAIR:]
        tick()
        span *= 2
    AkV = each(_mm, Aak, Vs)
    tick()
    X = each(lambda w, la, akv: _mm(w, jnp.concatenate([la, akv.astype(BF16)], axis=1)).astype(BF16),
             W, La, AkV)
    tick()
    Z = each(lambda arb, ark, x, vs:
             _mm(jnp.concatenate([arb.astype(BF16), ark.astype(BF16)], axis=1),
                 jnp.concatenate([x, jnp.concatenate([jnp.zeros_like(vs), vs], axis=1)], axis=0)),
             Arb, Ark, X, Vs)
    tick()
    Q = each(lambda z, lr, gm: (z[:, :PAIR] + lr.astype(F32)) * gm, Z, Lr, gm_row)
    Y0 = [z[:, PAIR:] for z in Z]
    tick()
    Mbd = each(lambda rbg, x, gm: _mm_tn(rbg, x[:, :PAIR]) * gm, Rbg, X, gm_row)
    NcT = each(lambda x, vs, rbg, rkg: _mm_tn(jnp.concatenate([x[:, PAIR:], vs], axis=0),
                                              jnp.concatenate([rbg, rkg], axis=0)), X, Vs, Rbg, Rkg)
    for _ in filler:
        pass

    S = [s_ref[p] for p in range(pairs)]
    for c in range(chunks):
        for p in range(pairs):
            i = c * pairs + p
            Ys = _mm_nt(Q[i], S[p]) + Y0[i]
            S[p] = S[p] * gT_row[i] + _mm_nt(S[p], Mbd[i]) + NcT[i]
            mu = jnp.sum(Ys, axis=-1, keepdims=True) * (1.0 / RW_HEAD)
            d = jnp.where(own, Ys - mu, 0.0)
            var = jnp.sum(d * d, axis=-1, keepdims=True) * (1.0 / RW_HEAD)
            dn = d * lax.rsqrt(var + GN_EPS)
            yn = (dn[:T, :] + dn[T:, :]) * gnw_ref[:, cols(p)] + gnb_ref[:, cols(p)]
            out = (yn + bonus[i].astype(F32)) * gate[i].astype(F32)
            o_ref[rows(c), cols(p)] = out.astype(BF16)
    for p in range(pairs):
        s_ref[p] = S[p]


def _swa_kernel(sink_ref, q_ref, kc_ref, kp_ref, vc_ref, vp_ref, qg_ref, kg_ref, o_ref, *,
                steps_per_seq, slopes, qb):
    first = (pl.program_id(0) % steps_per_seq) == 0
    mean_h = _head_ones(LANES, ATT_HEAD, 1.0 / ATT_HEAD)

    def head_rms(x, gain):
        parts = []
        for b in range(x.shape[1] // LANES):
            xb = x[:, b * LANES:(b + 1) * LANES]
            ms = _mm(xb * xb, mean_h)
            parts.append(xb * lax.rsqrt(ms + RMS_EPS))
        return jnp.concatenate(parts, axis=1) * gain

    q = head_rms(q_ref[...].astype(F32), qg_ref[...]) * (ATT_HEAD ** -0.5 * LOG2E)
    q = q.astype(BF16)
    kcat = jnp.concatenate([kp_ref[...], kc_ref[...]], axis=0).astype(F32)
    kcat = head_rms(kcat, kg_ref[...]).astype(BF16)
    vcat = jnp.concatenate([vp_ref[...], vc_ref[...]], axis=0)

    qi = lax.broadcasted_iota(jnp.int32, (BLOCK, 2 * BLOCK), 0)
    kj = lax.broadcasted_iota(jnp.int32, (BLOCK, 2 * BLOCK), 1)
    dist_i = BLOCK + qi - kj
    in_window = (dist_i >= 0) & (dist_i < WINDOW)
    first_key = jnp.where(first, BLOCK, 0)
    neg_dist_rest = jnp.where(in_window, -dist_i.astype(F32), NEG_BIG)
    neg_dist_first = jnp.where(in_window & (kj >= first_key), -dist_i.astype(F32), NEG_BIG)
    neg_dist = [neg_dist_first] + [neg_dist_rest] * (qb - 1)

    lane_q = lax.broadcasted_iota(jnp.int32, (BLOCK, LANES), 1)
    lane_kv = lax.broadcasted_iota(jnp.int32, (2 * BLOCK, LANES), 1)
    zero_q = jnp.zeros((BLOCK, LANES), BF16)
    one_kv = jnp.ones((2 * BLOCK, LANES), BF16)
    kv_lane_blocks = range(ATT_KV_DIM // LANES)
    k_all = [kcat[:, b * LANES:(b + 1) * LANES] for b in kv_lane_blocks]
    k_all_rolled = [pltpu.roll(kb, ATT_HEAD, 1) for kb in k_all]
    window = lambda x, t: x[t * BLOCK:(t + 2) * BLOCK, :]
    v_aug = [[jnp.where((lane_kv // ATT_HEAD) == (j % 2),
                        window(vcat[:, (j // 2) * LANES:(j // 2 + 1) * LANES], t), one_kv)
              for j in range(ATT_KV_HEADS)] for t in range(qb)]
    items = [(t, h) for t in range(qb) for h in range(ATT_HEADS)]
    kv_of = lambda h: h // ATT_GROUP
    aligned = lambda h: (h % 2) == (kv_of(h) % 2)

    def scores(t, h):
        qh = jnp.where((lane_q // ATT_HEAD) == (h % 2),
                       q[t * BLOCK:(t + 1) * BLOCK, (h // 2) * LANES:(h // 2 + 1) * LANES], zero_q)
        kh = window((k_all if aligned(h) else k_all_rolled)[kv_of(h) // 2], t)
        return _mm_nt(qh, kh) + (slopes[h] * LOG2E) * neg_dist[t]

    s = [scores(t, h) for t, h in items]
    sink = [sink_ref[h] * LOG2E for _, h in items]
    mx = [jnp.maximum(jnp.max(s_, axis=-1, keepdims=True), sk) for s_, sk in zip(s, sink)]
    p = [jnp.exp2(s_ - m_) for s_, m_ in zip(s, mx)]
    pv = [_mm(p_, v_aug[t][kv_of(h)]) for p_, (t, h) in zip(p, items)]
    pv_sw = [pltpu.roll(x, ATT_HEAD, 1) for x in pv]
    out = []
    for i, (t, h) in enumerate(items):
        num, den = (pv[i], pv_sw[i]) if aligned(h) else (pv_sw[i], pv[i])
        out.append(num * (1.0 / (den + jnp.exp2(sink[i] - mx[i]))))
    for t in range(qb):
        o = out[t * ATT_HEADS:(t + 1) * ATT_HEADS]
        o_ref[t * BLOCK:(t + 1) * BLOCK, :] = jnp.concatenate(
            [jnp.where((lane_q // ATT_HEAD) == 0, o[2 * m], o[2 * m + 1]) for m in range(ATT_HEADS // 2)],
            axis=1).astype(BF16)


def _swa(z, sinks, q_gain_t, k_gain_t, seq, qb=8):
    m = z.shape[0]
    rows = qb * BLOCK
    slopes = tuple(float(s) for s in
                   np.exp2(-8.0 * np.arange(1, ATT_HEADS + 1, dtype=np.float32) / ATT_HEADS).astype(np.float32))
    kv_cur = lambda c0: pl.BlockSpec((rows, ATT_KV_DIM), lambda n, c=c0 // ATT_KV_DIM: (n, c))
    kv_prev = lambda c0: pl.BlockSpec((BLOCK, ATT_KV_DIM),
                                      lambda n, c=c0 // ATT_KV_DIM: (jnp.maximum(n * qb - 1, 0), c))
    return pl.pallas_call(
        functools.partial(_swa_kernel, steps_per_seq=seq // rows, slopes=slopes, qb=qb),
        grid=(m // rows,),
        in_specs=[pl.BlockSpec(memory_space=pltpu.SMEM),
                  pl.BlockSpec((rows, ATT_Q_DIM), lambda n: (n, C_Q // ATT_Q_DIM)),
                  kv_cur(C_KA), kv_prev(C_KA), kv_cur(C_VA), kv_prev(C_VA),
                  pl.BlockSpec((1, ATT_Q_DIM), lambda n: (0, 0)),
                  pl.BlockSpec((1, ATT_KV_DIM), lambda n: (0, 0))],
        out_specs=pl.BlockSpec((rows, ATT_Q_DIM), lambda n: (n, 0)),
        out_shape=jax.ShapeDtypeStruct((m, ATT_Q_DIM), BF16),
        compiler_params=_cparams(("parallel",)),
        name="swa",
    )(sinks, z, z, z, z, z, q_gain_t, k_gain_t)


def _mix_out_kernel(orw_ref, oatt_ref, wb1_ref, wb2_ref, zg_ref, wo_ref, x_ref, g_ref, x1_ref, h2_ref):
    p_rw = jnp.dot(orw_ref[...], wb1_ref[...], preferred_element_type=F32)
    p_att = jnp.dot(oatt_ref[...], wb2_ref[...], preferred_element_type=F32)
    g_rw = jax.nn.sigmoid(zg_ref[:, :D_MODEL].astype(F32))
    g_att = jax.nn.sigmoid(zg_ref[:, D_MODEL:].astype(F32))
    mix = (g_rw * p_rw + g_att * p_att).astype(BF16)
    x1 = x_ref[...] + jnp.dot(mix, wo_ref[...], preferred_element_type=F32)
    x1_ref[...] = x1
    y = x1 * lax.rsqrt(jnp.mean(x1 * x1, axis=-1, keepdims=True) + RMS_EPS)
    h2_ref[...] = (y * g_ref[...]).astype(BF16)


def _mix_out(o_rw, o_att, w_branch_b, z, w_out_b, x2, g2, tm=512):
    m = x2.shape[0]
    const = lambda rows, cols, r0=0: pl.BlockSpec((rows, cols), lambda i: (r0, 0),
                                                  pipeline_mode=pl.Buffered(1))
    rows = lambda cols, c0=0: pl.BlockSpec((tm, cols), lambda i: (i, c0))
    return pl.pallas_call(
        _mix_out_kernel,
        grid=(m // tm,),
        in_specs=[rows(RW_DIM), rows(ATT_Q_DIM), const(RW_DIM, D_MODEL), const(ATT_Q_DIM, D_MODEL, 1),
                  rows(2 * D_MODEL, C_GRW // (2 * D_MODEL)), const(D_MODEL, D_MODEL),
                  rows(D_MODEL), const(1, D_MODEL)],
        out_specs=[rows(D_MODEL), rows(D_MODEL)],
        out_shape=[jax.ShapeDtypeStruct((m, D_MODEL), F32),
                   jax.ShapeDtypeStruct((m, D_MODEL), BF16)],
        compiler_params=_cparams(("parallel",)),
        name="mix_out",
    )(o_rw, o_att, w_branch_b, w_branch_b, z, w_out_b, x2, g2)


def _ffn_up_kernel(h_ref, wvb_ref, wgb_ref, cwv_ref, cwg_ref, cbv_ref, cbg_ref, wd_ref, a_ref, wdb_ref,
                   cv_ref, cg_ref, *, tm, sub, tiles_per_seq):
    halo = 8
    wdb_ref[...] = wd_ref[...].astype(BF16)

    @pl.when((pl.program_id(1) % tiles_per_seq) == 0)
    def _():
        cv_ref[...] = jnp.zeros_like(cv_ref)
        cg_ref[...] = jnp.zeros_like(cg_ref)

    row = lax.broadcasted_iota(jnp.int32, (halo, a_ref.shape[1]), 0)

    def conv(u, carry_ref, cw_ref, cb_ref):
        p1 = carry_ref[halo - 1:halo, :]
        p2 = carry_ref[halo - 2:halo - 1, :]
        r1 = pltpu.roll(u, 1, 0)
        r2 = pltpu.roll(u, 2, 0)
        prev1 = jnp.concatenate([jnp.where(row == 0, p1, r1[:halo, :]), r1[halo:, :]], axis=0)
        prev2 = jnp.concatenate(
            [jnp.where(row == 0, p2, jnp.where(row == 1, p1, r2[:halo, :])), r2[halo:, :]], axis=0)
        carry_ref[...] = u[sub - halo:sub, :]
        w = cw_ref[...].astype(BF16)
        return (prev2.astype(BF16) * w[0:1, :] + prev1.astype(BF16) * w[1:2, :] + u.astype(BF16) * w[2:3, :]
                + cb_ref[...].astype(BF16))

    for s in range(tm // sub):
        h = h_ref[s * sub:(s + 1) * sub, :]
        val = conv(jnp.dot(h, wvb_ref[...], preferred_element_type=F32), cv_ref, cwv_ref, cbv_ref)
        gate = conv(jnp.dot(h, wgb_ref[...], preferred_element_type=F32), cg_ref, cwg_ref, cbg_ref)
        a_ref[s * sub:(s + 1) * sub, :] = (gate * jax.nn.sigmoid(gate) * val).astype(BF16)


def _ffn_up(h2, w_up_b, conv_w, conv_b, w_down, seq, tm=2048, tn=512, sub=256):
    m = h2.shape[0]
    nj = D_FF // tn
    ni = m // tm
    wd_rows = w_down.shape[0] // (nj * ni)
    wd_spec = pl.BlockSpec((wd_rows, D_MODEL), lambda j, i: (j * ni + i, 0))
    return pl.pallas_call(
        functools.partial(_ffn_up_kernel, tm=tm, sub=sub, tiles_per_seq=seq // tm),
        grid=(nj, ni),
        in_specs=[pl.BlockSpec((tm, D_MODEL), lambda j, i: (i, 0)),
                  pl.BlockSpec((D_MODEL, tn), lambda j, i: (0, j)),
                  pl.BlockSpec((D_MODEL, tn), lambda j, i: (0, nj + j)),
                  pl.BlockSpec((3, tn), lambda j, i: (0, j)),
                  pl.BlockSpec((3, tn), lambda j, i: (0, nj + j)),
                  pl.BlockSpec((1, tn), lambda j, i: (0, j)),
                  pl.BlockSpec((1, tn), lambda j, i: (0, nj + j)),
                  wd_spec],
        out_specs=[pl.BlockSpec((tm, tn), lambda j, i: (i, j)), wd_spec],
        out_shape=[jax.ShapeDtypeStruct((m, D_FF), BF16), jax.ShapeDtypeStruct(w_down.shape, BF16)],
        scratch_shapes=[pltpu.VMEM((8, tn), F32), pltpu.VMEM((8, tn), F32)],
        compiler_params=_cparams(("parallel", "arbitrary")),
        name="ffn_up",
    )(h2, w_up_b, w_up_b, conv_w, conv_w, conv_b, conv_b, w_down)


def _ffn_down_kernel(a_ref, w_ref, x1_ref, o_ref):
    o_ref[...] = x1_ref[...] + jnp.dot(a_ref[...], w_ref[...], preferred_element_type=F32)


def _ffn_down(act, w_down_b, x1, tm=512, tn=1024):
    m = act.shape[0]
    return pl.pallas_call(
        _ffn_down_kernel,
        grid=(D_MODEL // tn, m // tm),
        in_specs=[pl.BlockSpec((tm, D_FF), lambda j, i: (i, 0)),
                  pl.BlockSpec((D_FF, tn), lambda j, i: (0, j)),
                  pl.BlockSpec((tm, tn), lambda j, i: (i, j))],
        out_specs=pl.BlockSpec((tm, tn), lambda j, i: (i, j)),
        out_shape=jax.ShapeDtypeStruct((m, D_MODEL), F32),
        compiler_params=_cparams(("parallel", "arbitrary")),
        name="ffn_down",
    )(act, w_down_b, x1)


def _pad_cols(w, n):
    return jnp.pad(w, ((0, 0), (0, n - w.shape[1])))


def _pad_rows(w, n):
    return jnp.pad(w, ((0, n - w.shape[0]), (0, 0)))


def _layer(x2, batch, seq, w_in_p, norm1_g, rw_mu, rw_w0, rw_w2, rw_a0, rw_a2, rw_g2, rw_k_k, rw_k_a,
           rw_r_k, rw_gn_w, rw_gn_b, q_norm_g, k_norm_g, attn_sinks, w_branch, w_out,
           norm2_g, w_up, conv_w, conv_b, w_down):
    row = lambda v: v.reshape(1, -1).astype(F32)
    o3 = 3 * RW_DIM
    o4 = o3 + DECAY_LORA
    o5 = o4 + ICLR_LORA
    o6 = o5 + GATE_LORA
    mu = rw_mu.reshape(1, -1)
    mu_l = jnp.concatenate([_pad_cols(mu[:, o3:o4], LORA_W), _pad_cols(mu[:, o4:o5], LORA_A),
                            mu[:, o5:o6]], axis=1)
    w2p = _pad_rows(rw_w2, LORA_W).astype(BF16)
    a2p = _pad_rows(rw_a2, LORA_A).astype(BF16)

    z, _ = _in_proj(x2, row(norm1_g), w_in_p, ())
    o_rw, (w_branch_b, w_out_b, w_up_b) = _rwkv(
        z, batch, seq, mu[:, :RW_DIM], mu[:, RW_DIM:2 * RW_DIM], mu[:, 2 * RW_DIM:o3], mu_l,
        row(rw_w0), row(rw_a0), row(rw_k_k), row(rw_k_a), row(rw_r_k),
        w2p, a2p, rw_g2.astype(BF16), row(rw_gn_w), row(rw_gn_b), (w_branch, w_out, w_up))
    o_att = _swa(z, attn_sinks.astype(F32), jnp.tile(row(q_norm_g), (1, ATT_HEADS)),
                 jnp.tile(row(k_norm_g), (1, ATT_KV_HEADS)), seq)
    x1, h2 = _mix_out(o_rw, o_att, w_branch_b, z, w_out_b, x2, row(norm2_g))
    act, w_down_b = _ffn_up(h2, w_up_b, conv_w.astype(F32), row(conv_b), w_down, seq)
    return _ffn_down(act, w_down_b, x1)


def kernel(x, norm1_g, w_in, rw_mu, rw_w0, rw_w2, rw_a0, rw_a2, rw_g2, rw_k_k, rw_k_a, rw_r_k,
           rw_gn_w, rw_gn_b, q_norm_g, k_norm_g, attn_sinks, w_branch, w_out, norm2_g, w_up,
           conv_w, conv_b, w_down):
    batch, seq, d = x.shape
    x2 = x.reshape(batch * seq, d)
    params = (norm1_g, rw_mu, rw_w0, rw_w2, rw_a0, rw_a2, rw_g2, rw_k_k, rw_k_a, rw_r_k,
              rw_gn_w, rw_gn_b, q_norm_g, k_norm_g, attn_sinks, w_branch, w_out, norm2_g, w_up,
              conv_w, conv_b, w_down)
    for layer in range(norm1_g.shape[0]):
        x2 = _layer(x2, batch, seq, _w_in_layout(w_in, layer), *(p[layer] for p in params))
    return x2.reshape(batch, seq, d)
```

```python
import functools

import jax
import jax.numpy as jnp
import numpy as np
from jax import lax
from jax.experimental import pallas as pl
from jax.experimental.pallas import tpu as pltpu

F32 = jnp.float32
BF16 = jnp.bfloat16

LANES = 128
BF16_SUBLANES = 16
VMEM_LIMIT = 56 * 1024 * 1024

D_MODEL = 2048
RW_HEADS = 16
RW_HEAD = 64
RW_DIM = RW_HEADS * RW_HEAD
DECAY_LORA = 96
ICLR_LORA = 96
GATE_LORA = 256
GN_EPS = 64e-5
ATT_HEADS = 16
ATT_KV_HEADS = 4
ATT_GROUP = ATT_HEADS // ATT_KV_HEADS
ATT_HEAD = 64
ATT_Q_DIM = ATT_HEADS * ATT_HEAD
ATT_KV_DIM = ATT_KV_HEADS * ATT_HEAD
WINDOW = 128
BLOCK = 128
D_FF = 5632
RMS_EPS = 1e-6
NEG_BIG = -1e30
LOG2E = 1.4426950408889634

C_R = 0
C_K = RW_DIM
C_V = 2 * RW_DIM
C_Q = 3 * RW_DIM
C_GRW = C_Q + ATT_Q_DIM
C_GATT = C_GRW + D_MODEL
C_LORA = C_GATT + D_MODEL
LORA_W = 128
LORA_A = 128
LORA_COLS = LORA_W + LORA_A + GATE_LORA
C_KA = C_LORA + LORA_COLS
C_VA = C_KA + ATT_KV_DIM
Z_COLS = C_VA + ATT_KV_DIM

CHUNK = 64
PAIR = 2 * RW_HEAD
N_PAIRS = RW_DIM // PAIR


def _cparams(sem):
    return pltpu.CompilerParams(dimension_semantics=sem, vmem_limit_bytes=VMEM_LIMIT)


def _mm(a, b):
    return jnp.dot(a.astype(BF16), b.astype(BF16), preferred_element_type=F32)


def _mm_nt(a, b):
    return lax.dot_general(a.astype(BF16), b.astype(BF16), (((1,), (1,)), ((), ())),
                           preferred_element_type=F32)


def _mm_tn(a, b):
    return lax.dot_general(a.astype(BF16), b.astype(BF16), (((0,), (0,)), ((), ())),
                           preferred_element_type=F32)


def _mm_split(m01, x):
    hi = x.astype(BF16)
    lo = (x - hi.astype(F32)).astype(BF16)
    return (jnp.dot(m01, hi, preferred_element_type=F32)
            + jnp.dot(m01, lo, preferred_element_type=F32))


def _head_ones(n, head, scale):
    r = lax.broadcasted_iota(jnp.int32, (n, n), 0) // head
    c = lax.broadcasted_iota(jnp.int32, (n, n), 1) // head
    return jnp.where(r == c, scale, 0.0).astype(BF16)


def _w_in_layout_kernel(wt_ref, o_ref, *, moves, pad):
    o_ref[pad[0]:pad[1], :] = jnp.zeros((pad[1] - pad[0], o_ref.shape[1]), BF16)
    for dst, src, width in moves:
        o_ref[dst:dst + width, :] = wt_ref[src:src + width, :].astype(BF16)


def _w_in_layout(w_in_layers, layer, cols=256):
    _, d, n = w_in_layers.shape
    wt_layers = jnp.swapaxes(w_in_layers, 1, 2)
    o3 = 3 * RW_DIM
    o4 = o3 + DECAY_LORA
    o5 = o4 + ICLR_LORA
    o6 = o5 + GATE_LORA
    oq = o6 + ATT_Q_DIM
    ok = oq + ATT_KV_DIM
    ov = ok + ATT_KV_DIM
    moves = ((C_R, 0, o3), (C_Q, o6, ATT_Q_DIM), (C_GRW, ov, 2 * D_MODEL),
             (C_LORA, o3, DECAY_LORA), (C_LORA + LORA_W, o4, ICLR_LORA),
             (C_LORA + LORA_W + LORA_A, o5, GATE_LORA), (C_KA, oq, ATT_KV_DIM), (C_VA, ok, ATT_KV_DIM))
    return pl.pallas_call(
        functools.partial(_w_in_layout_kernel, moves=moves, pad=(C_LORA, C_LORA + LORA_W + LORA_A)),
        grid=(d // cols,),
        in_specs=[pl.BlockSpec((None, n, cols), lambda i: (layer, 0, i))],
        out_specs=pl.BlockSpec((Z_COLS, cols), lambda i: (0, i)),
        out_shape=jax.ShapeDtypeStruct((Z_COLS, d), BF16),
        compiler_params=_cparams(("parallel",)),
        name="w_in_layout",
    )(wt_layers)


def _in_proj_kernel(x_ref, g_ref, wt_ref, *refs, n_side):
    side_in, z_ref, side_out, h_ref = refs[:n_side], refs[n_side], refs[n_side + 1:-1], refs[-1]

    @pl.when(pl.program_id(1) == 0)
    def _():
        xf = x_ref[...]
        y = xf * lax.rsqrt(jnp.mean(xf * xf, axis=-1, keepdims=True) + RMS_EPS)
        h_ref[...] = (y * g_ref[...]).astype(BF16)

    z_ref[...] = _mm_nt(h_ref[...], wt_ref[...]).astype(BF16)
    for w_ref, wb_ref in zip(side_in, side_out):
        wb_ref[...] = w_ref[...].astype(BF16)


def _in_proj(x2, g1, w_in_t, side_weights, tm=1024, tn=1536, side_rows=64):
    m, d = x2.shape
    n = w_in_t.shape[0]
    nj = n // tn
    for w in side_weights:
        assert w.shape[0] // side_rows <= (m // tm) * nj, "not enough grid steps to cast this weight"
    side_specs = [pl.BlockSpec((side_rows, w.shape[1]),
                               lambda i, j, last=w.shape[0] // side_rows - 1: (jnp.minimum(i * nj + j, last), 0))
                  for w in side_weights]
    outs = pl.pallas_call(
        functools.partial(_in_proj_kernel, n_side=len(side_weights)),
        grid=(m // tm, nj),
        in_specs=[pl.BlockSpec((tm, d), lambda i, j: (i, 0)),
                  pl.BlockSpec((1, d), lambda i, j: (0, 0)),
                  pl.BlockSpec((tn, d), lambda i, j: (j, 0))] + side_specs,
        out_specs=[pl.BlockSpec((tm, tn), lambda i, j: (i, j))] + side_specs,
        out_shape=[jax.ShapeDtypeStruct((m, n), BF16)]
        + [jax.ShapeDtypeStruct(w.shape, BF16) for w in side_weights],
        scratch_shapes=[pltpu.VMEM((tm, d), BF16)],
        compiler_params=_cparams(("arbitrary", "arbitrary")),
        name="in_proj",
    )(x2, g1, w_in_t, *side_weights)
    return outs[0], outs[1:]


def _rwkv_prep_kernel(zr_ref, zk_ref, zv_ref, zl_ref, pr_ref, pk_ref, pv_ref, pli_ref,
                      mur_ref, muk_ref, muv_ref, mul_ref,
                      w0_ref, a0_ref, kk_ref, ka_ref, rk_ref, w2_ref, a2_ref, g2_ref,
                      at_ref, rt_ref, bt_ref, kt_ref, bg_ref, kg_ref, v_ref, g_ref, bonus_ref,
                      gT_ref, gm_ref, *, tm, first):
    keep = jnp.where(first, 0.0, 1.0)
    nc = tm // CHUNK
    last = slice(BF16_SUBLANES - 1, BF16_SUBLANES)

    def shifted(z, prev_last, mu):
        z = z.astype(F32)
        row = lax.broadcasted_iota(jnp.int32, z.shape, 0)
        zprev = jnp.where(row == 0, prev_last.astype(F32) * keep, pltpu.roll(z, 1, 0))
        return z + (zprev - z) * mu

    lo = shifted(zl_ref[...], pli_ref[last, :], mul_ref[...])
    tanh_wd = jnp.tanh(lo[:, :LORA_W]).astype(BF16)
    ad = lo[:, LORA_W:LORA_W + LORA_A].astype(BF16)
    sig_gd = jax.nn.sigmoid(lo[:, LORA_W + LORA_A:]).astype(BF16)

    ti = lax.broadcasted_iota(jnp.int32, (tm, tm), 0)
    si = lax.broadcasted_iota(jnp.int32, (tm, tm), 1)
    tri = jnp.where(((ti // CHUNK) == (si // CHUNK)) & (si <= ti), 1.0, 0.0).astype(BF16)
    ones_h = _head_ones(PAIR, RW_HEAD, 1.0)
    w_lora = jnp.dot(tanh_wd, w2_ref[...], preferred_element_type=F32)
    a_lora = jnp.dot(ad, a2_ref[...], preferred_element_type=F32)
    g_ref[...] = jnp.dot(sig_gd, g2_ref[...], preferred_element_type=F32).astype(BF16)
    yield

    stash = []
    for p in range(N_PAIRS):
        cs = slice(p * PAIR, (p + 1) * PAIR)
        r = shifted(zr_ref[:, cs], pr_ref[last, cs], mur_ref[:, cs])
        k = shifted(zk_ref[:, cs], pk_ref[last, cs], muk_ref[:, cs])
        v = shifted(zv_ref[:, cs], pv_ref[last, cs], muv_ref[:, cs])

        wpre = w0_ref[:, cs] + w_lora[:, cs]
        w = jnp.minimum(wpre, 0.0) - jnp.log1p(jnp.exp(-jnp.abs(wpre))) - 0.5
        lw = -jnp.exp(w)
        a = jax.nn.sigmoid(a0_ref[:, cs] + a_lora[:, cs])

        kk = k * kk_ref[:, cs]
        kk = kk * lax.rsqrt(jnp.maximum(_mm(kk * kk, ones_h), 1e-24))
        k2 = k * (1.0 + (a - 1.0) * ka_ref[:, cs])
        bonus_ref[:, cs] = (_mm(r * k2 * rk_ref[:, cs], ones_h) * v).astype(BF16)
        v_ref[:, cs] = v.astype(BF16)
        stash.append((r, k2, -kk, kk * a, lw))
        yield

    c_all = _mm_split(tri, jnp.concatenate([st[4] for st in stash], axis=1))
    yield

    for p in range(N_PAIRS):
        cs = slice(p * PAIR, (p + 1) * PAIR)
        r, k2, ka_vec, kb_vec, lw = stash[p]
        c = c_all[:, cs]
        c3 = c.reshape(nc, CHUNK, PAIR)
        c_mid = c3[:, CHUNK // 2 - 1:CHUNK // 2, :]
        c_end = c3[:, CHUNK - 1:CHUNK, :]
        cm = jnp.broadcast_to(c_mid, c3.shape).reshape(tm, PAIR)
        cT = jnp.broadcast_to(c_end, c3.shape).reshape(tm, PAIR)

        e_in = jnp.exp(c - cm)
        e_out = jnp.exp(cm - c)
        e_end = jnp.exp(cT - c)
        at_ref[:, cs] = (ka_vec * jnp.exp(c - lw - cm)).astype(BF16)
        rt_ref[:, cs] = (r * e_in).astype(BF16)
        bt_ref[:, cs] = (kb_vec * e_out).astype(BF16)
        kt_ref[:, cs] = (k2 * e_out).astype(BF16)
        bg_ref[:, cs] = (kb_vec * e_end).astype(BF16)
        kg_ref[:, cs] = (k2 * e_end).astype(BF16)
        gT_ref[:, :, cs] = jnp.exp(c_end)
        gm_ref[:, :, cs] = jnp.exp(c_mid)
        yield


def _rwkv_kernel(*refs, tb, nb, n_side):
    prep_in, (gnw_ref, gnb_ref), refs = refs[:20], refs[20:22], refs[22:]
    side_in, o_ref, side_out, scratch = refs[:n_side], refs[n_side], refs[n_side + 1:2 * n_side + 1], refs[2 * n_side + 1:]
    operands, s_ref = scratch[:11], scratch[11]
    j = pl.program_id(1)
    prep = functools.partial(_rwkv_prep_kernel, *prep_in, *operands, tm=tb, first=j == 0)
    chunk = functools.partial(_rwkv_chunk_kernel, *operands, gnw_ref, gnb_ref, o_ref, s_ref,
                              chunks=tb // CHUNK, pairs=N_PAIRS, first=j == 1)

    def side_cast():
        for w_ref, wb_ref in zip(side_in, side_out):
            wb_ref[...] = w_ref[...].astype(BF16)

    @pl.when(j == 0)
    def _():
        side_cast()
        for _ in prep():
            pass

    @pl.when((j > 0) & (j < nb))
    def _():
        side_cast()
        chunk(filler=prep())

    @pl.when(j == nb)
    def _():
        side_cast()
        chunk()


def _rwkv(z, batch, seq, mu_r, mu_k, mu_v, mu_l, w0, a0, k_k, k_a, r_k, w2p, a2p, g2, gn_w, gn_b,
          side_weights, tb=128, side_rows=32):
    m = z.shape[0]
    nb = seq // tb
    for w in side_weights:
        assert w.shape[0] // side_rows <= batch * (nb + 1), "not enough grid steps to cast this weight"
    side_specs = [pl.BlockSpec((side_rows, w.shape[1]),
                               lambda b, j, last=w.shape[0] // side_rows - 1:
                               (jnp.minimum(b * (nb + 1) + j, last), 0)) for w in side_weights]
    pb = tb // BF16_SUBLANES
    blk = lambda b, j: b * nb + jnp.minimum(j, nb - 1)

    def cur(col0, width):
        return pl.BlockSpec((tb, width), lambda b, j, c0=col0 // width: (blk(b, j), c0))

    def prev(col0, width):
        return pl.BlockSpec((BF16_SUBLANES, width),
                            lambda b, j, c0=col0 // width: (jnp.maximum(blk(b, j) * pb - 1, 0), c0))

    def whole(rows, width):
        return pl.BlockSpec((rows, width), lambda b, j: (0, 0))

    in_specs = [cur(C_R, RW_DIM), cur(C_K, RW_DIM), cur(C_V, RW_DIM), cur(C_LORA, LORA_COLS),
                prev(C_R, RW_DIM), prev(C_K, RW_DIM), prev(C_V, RW_DIM), prev(C_LORA, LORA_COLS),
                whole(1, RW_DIM), whole(1, RW_DIM), whole(1, RW_DIM), whole(1, LORA_COLS),
                whole(1, RW_DIM), whole(1, RW_DIM), whole(1, RW_DIM), whole(1, RW_DIM), whole(1, RW_DIM),
                whole(LORA_W, RW_DIM), whole(LORA_A, RW_DIM), whole(GATE_LORA, RW_DIM),
                whole(1, RW_DIM), whole(1, RW_DIM)]
    big = pltpu.VMEM((tb, RW_DIM), BF16)
    per_chunk = pltpu.VMEM((tb // CHUNK, 1, RW_DIM), F32)
    outs = pl.pallas_call(
        functools.partial(_rwkv_kernel, tb=tb, nb=nb, n_side=len(side_weights)),
        grid=(batch, nb + 1),
        in_specs=in_specs + side_specs,
        out_specs=[pl.BlockSpec((tb, RW_DIM), lambda b, j: (b * nb + jnp.maximum(j - 1, 0), 0))] + side_specs,
        out_shape=[jax.ShapeDtypeStruct((m, RW_DIM), BF16)]
        + [jax.ShapeDtypeStruct(w.shape, BF16) for w in side_weights],
        scratch_shapes=[big] * 9 + [per_chunk] * 2 + [pltpu.VMEM((N_PAIRS, PAIR, PAIR), F32)],
        compiler_params=_cparams(("arbitrary", "arbitrary")),
        name="rwkv",
    )(z, z, z, z, z, z, z, z, mu_r, mu_k, mu_v, mu_l, w0, a0, k_k, k_a, r_k, w2p, a2p, g2, gn_w, gn_b,
      *side_weights)
    return outs[0], outs[1:]


def _rwkv_chunk_kernel(at_ref, rt_ref, bt_ref, kt_ref, bg_ref, kg_ref, v_ref, g_ref, bonus_ref,
                       gT_ref, gm_ref, gnw_ref, gnb_ref, o_ref, s_ref, *, chunks, pairs, first,
                       filler=None):
    @pl.when(first)
    def _():
        s_ref[...] = jnp.zeros_like(s_ref)

    T = CHUNK
    lane = lax.broadcasted_iota(jnp.int32, (T, PAIR), 1)
    head0 = lane < RW_HEAD
    ri = lax.broadcasted_iota(jnp.int32, (2 * T, 2 * T), 0)
    ci = lax.broadcasted_iota(jnp.int32, (2 * T, 2 * T), 1)
    same = (ri // T) == (ci // T)
    strict = same & ((ri % T) > (ci % T))
    incl = same & ((ri % T) >= (ci % T))
    eye = jnp.where(ri == ci, 1.0, 0.0)
    own = (ri // T) == (ci // RW_HEAD)

    items = [(p, c) for c in range(chunks) for p in range(pairs)]
    rows = lambda c: slice(c * T, (c + 1) * T)
    cols = lambda p: slice(p * PAIR, (p + 1) * PAIR)

    def stack(ref):
        out = []
        for p, c in items:
            x = ref[rows(c), cols(p)]
            zero = jnp.zeros_like(x)
            out.append(jnp.concatenate([jnp.where(head0, x, zero), jnp.where(head0, zero, x)], axis=0))
        return out

    each = lambda f, *ls: [f(*xs) for xs in zip(*ls)]
    La, Lr, Rb, Rk = stack(at_ref), stack(rt_ref), stack(bt_ref), stack(kt_ref)
    Rbg, Rkg, Vs = stack(bg_ref), stack(kg_ref), stack(v_ref)
    gm_row = [gm_ref[c][:, cols(p)] for p, c in items]
    gT_row = [gT_ref[c][:, cols(p)] for p, c in items]
    gate = [g_ref[rows(c), cols(p)] for p, c in items]
    bonus = [bonus_ref[rows(c), cols(p)] for p, c in items]
    filler = iter(()) if filler is None else filler
    tick = lambda: next(filler, None)

    AA = each(lambda la, lr, rb, rk: _mm_nt(jnp.concatenate([la, lr], axis=0),
                                            jnp.concatenate([rb, rk], axis=0)), La, Lr, Rb, Rk)
    tick()
    N = [jnp.where(strict, aa[:2 * T, :2 * T], 0.0) for aa in AA]
    Aak = [jnp.where(strict, aa[:2 * T, 2 * T:], 0.0) for aa in AA]
    Arb = [jnp.where(incl, aa[2 * T:, :2 * T], 0.0) for aa in AA]
    Ark = [jnp.where(incl, aa[2 * T:, 2 * T:], 0.0) for aa in AA]
    def live_rows(x, t0):
        return x if t0 == 0 else jnp.concatenate([x[t0:T, :], x[T + t0:, :]], axis=0)

    def all_rows(y, t0):
        if t0 == 0:
            return y
        zero = jnp.zeros((t0, y.shape[1]), y.dtype)
        return jnp.concatenate([zero, y[:T - t0, :], zero, y[T - t0:, :]], axis=0)

    def skip(span):
        return span if span % BF16_SUBLANES == 0 else 0

    def blockdiag(a, b):
        zero = jnp.zeros_like(a)
        return jnp.concatenate([jnp.concatenate([a, zero], axis=1),
                                jnp.concatenate([zero, b], axis=1)], axis=0)

    groups = [(i, i + 1) for i in range(0, len(items), 2)]
    W = each(lambda n: eye + n, N)
    P = [n.astype(BF16) for n in N]
    for a, b in groups:
        sq = _mm(jnp.concatenate([P[a], P[b]], axis=1), blockdiag(P[a], P[b]))
        P[a], P[b] = sq[:, :PAIR].astype(BF16), sq[:, PAIR:].astype(BF16)
    tick()
    span = 2
    while span < T:
        last = 2 * span >= T
        t0, t1 = skip(span), skip(2 * span)
        for a, b in groups:
            lhs = jnp.concatenate([live_rows(W[a].astype(BF16), t0), live_rows(W[b].astype(BF16), t0)], axis=1)
            if not last:
                lhs_p = jnp.concatenate([live_rows(P[a], t1), live_rows(P[b], t1)], axis=1)
                lhs = jnp.concatenate([lhs_p, lhs], axis=0)
            res = _mm(lhs, blockdiag(P[a], P[b]))
            if not last:
                sq = all_rows(res[:lhs_p.shape[0], :], t1)
                P[a], P[b] = sq[:, :PAIR].astype(BF16), sq[:, PAIR:].astype(BF16)
                res = res[lhs_p.shape[0]:, :]
            wp = all_rows(res, t0)
            W[a] = W[a] + wp[:, :PAIR]
            W[b] = W[b] + wp[:, PAIR:]
        tick()
        span *= 2
    AkV = each(_mm, Aak, Vs)
    tick()
    X = each(lambda w, la, akv: _mm(w, jnp.concatenate([la, akv.astype(BF16)], axis=1)).astype(BF16),
             W, La, AkV)
    tick()
    Z = each(lambda arb, ark, x, vs:
             _mm(jnp.concatenate([arb.astype(BF16), ark.astype(BF16)], axis=1),
                 jnp.concatenate([x, jnp.concatenate([jnp.zeros_like(vs), vs], axis=1)], axis=0)),
             Arb, Ark, X, Vs)
    tick()
    Q = each(lambda z, lr, gm: (z[:, :PAIR] + lr.astype(F32)) * gm, Z, Lr, gm_row)
    Y0 = [z[:, PAIR:] for z in Z]
    tick()
    Mbd = each(lambda rbg, x, gm: _mm_tn(rbg, x[:, :PAIR]) * gm, Rbg, X, gm_row)
    NcT = each(lambda x, vs, rbg, rkg: _mm_tn(jnp.concatenate([x[:, PAIR:], vs], axis=0),
                                              jnp.concatenate([rbg, rkg], axis=0)), X, Vs, Rbg, Rkg)
    for _ in filler:
        pass

    S = [s_ref[p] for p in range(pairs)]
    for c in range(chunks):
        for p in range(pairs):
            i = c * pairs + p
            Ys = _mm_nt(Q[i], S[p]) + Y0[i]
            S[p] = S[p] * gT_row[i] + _mm_nt(S[p], Mbd[i]) + NcT[i]
            mu = jnp.sum(Ys, axis=-1, keepdims=True) * (1.0 / RW_HEAD)
            d = jnp.where(own, Ys - mu, 0.0)
            var = jnp.sum(d * d, axis=-1, keepdims=True) * (1.0 / RW_HEAD)
            dn = d * lax.rsqrt(var + GN_EPS)
            yn = (dn[:T, :] + dn[T:, :]) * gnw_ref[:, cols(p)] + gnb_ref[:, cols(p)]
            out = (yn + bonus[i].astype(F32)) * gate[i].astype(F32)
            o_ref[rows(c), cols(p)] = out.astype(BF16)
    for p in range(pairs):
        s_ref[p] = S[p]


def _swa_kernel(sink_ref, q_ref, kc_ref, kp_ref, vc_ref, vp_ref, qg_ref, kg_ref, o_ref, *,
                steps_per_seq, slopes, qb):
    first = (pl.program_id(0) % steps_per_seq) == 0
    mean_h = _head_ones(LANES, ATT_HEAD, 1.0 / ATT_HEAD)

    def head_rms(x, gain):
        parts = []
        for b in range(x.shape[1] // LANES):
            xb = x[:, b * LANES:(b + 1) * LANES]
            ms = _mm(xb * xb, mean_h)
            parts.append(xb * lax.rsqrt(ms + RMS_EPS))
        return jnp.concatenate(parts, axis=1) * gain

    q = head_rms(q_ref[...].astype(F32), qg_ref[...]) * (ATT_HEAD ** -0.5 * LOG2E)
    q = q.astype(BF16)
    kcat = jnp.concatenate([kp_ref[...], kc_ref[...]], axis=0).astype(F32)
    kcat = head_rms(kcat, kg_ref[...]).astype(BF16)
    vcat = jnp.concatenate([vp_ref[...], vc_ref[...]], axis=0)

    qi = lax.broadcasted_iota(jnp.int32, (BLOCK, 2 * BLOCK), 0)
    kj = lax.broadcasted_iota(jnp.int32, (BLOCK, 2 * BLOCK), 1)
    dist_i = BLOCK + qi - kj
    in_window = (dist_i >= 0) & (dist_i < WINDOW)
    first_key = jnp.where(first, BLOCK, 0)
    neg_dist_rest = jnp.where(in_window, -dist_i.astype(F32), NEG_BIG)
    neg_dist_first = jnp.where(in_window & (kj >= first_key), -dist_i.astype(F32), NEG_BIG)
    neg_dist = [neg_dist_first] + [neg_dist_rest] * (qb - 1)

    lane_q = lax.broadcasted_iota(jnp.int32, (BLOCK, LANES), 1)
    lane_kv = lax.broadcasted_iota(jnp.int32, (2 * BLOCK, LANES), 1)
    zero_q = jnp.zeros((BLOCK, LANES), BF16)
    one_kv = jnp.ones((2 * BLOCK, LANES), BF16)
    kv_lane_blocks = range(ATT_KV_DIM // LANES)
    k_all = [kcat[:, b * LANES:(b + 1) * LANES] for b in kv_lane_blocks]
    k_all_rolled = [pltpu.roll(kb, ATT_HEAD, 1) for kb in k_all]
    window = lambda x, t: x[t * BLOCK:(t + 2) * BLOCK, :]
    v_aug = [[jnp.where((lane_kv // ATT_HEAD) == (j % 2),
                        window(vcat[:, (j // 2) * LANES:(j // 2 + 1) * LANES], t), one_kv)
              for j in range(ATT_KV_HEADS)] for t in range(qb)]
    items = [(t, h) for t in range(qb) for h in range(ATT_HEADS)]
    kv_of = lambda h: h // ATT_GROUP
    aligned = lambda h: (h % 2) == (kv_of(h) % 2)

    def scores(t, h):
        qh = jnp.where((lane_q // ATT_HEAD) == (h % 2),
                       q[t * BLOCK:(t + 1) * BLOCK, (h // 2) * LANES:(h // 2 + 1) * LANES], zero_q)
        kh = window((k_all if aligned(h) else k_all_rolled)[kv_of(h) // 2], t)
        return _mm_nt(qh, kh) + (slopes[h] * LOG2E) * neg_dist[t]

    s = [scores(t, h) for t, h in items]
    sink = [sink_ref[h] * LOG2E for _, h in items]
    mx = [jnp.maximum(jnp.max(s_, axis=-1, keepdims=True), sk) for s_, sk in zip(s, sink)]
    p = [jnp.exp2(s_ - m_) for s_, m_ in zip(s, mx)]
    pv = [_mm(p_, v_aug[t][kv_of(h)]) for p_, (t, h) in zip(p, items)]
    pv_sw = [pltpu.roll(x, ATT_HEAD, 1) for x in pv]
    out = []
    for i, (t, h) in enumerate(items):
        num, den = (pv[i], pv_sw[i]) if aligned(h) else (pv_sw[i], pv[i])
        out.append(num * (1.0 / (den + jnp.exp2(sink[i] - mx[i]))))
    for t in range(qb):
        o = out[t * ATT_HEADS:(t + 1) * ATT_HEADS]
        o_ref[t * BLOCK:(t + 1) * BLOCK, :] = jnp.concatenate(
            [jnp.where((lane_q // ATT_HEAD) == 0, o[2 * m], o[2 * m + 1]) for m in range(ATT_HEADS // 2)],
            axis=1).astype(BF16)


def _swa(z, sinks, q_gain_t, k_gain_t, seq, qb=8):
    m = z.shape[0]
    rows = qb * BLOCK
    slopes = tuple(float(s) for s in
                   np.exp2(-8.0 * np.arange(1, ATT_HEADS + 1, dtype=np.float32) / ATT_HEADS).astype(np.float32))
    kv_cur = lambda c0: pl.BlockSpec((rows, ATT_KV_DIM), lambda n, c=c0 // ATT_KV_DIM: (n, c))
    kv_prev = lambda c0: pl.BlockSpec((BLOCK, ATT_KV_DIM),
                                      lambda n, c=c0 // ATT_KV_DIM: (jnp.maximum(n * qb - 1, 0), c))
    return pl.pallas_call(
        functools.partial(_swa_kernel, steps_per_seq=seq // rows, slopes=slopes, qb=qb),
        grid=(m // rows,),
        in_specs=[pl.BlockSpec(memory_space=pltpu.SMEM),
                  pl.BlockSpec((rows, ATT_Q_DIM), lambda n: (n, C_Q // ATT_Q_DIM)),
                  kv_cur(C_KA), kv_prev(C_KA), kv_cur(C_VA), kv_prev(C_VA),
                  pl.BlockSpec((1, ATT_Q_DIM), lambda n: (0, 0)),
                  pl.BlockSpec((1, ATT_KV_DIM), lambda n: (0, 0))],
        out_specs=pl.BlockSpec((rows, ATT_Q_DIM), lambda n: (n, 0)),
        out_shape=jax.ShapeDtypeStruct((m, ATT_Q_DIM), BF16),
        compiler_params=_cparams(("parallel",)),
        name="swa",
    )(sinks, z, z, z, z, z, q_gain_t, k_gain_t)


def _mix_out_kernel(orw_ref, oatt_ref, wb1_ref, wb2_ref, zg_ref, wo_ref, x_ref, g_ref, x1_ref, h2_ref):
    p_rw = jnp.dot(orw_ref[...], wb1_ref[...], preferred_element_type=F32)
    p_att = jnp.dot(oatt_ref[...], wb2_ref[...], preferred_element_type=F32)
    g_rw = jax.nn.sigmoid(zg_ref[:, :D_MODEL].astype(F32))
    g_att = jax.nn.sigmoid(zg_ref[:, D_MODEL:].astype(F32))
    mix = (g_rw * p_rw + g_att * p_att).astype(BF16)
    x1 = x_ref[...] + jnp.dot(mix, wo_ref[...], preferred_element_type=F32)
    x1_ref[...] = x1
    y = x1 * lax.rsqrt(jnp.mean(x1 * x1, axis=-1, keepdims=True) + RMS_EPS)
    h2_ref[...] = (y * g_ref[...]).astype(BF16)


def _mix_out(o_rw, o_att, w_branch_b, z, w_out_b, x2, g2, tm=512):
    m = x2.shape[0]
    const = lambda rows, cols, r0=0: pl.BlockSpec((rows, cols), lambda i: (r0, 0),
                                                  pipeline_mode=pl.Buffered(1))
    rows = lambda cols, c0=0: pl.BlockSpec((tm, cols), lambda i: (i, c0))
    return pl.pallas_call(
        _mix_out_kernel,
        grid=(m // tm,),
        in_specs=[rows(RW_DIM), rows(ATT_Q_DIM), const(RW_DIM, D_MODEL), const(ATT_Q_DIM, D_MODEL, 1),
                  rows(2 * D_MODEL, C_GRW // (2 * D_MODEL)), const(D_MODEL, D_MODEL),
                  rows(D_MODEL), const(1, D_MODEL)],
        out_specs=[rows(D_MODEL), rows(D_MODEL)],
        out_shape=[jax.ShapeDtypeStruct((m, D_MODEL), F32),
                   jax.ShapeDtypeStruct((m, D_MODEL), BF16)],
        compiler_params=_cparams(("parallel",)),
        name="mix_out",
    )(o_rw, o_att, w_branch_b, w_branch_b, z, w_out_b, x2, g2)


def _ffn_up_kernel(h_ref, wvb_ref, wgb_ref, cwv_ref, cwg_ref, cbv_ref, cbg_ref, wd_ref, a_ref, wdb_ref,
                   cv_ref, cg_ref, *, tm, sub, tiles_per_seq):
    halo = 8
    wdb_ref[...] = wd_ref[...].astype(BF16)

    @pl.when((pl.program_id(1) % tiles_per_seq) == 0)
    def _():
        cv_ref[...] = jnp.zeros_like(cv_ref)
        cg_ref[...] = jnp.zeros_like(cg_ref)

    row = lax.broadcasted_iota(jnp.int32, (halo, a_ref.shape[1]), 0)

    def conv(u, carry_ref, cw_ref, cb_ref):
        p1 = carry_ref[halo - 1:halo, :]
        p2 = carry_ref[halo - 2:halo - 1, :]
        r1 = pltpu.roll(u, 1, 0)
        r2 = pltpu.roll(u, 2, 0)
        prev1 = jnp.concatenate([jnp.where(row == 0, p1, r1[:halo, :]), r1[halo:, :]], axis=0)
        prev2 = jnp.concatenate(
            [jnp.where(row == 0, p2, jnp.where(row == 1, p1, r2[:halo, :])), r2[halo:, :]], axis=0)
        carry_ref[...] = u[sub - halo:sub, :]
        w = cw_ref[...].astype(BF16)
        return (prev2.astype(BF16) * w[0:1, :] + prev1.astype(BF16) * w[1:2, :] + u.astype(BF16) * w[2:3, :]
                + cb_ref[...].astype(BF16))

    for s in range(tm // sub):
        h = h_ref[s * sub:(s + 1) * sub, :]
        val = conv(jnp.dot(h, wvb_ref[...], preferred_element_type=F32), cv_ref, cwv_ref, cbv_ref)
        gate = conv(jnp.dot(h, wgb_ref[...], preferred_element_type=F32), cg_ref, cwg_ref, cbg_ref)
        a_ref[s * sub:(s + 1) * sub, :] = (gate * jax.nn.sigmoid(gate) * val).astype(BF16)


def _ffn_up(h2, w_up_b, conv_w, conv_b, w_down, seq, tm=2048, tn=512, sub=256):
    m = h2.shape[0]
    nj = D_FF // tn
    ni = m // tm
    wd_rows = w_down.shape[0] // (nj * ni)
    wd_spec = pl.BlockSpec((wd_rows, D_MODEL), lambda j, i: (j * ni + i, 0))
    return pl.pallas_call(
        functools.partial(_ffn_up_kernel, tm=tm, sub=sub, tiles_per_seq=seq // tm),
        grid=(nj, ni),
        in_specs=[pl.BlockSpec((tm, D_MODEL), lambda j, i: (i, 0)),
                  pl.BlockSpec((D_MODEL, tn), lambda j, i: (0, j)),
                  pl.BlockSpec((D_MODEL, tn), lambda j, i: (0, nj + j)),
                  pl.BlockSpec((3, tn), lambda j, i: (0, j)),
                  pl.BlockSpec((3, tn), lambda j, i: (0, nj + j)),
                  pl.BlockSpec((1, tn), lambda j, i: (0, j)),
                  pl.BlockSpec((1, tn), lambda j, i: (0, nj + j)),
                  wd_spec],
        out_specs=[pl.BlockSpec((tm, tn), lambda j, i: (i, j)), wd_spec],
        out_shape=[jax.ShapeDtypeStruct((m, D_FF), BF16), jax.ShapeDtypeStruct(w_down.shape, BF16)],
        scratch_shapes=[pltpu.VMEM((8, tn), F32), pltpu.VMEM((8, tn), F32)],
        compiler_params=_cparams(("parallel", "arbitrary")),
        name="ffn_up",
    )(h2, w_up_b, w_up_b, conv_w, conv_w, conv_b, conv_b, w_down)


def _ffn_down_kernel(a_ref, w_ref, x1_ref, o_ref):
    o_ref[...] = x1_ref[...] + jnp.dot(a_ref[...], w_ref[...], preferred_element_type=F32)


def _ffn_down(act, w_down_b, x1, tm=512, tn=1024):
    m = act.shape[0]
    return pl.pallas_call(
        _ffn_down_kernel,
        grid=(D_MODEL // tn, m // tm),
        in_specs=[pl.BlockSpec((tm, D_FF), lambda j, i: (i, 0)),
                  pl.BlockSpec((D_FF, tn), lambda j, i: (0, j)),
                  pl.BlockSpec((tm, tn), lambda j, i: (i, j))],
        out_specs=pl.BlockSpec((tm, tn), lambda j, i: (i, j)),
        out_shape=jax.ShapeDtypeStruct((m, D_MODEL), F32),
        compiler_params=_cparams(("parallel", "arbitrary")),
        name="ffn_down",
    )(act, w_down_b, x1)


def _pad_cols(w, n):
    return jnp.pad(w, ((0, 0), (0, n - w.shape[1])))


def _pad_rows(w, n):
    return jnp.pad(w, ((0, n - w.shape[0]), (0, 0)))


def _layer(x2, batch, seq, w_in_p, norm1_g, rw_mu, rw_w0, rw_w2, rw_a0, rw_a2, rw_g2, rw_k_k, rw_k_a,
           rw_r_k, rw_gn_w, rw_gn_b, q_norm_g, k_norm_g, attn_sinks, w_branch, w_out,
           norm2_g, w_up, conv_w, conv_b, w_down):
    row = lambda v: v.reshape(1, -1).astype(F32)
    o3 = 3 * RW_DIM
    o4 = o3 + DECAY_LORA
    o5 = o4 + ICLR_LORA
    o6 = o5 + GATE_LORA
    mu = rw_mu.reshape(1, -1)
    mu_l = jnp.concatenate([_pad_cols(mu[:, o3:o4], LORA_W), _pad_cols(mu[:, o4:o5], LORA_A),
                            mu[:, o5:o6]], axis=1)
    w2p = _pad_rows(rw_w2, LORA_W).astype(BF16)
    a2p = _pad_rows(rw_a2, LORA_A).astype(BF16)

    z, _ = _in_proj(x2, row(norm1_g), w_in_p, ())
    o_rw, (w_branch_b, w_out_b, w_up_b) = _rwkv(
        z, batch, seq, mu[:, :RW_DIM], mu[:, RW_DIM:2 * RW_DIM], mu[:, 2 * RW_DIM:o3], mu_l,
        row(rw_w0), row(rw_a0), row(rw_k_k), row(rw_k_a), row(rw_r_k),
        w2p, a2p, rw_g2.astype(BF16), row(rw_gn_w), row(rw_gn_b), (w_branch, w_out, w_up))
    o_att = _swa(z, attn_sinks.astype(F32), jnp.tile(row(q_norm_g), (1, ATT_HEADS)),
                 jnp.tile(row(k_norm_g), (1, ATT_KV_HEADS)), seq)
    x1, h2 = _mix_out(o_rw, o_att, w_branch_b, z, w_out_b, x2, row(norm2_g))
    act, w_down_b = _ffn_up(h2, w_up_b, conv_w.astype(F32), row(conv_b), w_down, seq)
    return _ffn_down(act, w_down_b, x1)


def kernel(x, norm1_g, w_in, rw_mu, rw_w0, rw_w2, rw_a0, rw_a2, rw_g2, rw_k_k, rw_k_a, rw_r_k,
           rw_gn_w, rw_gn_b, q_norm_g, k_norm_g, attn_sinks, w_branch, w_out, norm2_g, w_up,
           conv_w, conv_b, w_down):
    batch, seq, d = x.shape
    x2 = x.reshape(batch * seq, d)
    params = (norm1_g, rw_mu, rw_w0, rw_w2, rw_a0, rw_a2, rw_g2, rw_k_k, rw_k_a, rw_r_k,
              rw_gn_w, rw_gn_b, q_norm_g, k_norm_g, attn_sinks, w_branch, w_out, norm2_g, w_up,
              conv_w, conv_b, w_down)
    for layer in range(norm1_g.shape[0]):
        x2 = _layer(x2, batch, seq, _w_in_layout(w_in, layer), *(p[layer] for p in params))
    return x2.reshape(batch, seq, d)
```

```python
import functools

import jax
import jax.numpy as jnp
import numpy as np
from jax import lax
from jax.experimental import pallas as pl
from jax.experimental.pallas import tpu as pltpu

F32 = jnp.float32
BF16 = jnp.bfloat16

LANES = 128
BF16_SUBLANES = 16
VMEM_LIMIT = 56 * 1024 * 1024

D_MODEL = 2048
RW_HEADS = 16
RW_HEAD = 64
RW_DIM = RW_HEADS * RW_HEAD
DECAY_LORA = 96
ICLR_LORA = 96
GATE_LORA = 256
GN_EPS = 64e-5
ATT_HEADS = 16
ATT_KV_HEADS = 4
ATT_GROUP = ATT_HEADS // ATT_KV_HEADS
ATT_HEAD = 64
ATT_Q_DIM = ATT_HEADS * ATT_HEAD
ATT_KV_DIM = ATT_KV_HEADS * ATT_HEAD
WINDOW = 128
BLOCK = 128
D_FF = 5632
RMS_EPS = 1e-6
NEG_BIG = -1e30
LOG2E = 1.4426950408889634

C_R = 0
C_K = RW_DIM
C_V = 2 * RW_DIM
C_Q = 3 * RW_DIM
C_GRW = C_Q + ATT_Q_DIM
C_GATT = C_GRW + D_MODEL
C_LORA = C_GATT + D_MODEL
LORA_W = 128
LORA_A = 128
LORA_COLS = LORA_W + LORA_A + GATE_LORA
C_KA = C_LORA + LORA_COLS
C_VA = C_KA + ATT_KV_DIM
Z_COLS = C_VA + ATT_KV_DIM

CHUNK = 64
PAIR = 2 * RW_HEAD
N_PAIRS = RW_DIM // PAIR


def _cparams(sem):
    return pltpu.CompilerParams(dimension_semantics=sem, vmem_limit_bytes=VMEM_LIMIT)


def _mm(a, b):
    return jnp.dot(a.astype(BF16), b.astype(BF16), preferred_element_type=F32)


def _mm_nt(a, b):
    return lax.dot_general(a.astype(BF16), b.astype(BF16), (((1,), (1,)), ((), ())),
                           preferred_element_type=F32)


def _mm_tn(a, b):
    return lax.dot_general(a.astype(BF16), b.astype(BF16), (((0,), (0,)), ((), ())),
                           preferred_element_type=F32)


def _mm_split(m01, x):
    hi = x.astype(BF16)
    lo = (x - hi.astype(F32)).astype(BF16)
    return (jnp.dot(m01, hi, preferred_element_type=F32)
            + jnp.dot(m01, lo, preferred_element_type=F32))


def _head_ones(n, head, scale):
    r = lax.broadcasted_iota(jnp.int32, (n, n), 0) // head
    c = lax.broadcasted_iota(jnp.int32, (n, n), 1) // head
    return jnp.where(r == c, scale, 0.0).astype(BF16)


def _w_in_layout_kernel(wt_ref, o_ref, *, moves, pad):
    o_ref[pad[0]:pad[1], :] = jnp.zeros((pad[1] - pad[0], o_ref.shape[1]), BF16)
    for dst, src, width in moves:
        o_ref[dst:dst + width, :] = wt_ref[src:src + width, :].astype(BF16)


def _w_in_layout(w_in_layers, layer, cols=256):
    _, d, n = w_in_layers.shape
    wt_layers = jnp.swapaxes(w_in_layers, 1, 2)
    o3 = 3 * RW_DIM
    o4 = o3 + DECAY_LORA
    o5 = o4 + ICLR_LORA
    o6 = o5 + GATE_LORA
    oq = o6 + ATT_Q_DIM
    ok = oq + ATT_KV_DIM
    ov = ok + ATT_KV_DIM
    moves = ((C_R, 0, o3), (C_Q, o6, ATT_Q_DIM), (C_GRW, ov, 2 * D_MODEL),
             (C_LORA, o3, DECAY_LORA), (C_LORA + LORA_W, o4, ICLR_LORA),
             (C_LORA + LORA_W + LORA_A, o5, GATE_LORA), (C_KA, oq, ATT_KV_DIM), (C_VA, ok, ATT_KV_DIM))
    return pl.pallas_call(
        functools.partial(_w_in_layout_kernel, moves=moves, pad=(C_LORA, C_LORA + LORA_W + LORA_A)),
        grid=(d // cols,),
        in_specs=[pl.BlockSpec((None, n, cols), lambda i: (layer, 0, i))],
        out_specs=pl.BlockSpec((Z_COLS, cols), lambda i: (0, i)),
        out_shape=jax.ShapeDtypeStruct((Z_COLS, d), BF16),
        compiler_params=_cparams(("parallel",)),
        name="w_in_layout",
    )(wt_layers)


def _in_proj_kernel(x_ref, g_ref, wt_ref, z_ref, h_ref):
    @pl.when(pl.program_id(1) == 0)
    def _():
        xf = x_ref[...]
        y = xf * lax.rsqrt(jnp.mean(xf * xf, axis=-1, keepdims=True) + RMS_EPS)
        h_ref[...] = (y * g_ref[...]).astype(BF16)

    z_ref[...] = _mm_nt(h_ref[...], wt_ref[...]).astype(BF16)


def _in_proj(x2, g1, w_in_t, tm=1024, tn=1536):
    m, d = x2.shape
    n = w_in_t.shape[0]
    return pl.pallas_call(
        _in_proj_kernel,
        grid=(m // tm, n // tn),
        in_specs=[pl.BlockSpec((tm, d), lambda i, j: (i, 0)),
                  pl.BlockSpec((1, d), lambda i, j: (0, 0)),
                  pl.BlockSpec((tn, d), lambda i, j: (j, 0))],
        out_specs=pl.BlockSpec((tm, tn), lambda i, j: (i, j)),
        out_shape=jax.ShapeDtypeStruct((m, n), BF16),
        scratch_shapes=[pltpu.VMEM((tm, d), BF16)],
        compiler_params=_cparams(("parallel", "arbitrary")),
        name="in_proj",
    )(x2, g1, w_in_t)


def _rwkv_prep_kernel(zr_ref, zk_ref, zv_ref, zl_ref, pr_ref, pk_ref, pv_ref, pli_ref,
                      mur_ref, muk_ref, muv_ref, mul_ref,
                      w0_ref, a0_ref, kk_ref, ka_ref, rk_ref, w2_ref, a2_ref, g2_ref,
                      at_ref, rt_ref, bt_ref, kt_ref, bg_ref, kg_ref, v_ref, g_ref, bonus_ref,
                      gT_ref, gm_ref, *, tm, first):
    keep = jnp.where(first, 0.0, 1.0)
    nc = tm // CHUNK
    last = slice(BF16_SUBLANES - 1, BF16_SUBLANES)

    def shifted(z, prev_last, mu):
        z = z.astype(F32)
        row = lax.broadcasted_iota(jnp.int32, z.shape, 0)
        zprev = jnp.where(row == 0, prev_last.astype(F32) * keep, pltpu.roll(z, 1, 0))
        return z + (zprev - z) * mu

    lo = shifted(zl_ref[...], pli_ref[last, :], mul_ref[...])
    tanh_wd = jnp.tanh(lo[:, :LORA_W]).astype(BF16)
    ad = lo[:, LORA_W:LORA_W + LORA_A].astype(BF16)
    sig_gd = jax.nn.sigmoid(lo[:, LORA_W + LORA_A:]).astype(BF16)

    ti = lax.broadcasted_iota(jnp.int32, (tm, tm), 0)
    si = lax.broadcasted_iota(jnp.int32, (tm, tm), 1)
    tri = jnp.where(((ti // CHUNK) == (si // CHUNK)) & (si <= ti), 1.0, 0.0).astype(BF16)
    ones_h = _head_ones(PAIR, RW_HEAD, 1.0)
    w_lora = jnp.dot(tanh_wd, w2_ref[...], preferred_element_type=F32)
    a_lora = jnp.dot(ad, a2_ref[...], preferred_element_type=F32)
    g_ref[...] = jnp.dot(sig_gd, g2_ref[...], preferred_element_type=F32).astype(BF16)
    yield

    stash = []
    for p in range(N_PAIRS):
        cs = slice(p * PAIR, (p + 1) * PAIR)
        r = shifted(zr_ref[:, cs], pr_ref[last, cs], mur_ref[:, cs])
        k = shifted(zk_ref[:, cs], pk_ref[last, cs], muk_ref[:, cs])
        v = shifted(zv_ref[:, cs], pv_ref[last, cs], muv_ref[:, cs])

        wpre = w0_ref[:, cs] + w_lora[:, cs]
        w = jnp.minimum(wpre, 0.0) - jnp.log1p(jnp.exp(-jnp.abs(wpre))) - 0.5
        lw = -jnp.exp(w)
        a = jax.nn.sigmoid(a0_ref[:, cs] + a_lora[:, cs])

        kk = k * kk_ref[:, cs]
        kk = kk * lax.rsqrt(jnp.maximum(_mm(kk * kk, ones_h), 1e-24))
        k2 = k * (1.0 + (a - 1.0) * ka_ref[:, cs])
        bonus_ref[:, cs] = (_mm(r * k2 * rk_ref[:, cs], ones_h) * v).astype(BF16)
        v_ref[:, cs] = v.astype(BF16)
        stash.append((r, k2, -kk, kk * a, lw))
        yield

    c_all = _mm_split(tri, jnp.concatenate([st[4] for st in stash], axis=1))
    yield

    for p in range(N_PAIRS):
        cs = slice(p * PAIR, (p + 1) * PAIR)
        r, k2, ka_vec, kb_vec, lw = stash[p]
        c = c_all[:, cs]
        c3 = c.reshape(nc, CHUNK, PAIR)
        c_mid = c3[:, CHUNK // 2 - 1:CHUNK // 2, :]
        c_end = c3[:, CHUNK - 1:CHUNK, :]
        cm = jnp.broadcast_to(c_mid, c3.shape).reshape(tm, PAIR)
        cT = jnp.broadcast_to(c_end, c3.shape).reshape(tm, PAIR)

        e_in = jnp.exp(c - cm)
        e_out = jnp.exp(cm - c)
        e_end = jnp.exp(cT - c)
        at_ref[:, cs] = (ka_vec * jnp.exp(c - lw - cm)).astype(BF16)
        rt_ref[:, cs] = (r * e_in).astype(BF16)
        bt_ref[:, cs] = (kb_vec * e_out).astype(BF16)
        kt_ref[:, cs] = (k2 * e_out).astype(BF16)
        bg_ref[:, cs] = (kb_vec * e_end).astype(BF16)
        kg_ref[:, cs] = (k2 * e_end).astype(BF16)
        gT_ref[:, :, cs] = jnp.exp(c_end)
        gm_ref[:, :, cs] = jnp.exp(c_mid)
        yield


def _rwkv_kernel(*refs, tb, nb, n_side):
    prep_in, (gnw_ref, gnb_ref), refs = refs[:20], refs[20:22], refs[22:]
    side_in, o_ref, side_out, scratch = refs[:n_side], refs[n_side], refs[n_side + 1:2 * n_side + 1], refs[2 * n_side + 1:]
    operands, s_ref = scratch[:11], scratch[11]
    j = pl.program_id(1)
    prep = functools.partial(_rwkv_prep_kernel, *prep_in, *operands, tm=tb, first=j == 0)
    chunk = functools.partial(_rwkv_chunk_kernel, *operands, gnw_ref, gnb_ref, o_ref, s_ref,
                              chunks=tb // CHUNK, pairs=N_PAIRS, first=j == 1)

    def side_cast():
        for w_ref, wb_ref in zip(side_in, side_out):
            wb_ref[...] = w_ref[...].astype(BF16)

    @pl.when(j == 0)
    def _():
        side_cast()
        for _ in prep():
            pass

    @pl.when((j > 0) & (j < nb))
    def _():
        side_cast()
        chunk(filler=prep())

    @pl.when(j == nb)
    def _():
        side_cast()
        chunk()


def _rwkv(z, batch, seq, mu_r, mu_k, mu_v, mu_l, w0, a0, k_k, k_a, r_k, w2p, a2p, g2, gn_w, gn_b,
          side_weights, tb=128, side_rows=32):
    m = z.shape[0]
    nb = seq // tb
    for w in side_weights:
        assert w.shape[0] // side_rows <= batch * (nb + 1), "not enough grid steps to cast this weight"
    side_specs = [pl.BlockSpec((side_rows, w.shape[1]),
                               lambda b, j, last=w.shape[0] // side_rows - 1:
                               (jnp.minimum(b * (nb + 1) + j, last), 0)) for w in side_weights]
    pb = tb // BF16_SUBLANES
    blk = lambda b, j: b * nb + jnp.minimum(j, nb - 1)

    def cur(col0, width):
        return pl.BlockSpec((tb, width), lambda b, j, c0=col0 // width: (blk(b, j), c0))

    def prev(col0, width):
        return pl.BlockSpec((BF16_SUBLANES, width),
                            lambda b, j, c0=col0 // width: (jnp.maximum(blk(b, j) * pb - 1, 0), c0))

    def whole(rows, width):
        return pl.BlockSpec((rows, width), lambda b, j: (0, 0))

    in_specs = [cur(C_R, RW_DIM), cur(C_K, RW_DIM), cur(C_V, RW_DIM), cur(C_LORA, LORA_COLS),
                prev(C_R, RW_DIM), prev(C_K, RW_DIM), prev(C_V, RW_DIM), prev(C_LORA, LORA_COLS),
                whole(1, RW_DIM), whole(1, RW_DIM), whole(1, RW_DIM), whole(1, LORA_COLS),
                whole(1, RW_DIM), whole(1, RW_DIM), whole(1, RW_DIM), whole(1, RW_DIM), whole(1, RW_DIM),
                whole(LORA_W, RW_DIM), whole(LORA_A, RW_DIM), whole(GATE_LORA, RW_DIM),
                whole(1, RW_DIM), whole(1, RW_DIM)]
    big = pltpu.VMEM((tb, RW_DIM), BF16)
    per_chunk = pltpu.VMEM((tb // CHUNK, 1, RW_DIM), F32)
    outs = pl.pallas_call(
        functools.partial(_rwkv_kernel, tb=tb, nb=nb, n_side=len(side_weights)),
        grid=(batch, nb + 1),
        in_specs=in_specs + side_specs,
        out_specs=[pl.BlockSpec((tb, RW_DIM), lambda b, j: (b * nb + jnp.maximum(j - 1, 0), 0))] + side_specs,
        out_shape=[jax.ShapeDtypeStruct((m, RW_DIM), BF16)]
        + [jax.ShapeDtypeStruct(w.shape, BF16) for w in side_weights],
        scratch_shapes=[big] * 9 + [per_chunk] * 2 + [pltpu.VMEM((N_PAIRS, PAIR, PAIR), F32)],
        compiler_params=_cparams(("arbitrary", "arbitrary")),
        name="rwkv",
    )(z, z, z, z, z, z, z, z, mu_r, mu_k, mu_v, mu_l, w0, a0, k_k, k_a, r_k, w2p, a2p, g2, gn_w, gn_b,
      *side_weights)
    return outs[0], outs[1:]


def _rwkv_chunk_kernel(at_ref, rt_ref, bt_ref, kt_ref, bg_ref, kg_ref, v_ref, g_ref, bonus_ref,
                       gT_ref, gm_ref, gnw_ref, gnb_ref, o_ref, s_ref, *, chunks, pairs, first,
                       filler=None):
    @pl.when(first)
    def _():
        s_ref[...] = jnp.zeros_like(s_ref)

    T = CHUNK
    lane = lax.broadcasted_iota(jnp.int32, (T, PAIR), 1)
    head0 = lane < RW_HEAD
    ri = lax.broadcasted_iota(jnp.int32, (2 * T, 2 * T), 0)
    ci = lax.broadcasted_iota(jnp.int32, (2 * T, 2 * T), 1)
    same = (ri // T) == (ci // T)
    strict = same & ((ri % T) > (ci % T))
    incl = same & ((ri % T) >= (ci % T))
    eye = jnp.where(ri == ci, 1.0, 0.0)
    own = (ri // T) == (ci // RW_HEAD)

    items = [(p, c) for c in range(chunks) for p in range(pairs)]
    rows = lambda c: slice(c * T, (c + 1) * T)
    cols = lambda p: slice(p * PAIR, (p + 1) * PAIR)

    def stack(ref):
        out = []
        for p, c in items:
            x = ref[rows(c), cols(p)]
            zero = jnp.zeros_like(x)
            out.append(jnp.concatenate([jnp.where(head0, x, zero), jnp.where(head0, zero, x)], axis=0))
        return out

    each = lambda f, *ls: [f(*xs) for xs in zip(*ls)]
    La, Lr, Rb, Rk = stack(at_ref), stack(rt_ref), stack(bt_ref), stack(kt_ref)
    Rbg, Rkg, Vs = stack(bg_ref), stack(kg_ref), stack(v_ref)
    gm_row = [gm_ref[c][:, cols(p)] for p, c in items]
    gT_row = [gT_ref[c][:, cols(p)] for p, c in items]
    gate = [g_ref[rows(c), cols(p)] for p, c in items]
    bonus = [bonus_ref[rows(c), cols(p)] for p, c in items]
    filler = iter(()) if filler is None else filler
    tick = lambda: next(filler, None)

    AA = each(lambda la, lr, rb, rk: _mm_nt(jnp.concatenate([la, lr], axis=0),
                                            jnp.concatenate([rb, rk], axis=0)), La, Lr, Rb, Rk)
    tick()
    N = [jnp.where(strict, aa[:2 * T, :2 * T], 0.0) for aa in AA]
    Aak = [jnp.where(strict, aa[:2 * T, 2 * T:], 0.0) for aa in AA]
    Arb = [jnp.where(incl, aa[2 * T:, :2 * T], 0.0) for aa in AA]
    Ark = [jnp.where(incl, aa[2 * T:, 2 * T:], 0.0) for aa in AA]
    def live_rows(x, t0):
        return x if t0 == 0 else jnp.concatenate([x[t0:T, :], x[T + t0:, :]], axis=0)

    def all_rows(y, t0):
        if t0 == 0:
            return y
        zero = jnp.zeros((t0, y.shape[1]), y.dtype)
        return jnp.concatenate([zero, y[:T - t0, :], zero, y[T - t0:, :]], axis=0)

    def skip(span):
        return span if span % BF16_SUBLANES == 0 else 0

    def blockdiag(a, b):
        zero = jnp.zeros_like(a)
        return jnp.concatenate([jnp.concatenate([a, zero], axis=1),
                                jnp.concatenate([zero, b], axis=1)], axis=0)

    groups = [(i, i + 1) for i in range(0, len(items), 2)]
    W = each(lambda n: eye + n, N)
    P = [n.astype(BF16) for n in N]
    for a, b in groups:
        sq = _mm(jnp.concatenate([P[a], P[b]], axis=1), blockdiag(P[a], P[b]))
        P[a], P[b] = sq[:, :PAIR].astype(BF16), sq[:, PAIR:].astype(BF16)
    tick()
    span = 2
    while span < T:
        last = 2 * span >= T
        t0, t1 = skip(span), skip(2 * span)
        for a, b in groups:
            lhs = jnp.concatenate([live_rows(W[a].astype(BF16), t0), live_rows(W[b].astype(BF16), t0)], axis=1)
            if not last:
                lhs_p = jnp.concatenate([live_rows(P[a], t1), live_rows(P[b], t1)], axis=1)
                lhs = jnp.concatenate([lhs_p, lhs], axis=0)
            res = _mm(lhs, blockdiag(P[a], P[b]))
            if not last:
                sq = all_rows(res[:lhs_p.shape[0], :], t1)
                P[a], P[b] = sq[:, :PAIR].astype(BF16), sq[:, PAIR:].astype(BF16)
                res = res[lhs_p.shape[0]:, :]
            wp = all_rows(res, t0)
            W[a] = W[a] + wp[:, :PAIR]
            W[b] = W[b] + wp[:, PAIR:]
        tick()
        span *= 2
    AkV = each(_mm, Aak, Vs)
    tick()
    X = each(lambda w, la, akv: _mm(w, jnp.concatenate([la, akv.astype(BF16)], axis=1)).astype(BF16),
             W, La, AkV)
    tick()
    Z = each(lambda arb, ark, x, vs:
             _mm(jnp.concatenate([arb.astype(BF16), ark.astype(BF16)], axis=1),
                 jnp.concatenate([x, jnp.concatenate([jnp.zeros_like(vs), vs], axis=1)], axis=0)),
             Arb, Ark, X, Vs)
    tick()
    Q = each(lambda z, lr, gm: (z[:, :PAIR] + lr.astype(F32)) * gm, Z, Lr, gm_row)
    Y0 = [z[:, PAIR:] for z in Z]
    tick()
    Mbd = each(lambda rbg, x, gm: _mm_tn(rbg, x[:, :PAIR]) * gm, Rbg, X, gm_row)
    NcT = each(lambda x, vs, rbg, rkg: _mm_tn(jnp.concatenate([x[:, PAIR:], vs], axis=0),
                                              jnp.concatenate([rbg, rkg], axis=0)), X, Vs, Rbg, Rkg)
    for _ in filler:
        pass

    S = [s_ref[p] for p in range(pairs)]
    for c in range(chunks):
        for p in range(pairs):
            i = c * pairs + p
            Ys = _mm_nt(Q[i], S[p]) + Y0[i]
            S[p] = S[p] * gT_row[i] + _mm_nt(S[p], Mbd[i]) + NcT[i]
            mu = jnp.sum(Ys, axis=-1, keepdims=True) * (1.0 / RW_HEAD)
            d = jnp.where(own, Ys - mu, 0.0)
            var = jnp.sum(d * d, axis=-1, keepdims=True) * (1.0 / RW_HEAD)
            dn = d * lax.rsqrt(var + GN_EPS)
            yn = (dn[:T, :] + dn[T:, :]) * gnw_ref[:, cols(p)] + gnb_ref[:, cols(p)]
            out = (yn + bonus[i].astype(F32)) * gate[i].astype(F32)
            o_ref[rows(c), cols(p)] = out.astype(BF16)
    for p in range(pairs):
        s_ref[p] = S[p]


def _swa_kernel(sink_ref, q_ref, kc_ref, kp_ref, vc_ref, vp_ref, qg_ref, kg_ref, o_ref, *,
                steps_per_seq, slopes, qb):
    first = (pl.program_id(0) % steps_per_seq) == 0
    mean_h = _head_ones(LANES, ATT_HEAD, 1.0 / ATT_HEAD)

    def head_rms(x, gain):
        parts = []
        for b in range(x.shape[1] // LANES):
            xb = x[:, b * LANES:(b + 1) * LANES]
            ms = _mm(xb * xb, mean_h)
            parts.append(xb * lax.rsqrt(ms + RMS_EPS))
        return jnp.concatenate(parts, axis=1) * gain

    q = head_rms(q_ref[...].astype(F32), qg_ref[...]) * (ATT_HEAD ** -0.5 * LOG2E)
    q = q.astype(BF16)
    kcat = jnp.concatenate([kp_ref[...], kc_ref[...]], axis=0).astype(F32)
    kcat = head_rms(kcat, kg_ref[...]).astype(BF16)
    vcat = jnp.concatenate([vp_ref[...], vc_ref[...]], axis=0)

    qi = lax.broadcasted_iota(jnp.int32, (BLOCK, 2 * BLOCK), 0)
    kj = lax.broadcasted_iota(jnp.int32, (BLOCK, 2 * BLOCK), 1)
    dist_i = BLOCK + qi - kj
    in_window = (dist_i >= 0) & (dist_i < WINDOW)
    first_key = jnp.where(first, BLOCK, 0)
    neg_dist_rest = jnp.where(in_window, -dist_i.astype(F32), NEG_BIG)
    neg_dist_first = jnp.where(in_window & (kj >= first_key), -dist_i.astype(F32), NEG_BIG)
    neg_dist = [neg_dist_first] + [neg_dist_rest] * (qb - 1)

    lane_q = lax.broadcasted_iota(jnp.int32, (BLOCK, LANES), 1)
    lane_kv = lax.broadcasted_iota(jnp.int32, (2 * BLOCK, LANES), 1)
    zero_q = jnp.zeros((BLOCK, LANES), BF16)
    one_kv = jnp.ones((2 * BLOCK, LANES), BF16)
    kv_lane_blocks = range(ATT_KV_DIM // LANES)
    k_all = [kcat[:, b * LANES:(b + 1) * LANES] for b in kv_lane_blocks]
    k_all_rolled = [pltpu.roll(kb, ATT_HEAD, 1) for kb in k_all]
    window = lambda x, t: x[t * BLOCK:(t + 2) * BLOCK, :]
    v_aug = [[jnp.where((lane_kv // ATT_HEAD) == (j % 2),
                        window(vcat[:, (j // 2) * LANES:(j // 2 + 1) * LANES], t), one_kv)
              for j in range(ATT_KV_HEADS)] for t in range(qb)]
    items = [(t, h) for t in range(qb) for h in range(ATT_HEADS)]
    kv_of = lambda h: h // ATT_GROUP
    aligned = lambda h: (h % 2) == (kv_of(h) % 2)

    def scores(t, h):
        qh = jnp.where((lane_q // ATT_HEAD) == (h % 2),
                       q[t * BLOCK:(t + 1) * BLOCK, (h // 2) * LANES:(h // 2 + 1) * LANES], zero_q)
        kh = window((k_all if aligned(h) else k_all_rolled)[kv_of(h) // 2], t)
        return _mm_nt(qh, kh) + (slopes[h] * LOG2E) * neg_dist[t]

    s = [scores(t, h) for t, h in items]
    sink = [sink_ref[h] * LOG2E for _, h in items]
    mx = [jnp.maximum(jnp.max(s_, axis=-1, keepdims=True), sk) for s_, sk in zip(s, sink)]
    p = [jnp.exp2(s_ - m_) for s_, m_ in zip(s, mx)]
    pv = [_mm(p_, v_aug[t][kv_of(h)]) for p_, (t, h) in zip(p, items)]
    pv_sw = [pltpu.roll(x, ATT_HEAD, 1) for x in pv]
    out = []
    for i, (t, h) in enumerate(items):
        num, den = (pv[i], pv_sw[i]) if aligned(h) else (pv_sw[i], pv[i])
        out.append(num * (1.0 / (den + jnp.exp2(sink[i] - mx[i]))))
    for t in range(qb):
        o = out[t * ATT_HEADS:(t + 1) * ATT_HEADS]
        o_ref[t * BLOCK:(t + 1) * BLOCK, :] = jnp.concatenate(
            [jnp.where((lane_q // ATT_HEAD) == 0, o[2 * m], o[2 * m + 1]) for m in range(ATT_HEADS // 2)],
            axis=1).astype(BF16)


def _swa(z, sinks, q_gain_t, k_gain_t, seq, qb=8):
    m = z.shape[0]
    rows = qb * BLOCK
    slopes = tuple(float(s) for s in
                   np.exp2(-8.0 * np.arange(1, ATT_HEADS + 1, dtype=np.float32) / ATT_HEADS).astype(np.float32))
    kv_cur = lambda c0: pl.BlockSpec((rows, ATT_KV_DIM), lambda n, c=c0 // ATT_KV_DIM: (n, c))
    kv_prev = lambda c0: pl.BlockSpec((BLOCK, ATT_KV_DIM),
                                      lambda n, c=c0 // ATT_KV_DIM: (jnp.maximum(n * qb - 1, 0), c))
    return pl.pallas_call(
        functools.partial(_swa_kernel, steps_per_seq=seq // rows, slopes=slopes, qb=qb),
        grid=(m // rows,),
        in_specs=[pl.BlockSpec(memory_space=pltpu.SMEM),
                  pl.BlockSpec((rows, ATT_Q_DIM), lambda n: (n, C_Q // ATT_Q_DIM)),
                  kv_cur(C_KA), kv_prev(C_KA), kv_cur(C_VA), kv_prev(C_VA),
                  pl.BlockSpec((1, ATT_Q_DIM), lambda n: (0, 0)),
                  pl.BlockSpec((1, ATT_KV_DIM), lambda n: (0, 0))],
        out_specs=pl.BlockSpec((rows, ATT_Q_DIM), lambda n: (n, 0)),
        out_shape=jax.ShapeDtypeStruct((m, ATT_Q_DIM), BF16),
        compiler_params=_cparams(("parallel",)),
        name="swa",
    )(sinks, z, z, z, z, z, q_gain_t, k_gain_t)


def _mix_out_kernel(orw_ref, oatt_ref, wb1_ref, wb2_ref, zg_ref, wo_ref, x_ref, g_ref, x1_ref, h2_ref):
    p_rw = jnp.dot(orw_ref[...], wb1_ref[...], preferred_element_type=F32)
    p_att = jnp.dot(oatt_ref[...], wb2_ref[...], preferred_element_type=F32)
    g_rw = jax.nn.sigmoid(zg_ref[:, :D_MODEL].astype(F32))
    g_att = jax.nn.sigmoid(zg_ref[:, D_MODEL:].astype(F32))
    mix = (g_rw * p_rw + g_att * p_att).astype(BF16)
    x1 = x_ref[...] + jnp.dot(mix, wo_ref[...], preferred_element_type=F32)
    x1_ref[...] = x1
    y = x1 * lax.rsqrt(jnp.mean(x1 * x1, axis=-1, keepdims=True) + RMS_EPS)
    h2_ref[...] = (y * g_ref[...]).astype(BF16)


def _mix_out(o_rw, o_att, w_branch_b, z, w_out_b, x2, g2, tm=512):
    m = x2.shape[0]
    const = lambda rows, cols, r0=0: pl.BlockSpec((rows, cols), lambda i: (r0, 0),
                                                  pipeline_mode=pl.Buffered(1))
    rows = lambda cols, c0=0: pl.BlockSpec((tm, cols), lambda i: (i, c0))
    return pl.pallas_call(
        _mix_out_kernel,
        grid=(m // tm,),
        in_specs=[rows(RW_DIM), rows(ATT_Q_DIM), const(RW_DIM, D_MODEL), const(ATT_Q_DIM, D_MODEL, 1),
                  rows(2 * D_MODEL, C_GRW // (2 * D_MODEL)), const(D_MODEL, D_MODEL),
                  rows(D_MODEL), const(1, D_MODEL)],
        out_specs=[rows(D_MODEL), rows(D_MODEL)],
        out_shape=[jax.ShapeDtypeStruct((m, D_MODEL), F32),
                   jax.ShapeDtypeStruct((m, D_MODEL), BF16)],
        compiler_params=_cparams(("parallel",)),
        name="mix_out",
    )(o_rw, o_att, w_branch_b, w_branch_b, z, w_out_b, x2, g2)


def _ffn_up_kernel(h_ref, wvb_ref, wgb_ref, cwv_ref, cwg_ref, cbv_ref, cbg_ref, wd_ref, a_ref, wdb_ref,
                   cv_ref, cg_ref, *, tm, sub, tiles_per_seq):
    halo = 8
    wdb_ref[...] = wd_ref[...].astype(BF16)

    @pl.when((pl.program_id(1) % tiles_per_seq) == 0)
    def _():
        cv_ref[...] = jnp.zeros_like(cv_ref)
        cg_ref[...] = jnp.zeros_like(cg_ref)

    row = lax.broadcasted_iota(jnp.int32, (halo, a_ref.shape[1]), 0)

    def conv(u, carry_ref, cw_ref, cb_ref):
        p1 = carry_ref[halo - 1:halo, :]
        p2 = carry_ref[halo - 2:halo - 1, :]
        r1 = pltpu.roll(u, 1, 0)
        r2 = pltpu.roll(u, 2, 0)
        prev1 = jnp.concatenate([jnp.where(row == 0, p1, r1[:halo, :]), r1[halo:, :]], axis=0)
        prev2 = jnp.concatenate(
            [jnp.where(row == 0, p2, jnp.where(row == 1, p1, r2[:halo, :])), r2[halo:, :]], axis=0)
        carry_ref[...] = u[sub - halo:sub, :]
        w = cw_ref[...].astype(BF16)
        return (prev2.astype(BF16) * w[0:1, :] + prev1.astype(BF16) * w[1:2, :] + u.astype(BF16) * w[2:3, :]
                + cb_ref[...].astype(BF16))

    for s in range(tm // sub):
        h = h_ref[s * sub:(s + 1) * sub, :]
        val = conv(jnp.dot(h, wvb_ref[...], preferred_element_type=F32), cv_ref, cwv_ref, cbv_ref)
        gate = conv(jnp.dot(h, wgb_ref[...], preferred_element_type=F32), cg_ref, cwg_ref, cbg_ref)
        a_ref[s * sub:(s + 1) * sub, :] = (gate * jax.nn.sigmoid(gate) * val).astype(BF16)


def _ffn_up(h2, w_up_b, conv_w, conv_b, w_down, seq, tm=4096, tn=512, sub=256):
    m = h2.shape[0]
    nj = D_FF // tn
    ni = m // tm
    wd_rows = w_down.shape[0] // (nj * ni)
    wd_spec = pl.BlockSpec((wd_rows, D_MODEL), lambda j, i: (j * ni + i, 0))
    return pl.pallas_call(
        functools.partial(_ffn_up_kernel, tm=tm, sub=sub, tiles_per_seq=seq // tm),
        grid=(nj, ni),
        in_specs=[pl.BlockSpec((tm, D_MODEL), lambda j, i: (i, 0)),
                  pl.BlockSpec((D_MODEL, tn), lambda j, i: (0, j)),
                  pl.BlockSpec((D_MODEL, tn), lambda j, i: (0, nj + j)),
                  pl.BlockSpec((3, tn), lambda j, i: (0, j)),
                  pl.BlockSpec((3, tn), lambda j, i: (0, nj + j)),
                  pl.BlockSpec((1, tn), lambda j, i: (0, j)),
                  pl.BlockSpec((1, tn), lambda j, i: (0, nj + j)),
                  wd_spec],
        out_specs=[pl.BlockSpec((tm, tn), lambda j, i: (i, j)), wd_spec],
        out_shape=[jax.ShapeDtypeStruct((m, D_FF), BF16), jax.ShapeDtypeStruct(w_down.shape, BF16)],
        scratch_shapes=[pltpu.VMEM((8, tn), F32), pltpu.VMEM((8, tn), F32)],
        compiler_params=_cparams(("parallel", "arbitrary")),
        name="ffn_up",
    )(h2, w_up_b, w_up_b, conv_w, conv_w, conv_b, conv_b, w_down)


def _ffn_down_kernel(a_ref, w_ref, x1_ref, o_ref):
    o_ref[...] = x1_ref[...] + jnp.dot(a_ref[...], w_ref[...], preferred_element_type=F32)


def _ffn_down(act, w_down_b, x1, tm=512, tn=1024):
    m = act.shape[0]
    return pl.pallas_call(
        _ffn_down_kernel,
        grid=(D_MODEL // tn, m // tm),
        in_specs=[pl.BlockSpec((tm, D_FF), lambda j, i: (i, 0)),
                  pl.BlockSpec((D_FF, tn), lambda j, i: (0, j)),
                  pl.BlockSpec((tm, tn), lambda j, i: (i, j))],
        out_specs=pl.BlockSpec((tm, tn), lambda j, i: (i, j)),
        out_shape=jax.ShapeDtypeStruct((m, D_MODEL), F32),
        compiler_params=_cparams(("parallel", "arbitrary")),
        name="ffn_down",
    )(act, w_down_b, x1)


def _pad_cols(w, n):
    return jnp.pad(w, ((0, 0), (0, n - w.shape[1])))


def _pad_rows(w, n):
    return jnp.pad(w, ((0, n - w.shape[0]), (0, 0)))


def _layer(x2, batch, seq, w_in_p, norm1_g, rw_mu, rw_w0, rw_w2, rw_a0, rw_a2, rw_g2, rw_k_k, rw_k_a,
           rw_r_k, rw_gn_w, rw_gn_b, q_norm_g, k_norm_g, attn_sinks, w_branch, w_out,
           norm2_g, w_up, conv_w, conv_b, w_down):
    row = lambda v: v.reshape(1, -1).astype(F32)
    o3 = 3 * RW_DIM
    o4 = o3 + DECAY_LORA
    o5 = o4 + ICLR_LORA
    o6 = o5 + GATE_LORA
    mu = rw_mu.reshape(1, -1)
    mu_l = jnp.concatenate([_pad_cols(mu[:, o3:o4], LORA_W), _pad_cols(mu[:, o4:o5], LORA_A),
                            mu[:, o5:o6]], axis=1)
    w2p = _pad_rows(rw_w2, LORA_W).astype(BF16)
    a2p = _pad_rows(rw_a2, LORA_A).astype(BF16)

    z = _in_proj(x2, row(norm1_g), w_in_p)
    o_rw, (w_branch_b, w_out_b, w_up_b) = _rwkv(
        z, batch, seq, mu[:, :RW_DIM], mu[:, RW_DIM:2 * RW_DIM], mu[:, 2 * RW_DIM:o3], mu_l,
        row(rw_w0), row(rw_a0), row(rw_k_k), row(rw_k_a), row(rw_r_k),
        w2p, a2p, rw_g2.astype(BF16), row(rw_gn_w), row(rw_gn_b), (w_branch, w_out, w_up))
    o_att = _swa(z, attn_sinks.astype(F32), jnp.tile(row(q_norm_g), (1, ATT_HEADS)),
                 jnp.tile(row(k_norm_g), (1, ATT_KV_HEADS)), seq)
    x1, h2 = _mix_out(o_rw, o_att, w_branch_b, z, w_out_b, x2, row(norm2_g))
    act, w_down_b = _ffn_up(h2, w_up_b, conv_w.astype(F32), row(conv_b), w_down, seq)
    return _ffn_down(act, w_down_b, x1)


def kernel(x, norm1_g, w_in, rw_mu, rw_w0, rw_w2, rw_a0, rw_a2, rw_g2, rw_k_k, rw_k_a, rw_r_k,
           rw_gn_w, rw_gn_b, q_norm_g, k_norm_g, attn_sinks, w_branch, w_out, norm2_g, w_up,
           conv_w, conv_b, w_down):
    batch, seq, d = x.shape
    x2 = x.reshape(batch * seq, d)
    params = (norm1_g, rw_mu, rw_w0, rw_w2, rw_a0, rw_a2, rw_g2, rw_k_k, rw_k_a, rw_r_k,
              rw_gn_w, rw_gn_b, q_norm_g, k_norm_g, attn_sinks, w_branch, w_out, norm2_g, w_up,
              conv_w, conv_b, w_down)
    for layer in range(norm1_g.shape[0]):
        x2 = _layer(x2, batch, seq, _w_in_layout(w_in, layer), *(p[layer] for p in params))
    return x2.reshape(batch, seq, d)
```

```python
import functools

import jax
import jax.numpy as jnp
import numpy as np
from jax import lax
from jax.experimental import pallas as pl
from jax.experimental.pallas import tpu as pltpu

F32 = jnp.float32
BF16 = jnp.bfloat16

LANES = 128
BF16_SUBLANES = 16
VMEM_LIMIT = 56 * 1024 * 1024

D_MODEL = 2048
RW_HEADS = 16
RW_HEAD = 64
RW_DIM = RW_HEADS * RW_HEAD
DECAY_LORA = 96
ICLR_LORA = 96
GATE_LORA = 256
GN_EPS = 64e-5
ATT_HEADS = 16
ATT_KV_HEADS = 4
ATT_GROUP = ATT_HEADS // ATT_KV_HEADS
ATT_HEAD = 64
ATT_Q_DIM = ATT_HEADS * ATT_HEAD
ATT_KV_DIM = ATT_KV_HEADS * ATT_HEAD
WINDOW = 128
BLOCK = 128
D_FF = 5632
RMS_EPS = 1e-6
NEG_BIG = -1e30
LOG2E = 1.4426950408889634
EXP_M_HALF = 0.6065306597126334

C_R = 0
C_K = RW_DIM
C_V = 2 * RW_DIM
C_Q = 3 * RW_DIM
C_GRW = C_Q + ATT_Q_DIM
C_GATT = C_GRW + D_MODEL
C_LORA = C_GATT + D_MODEL
LORA_W = 128
LORA_A = 128
LORA_COLS = LORA_W + LORA_A + GATE_LORA
C_KA = C_LORA + LORA_COLS
C_VA = C_KA + ATT_KV_DIM
Z_COLS = C_VA + ATT_KV_DIM

CHUNK = 64
PAIR = 2 * RW_HEAD
N_PAIRS = RW_DIM // PAIR


def _cparams(sem):
    return pltpu.CompilerParams(dimension_semantics=sem, vmem_limit_bytes=VMEM_LIMIT)


def _mm(a, b):
    return jnp.dot(a.astype(BF16), b.astype(BF16), preferred_element_type=F32)


def _mm_nt(a, b):
    return lax.dot_general(a.astype(BF16), b.astype(BF16), (((1,), (1,)), ((), ())),
                           preferred_element_type=F32)


def _mm_tn(a, b):
    return lax.dot_general(a.astype(BF16), b.astype(BF16), (((0,), (0,)), ((), ())),
                           preferred_element_type=F32)


def _mm_split(m01, x):
    hi = x.astype(BF16)
    lo = (x - hi.astype(F32)).astype(BF16)
    return (jnp.dot(m01, hi, preferred_element_type=F32)
            + jnp.dot(m01, lo, preferred_element_type=F32))


def _head_ones(n, head, scale):
    r = lax.broadcasted_iota(jnp.int32, (n, n), 0) // head
    c = lax.broadcasted_iota(jnp.int32, (n, n), 1) // head
    return jnp.where(r == c, scale, 0.0).astype(BF16)


def _w_in_layout_kernel(wt_ref, o_ref, *, moves, pad):
    o_ref[pad[0]:pad[1], :] = jnp.zeros((pad[1] - pad[0], o_ref.shape[1]), BF16)
    for dst, src, width in moves:
        o_ref[dst:dst + width, :] = wt_ref[src:src + width, :].astype(BF16)


def _w_in_layout(w_in_layers, layer, cols=256):
    _, d, n = w_in_layers.shape
    wt_layers = jnp.swapaxes(w_in_layers, 1, 2)
    o3 = 3 * RW_DIM
    o4 = o3 + DECAY_LORA
    o5 = o4 + ICLR_LORA
    o6 = o5 + GATE_LORA
    oq = o6 + ATT_Q_DIM
    ok = oq + ATT_KV_DIM
    ov = ok + ATT_KV_DIM
    moves = ((C_R, 0, o3), (C_Q, o6, ATT_Q_DIM), (C_GRW, ov, 2 * D_MODEL),
             (C_LORA, o3, DECAY_LORA), (C_LORA + LORA_W, o4, ICLR_LORA),
             (C_LORA + LORA_W + LORA_A, o5, GATE_LORA), (C_KA, oq, ATT_KV_DIM), (C_VA, ok, ATT_KV_DIM))
    return pl.pallas_call(
        functools.partial(_w_in_layout_kernel, moves=moves, pad=(C_LORA, C_LORA + LORA_W + LORA_A)),
        grid=(d // cols,),
        in_specs=[pl.BlockSpec((None, n, cols), lambda i: (layer, 0, i))],
        out_specs=pl.BlockSpec((Z_COLS, cols), lambda i: (0, i)),
        out_shape=jax.ShapeDtypeStruct((Z_COLS, d), BF16),
        compiler_params=_cparams(("parallel",)),
        name="w_in_layout",
    )(wt_layers)


def _in_proj_kernel(x_ref, g_ref, wt_ref, z_ref, h_ref):
    @pl.when(pl.program_id(1) == 0)
    def _():
        xf = x_ref[...]
        y = xf * lax.rsqrt(jnp.mean(xf * xf, axis=-1, keepdims=True) + RMS_EPS)
        h_ref[...] = (y * g_ref[...]).astype(BF16)

    z_ref[...] = _mm_nt(h_ref[...], wt_ref[...]).astype(BF16)


def _in_proj(x2, g1, w_in_t, tm=1024, tn=1536):
    m, d = x2.shape
    n = w_in_t.shape[0]
    return pl.pallas_call(
        _in_proj_kernel,
        grid=(m // tm, n // tn),
        in_specs=[pl.BlockSpec((tm, d), lambda i, j: (i, 0)),
                  pl.BlockSpec((1, d), lambda i, j: (0, 0)),
                  pl.BlockSpec((tn, d), lambda i, j: (j, 0))],
        out_specs=pl.BlockSpec((tm, tn), lambda i, j: (i, j)),
        out_shape=jax.ShapeDtypeStruct((m, n), BF16),
        scratch_shapes=[pltpu.VMEM((tm, d), BF16)],
        compiler_params=_cparams(("parallel", "arbitrary")),
        name="in_proj",
    )(x2, g1, w_in_t)


def _rwkv_prep_kernel(zr_ref, zk_ref, zv_ref, zl_ref, pr_ref, pk_ref, pv_ref, pli_ref,
                      mur_ref, muk_ref, muv_ref, mul_ref,
                      w0_ref, a0_ref, kk_ref, ka_ref, rk_ref, w2_ref, a2_ref, g2_ref,
                      at_ref, rt_ref, bt_ref, kt_ref, bg_ref, kg_ref, v_ref, g_ref, bonus_ref,
                      gT_ref, gm_ref, *, tm, first):
    keep = jnp.where(first, 0.0, 1.0)
    nc = tm // CHUNK
    last = slice(BF16_SUBLANES - 1, BF16_SUBLANES)

    def shifted(z, prev_last, mu):
        z = z.astype(F32)
        row = lax.broadcasted_iota(jnp.int32, z.shape, 0)
        zprev = jnp.where(row == 0, prev_last.astype(F32) * keep, pltpu.roll(z, 1, 0))
        return z + (zprev - z) * mu

    lo = shifted(zl_ref[...], pli_ref[last, :], mul_ref[...])
    tanh_wd = jnp.tanh(lo[:, :LORA_W]).astype(BF16)
    ad = lo[:, LORA_W:LORA_W + LORA_A].astype(BF16)
    sig_gd = jax.nn.sigmoid(lo[:, LORA_W + LORA_A:]).astype(BF16)

    ti = lax.broadcasted_iota(jnp.int32, (tm, tm), 0)
    si = lax.broadcasted_iota(jnp.int32, (tm, tm), 1)
    tri = jnp.where(((ti // CHUNK) == (si // CHUNK)) & (si <= ti), 1.0, 0.0).astype(BF16)
    ones_h = _head_ones(PAIR, RW_HEAD, 1.0)
    w_lora = jnp.dot(tanh_wd, w2_ref[...], preferred_element_type=F32)
    a_lora = jnp.dot(ad, a2_ref[...], preferred_element_type=F32)
    g_ref[...] = jnp.dot(sig_gd, g2_ref[...], preferred_element_type=F32).astype(BF16)
    yield

    stash = []
    for p in range(N_PAIRS):
        cs = slice(p * PAIR, (p + 1) * PAIR)
        r = shifted(zr_ref[:, cs], pr_ref[last, cs], mur_ref[:, cs])
        k = shifted(zk_ref[:, cs], pk_ref[last, cs], muk_ref[:, cs])
        v = shifted(zv_ref[:, cs], pv_ref[last, cs], muv_ref[:, cs])

        wpre = w0_ref[:, cs] + w_lora[:, cs]
        lw = -EXP_M_HALF * jax.nn.sigmoid(wpre)
        a = jax.nn.sigmoid(a0_ref[:, cs] + a_lora[:, cs])

        kk = k * kk_ref[:, cs]
        kk = kk * lax.rsqrt(jnp.maximum(_mm(kk * kk, ones_h), 1e-24))
        k2 = k * (1.0 + (a - 1.0) * ka_ref[:, cs])
        bonus_ref[:, cs] = (_mm(r * k2 * rk_ref[:, cs], ones_h) * v).astype(BF16)
        v_ref[:, cs] = v.astype(BF16)
        stash.append((r, k2, -kk, kk * a, lw))
        yield

    c_all = _mm_split(tri, jnp.concatenate([st[4] for st in stash], axis=1))
    yield

    for p in range(N_PAIRS):
        cs = slice(p * PAIR, (p + 1) * PAIR)
        r, k2, ka_vec, kb_vec, lw = stash[p]
        c = c_all[:, cs]
        c3 = c.reshape(nc, CHUNK, PAIR)
        c_mid = c3[:, CHUNK // 2 - 1:CHUNK // 2, :]
        c_end = c3[:, CHUNK - 1:CHUNK, :]
        cm = jnp.broadcast_to(c_mid, c3.shape).reshape(tm, PAIR)
        mid_to_end = jnp.broadcast_to(jnp.exp(c_end - c_mid), c3.shape).reshape(tm, PAIR)

        e_in = jnp.exp(c - cm)
        e_out = jnp.exp(cm - c)
        e_end = e_out * mid_to_end
        at_ref[:, cs] = (ka_vec * jnp.exp(c - lw - cm)).astype(BF16)
        rt_ref[:, cs] = (r * e_in).astype(BF16)
        bt_ref[:, cs] = (kb_vec * e_out).astype(BF16)
        kt_ref[:, cs] = (k2 * e_out).astype(BF16)
        bg_ref[:, cs] = (kb_vec * e_end).astype(BF16)
        kg_ref[:, cs] = (k2 * e_end).astype(BF16)
        gT_ref[:, :, cs] = jnp.exp(c_end)
        gm_ref[:, :, cs] = jnp.exp(c_mid)
        yield


def _rwkv_kernel(*refs, tb, nb, n_side):
    prep_in, (gnw_ref, gnb_ref), refs = refs[:20], refs[20:22], refs[22:]
    side_in, o_ref, side_out, scratch = refs[:n_side], refs[n_side], refs[n_side + 1:2 * n_side + 1], refs[2 * n_side + 1:]
    operands, s_ref = scratch[:11], scratch[11]
    j = pl.program_id(1)
    prep = functools.partial(_rwkv_prep_kernel, *prep_in, *operands, tm=tb, first=j == 0)
    chunk = functools.partial(_rwkv_chunk_kernel, *operands, gnw_ref, gnb_ref, o_ref, s_ref,
                              chunks=tb // CHUNK, pairs=N_PAIRS, first=j == 1)

    def side_cast():
        for w_ref, wb_ref in zip(side_in, side_out):
            wb_ref[...] = w_ref[...].astype(BF16)

    @pl.when(j == 0)
    def _():
        side_cast()
        for _ in prep():
            pass

    @pl.when((j > 0) & (j < nb))
    def _():
        side_cast()
        chunk(filler=prep())

    @pl.when(j == nb)
    def _():
        side_cast()
        chunk()


def _rwkv(z, batch, seq, mu_r, mu_k, mu_v, mu_l, w0, a0, k_k, k_a, r_k, w2p, a2p, g2, gn_w, gn_b,
          side_weights, tb=128, side_rows=32):
    m = z.shape[0]
    nb = seq // tb
    for w in side_weights:
        assert w.shape[0] // side_rows <= batch * (nb + 1), "not enough grid steps to cast this weight"
    side_specs = [pl.BlockSpec((side_rows, w.shape[1]),
                               lambda b, j, last=w.shape[0] // side_rows - 1:
                               (jnp.minimum(b * (nb + 1) + j, last), 0)) for w in side_weights]
    pb = tb // BF16_SUBLANES
    blk = lambda b, j: b * nb + jnp.minimum(j, nb - 1)

    def cur(col0, width):
        return pl.BlockSpec((tb, width), lambda b, j, c0=col0 // width: (blk(b, j), c0))

    def prev(col0, width):
        return pl.BlockSpec((BF16_SUBLANES, width),
                            lambda b, j, c0=col0 // width: (jnp.maximum(blk(b, j) * pb - 1, 0), c0))

    def whole(rows, width):
        return pl.BlockSpec((rows, width), lambda b, j: (0, 0))

    in_specs = [cur(C_R, RW_DIM), cur(C_K, RW_DIM), cur(C_V, RW_DIM), cur(C_LORA, LORA_COLS),
                prev(C_R, RW_DIM), prev(C_K, RW_DIM), prev(C_V, RW_DIM), prev(C_LORA, LORA_COLS),
                whole(1, RW_DIM), whole(1, RW_DIM), whole(1, RW_DIM), whole(1, LORA_COLS),
                whole(1, RW_DIM), whole(1, RW_DIM), whole(1, RW_DIM), whole(1, RW_DIM), whole(1, RW_DIM),
                whole(LORA_W, RW_DIM), whole(LORA_A, RW_DIM), whole(GATE_LORA, RW_DIM),
                whole(1, RW_DIM), whole(1, RW_DIM)]
    big = pltpu.VMEM((tb, RW_DIM), BF16)
    per_chunk = pltpu.VMEM((tb // CHUNK, 1, RW_DIM), F32)
    outs = pl.pallas_call(
        functools.partial(_rwkv_kernel, tb=tb, nb=nb, n_side=len(side_weights)),
        grid=(batch, nb + 1),
        in_specs=in_specs + side_specs,
        out_specs=[pl.BlockSpec((tb, RW_DIM), lambda b, j: (b * nb + jnp.maximum(j - 1, 0), 0))] + side_specs,
        out_shape=[jax.ShapeDtypeStruct((m, RW_DIM), BF16)]
        + [jax.ShapeDtypeStruct(w.shape, BF16) for w in side_weights],
        scratch_shapes=[big] * 9 + [per_chunk] * 2 + [pltpu.VMEM((N_PAIRS, PAIR, PAIR), F32)],
        compiler_params=_cparams(("arbitrary", "arbitrary")),
        name="rwkv",
    )(z, z, z, z, z, z, z, z, mu_r, mu_k, mu_v, mu_l, w0, a0, k_k, k_a, r_k, w2p, a2p, g2, gn_w, gn_b,
      *side_weights)
    return outs[0], outs[1:]


def _rwkv_chunk_kernel(at_ref, rt_ref, bt_ref, kt_ref, bg_ref, kg_ref, v_ref, g_ref, bonus_ref,
                       gT_ref, gm_ref, gnw_ref, gnb_ref, o_ref, s_ref, *, chunks, pairs, first,
                       filler=None):
    @pl.when(first)
    def _():
        s_ref[...] = jnp.zeros_like(s_ref)

    T = CHUNK
    lane = lax.broadcasted_iota(jnp.int32, (T, PAIR), 1)
    head0 = lane < RW_HEAD
    ri = lax.broadcasted_iota(jnp.int32, (2 * T, 2 * T), 0)
    ci = lax.broadcasted_iota(jnp.int32, (2 * T, 2 * T), 1)
    same = (ri // T) == (ci // T)
    strict = same & ((ri % T) > (ci % T))
    incl = same & ((ri % T) >= (ci % T))
    eye = jnp.where(ri == ci, 1.0, 0.0)
    own = (ri // T) == (ci // RW_HEAD)

    items = [(p, c) for c in range(chunks) for p in range(pairs)]
    rows = lambda c: slice(c * T, (c + 1) * T)
    cols = lambda p: slice(p * PAIR, (p + 1) * PAIR)

    def stack(ref):
        out = []
        for p, c in items:
            x = ref[rows(c), cols(p)]
            zero = jnp.zeros_like(x)
            out.append(jnp.concatenate([jnp.where(head0, x, zero), jnp.where(head0, zero, x)], axis=0))
        return out

    each = lambda f, *ls: [f(*xs) for xs in zip(*ls)]
    La, Lr, Rb, Rk = stack(at_ref), stack(rt_ref), stack(bt_ref), stack(kt_ref)
    Rbg, Rkg, Vs = stack(bg_ref), stack(kg_ref), stack(v_ref)
    gm_row = [gm_ref[c][:, cols(p)] for p, c in items]
    gT_row = [gT_ref[c][:, cols(p)] for p, c in items]
    gate = [g_ref[rows(c), cols(p)] for p, c in items]
    bonus = [bonus_ref[rows(c), cols(p)] for p, c in items]
    filler = iter(()) if filler is None else filler
    tick = lambda: next(filler, None)

    AA = each(lambda la, lr, rb, rk: _mm_nt(jnp.concatenate([la, lr], axis=0),
                                            jnp.concatenate([rb, rk], axis=0)), La, Lr, Rb, Rk)
    tick()
    N = [jnp.where(strict, aa[:2 * T, :2 * T], 0.0) for aa in AA]
    Aak = [jnp.where(strict, aa[:2 * T, 2 * T:], 0.0) for aa in AA]
    Arb = [jnp.where(incl, aa[2 * T:, :2 * T], 0.0) for aa in AA]
    Ark = [jnp.where(incl, aa[2 * T:, 2 * T:], 0.0) for aa in AA]
    def live_rows(x, t0):
        return x if t0 == 0 else jnp.concatenate([x[t0:T, :], x[T + t0:, :]], axis=0)

    def all_rows(y, t0):
        if t0 == 0:
            return y
        zero = jnp.zeros((t0, y.shape[1]), y.dtype)
        return jnp.concatenate([zero, y[:T - t0, :], zero, y[T - t0:, :]], axis=0)

    def skip(span):
        return span if span % BF16_SUBLANES == 0 else 0

    def blockdiag(a, b):
        zero = jnp.zeros_like(a)
        return jnp.concatenate([jnp.concatenate([a, zero], axis=1),
                                jnp.concatenate([zero, b], axis=1)], axis=0)

    groups = [(i, i + 1) for i in range(0, len(items), 2)]
    W = each(lambda n: eye + n, N)
    P = [n.astype(BF16) for n in N]
    for a, b in groups:
        sq = _mm(jnp.concatenate([P[a], P[b]], axis=1), blockdiag(P[a], P[b]))
        P[a], P[b] = sq[:, :PAIR].astype(BF16), sq[:, PAIR:].astype(BF16)
    tick()
    span = 2
    while span < T:
        last = 2 * span >= T
        t0, t1 = skip(span), skip(2 * span)
        for a, b in groups:
            lhs = jnp.concatenate([live_rows(W[a].astype(BF16), t0), live_rows(W[b].astype(BF16), t0)], axis=1)
            if not last:
                lhs_p = jnp.concatenate([live_rows(P[a], t1), live_rows(P[b], t1)], axis=1)
                lhs = jnp.concatenate([lhs_p, lhs], axis=0)
            res = _mm(lhs, blockdiag(P[a], P[b]))
            if not last:
                sq = all_rows(res[:lhs_p.shape[0], :], t1)
                P[a], P[b] = sq[:, :PAIR].astype(BF16), sq[:, PAIR:].astype(BF16)
                res = res[lhs_p.shape[0]:, :]
            wp = all_rows(res, t0)
            W[a] = W[a] + wp[:, :PAIR]
            W[b] = W[b] + wp[:, PAIR:]
        tick()
        span *= 2
    AkV = each(_mm, Aak, Vs)
    tick()
    X = each(lambda w, la, akv: _mm(w, jnp.concatenate([la, akv.astype(BF16)], axis=1)).astype(BF16),
             W, La, AkV)
    tick()
    Z = each(lambda arb, ark, x, vs:
             _mm(jnp.concatenate([arb.astype(BF16), ark.astype(BF16)], axis=1),
                 jnp.concatenate([x, jnp.concatenate([jnp.zeros_like(vs), vs], axis=1)], axis=0)),
             Arb, Ark, X, Vs)
    tick()
    Q = each(lambda z, lr, gm: (z[:, :PAIR] + lr.astype(F32)) * gm, Z, Lr, gm_row)
    Y0 = [z[:, PAIR:] for z in Z]
    tick()
    Mbd = each(lambda rbg, x, gm: _mm_tn(rbg, x[:, :PAIR]) * gm, Rbg, X, gm_row)
    NcT = each(lambda x, vs, rbg, rkg: _mm_tn(jnp.concatenate([x[:, PAIR:], vs], axis=0),
                                              jnp.concatenate([rbg, rkg], axis=0)), X, Vs, Rbg, Rkg)
    for _ in filler:
        pass

    S = [s_ref[p] for p in range(pairs)]
    for c in range(chunks):
        for p in range(pairs):
            i = c * pairs + p
            Ys = _mm_nt(Q[i], S[p]) + Y0[i]
            S[p] = S[p] * gT_row[i] + _mm_nt(S[p], Mbd[i]) + NcT[i]
            mu = jnp.sum(Ys, axis=-1, keepdims=True) * (1.0 / RW_HEAD)
            d = jnp.where(own, Ys - mu, 0.0)
            var = jnp.sum(d * d, axis=-1, keepdims=True) * (1.0 / RW_HEAD)
            dn = d * lax.rsqrt(var + GN_EPS)
            yn = (dn[:T, :] + dn[T:, :]) * gnw_ref[:, cols(p)] + gnb_ref[:, cols(p)]
            out = (yn + bonus[i].astype(F32)) * gate[i].astype(F32)
            o_ref[rows(c), cols(p)] = out.astype(BF16)
    for p in range(pairs):
        s_ref[p] = S[p]


def _swa_kernel(sink_ref, q_ref, kc_ref, kp_ref, vc_ref, vp_ref, qg_ref, kg_ref, o_ref, *,
                steps_per_seq, slopes, qb):
    first = (pl.program_id(0) % steps_per_seq) == 0
    mean_h = _head_ones(LANES, ATT_HEAD, 1.0 / ATT_HEAD)

    def head_rms(x, gain):
        parts = []
        for b in range(x.shape[1] // LANES):
            xb = x[:, b * LANES:(b + 1) * LANES]
            ms = _mm(xb * xb, mean_h)
            parts.append(xb * lax.rsqrt(ms + RMS_EPS))
        return jnp.concatenate(parts, axis=1) * gain

    q = head_rms(q_ref[...].astype(F32), qg_ref[...]) * (ATT_HEAD ** -0.5 * LOG2E)
    q = q.astype(BF16)
    kcat = jnp.concatenate([kp_ref[...], kc_ref[...]], axis=0).astype(F32)
    kcat = head_rms(kcat, kg_ref[...]).astype(BF16)
    vcat = jnp.concatenate([vp_ref[...], vc_ref[...]], axis=0)

    qi = lax.broadcasted_iota(jnp.int32, (BLOCK, 2 * BLOCK), 0)
    kj = lax.broadcasted_iota(jnp.int32, (BLOCK, 2 * BLOCK), 1)
    dist_i = BLOCK + qi - kj
    in_window = (dist_i >= 0) & (dist_i < WINDOW)
    first_key = jnp.where(first, BLOCK, 0)
    neg_dist_rest = jnp.where(in_window, -dist_i.astype(F32), NEG_BIG)
    neg_dist_first = jnp.where(in_window & (kj >= first_key), -dist_i.astype(F32), NEG_BIG)
    neg_dist = [neg_dist_first] + [neg_dist_rest] * (qb - 1)

    lane_q = lax.broadcasted_iota(jnp.int32, (BLOCK, LANES), 1)
    lane_kv = lax.broadcasted_iota(jnp.int32, (2 * BLOCK, LANES), 1)
    zero_q = jnp.zeros((BLOCK, LANES), BF16)
    one_kv = jnp.ones((2 * BLOCK, LANES), BF16)
    kv_lane_blocks = range(ATT_KV_DIM // LANES)
    k_all = [kcat[:, b * LANES:(b + 1) * LANES] for b in kv_lane_blocks]
    k_all_rolled = [pltpu.roll(kb, ATT_HEAD, 1) for kb in k_all]
    window = lambda x, t: x[t * BLOCK:(t + 2) * BLOCK, :]
    v_aug = [[jnp.where((lane_kv // ATT_HEAD) == (j % 2),
                        window(vcat[:, (j // 2) * LANES:(j // 2 + 1) * LANES], t), one_kv)
              for j in range(ATT_KV_HEADS)] for t in range(qb)]
    items = [(t, h) for t in range(qb) for h in range(ATT_HEADS)]
    kv_of = lambda h: h // ATT_GROUP
    aligned = lambda h: (h % 2) == (kv_of(h) % 2)

    def scores(t, h):
        qh = jnp.where((lane_q // ATT_HEAD) == (h % 2),
                       q[t * BLOCK:(t + 1) * BLOCK, (h // 2) * LANES:(h // 2 + 1) * LANES], zero_q)
        kh = window((k_all if aligned(h) else k_all_rolled)[kv_of(h) // 2], t)
        return _mm_nt(qh, kh) + (slopes[h] * LOG2E) * neg_dist[t]

    s = [scores(t, h) for t, h in items]
    sink = [sink_ref[h] * LOG2E for _, h in items]
    mx = [jnp.maximum(jnp.max(s_, axis=-1, keepdims=True), sk) for s_, sk in zip(s, sink)]
    p = [jnp.exp2(s_ - m_) for s_, m_ in zip(s, mx)]
    pv = [_mm(p_, v_aug[t][kv_of(h)]) for p_, (t, h) in zip(p, items)]
    pv_sw = [pltpu.roll(x, ATT_HEAD, 1) for x in pv]
    out = []
    for i, (t, h) in enumerate(items):
        num, den = (pv[i], pv_sw[i]) if aligned(h) else (pv_sw[i], pv[i])
        out.append(num * (1.0 / (den + jnp.exp2(sink[i] - mx[i]))))
    for t in range(qb):
        o = out[t * ATT_HEADS:(t + 1) * ATT_HEADS]
        o_ref[t * BLOCK:(t + 1) * BLOCK, :] = jnp.concatenate(
            [jnp.where((lane_q // ATT_HEAD) == 0, o[2 * m], o[2 * m + 1]) for m in range(ATT_HEADS // 2)],
            axis=1).astype(BF16)


def _swa(z, sinks, q_gain_t, k_gain_t, seq, qb=8):
    m = z.shape[0]
    rows = qb * BLOCK
    slopes = tuple(float(s) for s in
                   np.exp2(-8.0 * np.arange(1, ATT_HEADS + 1, dtype=np.float32) / ATT_HEADS).astype(np.float32))
    kv_cur = lambda c0: pl.BlockSpec((rows, ATT_KV_DIM), lambda n, c=c0 // ATT_KV_DIM: (n, c))
    kv_prev = lambda c0: pl.BlockSpec((BLOCK, ATT_KV_DIM),
                                      lambda n, c=c0 // ATT_KV_DIM: (jnp.maximum(n * qb - 1, 0), c))
    return pl.pallas_call(
        functools.partial(_swa_kernel, steps_per_seq=seq // rows, slopes=slopes, qb=qb),
        grid=(m // rows,),
        in_specs=[pl.BlockSpec(memory_space=pltpu.SMEM),
                  pl.BlockSpec((rows, ATT_Q_DIM), lambda n: (n, C_Q // ATT_Q_DIM)),
                  kv_cur(C_KA), kv_prev(C_KA), kv_cur(C_VA), kv_prev(C_VA),
                  pl.BlockSpec((1, ATT_Q_DIM), lambda n: (0, 0)),
                  pl.BlockSpec((1, ATT_KV_DIM), lambda n: (0, 0))],
        out_specs=pl.BlockSpec((rows, ATT_Q_DIM), lambda n: (n, 0)),
        out_shape=jax.ShapeDtypeStruct((m, ATT_Q_DIM), BF16),
        compiler_params=_cparams(("parallel",)),
        name="swa",
    )(sinks, z, z, z, z, z, q_gain_t, k_gain_t)


def _mix_out_kernel(orw_ref, oatt_ref, wb1_ref, wb2_ref, zg_ref, wo_ref, x_ref, g_ref, x1_ref, h2_ref):
    p_rw = jnp.dot(orw_ref[...], wb1_ref[...], preferred_element_type=F32)
    p_att = jnp.dot(oatt_ref[...], wb2_ref[...], preferred_element_type=F32)
    g_rw = jax.nn.sigmoid(zg_ref[:, :D_MODEL].astype(F32))
    g_att = jax.nn.sigmoid(zg_ref[:, D_MODEL:].astype(F32))
    mix = (g_rw * p_rw + g_att * p_att).astype(BF16)
    x1 = x_ref[...] + jnp.dot(mix, wo_ref[...], preferred_element_type=F32)
    x1_ref[...] = x1
    y = x1 * lax.rsqrt(jnp.mean(x1 * x1, axis=-1, keepdims=True) + RMS_EPS)
    h2_ref[...] = (y * g_ref[...]).astype(BF16)


def _mix_out(o_rw, o_att, w_branch_b, z, w_out_b, x2, g2, tm=512):
    m = x2.shape[0]
    const = lambda rows, cols, r0=0: pl.BlockSpec((rows, cols), lambda i: (r0, 0),
                                                  pipeline_mode=pl.Buffered(1))
    rows = lambda cols, c0=0: pl.BlockSpec((tm, cols), lambda i: (i, c0))
    return pl.pallas_call(
        _mix_out_kernel,
        grid=(m // tm,),
        in_specs=[rows(RW_DIM), rows(ATT_Q_DIM), const(RW_DIM, D_MODEL), const(ATT_Q_DIM, D_MODEL, 1),
                  rows(2 * D_MODEL, C_GRW // (2 * D_MODEL)), const(D_MODEL, D_MODEL),
                  rows(D_MODEL), const(1, D_MODEL)],
        out_specs=[rows(D_MODEL), rows(D_MODEL)],
        out_shape=[jax.ShapeDtypeStruct((m, D_MODEL), F32),
                   jax.ShapeDtypeStruct((m, D_MODEL), BF16)],
        compiler_params=_cparams(("parallel",)),
        name="mix_out",
    )(o_rw, o_att, w_branch_b, w_branch_b, z, w_out_b, x2, g2)


def _ffn_up_kernel(h_ref, wvb_ref, wgb_ref, cwv_ref, cwg_ref, cbv_ref, cbg_ref, wd_ref, a_ref, wdb_ref,
                   cv_ref, cg_ref, *, tm, sub, tiles_per_seq):
    halo = 8
    wdb_ref[...] = wd_ref[...].astype(BF16)

    @pl.when((pl.program_id(1) % tiles_per_seq) == 0)
    def _():
        cv_ref[...] = jnp.zeros_like(cv_ref)
        cg_ref[...] = jnp.zeros_like(cg_ref)

    row = lax.broadcasted_iota(jnp.int32, (halo, a_ref.shape[1]), 0)

    def conv(u, carry_ref, cw_ref, cb_ref):
        p1 = carry_ref[halo - 1:halo, :]
        p2 = carry_ref[halo - 2:halo - 1, :]
        r1 = pltpu.roll(u, 1, 0)
        r2 = pltpu.roll(u, 2, 0)
        prev1 = jnp.concatenate([jnp.where(row == 0, p1, r1[:halo, :]), r1[halo:, :]], axis=0)
        prev2 = jnp.concatenate(
            [jnp.where(row == 0, p2, jnp.where(row == 1, p1, r2[:halo, :])), r2[halo:, :]], axis=0)
        carry_ref[...] = u[sub - halo:sub, :]
        w = cw_ref[...].astype(BF16)
        return (prev2.astype(BF16) * w[0:1, :] + prev1.astype(BF16) * w[1:2, :] + u.astype(BF16) * w[2:3, :]
                + cb_ref[...].astype(BF16))

    for s in range(tm // sub):
        h = h_ref[s * sub:(s + 1) * sub, :]
        val = conv(jnp.dot(h, wvb_ref[...], preferred_element_type=F32), cv_ref, cwv_ref, cbv_ref)
        gate = conv(jnp.dot(h, wgb_ref[...], preferred_element_type=F32), cg_ref, cwg_ref, cbg_ref)
        a_ref[s * sub:(s + 1) * sub, :] = (gate * jax.nn.sigmoid(gate) * val).astype(BF16)


def _ffn_up(h2, w_up_b, conv_w, conv_b, w_down, seq, tm=4096, tn=512, sub=256):
    m = h2.shape[0]
    nj = D_FF // tn
    ni = m // tm
    wd_rows = w_down.shape[0] // (nj * ni)
    wd_spec = pl.BlockSpec((wd_rows, D_MODEL), lambda j, i: (j * ni + i, 0))
    return pl.pallas_call(
        functools.partial(_ffn_up_kernel, tm=tm, sub=sub, tiles_per_seq=seq // tm),
        grid=(nj, ni),
        in_specs=[pl.BlockSpec((tm, D_MODEL), lambda j, i: (i, 0)),
                  pl.BlockSpec((D_MODEL, tn), lambda j, i: (0, j)),
                  pl.BlockSpec((D_MODEL, tn), lambda j, i: (0, nj + j)),
                  pl.BlockSpec((3, tn), lambda j, i: (0, j)),
                  pl.BlockSpec((3, tn), lambda j, i: (0, nj + j)),
                  pl.BlockSpec((1, tn), lambda j, i: (0, j)),
                  pl.BlockSpec((1, tn), lambda j, i: (0, nj + j)),
                  wd_spec],
        out_specs=[pl.BlockSpec((tm, tn), lambda j, i: (i, j)), wd_spec],
        out_shape=[jax.ShapeDtypeStruct((m, D_FF), BF16), jax.ShapeDtypeStruct(w_down.shape, BF16)],
        scratch_shapes=[pltpu.VMEM((8, tn), F32), pltpu.VMEM((8, tn), F32)],
        compiler_params=_cparams(("parallel", "arbitrary")),
        name="ffn_up",
    )(h2, w_up_b, w_up_b, conv_w, conv_w, conv_b, conv_b, w_down)


def _ffn_down_kernel(a_ref, w_ref, x1_ref, o_ref):
    o_ref[...] = x1_ref[...] + jnp.dot(a_ref[...], w_ref[...], preferred_element_type=F32)


def _ffn_down(act, w_down_b, x1, tm=512, tn=1024):
    m = act.shape[0]
    return pl.pallas_call(
        _ffn_down_kernel,
        grid=(D_MODEL // tn, m // tm),
        in_specs=[pl.BlockSpec((tm, D_FF), lambda j, i: (i, 0)),
                  pl.BlockSpec((D_FF, tn), lambda j, i: (0, j)),
                  pl.BlockSpec((tm, tn), lambda j, i: (i, j))],
        out_specs=pl.BlockSpec((tm, tn), lambda j, i: (i, j)),
        out_shape=jax.ShapeDtypeStruct((m, D_MODEL), F32),
        compiler_params=_cparams(("parallel", "arbitrary")),
        name="ffn_down",
    )(act, w_down_b, x1)


def _pad_cols(w, n):
    return jnp.pad(w, ((0, 0), (0, n - w.shape[1])))


def _pad_rows(w, n):
    return jnp.pad(w, ((0, n - w.shape[0]), (0, 0)))


def _layer(x2, batch, seq, w_in_p, norm1_g, rw_mu, rw_w0, rw_w2, rw_a0, rw_a2, rw_g2, rw_k_k, rw_k_a,
           rw_r_k, rw_gn_w, rw_gn_b, q_norm_g, k_norm_g, attn_sinks, w_branch, w_out,
           norm2_g, w_up, conv_w, conv_b, w_down):
    row = lambda v: v.reshape(1, -1).astype(F32)
    o3 = 3 * RW_DIM
    o4 = o3 + DECAY_LORA
    o5 = o4 + ICLR_LORA
    o6 = o5 + GATE_LORA
    mu = rw_mu.reshape(1, -1)
    mu_l = jnp.concatenate([_pad_cols(mu[:, o3:o4], LORA_W), _pad_cols(mu[:, o4:o5], LORA_A),
                            mu[:, o5:o6]], axis=1)
    w2p = _pad_rows(rw_w2, LORA_W).astype(BF16)
    a2p = _pad_rows(rw_a2, LORA_A).astype(BF16)

    z = _in_proj(x2, row(norm1_g), w_in_p)
    o_rw, (w_branch_b, w_out_b, w_up_b) = _rwkv(
        z, batch, seq, mu[:, :RW_DIM], mu[:, RW_DIM:2 * RW_DIM], mu[:, 2 * RW_DIM:o3], mu_l,
        row(rw_w0), row(rw_a0), row(rw_k_k), row(rw_k_a), row(rw_r_k),
        w2p, a2p, rw_g2.astype(BF16), row(rw_gn_w), row(rw_gn_b), (w_branch, w_out, w_up))
    o_att = _swa(z, attn_sinks.astype(F32), jnp.tile(row(q_norm_g), (1, ATT_HEADS)),
                 jnp.tile(row(k_norm_g), (1, ATT_KV_HEADS)), seq)
    x1, h2 = _mix_out(o_rw, o_att, w_branch_b, z, w_out_b, x2, row(norm2_g))
    act, w_down_b = _ffn_up(h2, w_up_b, conv_w.astype(F32), row(conv_b), w_down, seq)
    return _ffn_down(act, w_down_b, x1)


def kernel(x, norm1_g, w_in, rw_mu, rw_w0, rw_w2, rw_a0, rw_a2, rw_g2, rw_k_k, rw_k_a, rw_r_k,
           rw_gn_w, rw_gn_b, q_norm_g, k_norm_g, attn_sinks, w_branch, w_out, norm2_g, w_up,
           conv_w, conv_b, w_down):
    batch, seq, d = x.shape
    x2 = x.reshape(batch * seq, d)
    params = (norm1_g, rw_mu, rw_w0, rw_w2, rw_a0, rw_a2, rw_g2, rw_k_k, rw_k_a, rw_r_k,
              rw_gn_w, rw_gn_b, q_norm_g, k_norm_g, attn_sinks, w_branch, w_out, norm2_g, w_up,
              conv_w, conv_b, w_down)
    for layer in range(norm1_g.shape[0]):
        x2 = _layer(x2, batch, seq, _w_in_layout(w_in, layer), *(p[layer] for p in params))
    return x2.reshape(batch, seq, d)
```

```python
import functools

import jax
import jax.numpy as jnp
import numpy as np
from jax import lax
from jax.experimental import pallas as pl
from jax.experimental.pallas import tpu as pltpu

F32 = jnp.float32
BF16 = jnp.bfloat16

LANES = 128
BF16_SUBLANES = 16
VMEM_LIMIT = 56 * 1024 * 1024

D_MODEL = 2048
RW_HEADS = 16
RW_HEAD = 64
RW_DIM = RW_HEADS * RW_HEAD
DECAY_LORA = 96
ICLR_LORA = 96
GATE_LORA = 256
GN_EPS = 64e-5
ATT_HEADS = 16
ATT_KV_HEADS = 4
ATT_GROUP = ATT_HEADS // ATT_KV_HEADS
ATT_HEAD = 64
ATT_Q_DIM = ATT_HEADS * ATT_HEAD
ATT_KV_DIM = ATT_KV_HEADS * ATT_HEAD
WINDOW = 128
BLOCK = 128
D_FF = 5632
RMS_EPS = 1e-6
NEG_BIG = -1e30
LOG2E = 1.4426950408889634
EXP_M_HALF = 0.6065306597126334

C_R = 0
C_K = RW_DIM
C_V = 2 * RW_DIM
C_Q = 3 * RW_DIM
C_GRW = C_Q + ATT_Q_DIM
C_GATT = C_GRW + D_MODEL
C_LORA = C_GATT + D_MODEL
LORA_W = 128
LORA_A = 128
LORA_COLS = LORA_W + LORA_A + GATE_LORA
C_KA = C_LORA + LORA_COLS
C_VA = C_KA + ATT_KV_DIM
Z_COLS = C_VA + ATT_KV_DIM

CHUNK = 64
PAIR = 2 * RW_HEAD
N_PAIRS = RW_DIM // PAIR


def _cparams(sem):
    return pltpu.CompilerParams(dimension_semantics=sem, vmem_limit_bytes=VMEM_LIMIT)


def _mm(a, b):
    return jnp.dot(a.astype(BF16), b.astype(BF16), preferred_element_type=F32)


def _mm_nt(a, b):
    return lax.dot_general(a.astype(BF16), b.astype(BF16), (((1,), (1,)), ((), ())),
                           preferred_element_type=F32)


def _mm_tn(a, b):
    return lax.dot_general(a.astype(BF16), b.astype(BF16), (((0,), (0,)), ((), ())),
                           preferred_element_type=F32)


def _mm_split(m01, x):
    hi = x.astype(BF16)
    lo = (x - hi.astype(F32)).astype(BF16)
    return (jnp.dot(m01, hi, preferred_element_type=F32)
            + jnp.dot(m01, lo, preferred_element_type=F32))


def _head_ones(n, head, scale):
    r = lax.broadcasted_iota(jnp.int32, (n, n), 0) // head
    c = lax.broadcasted_iota(jnp.int32, (n, n), 1) // head
    return jnp.where(r == c, scale, 0.0).astype(BF16)


def _w_in_layout_kernel(wt_ref, o_ref, *, moves, pad):
    o_ref[pad[0]:pad[1], :] = jnp.zeros((pad[1] - pad[0], o_ref.shape[1]), BF16)
    for dst, src, width in moves:
        o_ref[dst:dst + width, :] = wt_ref[src:src + width, :].astype(BF16)


def _w_in_layout(w_in_layers, layer, cols=256):
    _, d, n = w_in_layers.shape
    wt_layers = jnp.swapaxes(w_in_layers, 1, 2)
    o3 = 3 * RW_DIM
    o4 = o3 + DECAY_LORA
    o5 = o4 + ICLR_LORA
    o6 = o5 + GATE_LORA
    oq = o6 + ATT_Q_DIM
    ok = oq + ATT_KV_DIM
    ov = ok + ATT_KV_DIM
    moves = ((C_R, 0, o3), (C_Q, o6, ATT_Q_DIM), (C_GRW, ov, 2 * D_MODEL),
             (C_LORA, o3, DECAY_LORA), (C_LORA + LORA_W, o4, ICLR_LORA),
             (C_LORA + LORA_W + LORA_A, o5, GATE_LORA), (C_KA, oq, ATT_KV_DIM), (C_VA, ok, ATT_KV_DIM))
    return pl.pallas_call(
        functools.partial(_w_in_layout_kernel, moves=moves, pad=(C_LORA, C_LORA + LORA_W + LORA_A)),
        grid=(d // cols,),
        in_specs=[pl.BlockSpec((None, n, cols), lambda i: (layer, 0, i))],
        out_specs=pl.BlockSpec((Z_COLS, cols), lambda i: (0, i)),
        out_shape=jax.ShapeDtypeStruct((Z_COLS, d), BF16),
        compiler_params=_cparams(("parallel",)),
        name="w_in_layout",
    )(wt_layers)


def _in_proj_kernel(x_ref, g_ref, wt_ref, z_ref, h_ref):
    @pl.when(pl.program_id(1) == 0)
    def _():
        xf = x_ref[...]
        y = xf * lax.rsqrt(jnp.mean(xf * xf, axis=-1, keepdims=True) + RMS_EPS)
        h_ref[...] = (y * g_ref[...]).astype(BF16)

    z_ref[...] = _mm_nt(h_ref[...], wt_ref[...]).astype(BF16)


def _in_proj(x2, g1, w_in_t, tm=1024, tn=1536):
    m, d = x2.shape
    n = w_in_t.shape[0]
    return pl.pallas_call(
        _in_proj_kernel,
        grid=(m // tm, n // tn),
        in_specs=[pl.BlockSpec((tm, d), lambda i, j: (i, 0)),
                  pl.BlockSpec((1, d), lambda i, j: (0, 0)),
                  pl.BlockSpec((tn, d), lambda i, j: (j, 0))],
        out_specs=pl.BlockSpec((tm, tn), lambda i, j: (i, j)),
        out_shape=jax.ShapeDtypeStruct((m, n), BF16),
        scratch_shapes=[pltpu.VMEM((tm, d), BF16)],
        compiler_params=_cparams(("parallel", "arbitrary")),
        name="in_proj",
    )(x2, g1, w_in_t)


def _rwkv_prep_kernel(zr_ref, zk_ref, zv_ref, zl_ref, pr_ref, pk_ref, pv_ref, pli_ref,
                      mur_ref, muk_ref, muv_ref, mul_ref,
                      w0_ref, a0_ref, kk_ref, ka_ref, rk_ref, w2_ref, a2_ref, g2_ref,
                      at_ref, rt_ref, bt_ref, kt_ref, bg_ref, kg_ref, v_ref, g_ref, bonus_ref,
                      gT_ref, gm_ref, *, tm, first):
    keep = jnp.where(first, 0.0, 1.0)
    nc = tm // CHUNK
    last = slice(BF16_SUBLANES - 1, BF16_SUBLANES)

    def shifted(z, prev_last, mu):
        z = z.astype(F32)
        row = lax.broadcasted_iota(jnp.int32, z.shape, 0)
        zprev = jnp.where(row == 0, prev_last.astype(F32) * keep, pltpu.roll(z, 1, 0))
        return z + (zprev - z) * mu

    lo = shifted(zl_ref[...], pli_ref[last, :], mul_ref[...])
    tanh_wd = jnp.tanh(lo[:, :LORA_W]).astype(BF16)
    ad = lo[:, LORA_W:LORA_W + LORA_A].astype(BF16)
    sig_gd = jax.nn.sigmoid(lo[:, LORA_W + LORA_A:]).astype(BF16)

    ti = lax.broadcasted_iota(jnp.int32, (tm, tm), 0)
    si = lax.broadcasted_iota(jnp.int32, (tm, tm), 1)
    tri = jnp.where(((ti // CHUNK) == (si // CHUNK)) & (si <= ti), 1.0, 0.0).astype(BF16)
    ones_h = _head_ones(PAIR, RW_HEAD, 1.0)
    w_lora = jnp.dot(tanh_wd, w2_ref[...], preferred_element_type=F32)
    a_lora = jnp.dot(ad, a2_ref[...], preferred_element_type=F32)
    g_ref[...] = jnp.dot(sig_gd, g2_ref[...], preferred_element_type=F32).astype(BF16)
    yield

    stash = []
    for p in range(N_PAIRS):
        cs = slice(p * PAIR, (p + 1) * PAIR)
        r = shifted(zr_ref[:, cs], pr_ref[last, cs], mur_ref[:, cs])
        k = shifted(zk_ref[:, cs], pk_ref[last, cs], muk_ref[:, cs])
        v = shifted(zv_ref[:, cs], pv_ref[last, cs], muv_ref[:, cs])

        wpre = w0_ref[:, cs] + w_lora[:, cs]
        lw = -EXP_M_HALF * jax.nn.sigmoid(wpre)
        a = jax.nn.sigmoid(a0_ref[:, cs] + a_lora[:, cs])

        kk = k * kk_ref[:, cs]
        kk = kk * lax.rsqrt(jnp.maximum(_mm(kk * kk, ones_h), 1e-24))
        k2 = k * (1.0 + (a - 1.0) * ka_ref[:, cs])
        bonus_ref[:, cs] = (_mm(r * k2 * rk_ref[:, cs], ones_h) * v).astype(BF16)
        v_ref[:, cs] = v.astype(BF16)
        stash.append((r, k2, -kk, kk * a, lw))
        yield

    c_all = _mm_split(tri, jnp.concatenate([st[4] for st in stash], axis=1))
    yield

    for p in range(N_PAIRS):
        cs = slice(p * PAIR, (p + 1) * PAIR)
        r, k2, ka_vec, kb_vec, lw = stash[p]
        c = c_all[:, cs]
        c3 = c.reshape(nc, CHUNK, PAIR)
        c_mid = c3[:, CHUNK // 2 - 1:CHUNK // 2, :]
        c_end = c3[:, CHUNK - 1:CHUNK, :]
        cm = jnp.broadcast_to(c_mid, c3.shape).reshape(tm, PAIR)
        mid_to_end = jnp.broadcast_to(jnp.exp(c_end - c_mid), c3.shape).reshape(tm, PAIR)

        e_in = jnp.exp(c - cm)
        e_out = jnp.exp(cm - c)
        e_end = e_out * mid_to_end
        at_ref[:, cs] = (ka_vec * jnp.exp(c - lw - cm)).astype(BF16)
        rt_ref[:, cs] = (r * e_in).astype(BF16)
        bt_ref[:, cs] = (kb_vec * e_out).astype(BF16)
        kt_ref[:, cs] = (k2 * e_out).astype(BF16)
        bg_ref[:, cs] = (kb_vec * e_end).astype(BF16)
        kg_ref[:, cs] = (k2 * e_end).astype(BF16)
        gT_ref[:, :, cs] = jnp.exp(c_end)
        gm_ref[:, :, cs] = jnp.exp(c_mid)
        yield


def _rwkv_kernel(*refs, tb, nb, n_side):
    prep_in, (gnw_ref, gnb_ref), refs = refs[:20], refs[20:22], refs[22:]
    side_in, o_ref, side_out, scratch = refs[:n_side], refs[n_side], refs[n_side + 1:2 * n_side + 1], refs[2 * n_side + 1:]
    operands, s_ref = scratch[:11], scratch[11]
    j = pl.program_id(1)
    prep = functools.partial(_rwkv_prep_kernel, *prep_in, *operands, tm=tb, first=j == 0)
    chunk = functools.partial(_rwkv_chunk_kernel, *operands, gnw_ref, gnb_ref, o_ref, s_ref,
                              chunks=tb // CHUNK, pairs=N_PAIRS, first=j == 1)

    def side_cast():
        for w_ref, wb_ref in zip(side_in, side_out):
            wb_ref[...] = w_ref[...].astype(BF16)

    @pl.when(j == 0)
    def _():
        side_cast()
        for _ in prep():
            pass

    @pl.when((j > 0) & (j < nb))
    def _():
        side_cast()
        chunk(filler=prep())

    @pl.when(j == nb)
    def _():
        side_cast()
        chunk()


def _rwkv(z, batch, seq, mu_r, mu_k, mu_v, mu_l, w0, a0, k_k, k_a, r_k, w2p, a2p, g2, gn_w, gn_b,
          side_weights, tb=128, side_rows=128):
    m = z.shape[0]
    nb = seq // tb
    for w in side_weights:
        assert w.shape[0] // side_rows <= batch * (nb + 1), "not enough grid steps to cast this weight"
    side_specs = [pl.BlockSpec((side_rows, w.shape[1]),
                               lambda b, j, last=w.shape[0] // side_rows - 1:
                               (jnp.minimum(b * (nb + 1) + j, last), 0)) for w in side_weights]
    pb = tb // BF16_SUBLANES
    blk = lambda b, j: b * nb + jnp.minimum(j, nb - 1)

    def cur(col0, width):
        return pl.BlockSpec((tb, width), lambda b, j, c0=col0 // width: (blk(b, j), c0))

    def prev(col0, width):
        return pl.BlockSpec((BF16_SUBLANES, width),
                            lambda b, j, c0=col0 // width: (jnp.maximum(blk(b, j) * pb - 1, 0), c0))

    def whole(rows, width):
        return pl.BlockSpec((rows, width), lambda b, j: (0, 0))

    in_specs = [cur(C_R, RW_DIM), cur(C_K, RW_DIM), cur(C_V, RW_DIM), cur(C_LORA, LORA_COLS),
                prev(C_R, RW_DIM), prev(C_K, RW_DIM), prev(C_V, RW_DIM), prev(C_LORA, LORA_COLS),
                whole(1, RW_DIM), whole(1, RW_DIM), whole(1, RW_DIM), whole(1, LORA_COLS),
                whole(1, RW_DIM), whole(1, RW_DIM), whole(1, RW_DIM), whole(1, RW_DIM), whole(1, RW_DIM),
                whole(LORA_W, RW_DIM), whole(LORA_A, RW_DIM), whole(GATE_LORA, RW_DIM),
                whole(1, RW_DIM), whole(1, RW_DIM)]
    big = pltpu.VMEM((tb, RW_DIM), BF16)
    per_chunk = pltpu.VMEM((tb // CHUNK, 1, RW_DIM), F32)
    outs = pl.pallas_call(
        functools.partial(_rwkv_kernel, tb=tb, nb=nb, n_side=len(side_weights)),
        grid=(batch, nb + 1),
        in_specs=in_specs + side_specs,
        out_specs=[pl.BlockSpec((tb, RW_DIM), lambda b, j: (b * nb + jnp.maximum(j - 1, 0), 0))] + side_specs,
        out_shape=[jax.ShapeDtypeStruct((m, RW_DIM), BF16)]
        + [jax.ShapeDtypeStruct(w.shape, BF16) for w in side_weights],
        scratch_shapes=[big] * 9 + [per_chunk] * 2 + [pltpu.VMEM((N_PAIRS, PAIR, PAIR), F32)],
        compiler_params=_cparams(("arbitrary", "arbitrary")),
        name="rwkv",
    )(z, z, z, z, z, z, z, z, mu_r, mu_k, mu_v, mu_l, w0, a0, k_k, k_a, r_k, w2p, a2p, g2, gn_w, gn_b,
      *side_weights)
    return outs[0], outs[1:]


def _rwkv_chunk_kernel(at_ref, rt_ref, bt_ref, kt_ref, bg_ref, kg_ref, v_ref, g_ref, bonus_ref,
                       gT_ref, gm_ref, gnw_ref, gnb_ref, o_ref, s_ref, *, chunks, pairs, first,
                       filler=None):
    @pl.when(first)
    def _():
        s_ref[...] = jnp.zeros_like(s_ref)

    T = CHUNK
    lane = lax.broadcasted_iota(jnp.int32, (T, PAIR), 1)
    head0 = lane < RW_HEAD
    ri = lax.broadcasted_iota(jnp.int32, (2 * T, 2 * T), 0)
    ci = lax.broadcasted_iota(jnp.int32, (2 * T, 2 * T), 1)
    same = (ri // T) == (ci // T)
    strict = same & ((ri % T) > (ci % T))
    incl = same & ((ri % T) >= (ci % T))
    eye = jnp.where(ri == ci, 1.0, 0.0)
    own = (ri // T) == (ci // RW_HEAD)

    items = [(p, c) for c in range(chunks) for p in range(pairs)]
    rows = lambda c: slice(c * T, (c + 1) * T)
    cols = lambda p: slice(p * PAIR, (p + 1) * PAIR)

    def stack(ref):
        out = []
        for p, c in items:
            x = ref[rows(c), cols(p)]
            zero = jnp.zeros_like(x)
            out.append(jnp.concatenate([jnp.where(head0, x, zero), jnp.where(head0, zero, x)], axis=0))
        return out

    each = lambda f, *ls: [f(*xs) for xs in zip(*ls)]
    La, Lr, Rb, Rk = stack(at_ref), stack(rt_ref), stack(bt_ref), stack(kt_ref)
    Rbg, Rkg, Vs = stack(bg_ref), stack(kg_ref), stack(v_ref)
    gm_row = [gm_ref[c][:, cols(p)] for p, c in items]
    gT_row = [gT_ref[c][:, cols(p)] for p, c in items]
    gate = [g_ref[rows(c), cols(p)] for p, c in items]
    bonus = [bonus_ref[rows(c), cols(p)] for p, c in items]
    filler = iter(()) if filler is None else filler
    tick = lambda: next(filler, None)

    AA = each(lambda la, lr, rb, rk: _mm_nt(jnp.concatenate([la, lr], axis=0),
                                            jnp.concatenate([rb, rk], axis=0)), La, Lr, Rb, Rk)
    tick()
    N = [jnp.where(strict, aa[:2 * T, :2 * T], 0.0) for aa in AA]
    Aak = [jnp.where(strict, aa[:2 * T, 2 * T:], 0.0) for aa in AA]
    Arb = [jnp.where(incl, aa[2 * T:, :2 * T], 0.0) for aa in AA]
    Ark = [jnp.where(incl, aa[2 * T:, 2 * T:], 0.0) for aa in AA]
    def live_rows(x, t0):
        return x if t0 == 0 else jnp.concatenate([x[t0:T, :], x[T + t0:, :]], axis=0)

    def all_rows(y, t0):
        if t0 == 0:
            return y
        zero = jnp.zeros((t0, y.shape[1]), y.dtype)
        return jnp.concatenate([zero, y[:T - t0, :], zero, y[T - t0:, :]], axis=0)

    def skip(span):
        return span if span % BF16_SUBLANES == 0 else 0

    def blockdiag(a, b):
        zero = jnp.zeros_like(a)
        return jnp.concatenate([jnp.concatenate([a, zero], axis=1),
                                jnp.concatenate([zero, b], axis=1)], axis=0)

    groups = [(i, i + 1) for i in range(0, len(items), 2)]
    W = each(lambda n: eye + n, N)
    P = [n.astype(BF16) for n in N]
    for a, b in groups:
        sq = _mm(jnp.concatenate([P[a], P[b]], axis=1), blockdiag(P[a], P[b]))
        P[a], P[b] = sq[:, :PAIR].astype(BF16), sq[:, PAIR:].astype(BF16)
    tick()
    span = 2
    while span < T:
        last = 2 * span >= T
        t0, t1 = skip(span), skip(2 * span)
        for a, b in groups:
            lhs = jnp.concatenate([live_rows(W[a].astype(BF16), t0), live_rows(W[b].astype(BF16), t0)], axis=1)
            if not last:
                lhs_p = jnp.concatenate([live_rows(P[a], t1), live_rows(P[b], t1)], axis=1)
                lhs = jnp.concatenate([lhs_p, lhs], axis=0)
            res = _mm(lhs, blockdiag(P[a], P[b]))
            if not last:
                sq = all_rows(res[:lhs_p.shape[0], :], t1)
                P[a], P[b] = sq[:, :PAIR].astype(BF16), sq[:, PAIR:].astype(BF16)
                res = res[lhs_p.shape[0]:, :]
            wp = all_rows(res, t0)
            W[a] = W[a] + wp[:, :PAIR]
            W[b] = W[b] + wp[:, PAIR:]
        tick()
        span *= 2
    AkV = each(_mm, Aak, Vs)
    tick()
    X = each(lambda w, la, akv: _mm(w, jnp.concatenate([la, akv.astype(BF16)], axis=1)).astype(BF16),
             W, La, AkV)
    tick()
    Z = each(lambda arb, ark, x, vs:
             _mm(jnp.concatenate([arb.astype(BF16), ark.astype(BF16)], axis=1),
                 jnp.concatenate([x, jnp.concatenate([jnp.zeros_like(vs), vs], axis=1)], axis=0)),
             Arb, Ark, X, Vs)
    tick()
    Q = each(lambda z, lr, gm: (z[:, :PAIR] + lr.astype(F32)) * gm, Z, Lr, gm_row)
    Y0 = [z[:, PAIR:] for z in Z]
    tick()
    Mbd = each(lambda rbg, x, gm: _mm_tn(rbg, x[:, :PAIR]) * gm, Rbg, X, gm_row)
    NcT = each(lambda x, vs, rbg, rkg: _mm_tn(jnp.concatenate([x[:, PAIR:], vs], axis=0),
                                              jnp.concatenate([rbg, rkg], axis=0)), X, Vs, Rbg, Rkg)
    for _ in filler:
        pass

    S = [s_ref[p] for p in range(pairs)]
    for c in range(chunks):
        for p in range(pairs):
            i = c * pairs + p
            Ys = _mm_nt(Q[i], S[p]) + Y0[i]
            S[p] = S[p] * gT_row[i] + _mm_nt(S[p], Mbd[i]) + NcT[i]
            mu = jnp.sum(Ys, axis=-1, keepdims=True) * (1.0 / RW_HEAD)
            d = jnp.where(own, Ys - mu, 0.0)
            var = jnp.sum(d * d, axis=-1, keepdims=True) * (1.0 / RW_HEAD)
            dn = d * lax.rsqrt(var + GN_EPS)
            yn = (dn[:T, :] + dn[T:, :]) * gnw_ref[:, cols(p)] + gnb_ref[:, cols(p)]
            out = (yn + bonus[i].astype(F32)) * gate[i].astype(F32)
            o_ref[rows(c), cols(p)] = out.astype(BF16)
    for p in range(pairs):
        s_ref[p] = S[p]


def _swa_kernel(sink_ref, q_ref, kc_ref, kp_ref, vc_ref, vp_ref, qg_ref, kg_ref, o_ref, *,
                steps_per_seq, slopes, qb):
    first = (pl.program_id(0) % steps_per_seq) == 0
    mean_h = _head_ones(LANES, ATT_HEAD, 1.0 / ATT_HEAD)

    def head_rms(x, gain):
        parts = []
        for b in range(x.shape[1] // LANES):
            xb = x[:, b * LANES:(b + 1) * LANES]
            ms = _mm(xb * xb, mean_h)
            parts.append(xb * lax.rsqrt(ms + RMS_EPS))
        return jnp.concatenate(parts, axis=1) * gain

    q = head_rms(q_ref[...].astype(F32), qg_ref[...]) * (ATT_HEAD ** -0.5 * LOG2E)
    q = q.astype(BF16)
    kcat = jnp.concatenate([kp_ref[...], kc_ref[...]], axis=0).astype(F32)
    kcat = head_rms(kcat, kg_ref[...]).astype(BF16)
    vcat = jnp.concatenate([vp_ref[...], vc_ref[...]], axis=0)

    qi = lax.broadcasted_iota(jnp.int32, (BLOCK, 2 * BLOCK), 0)
    kj = lax.broadcasted_iota(jnp.int32, (BLOCK, 2 * BLOCK), 1)
    dist_i = BLOCK + qi - kj
    in_window = (dist_i >= 0) & (dist_i < WINDOW)
    first_key = jnp.where(first, BLOCK, 0)
    neg_dist_rest = jnp.where(in_window, -dist_i.astype(F32), NEG_BIG)
    neg_dist_first = jnp.where(in_window & (kj >= first_key), -dist_i.astype(F32), NEG_BIG)
    neg_dist = [neg_dist_first] + [neg_dist_rest] * (qb - 1)

    lane_q = lax.broadcasted_iota(jnp.int32, (BLOCK, LANES), 1)
    lane_kv = lax.broadcasted_iota(jnp.int32, (2 * BLOCK, LANES), 1)
    zero_q = jnp.zeros((BLOCK, LANES), BF16)
    one_kv = jnp.ones((2 * BLOCK, LANES), BF16)
    kv_lane_blocks = range(ATT_KV_DIM // LANES)
    k_all = [kcat[:, b * LANES:(b + 1) * LANES] for b in kv_lane_blocks]
    k_all_rolled = [pltpu.roll(kb, ATT_HEAD, 1) for kb in k_all]
    window = lambda x, t: x[t * BLOCK:(t + 2) * BLOCK, :]
    v_aug = [[jnp.where((lane_kv // ATT_HEAD) == (j % 2),
                        window(vcat[:, (j // 2) * LANES:(j // 2 + 1) * LANES], t), one_kv)
              for j in range(ATT_KV_HEADS)] for t in range(qb)]
    items = [(t, h) for t in range(qb) for h in range(ATT_HEADS)]
    kv_of = lambda h: h // ATT_GROUP
    aligned = lambda h: (h % 2) == (kv_of(h) % 2)

    def scores(t, h):
        qh = jnp.where((lane_q // ATT_HEAD) == (h % 2),
                       q[t * BLOCK:(t + 1) * BLOCK, (h // 2) * LANES:(h // 2 + 1) * LANES], zero_q)
        kh = window((k_all if aligned(h) else k_all_rolled)[kv_of(h) // 2], t)
        return _mm_nt(qh, kh) + (slopes[h] * LOG2E) * neg_dist[t]

    s = [scores(t, h) for t, h in items]
    sink = [sink_ref[h] * LOG2E for _, h in items]
    mx = [jnp.maximum(jnp.max(s_, axis=-1, keepdims=True), sk) for s_, sk in zip(s, sink)]
    p = [jnp.exp2(s_ - m_) for s_, m_ in zip(s, mx)]
    pv = [_mm(p_, v_aug[t][kv_of(h)]) for p_, (t, h) in zip(p, items)]
    pv_sw = [pltpu.roll(x, ATT_HEAD, 1) for x in pv]
    out = []
    for i, (t, h) in enumerate(items):
        num, den = (pv[i], pv_sw[i]) if aligned(h) else (pv_sw[i], pv[i])
        out.append(num * (1.0 / (den + jnp.exp2(sink[i] - mx[i]))))
    for t in range(qb):
        o = out[t * ATT_HEADS:(t + 1) * ATT_HEADS]
        o_ref[t * BLOCK:(t + 1) * BLOCK, :] = jnp.concatenate(
            [jnp.where((lane_q // ATT_HEAD) == 0, o[2 * m], o[2 * m + 1]) for m in range(ATT_HEADS // 2)],
            axis=1).astype(BF16)


def _swa(z, sinks, q_gain_t, k_gain_t, seq, qb=8):
    m = z.shape[0]
    rows = qb * BLOCK
    slopes = tuple(float(s) for s in
                   np.exp2(-8.0 * np.arange(1, ATT_HEADS + 1, dtype=np.float32) / ATT_HEADS).astype(np.float32))
    kv_cur = lambda c0: pl.BlockSpec((rows, ATT_KV_DIM), lambda n, c=c0 // ATT_KV_DIM: (n, c))
    kv_prev = lambda c0: pl.BlockSpec((BLOCK, ATT_KV_DIM),
                                      lambda n, c=c0 // ATT_KV_DIM: (jnp.maximum(n * qb - 1, 0), c))
    return pl.pallas_call(
        functools.partial(_swa_kernel, steps_per_seq=seq // rows, slopes=slopes, qb=qb),
        grid=(m // rows,),
        in_specs=[pl.BlockSpec(memory_space=pltpu.SMEM),
                  pl.BlockSpec((rows, ATT_Q_DIM), lambda n: (n, C_Q // ATT_Q_DIM)),
                  kv_cur(C_KA), kv_prev(C_KA), kv_cur(C_VA), kv_prev(C_VA),
                  pl.BlockSpec((1, ATT_Q_DIM), lambda n: (0, 0)),
                  pl.BlockSpec((1, ATT_KV_DIM), lambda n: (0, 0))],
        out_specs=pl.BlockSpec((rows, ATT_Q_DIM), lambda n: (n, 0)),
        out_shape=jax.ShapeDtypeStruct((m, ATT_Q_DIM), BF16),
        compiler_params=_cparams(("parallel",)),
        name="swa",
    )(sinks, z, z, z, z, z, q_gain_t, k_gain_t)


def _mix_out_kernel(orw_ref, oatt_ref, wb1_ref, wb2_ref, zg_ref, wo_ref, x_ref, g_ref, x1_ref, h2_ref):
    p_rw = jnp.dot(orw_ref[...], wb1_ref[...], preferred_element_type=F32)
    p_att = jnp.dot(oatt_ref[...], wb2_ref[...], preferred_element_type=F32)
    g_rw = jax.nn.sigmoid(zg_ref[:, :D_MODEL].astype(F32))
    g_att = jax.nn.sigmoid(zg_ref[:, D_MODEL:].astype(F32))
    mix = (g_rw * p_rw + g_att * p_att).astype(BF16)
    x1 = x_ref[...] + jnp.dot(mix, wo_ref[...], preferred_element_type=F32)
    x1_ref[...] = x1
    y = x1 * lax.rsqrt(jnp.mean(x1 * x1, axis=-1, keepdims=True) + RMS_EPS)
    h2_ref[...] = (y * g_ref[...]).astype(BF16)


def _mix_out(o_rw, o_att, w_branch_b, z, w_out_b, x2, g2, tm=512):
    m = x2.shape[0]
    const = lambda rows, cols, r0=0: pl.BlockSpec((rows, cols), lambda i: (r0, 0),
                                                  pipeline_mode=pl.Buffered(1))
    rows = lambda cols, c0=0: pl.BlockSpec((tm, cols), lambda i: (i, c0))
    return pl.pallas_call(
        _mix_out_kernel,
        grid=(m // tm,),
        in_specs=[rows(RW_DIM), rows(ATT_Q_DIM), const(RW_DIM, D_MODEL), const(ATT_Q_DIM, D_MODEL, 1),
                  rows(2 * D_MODEL, C_GRW // (2 * D_MODEL)), const(D_MODEL, D_MODEL),
                  rows(D_MODEL), const(1, D_MODEL)],
        out_specs=[rows(D_MODEL), rows(D_MODEL)],
        out_shape=[jax.ShapeDtypeStruct((m, D_MODEL), F32),
                   jax.ShapeDtypeStruct((m, D_MODEL), BF16)],
        compiler_params=_cparams(("parallel",)),
        name="mix_out",
    )(o_rw, o_att, w_branch_b, w_branch_b, z, w_out_b, x2, g2)


def _ffn_up_kernel(h_ref, wvb_ref, wgb_ref, cwv_ref, cwg_ref, cbv_ref, cbg_ref, wd_ref, a_ref, wdb_ref,
                   cv_ref, cg_ref, *, tm, sub, tiles_per_seq):
    halo = 8
    wdb_ref[...] = wd_ref[...].astype(BF16)

    @pl.when((pl.program_id(1) % tiles_per_seq) == 0)
    def _():
        cv_ref[...] = jnp.zeros_like(cv_ref)
        cg_ref[...] = jnp.zeros_like(cg_ref)

    row = lax.broadcasted_iota(jnp.int32, (halo, a_ref.shape[1]), 0)

    def conv(u, carry_ref, cw_ref, cb_ref):
        p1 = carry_ref[halo - 1:halo, :]
        p2 = carry_ref[halo - 2:halo - 1, :]
        r1 = pltpu.roll(u, 1, 0)
        r2 = pltpu.roll(u, 2, 0)
        prev1 = jnp.concatenate([jnp.where(row == 0, p1, r1[:halo, :]), r1[halo:, :]], axis=0)
        prev2 = jnp.concatenate(
            [jnp.where(row == 0, p2, jnp.where(row == 1, p1, r2[:halo, :])), r2[halo:, :]], axis=0)
        carry_ref[...] = u[sub - halo:sub, :]
        w = cw_ref[...].astype(BF16)
        return (prev2.astype(BF16) * w[0:1, :] + prev1.astype(BF16) * w[1:2, :] + u.astype(BF16) * w[2:3, :]
                + cb_ref[...].astype(BF16))

    for s in range(tm // sub):
        h = h_ref[s * sub:(s + 1) * sub, :]
        val = conv(jnp.dot(h, wvb_ref[...], preferred_element_type=F32), cv_ref, cwv_ref, cbv_ref)
        gate = conv(jnp.dot(h, wgb_ref[...], preferred_element_type=F32), cg_ref, cwg_ref, cbg_ref)
        a_ref[s * sub:(s + 1) * sub, :] = (gate * jax.nn.sigmoid(gate) * val).astype(BF16)


def _ffn_up(h2, w_up_b, conv_w, conv_b, w_down, seq, tm=4096, tn=512, sub=256):
    m = h2.shape[0]
    nj = D_FF // tn
    ni = m // tm
    wd_rows = w_down.shape[0] // (nj * ni)
    wd_spec = pl.BlockSpec((wd_rows, D_MODEL), lambda j, i: (j * ni + i, 0))
    return pl.pallas_call(
        functools.partial(_ffn_up_kernel, tm=tm, sub=sub, tiles_per_seq=seq // tm),
        grid=(nj, ni),
        in_specs=[pl.BlockSpec((tm, D_MODEL), lambda j, i: (i, 0)),
                  pl.BlockSpec((D_MODEL, tn), lambda j, i: (0, j)),
                  pl.BlockSpec((D_MODEL, tn), lambda j, i: (0, nj + j)),
                  pl.BlockSpec((3, tn), lambda j, i: (0, j)),
                  pl.BlockSpec((3, tn), lambda j, i: (0, nj + j)),
                  pl.BlockSpec((1, tn), lambda j, i: (0, j)),
                  pl.BlockSpec((1, tn), lambda j, i: (0, nj + j)),
                  wd_spec],
        out_specs=[pl.BlockSpec((tm, tn), lambda j, i: (i, j)), wd_spec],
        out_shape=[jax.ShapeDtypeStruct((m, D_FF), BF16), jax.ShapeDtypeStruct(w_down.shape, BF16)],
        scratch_shapes=[pltpu.VMEM((8, tn), F32), pltpu.VMEM((8, tn), F32)],
        compiler_params=_cparams(("parallel", "arbitrary")),
        name="ffn_up",
    )(h2, w_up_b, w_up_b, conv_w, conv_w, conv_b, conv_b, w_down)


def _ffn_down_kernel(a_ref, w_ref, x1_ref, o_ref):
    o_ref[...] = x1_ref[...] + jnp.dot(a_ref[...], w_ref[...], preferred_element_type=F32)


def _ffn_down(act, w_down_b, x1, tm=512, tn=1024):
    m = act.shape[0]
    return pl.pallas_call(
        _ffn_down_kernel,
        grid=(D_MODEL // tn, m // tm),
        in_specs=[pl.BlockSpec((tm, D_FF), lambda j, i: (i, 0)),
                  pl.BlockSpec((D_FF, tn), lambda j, i: (0, j)),
                  pl.BlockSpec((tm, tn), lambda j, i: (i, j))],
        out_specs=pl.BlockSpec((tm, tn), lambda j, i: (i, j)),
        out_shape=jax.ShapeDtypeStruct((m, D_MODEL), F32),
        compiler_params=_cparams(("parallel", "arbitrary")),
        name="ffn_down",
    )(act, w_down_b, x1)


def _ffn_kernel(h_ref, wvb_ref, wgb_ref, cwv_ref, cwg_ref, cbv_ref, cbg_ref, wd_ref, x1_ref, o_ref,
                cv_ref, cg_ref, *, tm, sub, tiles_per_seq):
    halo = 8
    j = pl.program_id(1)
    cv, cg = cv_ref.at[j], cg_ref.at[j]

    @pl.when((pl.program_id(0) % tiles_per_seq) == 0)
    def _():
        cv[...] = jnp.zeros_like(cv)
        cg[...] = jnp.zeros_like(cg)

    @pl.when(j == 0)
    def _():
        o_ref[...] = x1_ref[...]

    row = lax.broadcasted_iota(jnp.int32, (halo, wvb_ref.shape[1]), 0)

    def conv(u, carry_ref, cw_ref, cb_ref):
        p1 = carry_ref[halo - 1:halo, :]
        p2 = carry_ref[halo - 2:halo - 1, :]
        r1 = pltpu.roll(u, 1, 0)
        r2 = pltpu.roll(u, 2, 0)
        prev1 = jnp.concatenate([jnp.where(row == 0, p1, r1[:halo, :]), r1[halo:, :]], axis=0)
        prev2 = jnp.concatenate(
            [jnp.where(row == 0, p2, jnp.where(row == 1, p1, r2[:halo, :])), r2[halo:, :]], axis=0)
        carry_ref[...] = u[sub - halo:sub, :]
        w = cw_ref[...].astype(BF16)
        return (prev2.astype(BF16) * w[0:1, :] + prev1.astype(BF16) * w[1:2, :] + u.astype(BF16) * w[2:3, :]
                + cb_ref[...].astype(BF16))

    for s in range(tm // sub):
        rows = slice(s * sub, (s + 1) * sub)
        h = h_ref[rows, :]
        val = conv(jnp.dot(h, wvb_ref[...], preferred_element_type=F32), cv, cwv_ref, cbv_ref)
        gate = conv(jnp.dot(h, wgb_ref[...], preferred_element_type=F32), cg, cwg_ref, cbg_ref)
        act = (gate * jax.nn.sigmoid(gate) * val).astype(BF16)
        o_ref[rows, :] += jnp.dot(act, wd_ref[...], preferred_element_type=F32)


def _ffn(h2, w_up_b, conv_w, conv_b, w_down_b, x1, seq, tm=1024, tn=512, sub=256):
    m = h2.shape[0]
    nj = D_FF // tn
    row_tile = lambda cols: pl.BlockSpec((tm, cols), lambda i, j: (i, 0))
    return pl.pallas_call(
        functools.partial(_ffn_kernel, tm=tm, sub=sub, tiles_per_seq=seq // tm),
        grid=(m // tm, nj),
        in_specs=[row_tile(D_MODEL),
                  pl.BlockSpec((D_MODEL, tn), lambda i, j: (0, j)),
                  pl.BlockSpec((D_MODEL, tn), lambda i, j: (0, nj + j)),
                  pl.BlockSpec((3, tn), lambda i, j: (0, j)),
                  pl.BlockSpec((3, tn), lambda i, j: (0, nj + j)),
                  pl.BlockSpec((1, tn), lambda i, j: (0, j)),
                  pl.BlockSpec((1, tn), lambda i, j: (0, nj + j)),
                  pl.BlockSpec((tn, D_MODEL), lambda i, j: (j, 0)),
                  pl.BlockSpec((tm, D_MODEL), lambda i, j: (i, 0), pipeline_mode=pl.Buffered(1))],
        out_specs=row_tile(D_MODEL),
        out_shape=jax.ShapeDtypeStruct((m, D_MODEL), F32),
        scratch_shapes=[pltpu.VMEM((nj, 8, tn), F32), pltpu.VMEM((nj, 8, tn), F32)],
        compiler_params=_cparams(("arbitrary", "arbitrary")),
        name="ffn",
    )(h2, w_up_b, w_up_b, conv_w, conv_w, conv_b, conv_b, w_down_b, x1)


def _pad_cols(w, n):
    return jnp.pad(w, ((0, 0), (0, n - w.shape[1])))


def _pad_rows(w, n):
    return jnp.pad(w, ((0, n - w.shape[0]), (0, 0)))


def _layer(x2, batch, seq, w_in_p, norm1_g, rw_mu, rw_w0, rw_w2, rw_a0, rw_a2, rw_g2, rw_k_k, rw_k_a,
           rw_r_k, rw_gn_w, rw_gn_b, q_norm_g, k_norm_g, attn_sinks, w_branch, w_out,
           norm2_g, w_up, conv_w, conv_b, w_down):
    row = lambda v: v.reshape(1, -1).astype(F32)
    o3 = 3 * RW_DIM
    o4 = o3 + DECAY_LORA
    o5 = o4 + ICLR_LORA
    o6 = o5 + GATE_LORA
    mu = rw_mu.reshape(1, -1)
    mu_l = jnp.concatenate([_pad_cols(mu[:, o3:o4], LORA_W), _pad_cols(mu[:, o4:o5], LORA_A),
                            mu[:, o5:o6]], axis=1)
    w2p = _pad_rows(rw_w2, LORA_W).astype(BF16)
    a2p = _pad_rows(rw_a2, LORA_A).astype(BF16)

    z = _in_proj(x2, row(norm1_g), w_in_p)
    o_rw, (w_branch_b, w_out_b, w_up_b, w_down_b) = _rwkv(
        z, batch, seq, mu[:, :RW_DIM], mu[:, RW_DIM:2 * RW_DIM], mu[:, 2 * RW_DIM:o3], mu_l,
        row(rw_w0), row(rw_a0), row(rw_k_k), row(rw_k_a), row(rw_r_k),
        w2p, a2p, rw_g2.astype(BF16), row(rw_gn_w), row(rw_gn_b), (w_branch, w_out, w_up, w_down))
    o_att = _swa(z, attn_sinks.astype(F32), jnp.tile(row(q_norm_g), (1, ATT_HEADS)),
                 jnp.tile(row(k_norm_g), (1, ATT_KV_HEADS)), seq)
    x1, h2 = _mix_out(o_rw, o_att, w_branch_b, z, w_out_b, x2, row(norm2_g))
    return _ffn(h2, w_up_b, conv_w.astype(F32), row(conv_b), w_down_b, x1, seq)


def kernel(x, norm1_g, w_in, rw_mu, rw_w0, rw_w2, rw_a0, rw_a2, rw_g2, rw_k_k, rw_k_a, rw_r_k,
           rw_gn_w, rw_gn_b, q_norm_g, k_norm_g, attn_sinks, w_branch, w_out, norm2_g, w_up,
           conv_w, conv_b, w_down):
    batch, seq, d = x.shape
    x2 = x.reshape(batch * seq, d)
    params = (norm1_g, rw_mu, rw_w0, rw_w2, rw_a0, rw_a2, rw_g2, rw_k_k, rw_k_a, rw_r_k,
              rw_gn_w, rw_gn_b, q_norm_g, k_norm_g, attn_sinks, w_branch, w_out, norm2_g, w_up,
              conv_w, conv_b, w_down)
    for layer in range(norm1_g.shape[0]):
        x2 = _layer(x2, batch, seq, _w_in_layout(w_in, layer), *(p[layer] for p in params))
    return x2.reshape(batch, seq, d)
```

```python
import functools

import jax
import jax.numpy as jnp
import numpy as np
from jax import lax
from jax.experimental import pallas as pl
from jax.experimental.pallas import tpu as pltpu

F32 = jnp.float32
BF16 = jnp.bfloat16

LANES = 128
BF16_SUBLANES = 16
VMEM_LIMIT = 56 * 1024 * 1024

D_MODEL = 2048
RW_HEADS = 16
RW_HEAD = 64
RW_DIM = RW_HEADS * RW_HEAD
DECAY_LORA = 96
ICLR_LORA = 96
GATE_LORA = 256
GN_EPS = 64e-5
ATT_HEADS = 16
ATT_KV_HEADS = 4
ATT_GROUP = ATT_HEADS // ATT_KV_HEADS
ATT_HEAD = 64
ATT_Q_DIM = ATT_HEADS * ATT_HEAD
ATT_KV_DIM = ATT_KV_HEADS * ATT_HEAD
WINDOW = 128
BLOCK = 128
D_FF = 5632
RMS_EPS = 1e-6
NEG_BIG = -1e30
LOG2E = 1.4426950408889634
EXP_M_HALF = 0.6065306597126334

C_R = 0
C_K = RW_DIM
C_V = 2 * RW_DIM
C_Q = 3 * RW_DIM
C_GRW = C_Q + ATT_Q_DIM
C_GATT = C_GRW + D_MODEL
C_LORA = C_GATT + D_MODEL
LORA_W = 128
LORA_A = 128
LORA_COLS = LORA_W + LORA_A + GATE_LORA
C_KA = C_LORA + LORA_COLS
C_VA = C_KA + ATT_KV_DIM
Z_COLS = C_VA + ATT_KV_DIM

CHUNK = 64
PAIR = 2 * RW_HEAD
N_PAIRS = RW_DIM // PAIR


def _cparams(sem):
    return pltpu.CompilerParams(dimension_semantics=sem, vmem_limit_bytes=VMEM_LIMIT)


def _mm(a, b):
    return jnp.dot(a.astype(BF16), b.astype(BF16), preferred_element_type=F32)


def _mm_nt(a, b):
    return lax.dot_general(a.astype(BF16), b.astype(BF16), (((1,), (1,)), ((), ())),
                           preferred_element_type=F32)


def _mm_tn(a, b):
    return lax.dot_general(a.astype(BF16), b.astype(BF16), (((0,), (0,)), ((), ())),
                           preferred_element_type=F32)


def _mm_split(m01, x):
    hi = x.astype(BF16)
    lo = (x - hi.astype(F32)).astype(BF16)
    return (jnp.dot(m01, hi, preferred_element_type=F32)
            + jnp.dot(m01, lo, preferred_element_type=F32))


def _head_ones(n, head, scale):
    r = lax.broadcasted_iota(jnp.int32, (n, n), 0) // head
    c = lax.broadcasted_iota(jnp.int32, (n, n), 1) // head
    return jnp.where(r == c, scale, 0.0).astype(BF16)


def _w_in_layout_kernel(wt_ref, o_ref, *, moves, pad):
    o_ref[pad[0]:pad[1], :] = jnp.zeros((pad[1] - pad[0], o_ref.shape[1]), BF16)
    for dst, src, width in moves:
        o_ref[dst:dst + width, :] = wt_ref[src:src + width, :].astype(BF16)


def _w_in_layout(w_in_layers, layer, cols=256):
    _, d, n = w_in_layers.shape
    wt_layers = jnp.swapaxes(w_in_layers, 1, 2)
    o3 = 3 * RW_DIM
    o4 = o3 + DECAY_LORA
    o5 = o4 + ICLR_LORA
    o6 = o5 + GATE_LORA
    oq = o6 + ATT_Q_DIM
    ok = oq + ATT_KV_DIM
    ov = ok + ATT_KV_DIM
    moves = ((C_R, 0, o3), (C_Q, o6, ATT_Q_DIM), (C_GRW, ov, 2 * D_MODEL),
             (C_LORA, o3, DECAY_LORA), (C_LORA + LORA_W, o4, ICLR_LORA),
             (C_LORA + LORA_W + LORA_A, o5, GATE_LORA), (C_KA, oq, ATT_KV_DIM), (C_VA, ok, ATT_KV_DIM))
    return pl.pallas_call(
        functools.partial(_w_in_layout_kernel, moves=moves, pad=(C_LORA, C_LORA + LORA_W + LORA_A)),
        grid=(d // cols,),
        in_specs=[pl.BlockSpec((None, n, cols), lambda i: (layer, 0, i))],
        out_specs=pl.BlockSpec((Z_COLS, cols), lambda i: (0, i)),
        out_shape=jax.ShapeDtypeStruct((Z_COLS, d), BF16),
        compiler_params=_cparams(("parallel",)),
        name="w_in_layout",
    )(wt_layers)


def _in_proj_kernel(x_ref, g_ref, wt_ref, z_ref, h_ref):
    @pl.when(pl.program_id(1) == 0)
    def _():
        xf = x_ref[...]
        y = xf * lax.rsqrt(jnp.mean(xf * xf, axis=-1, keepdims=True) + RMS_EPS)
        h_ref[...] = (y * g_ref[...]).astype(BF16)

    z_ref[...] = _mm_nt(h_ref[...], wt_ref[...]).astype(BF16)


def _in_proj(x2, g1, w_in_t, tm=1024, tn=1536):
    m, d = x2.shape
    n = w_in_t.shape[0]
    return pl.pallas_call(
        _in_proj_kernel,
        grid=(m // tm, n // tn),
        in_specs=[pl.BlockSpec((tm, d), lambda i, j: (i, 0)),
                  pl.BlockSpec((1, d), lambda i, j: (0, 0)),
                  pl.BlockSpec((tn, d), lambda i, j: (j, 0))],
        out_specs=pl.BlockSpec((tm, tn), lambda i, j: (i, j)),
        out_shape=jax.ShapeDtypeStruct((m, n), BF16),
        scratch_shapes=[pltpu.VMEM((tm, d), BF16)],
        compiler_params=_cparams(("parallel", "arbitrary")),
        name="in_proj",
    )(x2, g1, w_in_t)


def _rwkv_prep_kernel(zr_ref, zk_ref, zv_ref, zl_ref, pr_ref, pk_ref, pv_ref, pli_ref,
                      mur_ref, muk_ref, muv_ref, mul_ref,
                      w0_ref, a0_ref, kk_ref, ka_ref, rk_ref, w2_ref, a2_ref, g2_ref,
                      at_ref, rt_ref, bt_ref, kt_ref, bg_ref, kg_ref, v_ref, g_ref, bonus_ref,
                      gT_ref, gm_ref, *, tm, first):
    keep = jnp.where(first, 0.0, 1.0)
    nc = tm // CHUNK
    last = slice(BF16_SUBLANES - 1, BF16_SUBLANES)

    def shifted(z, prev_last, mu):
        z = z.astype(F32)
        row = lax.broadcasted_iota(jnp.int32, z.shape, 0)
        zprev = jnp.where(row == 0, prev_last.astype(F32) * keep, pltpu.roll(z, 1, 0))
        return z + (zprev - z) * mu

    lo = shifted(zl_ref[...], pli_ref[last, :], mul_ref[...])
    tanh_wd = jnp.tanh(lo[:, :LORA_W]).astype(BF16)
    ad = lo[:, LORA_W:LORA_W + LORA_A].astype(BF16)
    sig_gd = jax.nn.sigmoid(lo[:, LORA_W + LORA_A:]).astype(BF16)

    ti = lax.broadcasted_iota(jnp.int32, (tm, tm), 0)
    si = lax.broadcasted_iota(jnp.int32, (tm, tm), 1)
    tri = jnp.where(((ti // CHUNK) == (si // CHUNK)) & (si <= ti), 1.0, 0.0).astype(BF16)
    ones_h = _head_ones(PAIR, RW_HEAD, 1.0)
    w_lora = jnp.dot(tanh_wd, w2_ref[...], preferred_element_type=F32)
    a_lora = jnp.dot(ad, a2_ref[...], preferred_element_type=F32)
    g_ref[...] = jnp.dot(sig_gd, g2_ref[...], preferred_element_type=F32).astype(BF16)
    yield

    stash = []
    for p in range(N_PAIRS):
        cs = slice(p * PAIR, (p + 1) * PAIR)
        r = shifted(zr_ref[:, cs], pr_ref[last, cs], mur_ref[:, cs])
        k = shifted(zk_ref[:, cs], pk_ref[last, cs], muk_ref[:, cs])
        v = shifted(zv_ref[:, cs], pv_ref[last, cs], muv_ref[:, cs])

        wpre = w0_ref[:, cs] + w_lora[:, cs]
        lw = -EXP_M_HALF * jax.nn.sigmoid(wpre)
        a = jax.nn.sigmoid(a0_ref[:, cs] + a_lora[:, cs])

        kk = k * kk_ref[:, cs]
        kk = kk * lax.rsqrt(jnp.maximum(_mm(kk * kk, ones_h), 1e-24))
        k2 = k * (1.0 + (a - 1.0) * ka_ref[:, cs])
        bonus_ref[:, cs] = (_mm(r * k2 * rk_ref[:, cs], ones_h) * v).astype(BF16)
        v_ref[:, cs] = v.astype(BF16)
        stash.append((r, k2, -kk, kk * a, lw))
        yield

    c_all = _mm_split(tri, jnp.concatenate([st[4] for st in stash], axis=1))
    yield

    for p in range(N_PAIRS):
        cs = slice(p * PAIR, (p + 1) * PAIR)
        r, k2, ka_vec, kb_vec, lw = stash[p]
        c = c_all[:, cs]
        c3 = c.reshape(nc, CHUNK, PAIR)
        c_mid = c3[:, CHUNK // 2 - 1:CHUNK // 2, :]
        c_end = c3[:, CHUNK - 1:CHUNK, :]
        cm = jnp.broadcast_to(c_mid, c3.shape).reshape(tm, PAIR)
        mid_to_end = jnp.broadcast_to(jnp.exp(c_end - c_mid), c3.shape).reshape(tm, PAIR)

        e_in = jnp.exp(c - cm)
        e_out = jnp.exp(cm - c)
        e_end = e_out * mid_to_end
        at_ref[:, cs] = (ka_vec * jnp.exp(c - lw - cm)).astype(BF16)
        rt_ref[:, cs] = (r * e_in).astype(BF16)
        bt_ref[:, cs] = (kb_vec * e_out).astype(BF16)
        kt_ref[:, cs] = (k2 * e_out).astype(BF16)
        bg_ref[:, cs] = (kb_vec * e_end).astype(BF16)
        kg_ref[:, cs] = (k2 * e_end).astype(BF16)
        gT_ref[:, :, cs] = jnp.exp(c_end)
        gm_ref[:, :, cs] = jnp.exp(c_mid)
        yield


def _rwkv_kernel(*refs, tb, nb, n_side):
    prep_in, (gnw_ref, gnb_ref), refs = refs[:20], refs[20:22], refs[22:]
    side_in, o_ref, side_out, scratch = refs[:n_side], refs[n_side], refs[n_side + 1:2 * n_side + 1], refs[2 * n_side + 1:]
    operands, s_ref = scratch[:11], scratch[11]
    j = pl.program_id(1)
    prep = functools.partial(_rwkv_prep_kernel, *prep_in, *operands, tm=tb, first=j == 0)
    chunk = functools.partial(_rwkv_chunk_kernel, *operands, gnw_ref, gnb_ref, o_ref, s_ref,
                              chunks=tb // CHUNK, pairs=N_PAIRS, first=j == 1)

    def side_cast():
        for w_ref, wb_ref in zip(side_in, side_out):
            wb_ref[...] = w_ref[...].astype(BF16)

    @pl.when(j == 0)
    def _():
        side_cast()
        for _ in prep():
            pass

    @pl.when((j > 0) & (j < nb))
    def _():
        side_cast()
        chunk(filler=prep())

    @pl.when(j == nb)
    def _():
        side_cast()
        chunk()


def _rwkv(z, batch, seq, mu_r, mu_k, mu_v, mu_l, w0, a0, k_k, k_a, r_k, w2p, a2p, g2, gn_w, gn_b,
          side_weights, tb=128, side_rows=32):
    m = z.shape[0]
    nb = seq // tb
    for w in side_weights:
        assert w.shape[0] // side_rows <= batch * (nb + 1), "not enough grid steps to cast this weight"
    side_specs = [pl.BlockSpec((side_rows, w.shape[1]),
                               lambda b, j, last=w.shape[0] // side_rows - 1:
                               (jnp.minimum(b * (nb + 1) + j, last), 0)) for w in side_weights]
    pb = tb // BF16_SUBLANES
    blk = lambda b, j: b * nb + jnp.minimum(j, nb - 1)

    def cur(col0, width):
        return pl.BlockSpec((tb, width), lambda b, j, c0=col0 // width: (blk(b, j), c0))

    def prev(col0, width):
        return pl.BlockSpec((BF16_SUBLANES, width),
                            lambda b, j, c0=col0 // width: (jnp.maximum(blk(b, j) * pb - 1, 0), c0))

    def whole(rows, width):
        return pl.BlockSpec((rows, width), lambda b, j: (0, 0))

    in_specs = [cur(C_R, RW_DIM), cur(C_K, RW_DIM), cur(C_V, RW_DIM), cur(C_LORA, LORA_COLS),
                prev(C_R, RW_DIM), prev(C_K, RW_DIM), prev(C_V, RW_DIM), prev(C_LORA, LORA_COLS),
                whole(1, RW_DIM), whole(1, RW_DIM), whole(1, RW_DIM), whole(1, LORA_COLS),
                whole(1, RW_DIM), whole(1, RW_DIM), whole(1, RW_DIM), whole(1, RW_DIM), whole(1, RW_DIM),
                whole(LORA_W, RW_DIM), whole(LORA_A, RW_DIM), whole(GATE_LORA, RW_DIM),
                whole(1, RW_DIM), whole(1, RW_DIM)]
    big = pltpu.VMEM((tb, RW_DIM), BF16)
    per_chunk = pltpu.VMEM((tb // CHUNK, 1, RW_DIM), F32)
    outs = pl.pallas_call(
        functools.partial(_rwkv_kernel, tb=tb, nb=nb, n_side=len(side_weights)),
        grid=(batch, nb + 1),
        in_specs=in_specs + side_specs,
        out_specs=[pl.BlockSpec((tb, RW_DIM), lambda b, j: (b * nb + jnp.maximum(j - 1, 0), 0))] + side_specs,
        out_shape=[jax.ShapeDtypeStruct((m, RW_DIM), BF16)]
        + [jax.ShapeDtypeStruct(w.shape, BF16) for w in side_weights],
        scratch_shapes=[big] * 9 + [per_chunk] * 2 + [pltpu.VMEM((N_PAIRS, PAIR, PAIR), F32)],
        compiler_params=_cparams(("arbitrary", "arbitrary")),
        name="rwkv",
    )(z, z, z, z, z, z, z, z, mu_r, mu_k, mu_v, mu_l, w0, a0, k_k, k_a, r_k, w2p, a2p, g2, gn_w, gn_b,
      *side_weights)
    return outs[0], outs[1:]


def _rwkv_chunk_kernel(at_ref, rt_ref, bt_ref, kt_ref, bg_ref, kg_ref, v_ref, g_ref, bonus_ref,
                       gT_ref, gm_ref, gnw_ref, gnb_ref, o_ref, s_ref, *, chunks, pairs, first,
                       filler=None):
    @pl.when(first)
    def _():
        s_ref[...] = jnp.zeros_like(s_ref)

    T = CHUNK
    lane = lax.broadcasted_iota(jnp.int32, (T, PAIR), 1)
    head0 = lane < RW_HEAD
    ri = lax.broadcasted_iota(jnp.int32, (2 * T, 2 * T), 0)
    ci = lax.broadcasted_iota(jnp.int32, (2 * T, 2 * T), 1)
    same = (ri // T) == (ci // T)
    strict = same & ((ri % T) > (ci % T))
    incl = same & ((ri % T) >= (ci % T))
    eye = jnp.where(ri == ci, 1.0, 0.0)
    own = (ri // T) == (ci // RW_HEAD)

    items = [(p, c) for c in range(chunks) for p in range(pairs)]
    rows = lambda c: slice(c * T, (c + 1) * T)
    cols = lambda p: slice(p * PAIR, (p + 1) * PAIR)

    def stack(ref):
        out = []
        for p, c in items:
            x = ref[rows(c), cols(p)]
            zero = jnp.zeros_like(x)
            out.append(jnp.concatenate([jnp.where(head0, x, zero), jnp.where(head0, zero, x)], axis=0))
        return out

    each = lambda f, *ls: [f(*xs) for xs in zip(*ls)]
    La, Lr, Rb, Rk = stack(at_ref), stack(rt_ref), stack(bt_ref), stack(kt_ref)
    Rbg, Rkg, Vs = stack(bg_ref), stack(kg_ref), stack(v_ref)
    gm_row = [gm_ref[c][:, cols(p)] for p, c in items]
    gT_row = [gT_ref[c][:, cols(p)] for p, c in items]
    gate = [g_ref[rows(c), cols(p)] for p, c in items]
    bonus = [bonus_ref[rows(c), cols(p)] for p, c in items]
    filler = iter(()) if filler is None else filler
    tick = lambda: next(filler, None)

    AA = each(lambda la, lr, rb, rk: _mm_nt(jnp.concatenate([la, lr], axis=0),
                                            jnp.concatenate([rb, rk], axis=0)), La, Lr, Rb, Rk)
    tick()
    N = [jnp.where(strict, aa[:2 * T, :2 * T], 0.0) for aa in AA]
    Aak = [jnp.where(strict, aa[:2 * T, 2 * T:], 0.0) for aa in AA]
    Arb = [jnp.where(incl, aa[2 * T:, :2 * T], 0.0) for aa in AA]
    Ark = [jnp.where(incl, aa[2 * T:, 2 * T:], 0.0) for aa in AA]
    def live_rows(x, t0):
        return x if t0 == 0 else jnp.concatenate([x[t0:T, :], x[T + t0:, :]], axis=0)

    def all_rows(y, t0):
        if t0 == 0:
            return y
        zero = jnp.zeros((t0, y.shape[1]), y.dtype)
        return jnp.concatenate([zero, y[:T - t0, :], zero, y[T - t0:, :]], axis=0)

    def skip(span):
        return span if span % BF16_SUBLANES == 0 else 0

    def blockdiag(a, b):
        zero = jnp.zeros_like(a)
        return jnp.concatenate([jnp.concatenate([a, zero], axis=1),
                                jnp.concatenate([zero, b], axis=1)], axis=0)

    groups = [(i, i + 1) for i in range(0, len(items), 2)]
    W = each(lambda n: eye + n, N)
    P = [n.astype(BF16) for n in N]
    for a, b in groups:
        sq = _mm(jnp.concatenate([P[a], P[b]], axis=1), blockdiag(P[a], P[b]))
        P[a], P[b] = sq[:, :PAIR].astype(BF16), sq[:, PAIR:].astype(BF16)
    tick()
    span = 2
    while span < T:
        last = 2 * span >= T
        t0, t1 = skip(span), skip(2 * span)
        for a, b in groups:
            lhs = jnp.concatenate([live_rows(W[a].astype(BF16), t0), live_rows(W[b].astype(BF16), t0)], axis=1)
            if not last:
                lhs_p = jnp.concatenate([live_rows(P[a], t1), live_rows(P[b], t1)], axis=1)
                lhs = jnp.concatenate([lhs_p, lhs], axis=0)
            res = _mm(lhs, blockdiag(P[a], P[b]))
            if not last:
                sq = all_rows(res[:lhs_p.shape[0], :], t1)
                P[a], P[b] = sq[:, :PAIR].astype(BF16), sq[:, PAIR:].astype(BF16)
                res = res[lhs_p.shape[0]:, :]
            wp = all_rows(res, t0)
            W[a] = W[a] + wp[:, :PAIR]
            W[b] = W[b] + wp[:, PAIR:]
        tick()
        span *= 2
    AkV = each(_mm, Aak, Vs)
    tick()
    X = each(lambda w, la, akv: _mm(w, jnp.concatenate([la, akv.astype(BF16)], axis=1)).astype(BF16),
             W, La, AkV)
    tick()
    Z = each(lambda arb, ark, x, vs:
             _mm(jnp.concatenate([arb.astype(BF16), ark.astype(BF16)], axis=1),
                 jnp.concatenate([x, jnp.concatenate([jnp.zeros_like(vs), vs], axis=1)], axis=0)),
             Arb, Ark, X, Vs)
    tick()
    Q = each(lambda z, lr, gm: (z[:, :PAIR] + lr.astype(F32)) * gm, Z, Lr, gm_row)
    Y0 = [z[:, PAIR:] for z in Z]
    tick()
    Mbd = each(lambda rbg, x, gm: _mm_tn(rbg, x[:, :PAIR]) * gm, Rbg, X, gm_row)
    NcT = each(lambda x, vs, rbg, rkg: _mm_tn(jnp.concatenate([x[:, PAIR:], vs], axis=0),
                                              jnp.concatenate([rbg, rkg], axis=0)), X, Vs, Rbg, Rkg)
    for _ in filler:
        pass

    S = [s_ref[p] for p in range(pairs)]
    for c in range(chunks):
        for p in range(pairs):
            i = c * pairs + p
            Ys = _mm_nt(Q[i], S[p]) + Y0[i]
            S[p] = S[p] * gT_row[i] + _mm_nt(S[p], Mbd[i]) + NcT[i]
            mu = jnp.sum(Ys, axis=-1, keepdims=True) * (1.0 / RW_HEAD)
            d = jnp.where(own, Ys - mu, 0.0)
            var = jnp.sum(d * d, axis=-1, keepdims=True) * (1.0 / RW_HEAD)
            dn = d * lax.rsqrt(var + GN_EPS)
            yn = (dn[:T, :] + dn[T:, :]) * gnw_ref[:, cols(p)] + gnb_ref[:, cols(p)]
            out = (yn + bonus[i].astype(F32)) * gate[i].astype(F32)
            o_ref[rows(c), cols(p)] = out.astype(BF16)
    for p in range(pairs):
        s_ref[p] = S[p]


def _swa_kernel(sink_ref, q_ref, kc_ref, kp_ref, vc_ref, vp_ref, qg_ref, kg_ref, o_ref, *,
                steps_per_seq, slopes, qb):
    first = (pl.program_id(0) % steps_per_seq) == 0
    mean_h = _head_ones(LANES, ATT_HEAD, 1.0 / ATT_HEAD)

    def head_rms(x, gain):
        parts = []
        for b in range(x.shape[1] // LANES):
            xb = x[:, b * LANES:(b + 1) * LANES]
            ms = _mm(xb * xb, mean_h)
            parts.append(xb * lax.rsqrt(ms + RMS_EPS))
        return jnp.concatenate(parts, axis=1) * gain

    q = head_rms(q_ref[...].astype(F32), qg_ref[...]) * (ATT_HEAD ** -0.5 * LOG2E)
    q = q.astype(BF16)
    kcat = jnp.concatenate([kp_ref[...], kc_ref[...]], axis=0).astype(F32)
    kcat = head_rms(kcat, kg_ref[...]).astype(BF16)
    vcat = jnp.concatenate([vp_ref[...], vc_ref[...]], axis=0)

    qi = lax.broadcasted_iota(jnp.int32, (BLOCK, 2 * BLOCK), 0)
    kj = lax.broadcasted_iota(jnp.int32, (BLOCK, 2 * BLOCK), 1)
    dist_i = BLOCK + qi - kj
    in_window = (dist_i >= 0) & (dist_i < WINDOW)
    first_key = jnp.where(first, BLOCK, 0)
    neg_dist_rest = jnp.where(in_window, -dist_i.astype(F32), NEG_BIG)
    neg_dist_first = jnp.where(in_window & (kj >= first_key), -dist_i.astype(F32), NEG_BIG)
    neg_dist = [neg_dist_first] + [neg_dist_rest] * (qb - 1)

    lane_q = lax.broadcasted_iota(jnp.int32, (BLOCK, LANES), 1)
    lane_kv = lax.broadcasted_iota(jnp.int32, (2 * BLOCK, LANES), 1)
    zero_q = jnp.zeros((BLOCK, LANES), BF16)
    one_kv = jnp.ones((2 * BLOCK, LANES), BF16)
    kv_lane_blocks = range(ATT_KV_DIM // LANES)
    k_all = [kcat[:, b * LANES:(b + 1) * LANES] for b in kv_lane_blocks]
    k_all_rolled = [pltpu.roll(kb, ATT_HEAD, 1) for kb in k_all]
    window = lambda x, t: x[t * BLOCK:(t + 2) * BLOCK, :]
    v_aug = [[jnp.where((lane_kv // ATT_HEAD) == (j % 2),
                        window(vcat[:, (j // 2) * LANES:(j // 2 + 1) * LANES], t), one_kv)
              for j in range(ATT_KV_HEADS)] for t in range(qb)]
    items = [(t, h) for t in range(qb) for h in range(ATT_HEADS)]
    kv_of = lambda h: h // ATT_GROUP
    aligned = lambda h: (h % 2) == (kv_of(h) % 2)

    def scores(t, h):
        qh = jnp.where((lane_q // ATT_HEAD) == (h % 2),
                       q[t * BLOCK:(t + 1) * BLOCK, (h // 2) * LANES:(h // 2 + 1) * LANES], zero_q)
        kh = window((k_all if aligned(h) else k_all_rolled)[kv_of(h) // 2], t)
        return _mm_nt(qh, kh) + (slopes[h] * LOG2E) * neg_dist[t]

    s = [scores(t, h) for t, h in items]
    sink = [sink_ref[h] * LOG2E for _, h in items]
    mx = [jnp.maximum(jnp.max(s_, axis=-1, keepdims=True), sk) for s_, sk in zip(s, sink)]
    p = [jnp.exp2(s_ - m_) for s_, m_ in zip(s, mx)]
    pv = [_mm(p_, v_aug[t][kv_of(h)]) for p_, (t, h) in zip(p, items)]
    pv_sw = [pltpu.roll(x, ATT_HEAD, 1) for x in pv]
    out = []
    for i, (t, h) in enumerate(items):
        num, den = (pv[i], pv_sw[i]) if aligned(h) else (pv_sw[i], pv[i])
        out.append(num * (1.0 / (den + jnp.exp2(sink[i] - mx[i]))))
    for t in range(qb):
        o = out[t * ATT_HEADS:(t + 1) * ATT_HEADS]
        o_ref[t * BLOCK:(t + 1) * BLOCK, :] = jnp.concatenate(
            [jnp.where((lane_q // ATT_HEAD) == 0, o[2 * m], o[2 * m + 1]) for m in range(ATT_HEADS // 2)],
            axis=1).astype(BF16)


def _swa(z, sinks, q_gain_t, k_gain_t, seq, qb=8):
    m = z.shape[0]
    rows = qb * BLOCK
    slopes = tuple(float(s) for s in
                   np.exp2(-8.0 * np.arange(1, ATT_HEADS + 1, dtype=np.float32) / ATT_HEADS).astype(np.float32))
    kv_cur = lambda c0: pl.BlockSpec((rows, ATT_KV_DIM), lambda n, c=c0 // ATT_KV_DIM: (n, c))
    kv_prev = lambda c0: pl.BlockSpec((BLOCK, ATT_KV_DIM),
                                      lambda n, c=c0 // ATT_KV_DIM: (jnp.maximum(n * qb - 1, 0), c))
    return pl.pallas_call(
        functools.partial(_swa_kernel, steps_per_seq=seq // rows, slopes=slopes, qb=qb),
        grid=(m // rows,),
        in_specs=[pl.BlockSpec(memory_space=pltpu.SMEM),
                  pl.BlockSpec((rows, ATT_Q_DIM), lambda n: (n, C_Q // ATT_Q_DIM)),
                  kv_cur(C_KA), kv_prev(C_KA), kv_cur(C_VA), kv_prev(C_VA),
                  pl.BlockSpec((1, ATT_Q_DIM), lambda n: (0, 0)),
                  pl.BlockSpec((1, ATT_KV_DIM), lambda n: (0, 0))],
        out_specs=pl.BlockSpec((rows, ATT_Q_DIM), lambda n: (n, 0)),
        out_shape=jax.ShapeDtypeStruct((m, ATT_Q_DIM), BF16),
        compiler_params=_cparams(("parallel",)),
        name="swa",
    )(sinks, z, z, z, z, z, q_gain_t, k_gain_t)


def _mix_out_kernel(orw_ref, oatt_ref, wb1_ref, wb2_ref, zg_ref, wo_ref, x_ref, g_ref, x1_ref, h2_ref):
    p_rw = jnp.dot(orw_ref[...], wb1_ref[...], preferred_element_type=F32)
    p_att = jnp.dot(oatt_ref[...], wb2_ref[...], preferred_element_type=F32)
    g_rw = jax.nn.sigmoid(zg_ref[:, :D_MODEL].astype(F32))
    g_att = jax.nn.sigmoid(zg_ref[:, D_MODEL:].astype(F32))
    mix = (g_rw * p_rw + g_att * p_att).astype(BF16)
    x1 = x_ref[...] + jnp.dot(mix, wo_ref[...], preferred_element_type=F32)
    x1_ref[...] = x1
    y = x1 * lax.rsqrt(jnp.mean(x1 * x1, axis=-1, keepdims=True) + RMS_EPS)
    h2_ref[...] = (y * g_ref[...]).astype(BF16)


def _mix_out(o_rw, o_att, w_branch_b, z, w_out_b, x2, g2, tm=512):
    m = x2.shape[0]
    const = lambda rows, cols, r0=0: pl.BlockSpec((rows, cols), lambda i: (r0, 0),
                                                  pipeline_mode=pl.Buffered(1))
    rows = lambda cols, c0=0: pl.BlockSpec((tm, cols), lambda i: (i, c0))
    return pl.pallas_call(
        _mix_out_kernel,
        grid=(m // tm,),
        in_specs=[rows(RW_DIM), rows(ATT_Q_DIM), const(RW_DIM, D_MODEL), const(ATT_Q_DIM, D_MODEL, 1),
                  rows(2 * D_MODEL, C_GRW // (2 * D_MODEL)), const(D_MODEL, D_MODEL),
                  rows(D_MODEL), const(1, D_MODEL)],
        out_specs=[rows(D_MODEL), rows(D_MODEL)],
        out_shape=[jax.ShapeDtypeStruct((m, D_MODEL), F32),
                   jax.ShapeDtypeStruct((m, D_MODEL), BF16)],
        compiler_params=_cparams(("parallel",)),
        name="mix_out",
    )(o_rw, o_att, w_branch_b, w_branch_b, z, w_out_b, x2, g2)


def _ffn_up_kernel(h_ref, wvb_ref, wgb_ref, cwv_ref, cwg_ref, cbv_ref, cbg_ref, wd_ref, a_ref, wdb_ref,
                   cv_ref, cg_ref, *, tm, sub, tiles_per_seq):
    halo = 8
    wdb_ref[...] = wd_ref[...].astype(BF16)

    @pl.when((pl.program_id(1) % tiles_per_seq) == 0)
    def _():
        cv_ref[...] = jnp.zeros_like(cv_ref)
        cg_ref[...] = jnp.zeros_like(cg_ref)

    row = lax.broadcasted_iota(jnp.int32, (halo, a_ref.shape[1]), 0)

    def conv(u, carry_ref, cw_ref, cb_ref):
        p1 = carry_ref[halo - 1:halo, :]
        p2 = carry_ref[halo - 2:halo - 1, :]
        r1 = pltpu.roll(u, 1, 0)
        r2 = pltpu.roll(u, 2, 0)
        prev1 = jnp.concatenate([jnp.where(row == 0, p1, r1[:halo, :]), r1[halo:, :]], axis=0)
        prev2 = jnp.concatenate(
            [jnp.where(row == 0, p2, jnp.where(row == 1, p1, r2[:halo, :])), r2[halo:, :]], axis=0)
        carry_ref[...] = u[sub - halo:sub, :]
        w = cw_ref[...].astype(BF16)
        return (prev2.astype(BF16) * w[0:1, :] + prev1.astype(BF16) * w[1:2, :] + u.astype(BF16) * w[2:3, :]
                + cb_ref[...].astype(BF16))

    for s in range(tm // sub):
        h = h_ref[s * sub:(s + 1) * sub, :]
        val = conv(jnp.dot(h, wvb_ref[...], preferred_element_type=F32), cv_ref, cwv_ref, cbv_ref)
        gate = conv(jnp.dot(h, wgb_ref[...], preferred_element_type=F32), cg_ref, cwg_ref, cbg_ref)
        a_ref[s * sub:(s + 1) * sub, :] = (gate * jax.nn.sigmoid(gate) * val).astype(BF16)


def _ffn_up(h2, w_up_b, conv_w, conv_b, w_down, seq, tm=4096, tn=512, sub=256):
    m = h2.shape[0]
    nj = D_FF // tn
    ni = m // tm
    wd_rows = w_down.shape[0] // (nj * ni)
    wd_spec = pl.BlockSpec((wd_rows, D_MODEL), lambda j, i: (j * ni + i, 0))
    return pl.pallas_call(
        functools.partial(_ffn_up_kernel, tm=tm, sub=sub, tiles_per_seq=seq // tm),
        grid=(nj, ni),
        in_specs=[pl.BlockSpec((tm, D_MODEL), lambda j, i: (i, 0)),
                  pl.BlockSpec((D_MODEL, tn), lambda j, i: (0, j)),
                  pl.BlockSpec((D_MODEL, tn), lambda j, i: (0, nj + j)),
                  pl.BlockSpec((3, tn), lambda j, i: (0, j)),
                  pl.BlockSpec((3, tn), lambda j, i: (0, nj + j)),
                  pl.BlockSpec((1, tn), lambda j, i: (0, j)),
                  pl.BlockSpec((1, tn), lambda j, i: (0, nj + j)),
                  wd_spec],
        out_specs=[pl.BlockSpec((tm, tn), lambda j, i: (i, j)), wd_spec],
        out_shape=[jax.ShapeDtypeStruct((m, D_FF), BF16), jax.ShapeDtypeStruct(w_down.shape, BF16)],
        scratch_shapes=[pltpu.VMEM((8, tn), F32), pltpu.VMEM((8, tn), F32)],
        compiler_params=_cparams(("parallel", "arbitrary")),
        name="ffn_up",
    )(h2, w_up_b, w_up_b, conv_w, conv_w, conv_b, conv_b, w_down)


def _ffn_down_kernel(a_ref, w_ref, x1_ref, o_ref):
    o_ref[...] = x1_ref[...] + jnp.dot(a_ref[...], w_ref[...], preferred_element_type=F32)


def _ffn_down(act, w_down_b, x1, tm=512, tn=2048):
    m = act.shape[0]
    return pl.pallas_call(
        _ffn_down_kernel,
        grid=(D_MODEL // tn, m // tm),
        in_specs=[pl.BlockSpec((tm, D_FF), lambda j, i: (i, 0)),
                  pl.BlockSpec((D_FF, tn), lambda j, i: (0, j), pipeline_mode=pl.Buffered(1)),
                  pl.BlockSpec((tm, tn), lambda j, i: (i, j))],
        out_specs=pl.BlockSpec((tm, tn), lambda j, i: (i, j)),
        out_shape=jax.ShapeDtypeStruct((m, D_MODEL), F32),
        compiler_params=_cparams(("parallel", "arbitrary")),
        name="ffn_down",
    )(act, w_down_b, x1)


def _pad_cols(w, n):
    return jnp.pad(w, ((0, 0), (0, n - w.shape[1])))


def _pad_rows(w, n):
    return jnp.pad(w, ((0, n - w.shape[0]), (0, 0)))


def _layer(x2, batch, seq, w_in_p, norm1_g, rw_mu, rw_w0, rw_w2, rw_a0, rw_a2, rw_g2, rw_k_k, rw_k_a,
           rw_r_k, rw_gn_w, rw_gn_b, q_norm_g, k_norm_g, attn_sinks, w_branch, w_out,
           norm2_g, w_up, conv_w, conv_b, w_down):
    row = lambda v: v.reshape(1, -1).astype(F32)
    o3 = 3 * RW_DIM
    o4 = o3 + DECAY_LORA
    o5 = o4 + ICLR_LORA
    o6 = o5 + GATE_LORA
    mu = rw_mu.reshape(1, -1)
    mu_l = jnp.concatenate([_pad_cols(mu[:, o3:o4], LORA_W), _pad_cols(mu[:, o4:o5], LORA_A),
                            mu[:, o5:o6]], axis=1)
    w2p = _pad_rows(rw_w2, LORA_W).astype(BF16)
    a2p = _pad_rows(rw_a2, LORA_A).astype(BF16)

    z = _in_proj(x2, row(norm1_g), w_in_p)
    o_rw, (w_branch_b, w_out_b, w_up_b) = _rwkv(
        z, batch, seq, mu[:, :RW_DIM], mu[:, RW_DIM:2 * RW_DIM], mu[:, 2 * RW_DIM:o3], mu_l,
        row(rw_w0), row(rw_a0), row(rw_k_k), row(rw_k_a), row(rw_r_k),
        w2p, a2p, rw_g2.astype(BF16), row(rw_gn_w), row(rw_gn_b), (w_branch, w_out, w_up))
    o_att = _swa(z, attn_sinks.astype(F32), jnp.tile(row(q_norm_g), (1, ATT_HEADS)),
                 jnp.tile(row(k_norm_g), (1, ATT_KV_HEADS)), seq)
    x1, h2 = _mix_out(o_rw, o_att, w_branch_b, z, w_out_b, x2, row(norm2_g))
    act, w_down_b = _ffn_up(h2, w_up_b, conv_w.astype(F32), row(conv_b), w_down, seq)
    return _ffn_down(act, w_down_b, x1)


def kernel(x, norm1_g, w_in, rw_mu, rw_w0, rw_w2, rw_a0, rw_a2, rw_g2, rw_k_k, rw_k_a, rw_r_k,
           rw_gn_w, rw_gn_b, q_norm_g, k_norm_g, attn_sinks, w_branch, w_out, norm2_g, w_up,
           conv_w, conv_b, w_down):
    batch, seq, d = x.shape
    x2 = x.reshape(batch * seq, d)
    params = (norm1_g, rw_mu, rw_w0, rw_w2, rw_a0, rw_a2, rw_g2, rw_k_k, rw_k_a, rw_r_k,
              rw_gn_w, rw_gn_b, q_norm_g, k_norm_g, attn_sinks, w_branch, w_out, norm2_g, w_up,
              conv_w, conv_b, w_down)
    for layer in range(norm1_g.shape[0]):
        x2 = _layer(x2, batch, seq, _w_in_layout(w_in, layer), *(p[layer] for p in params))
    return x2.reshape(batch, seq, d)
```

```python
import functools

import jax
import jax.numpy as jnp
import numpy as np
from jax import lax
from jax.experimental import pallas as pl
from jax.experimental.pallas import tpu as pltpu

F32 = jnp.float32
BF16 = jnp.bfloat16

LANES = 128
BF16_SUBLANES = 16
VMEM_LIMIT = 56 * 1024 * 1024

D_MODEL = 2048
RW_HEADS = 16
RW_HEAD = 64
RW_DIM = RW_HEADS * RW_HEAD
DECAY_LORA = 96
ICLR_LORA = 96
GATE_LORA = 256
GN_EPS = 64e-5
ATT_HEADS = 16
ATT_KV_HEADS = 4
ATT_GROUP = ATT_HEADS // ATT_KV_HEADS
ATT_HEAD = 64
ATT_Q_DIM = ATT_HEADS * ATT_HEAD
ATT_KV_DIM = ATT_KV_HEADS * ATT_HEAD
WINDOW = 128
BLOCK = 128
D_FF = 5632
RMS_EPS = 1e-6
NEG_BIG = -1e30
LOG2E = 1.4426950408889634
EXP_M_HALF = 0.6065306597126334

C_R = 0
C_K = RW_DIM
C_V = 2 * RW_DIM
C_Q = 3 * RW_DIM
C_GRW = C_Q + ATT_Q_DIM
C_GATT = C_GRW + D_MODEL
C_LORA = C_GATT + D_MODEL
LORA_W = 128
LORA_A = 128
LORA_COLS = LORA_W + LORA_A + GATE_LORA
C_KA = C_LORA + LORA_COLS
C_VA = C_KA + ATT_KV_DIM
Z_COLS = C_VA + ATT_KV_DIM

CHUNK = 64
PAIR = 2 * RW_HEAD
N_PAIRS = RW_DIM // PAIR


def _cparams(sem):
    return pltpu.CompilerParams(dimension_semantics=sem, vmem_limit_bytes=VMEM_LIMIT)


def _mm(a, b):
    return jnp.dot(a.astype(BF16), b.astype(BF16), preferred_element_type=F32)


def _mm_nt(a, b):
    return lax.dot_general(a.astype(BF16), b.astype(BF16), (((1,), (1,)), ((), ())),
                           preferred_element_type=F32)


def _mm_tn(a, b):
    return lax.dot_general(a.astype(BF16), b.astype(BF16), (((0,), (0,)), ((), ())),
                           preferred_element_type=F32)


def _mm_split(m01, x):
    hi = x.astype(BF16)
    lo = (x - hi.astype(F32)).astype(BF16)
    return (jnp.dot(m01, hi, preferred_element_type=F32)
            + jnp.dot(m01, lo, preferred_element_type=F32))


def _head_ones(n, head, scale):
    r = lax.broadcasted_iota(jnp.int32, (n, n), 0) // head
    c = lax.broadcasted_iota(jnp.int32, (n, n), 1) // head
    return jnp.where(r == c, scale, 0.0).astype(BF16)


def _w_in_layout_kernel(wt_ref, o_ref, *, moves, pad):
    o_ref[pad[0]:pad[1], :] = jnp.zeros((pad[1] - pad[0], o_ref.shape[1]), BF16)
    for dst, src, width in moves:
        o_ref[dst:dst + width, :] = wt_ref[src:src + width, :].astype(BF16)


def _w_in_layout(w_in_layers, layer, cols=256):
    _, d, n = w_in_layers.shape
    wt_layers = jnp.swapaxes(w_in_layers, 1, 2)
    o3 = 3 * RW_DIM
    o4 = o3 + DECAY_LORA
    o5 = o4 + ICLR_LORA
    o6 = o5 + GATE_LORA
    oq = o6 + ATT_Q_DIM
    ok = oq + ATT_KV_DIM
    ov = ok + ATT_KV_DIM
    moves = ((C_R, 0, o3), (C_Q, o6, ATT_Q_DIM), (C_GRW, ov, 2 * D_MODEL),
             (C_LORA, o3, DECAY_LORA), (C_LORA + LORA_W, o4, ICLR_LORA),
             (C_LORA + LORA_W + LORA_A, o5, GATE_LORA), (C_KA, oq, ATT_KV_DIM), (C_VA, ok, ATT_KV_DIM))
    return pl.pallas_call(
        functools.partial(_w_in_layout_kernel, moves=moves, pad=(C_LORA, C_LORA + LORA_W + LORA_A)),
        grid=(d // cols,),
        in_specs=[pl.BlockSpec((None, n, cols), lambda i: (layer, 0, i))],
        out_specs=pl.BlockSpec((Z_COLS, cols), lambda i: (0, i)),
        out_shape=jax.ShapeDtypeStruct((Z_COLS, d), BF16),
        compiler_params=_cparams(("parallel",)),
        name="w_in_layout",
    )(wt_layers)


def _in_proj_kernel(x_ref, g_ref, wt_ref, z_ref, h_ref):
    @pl.when(pl.program_id(1) == 0)
    def _():
        rows = 512
        for r0 in range(0, x_ref.shape[0], rows):
            xf = x_ref[r0:r0 + rows, :]
            y = xf * lax.rsqrt(jnp.mean(xf * xf, axis=-1, keepdims=True) + RMS_EPS)
            h_ref[r0:r0 + rows, :] = (y * g_ref[...]).astype(BF16)

    z_ref[...] = _mm_nt(h_ref[...], wt_ref[...]).astype(BF16)


def _in_proj(x2, g1, w_in_t, tm=2048, tn=1024):
    m, d = x2.shape
    n = w_in_t.shape[0]
    return pl.pallas_call(
        _in_proj_kernel,
        grid=(m // tm, n // tn),
        in_specs=[pl.BlockSpec((tm, d), lambda i, j: (i, 0), pipeline_mode=pl.Buffered(1)),
                  pl.BlockSpec((1, d), lambda i, j: (0, 0)),
                  pl.BlockSpec((tn, d), lambda i, j: (j, 0))],
        out_specs=pl.BlockSpec((tm, tn), lambda i, j: (i, j)),
        out_shape=jax.ShapeDtypeStruct((m, n), BF16),
        scratch_shapes=[pltpu.VMEM((tm, d), BF16)],
        compiler_params=_cparams(("parallel", "arbitrary")),
        name="in_proj",
    )(x2, g1, w_in_t)


def _rwkv_prep_kernel(zr_ref, zk_ref, zv_ref, zl_ref, pr_ref, pk_ref, pv_ref, pli_ref,
                      mur_ref, muk_ref, muv_ref, mul_ref,
                      w0_ref, a0_ref, kk_ref, ka_ref, rk_ref, w2_ref, a2_ref, g2_ref,
                      at_ref, rt_ref, bt_ref, kt_ref, bg_ref, kg_ref, v_ref, g_ref, bonus_ref,
                      gT_ref, gm_ref, *, tm, first):
    keep = jnp.where(first, 0.0, 1.0)
    nc = tm // CHUNK
    last = slice(BF16_SUBLANES - 1, BF16_SUBLANES)

    def shifted(z, prev_last, mu):
        z = z.astype(F32)
        row = lax.broadcasted_iota(jnp.int32, z.shape, 0)
        zprev = jnp.where(row == 0, prev_last.astype(F32) * keep, pltpu.roll(z, 1, 0))
        return z + (zprev - z) * mu

    lo = shifted(zl_ref[...], pli_ref[last, :], mul_ref[...])
    tanh_wd = jnp.tanh(lo[:, :LORA_W]).astype(BF16)
    ad = lo[:, LORA_W:LORA_W + LORA_A].astype(BF16)
    sig_gd = jax.nn.sigmoid(lo[:, LORA_W + LORA_A:]).astype(BF16)

    ti = lax.broadcasted_iota(jnp.int32, (tm, tm), 0)
    si = lax.broadcasted_iota(jnp.int32, (tm, tm), 1)
    tri = jnp.where(((ti // CHUNK) == (si // CHUNK)) & (si <= ti), 1.0, 0.0).astype(BF16)
    ones_h = _head_ones(PAIR, RW_HEAD, 1.0)
    w_lora = jnp.dot(tanh_wd, w2_ref[...], preferred_element_type=F32)
    a_lora = jnp.dot(ad, a2_ref[...], preferred_element_type=F32)
    g_ref[...] = jnp.dot(sig_gd, g2_ref[...], preferred_element_type=F32).astype(BF16)
    yield

    stash = []
    for p in range(N_PAIRS):
        cs = slice(p * PAIR, (p + 1) * PAIR)
        r = shifted(zr_ref[:, cs], pr_ref[last, cs], mur_ref[:, cs])
        k = shifted(zk_ref[:, cs], pk_ref[last, cs], muk_ref[:, cs])
        v = shifted(zv_ref[:, cs], pv_ref[last, cs], muv_ref[:, cs])

        wpre = w0_ref[:, cs] + w_lora[:, cs]
        lw = -EXP_M_HALF * jax.nn.sigmoid(wpre)
        a = jax.nn.sigmoid(a0_ref[:, cs] + a_lora[:, cs])

        kk = k * kk_ref[:, cs]
        kk = kk * lax.rsqrt(jnp.maximum(_mm(kk * kk, ones_h), 1e-24))
        k2 = k * (1.0 + (a - 1.0) * ka_ref[:, cs])
        bonus_ref[:, cs] = (_mm(r * k2 * rk_ref[:, cs], ones_h) * v).astype(BF16)
        v_ref[:, cs] = v.astype(BF16)
        stash.append((r, k2, -kk, kk * a, lw))
        yield

    c_all = _mm_split(tri, jnp.concatenate([st[4] for st in stash], axis=1))
    yield

    for p in range(N_PAIRS):
        cs = slice(p * PAIR, (p + 1) * PAIR)
        r, k2, ka_vec, kb_vec, lw = stash[p]
        c = c_all[:, cs]
        c3 = c.reshape(nc, CHUNK, PAIR)
        c_mid = c3[:, CHUNK // 2 - 1:CHUNK // 2, :]
        c_end = c3[:, CHUNK - 1:CHUNK, :]
        cm = jnp.broadcast_to(c_mid, c3.shape).reshape(tm, PAIR)
        mid_to_end = jnp.broadcast_to(jnp.exp(c_end - c_mid), c3.shape).reshape(tm, PAIR)

        e_in = jnp.exp(c - cm)
        e_out = jnp.exp(cm - c)
        e_end = e_out * mid_to_end
        at_ref[:, cs] = (ka_vec * jnp.exp(c - lw - cm)).astype(BF16)
        rt_ref[:, cs] = (r * e_in).astype(BF16)
        bt_ref[:, cs] = (kb_vec * e_out).astype(BF16)
        kt_ref[:, cs] = (k2 * e_out).astype(BF16)
        bg_ref[:, cs] = (kb_vec * e_end).astype(BF16)
        kg_ref[:, cs] = (k2 * e_end).astype(BF16)
        gT_ref[:, :, cs] = jnp.exp(c_end)
        gm_ref[:, :, cs] = jnp.exp(c_mid)
        yield


def _rwkv_kernel(*refs, tb, nb, n_side):
    prep_in, (gnw_ref, gnb_ref), refs = refs[:20], refs[20:22], refs[22:]
    side_in, o_ref, side_out, scratch = refs[:n_side], refs[n_side], refs[n_side + 1:2 * n_side + 1], refs[2 * n_side + 1:]
    operands, s_ref = scratch[:11], scratch[11]
    j = pl.program_id(1)
    prep = functools.partial(_rwkv_prep_kernel, *prep_in, *operands, tm=tb, first=j == 0)
    chunk = functools.partial(_rwkv_chunk_kernel, *operands, gnw_ref, gnb_ref, o_ref, s_ref,
                              chunks=tb // CHUNK, pairs=N_PAIRS, first=j == 1)

    def side_cast():
        for w_ref, wb_ref in zip(side_in, side_out):
            wb_ref[...] = w_ref[...].astype(BF16)

    @pl.when(j == 0)
    def _():
        side_cast()
        for _ in prep():
            pass

    @pl.when((j > 0) & (j < nb))
    def _():
        side_cast()
        chunk(filler=prep())

    @pl.when(j == nb)
    def _():
        side_cast()
        chunk()


def _rwkv(z, batch, seq, mu_r, mu_k, mu_v, mu_l, w0, a0, k_k, k_a, r_k, w2p, a2p, g2, gn_w, gn_b,
          side_weights, tb=128, side_rows=32):
    m = z.shape[0]
    nb = seq // tb
    for w in side_weights:
        assert w.shape[0] // side_rows <= batch * (nb + 1), "not enough grid steps to cast this weight"
    side_specs = [pl.BlockSpec((side_rows, w.shape[1]),
                               lambda b, j, last=w.shape[0] // side_rows - 1:
                               (jnp.minimum(b * (nb + 1) + j, last), 0)) for w in side_weights]
    pb = tb // BF16_SUBLANES
    blk = lambda b, j: b * nb + jnp.minimum(j, nb - 1)

    def cur(col0, width):
        return pl.BlockSpec((tb, width), lambda b, j, c0=col0 // width: (blk(b, j), c0))

    def prev(col0, width):
        return pl.BlockSpec((BF16_SUBLANES, width),
                            lambda b, j, c0=col0 // width: (jnp.maximum(blk(b, j) * pb - 1, 0), c0))

    def whole(rows, width):
        return pl.BlockSpec((rows, width), lambda b, j: (0, 0))

    in_specs = [cur(C_R, RW_DIM), cur(C_K, RW_DIM), cur(C_V, RW_DIM), cur(C_LORA, LORA_COLS),
                prev(C_R, RW_DIM), prev(C_K, RW_DIM), prev(C_V, RW_DIM), prev(C_LORA, LORA_COLS),
                whole(1, RW_DIM), whole(1, RW_DIM), whole(1, RW_DIM), whole(1, LORA_COLS),
                whole(1, RW_DIM), whole(1, RW_DIM), whole(1, RW_DIM), whole(1, RW_DIM), whole(1, RW_DIM),
                whole(LORA_W, RW_DIM), whole(LORA_A, RW_DIM), whole(GATE_LORA, RW_DIM),
                whole(1, RW_DIM), whole(1, RW_DIM)]
    big = pltpu.VMEM((tb, RW_DIM), BF16)
    per_chunk = pltpu.VMEM((tb // CHUNK, 1, RW_DIM), F32)
    outs = pl.pallas_call(
        functools.partial(_rwkv_kernel, tb=tb, nb=nb, n_side=len(side_weights)),
        grid=(batch, nb + 1),
        in_specs=in_specs + side_specs,
        out_specs=[pl.BlockSpec((tb, RW_DIM), lambda b, j: (b * nb + jnp.maximum(j - 1, 0), 0))] + side_specs,
        out_shape=[jax.ShapeDtypeStruct((m, RW_DIM), BF16)]
        + [jax.ShapeDtypeStruct(w.shape, BF16) for w in side_weights],
        scratch_shapes=[big] * 9 + [per_chunk] * 2 + [pltpu.VMEM((N_PAIRS, PAIR, PAIR), F32)],
        compiler_params=_cparams(("arbitrary", "arbitrary")),
        name="rwkv",
    )(z, z, z, z, z, z, z, z, mu_r, mu_k, mu_v, mu_l, w0, a0, k_k, k_a, r_k, w2p, a2p, g2, gn_w, gn_b,
      *side_weights)
    return outs[0], outs[1:]


def _rwkv_chunk_kernel(at_ref, rt_ref, bt_ref, kt_ref, bg_ref, kg_ref, v_ref, g_ref, bonus_ref,
                       gT_ref, gm_ref, gnw_ref, gnb_ref, o_ref, s_ref, *, chunks, pairs, first,
                       filler=None):
    @pl.when(first)
    def _():
        s_ref[...] = jnp.zeros_like(s_ref)

    T = CHUNK
    lane = lax.broadcasted_iota(jnp.int32, (T, PAIR), 1)
    head0 = lane < RW_HEAD
    ri = lax.broadcasted_iota(jnp.int32, (2 * T, 2 * T), 0)
    ci = lax.broadcasted_iota(jnp.int32, (2 * T, 2 * T), 1)
    same = (ri // T) == (ci // T)
    strict = same & ((ri % T) > (ci % T))
    incl = same & ((ri % T) >= (ci % T))
    eye = jnp.where(ri == ci, 1.0, 0.0)
    own = (ri // T) == (ci // RW_HEAD)

    items = [(p, c) for c in range(chunks) for p in range(pairs)]
    rows = lambda c: slice(c * T, (c + 1) * T)
    cols = lambda p: slice(p * PAIR, (p + 1) * PAIR)

    def stack(ref):
        out = []
        for p, c in items:
            x = ref[rows(c), cols(p)]
            zero = jnp.zeros_like(x)
            out.append(jnp.concatenate([jnp.where(head0, x, zero), jnp.where(head0, zero, x)], axis=0))
        return out

    each = lambda f, *ls: [f(*xs) for xs in zip(*ls)]
    La, Lr, Rb, Rk = stack(at_ref), stack(rt_ref), stack(bt_ref), stack(kt_ref)
    Rbg, Rkg, Vs = stack(bg_ref), stack(kg_ref), stack(v_ref)
    gm_row = [gm_ref[c][:, cols(p)] for p, c in items]
    gT_row = [gT_ref[c][:, cols(p)] for p, c in items]
    gate = [g_ref[rows(c), cols(p)] for p, c in items]
    bonus = [bonus_ref[rows(c), cols(p)] for p, c in items]
    filler = iter(()) if filler is None else filler
    tick = lambda: next(filler, None)

    AA = each(lambda la, lr, rb, rk: _mm_nt(jnp.concatenate([la, lr], axis=0),
                                            jnp.concatenate([rb, rk], axis=0)), La, Lr, Rb, Rk)
    tick()
    N = [jnp.where(strict, aa[:2 * T, :2 * T], 0.0) for aa in AA]
    Aak = [jnp.where(strict, aa[:2 * T, 2 * T:], 0.0) for aa in AA]
    Arb = [jnp.where(incl, aa[2 * T:, :2 * T], 0.0) for aa in AA]
    Ark = [jnp.where(incl, aa[2 * T:, 2 * T:], 0.0) for aa in AA]
    def live_rows(x, t0):
        return x if t0 == 0 else jnp.concatenate([x[t0:T, :], x[T + t0:, :]], axis=0)

    def all_rows(y, t0):
        if t0 == 0:
            return y
        zero = jnp.zeros((t0, y.shape[1]), y.dtype)
        return jnp.concatenate([zero, y[:T - t0, :], zero, y[T - t0:, :]], axis=0)

    def skip(span):
        return span if span % BF16_SUBLANES == 0 else 0

    def blockdiag(a, b):
        zero = jnp.zeros_like(a)
        return jnp.concatenate([jnp.concatenate([a, zero], axis=1),
                                jnp.concatenate([zero, b], axis=1)], axis=0)

    groups = [(i, i + 1) for i in range(0, len(items), 2)]
    W = each(lambda n: eye + n, N)
    P = [n.astype(BF16) for n in N]
    for a, b in groups:
        sq = _mm(jnp.concatenate([P[a], P[b]], axis=1), blockdiag(P[a], P[b]))
        P[a], P[b] = sq[:, :PAIR].astype(BF16), sq[:, PAIR:].astype(BF16)
    tick()
    span = 2
    while span < T:
        last = 2 * span >= T
        t0, t1 = skip(span), skip(2 * span)
        for a, b in groups:
            lhs = jnp.concatenate([live_rows(W[a].astype(BF16), t0), live_rows(W[b].astype(BF16), t0)], axis=1)
            if not last:
                lhs_p = jnp.concatenate([live_rows(P[a], t1), live_rows(P[b], t1)], axis=1)
                lhs = jnp.concatenate([lhs_p, lhs], axis=0)
            res = _mm(lhs, blockdiag(P[a], P[b]))
            if not last:
                sq = all_rows(res[:lhs_p.shape[0], :], t1)
                P[a], P[b] = sq[:, :PAIR].astype(BF16), sq[:, PAIR:].astype(BF16)
                res = res[lhs_p.shape[0]:, :]
            wp = all_rows(res, t0)
            W[a] = W[a] + wp[:, :PAIR]
            W[b] = W[b] + wp[:, PAIR:]
        tick()
        span *= 2
    AkV = each(_mm, Aak, Vs)
    tick()
    X = each(lambda w, la, akv: _mm(w, jnp.concatenate([la, akv.astype(BF16)], axis=1)).astype(BF16),
             W, La, AkV)
    tick()
    Z = each(lambda arb, ark, x, vs:
             _mm(jnp.concatenate([arb.astype(BF16), ark.astype(BF16)], axis=1),
                 jnp.concatenate([x, jnp.concatenate([jnp.zeros_like(vs), vs], axis=1)], axis=0)),
             Arb, Ark, X, Vs)
    tick()
    Q = each(lambda z, lr, gm: (z[:, :PAIR] + lr.astype(F32)) * gm, Z, Lr, gm_row)
    Y0 = [z[:, PAIR:] for z in Z]
    tick()
    Mbd = each(lambda rbg, x, gm: _mm_tn(rbg, x[:, :PAIR]) * gm, Rbg, X, gm_row)
    NcT = each(lambda x, vs, rbg, rkg: _mm_tn(jnp.concatenate([x[:, PAIR:], vs], axis=0),
                                              jnp.concatenate([rbg, rkg], axis=0)), X, Vs, Rbg, Rkg)
    for _ in filler:
        pass

    S = [s_ref[p] for p in range(pairs)]
    for c in range(chunks):
        for p in range(pairs):
            i = c * pairs + p
            Ys = _mm_nt(Q[i], S[p]) + Y0[i]
            S[p] = S[p] * gT_row[i] + _mm_nt(S[p], Mbd[i]) + NcT[i]
            mu = jnp.sum(Ys, axis=-1, keepdims=True) * (1.0 / RW_HEAD)
            d = jnp.where(own, Ys - mu, 0.0)
            var = jnp.sum(d * d, axis=-1, keepdims=True) * (1.0 / RW_HEAD)
            dn = d * lax.rsqrt(var + GN_EPS)
            yn = (dn[:T, :] + dn[T:, :]) * gnw_ref[:, cols(p)] + gnb_ref[:, cols(p)]
            out = (yn + bonus[i].astype(F32)) * gate[i].astype(F32)
            o_ref[rows(c), cols(p)] = out.astype(BF16)
    for p in range(pairs):
        s_ref[p] = S[p]


def _swa_kernel(sink_ref, q_ref, kc_ref, kp_ref, vc_ref, vp_ref, qg_ref, kg_ref, o_ref, *,
                steps_per_seq, slopes, qb):
    first = (pl.program_id(0) % steps_per_seq) == 0
    mean_h = _head_ones(LANES, ATT_HEAD, 1.0 / ATT_HEAD)

    def head_rms(x, gain):
        parts = []
        for b in range(x.shape[1] // LANES):
            xb = x[:, b * LANES:(b + 1) * LANES]
            ms = _mm(xb * xb, mean_h)
            parts.append(xb * lax.rsqrt(ms + RMS_EPS))
        return jnp.concatenate(parts, axis=1) * gain

    q = head_rms(q_ref[...].astype(F32), qg_ref[...]) * (ATT_HEAD ** -0.5 * LOG2E)
    q = q.astype(BF16)
    kcat = jnp.concatenate([kp_ref[...], kc_ref[...]], axis=0).astype(F32)
    kcat = head_rms(kcat, kg_ref[...]).astype(BF16)
    vcat = jnp.concatenate([vp_ref[...], vc_ref[...]], axis=0)

    qi = lax.broadcasted_iota(jnp.int32, (BLOCK, 2 * BLOCK), 0)
    kj = lax.broadcasted_iota(jnp.int32, (BLOCK, 2 * BLOCK), 1)
    dist_i = BLOCK + qi - kj
    in_window = (dist_i >= 0) & (dist_i < WINDOW)
    first_key = jnp.where(first, BLOCK, 0)
    neg_dist_rest = jnp.where(in_window, -dist_i.astype(F32), NEG_BIG)
    neg_dist_first = jnp.where(in_window & (kj >= first_key), -dist_i.astype(F32), NEG_BIG)
    neg_dist = [neg_dist_first] + [neg_dist_rest] * (qb - 1)

    lane_q = lax.broadcasted_iota(jnp.int32, (BLOCK, LANES), 1)
    lane_kv = lax.broadcasted_iota(jnp.int32, (2 * BLOCK, LANES), 1)
    zero_q = jnp.zeros((BLOCK, LANES), BF16)
    one_kv = jnp.ones((2 * BLOCK, LANES), BF16)
    kv_lane_blocks = range(ATT_KV_DIM // LANES)
    k_all = [kcat[:, b * LANES:(b + 1) * LANES] for b in kv_lane_blocks]
    k_all_rolled = [pltpu.roll(kb, ATT_HEAD, 1) for kb in k_all]
    window = lambda x, t: x[t * BLOCK:(t + 2) * BLOCK, :]
    v_aug = [[jnp.where((lane_kv // ATT_HEAD) == (j % 2),
                        window(vcat[:, (j // 2) * LANES:(j // 2 + 1) * LANES], t), one_kv)
              for j in range(ATT_KV_HEADS)] for t in range(qb)]
    items = [(t, h) for t in range(qb) for h in range(ATT_HEADS)]
    kv_of = lambda h: h // ATT_GROUP
    aligned = lambda h: (h % 2) == (kv_of(h) % 2)

    def scores(t, h):
        qh = jnp.where((lane_q // ATT_HEAD) == (h % 2),
                       q[t * BLOCK:(t + 1) * BLOCK, (h // 2) * LANES:(h // 2 + 1) * LANES], zero_q)
        kh = window((k_all if aligned(h) else k_all_rolled)[kv_of(h) // 2], t)
        return _mm_nt(qh, kh) + (slopes[h] * LOG2E) * neg_dist[t]

    s = [scores(t, h) for t, h in items]
    sink = [sink_ref[h] * LOG2E for _, h in items]
    mx = [jnp.maximum(jnp.max(s_, axis=-1, keepdims=True), sk) for s_, sk in zip(s, sink)]
    p = [jnp.exp2(s_ - m_) for s_, m_ in zip(s, mx)]
    pv = [_mm(p_, v_aug[t][kv_of(h)]) for p_, (t, h) in zip(p, items)]
    pv_sw = [pltpu.roll(x, ATT_HEAD, 1) for x in pv]
    out = []
    for i, (t, h) in enumerate(items):
        num, den = (pv[i], pv_sw[i]) if aligned(h) else (pv_sw[i], pv[i])
        out.append(num * (1.0 / (den + jnp.exp2(sink[i] - mx[i]))))
    for t in range(qb):
        o = out[t * ATT_HEADS:(t + 1) * ATT_HEADS]
        o_ref[t * BLOCK:(t + 1) * BLOCK, :] = jnp.concatenate(
            [jnp.where((lane_q // ATT_HEAD) == 0, o[2 * m], o[2 * m + 1]) for m in range(ATT_HEADS // 2)],
            axis=1).astype(BF16)


def _swa(z, sinks, q_gain_t, k_gain_t, seq, qb=8):
    m = z.shape[0]
    rows = qb * BLOCK
    slopes = tuple(float(s) for s in
                   np.exp2(-8.0 * np.arange(1, ATT_HEADS + 1, dtype=np.float32) / ATT_HEADS).astype(np.float32))
    kv_cur = lambda c0: pl.BlockSpec((rows, ATT_KV_DIM), lambda n, c=c0 // ATT_KV_DIM: (n, c))
    kv_prev = lambda c0: pl.BlockSpec((BLOCK, ATT_KV_DIM),
                                      lambda n, c=c0 // ATT_KV_DIM: (jnp.maximum(n * qb - 1, 0), c))
    return pl.pallas_call(
        functools.partial(_swa_kernel, steps_per_seq=seq // rows, slopes=slopes, qb=qb),
        grid=(m // rows,),
        in_specs=[pl.BlockSpec(memory_space=pltpu.SMEM),
                  pl.BlockSpec((rows, ATT_Q_DIM), lambda n: (n, C_Q // ATT_Q_DIM)),
                  kv_cur(C_KA), kv_prev(C_KA), kv_cur(C_VA), kv_prev(C_VA),
                  pl.BlockSpec((1, ATT_Q_DIM), lambda n: (0, 0)),
                  pl.BlockSpec((1, ATT_KV_DIM), lambda n: (0, 0))],
        out_specs=pl.BlockSpec((rows, ATT_Q_DIM), lambda n: (n, 0)),
        out_shape=jax.ShapeDtypeStruct((m, ATT_Q_DIM), BF16),
        compiler_params=_cparams(("parallel",)),
        name="swa",
    )(sinks, z, z, z, z, z, q_gain_t, k_gain_t)


def _mix_out_kernel(orw_ref, oatt_ref, wb1_ref, wb2_ref, zg_ref, wo_ref, x_ref, g_ref, x1_ref, h2_ref):
    p_rw = jnp.dot(orw_ref[...], wb1_ref[...], preferred_element_type=F32)
    p_att = jnp.dot(oatt_ref[...], wb2_ref[...], preferred_element_type=F32)
    g_rw = jax.nn.sigmoid(zg_ref[:, :D_MODEL].astype(F32))
    g_att = jax.nn.sigmoid(zg_ref[:, D_MODEL:].astype(F32))
    mix = (g_rw * p_rw + g_att * p_att).astype(BF16)
    x1 = x_ref[...] + jnp.dot(mix, wo_ref[...], preferred_element_type=F32)
    x1_ref[...] = x1
    y = x1 * lax.rsqrt(jnp.mean(x1 * x1, axis=-1, keepdims=True) + RMS_EPS)
    h2_ref[...] = (y * g_ref[...]).astype(BF16)


def _mix_out(o_rw, o_att, w_branch_b, z, w_out_b, x2, g2, tm=512):
    m = x2.shape[0]
    const = lambda rows, cols, r0=0: pl.BlockSpec((rows, cols), lambda i: (r0, 0),
                                                  pipeline_mode=pl.Buffered(1))
    rows = lambda cols, c0=0: pl.BlockSpec((tm, cols), lambda i: (i, c0))
    return pl.pallas_call(
        _mix_out_kernel,
        grid=(m // tm,),
        in_specs=[rows(RW_DIM), rows(ATT_Q_DIM), const(RW_DIM, D_MODEL), const(ATT_Q_DIM, D_MODEL, 1),
                  rows(2 * D_MODEL, C_GRW // (2 * D_MODEL)), const(D_MODEL, D_MODEL),
                  rows(D_MODEL), const(1, D_MODEL)],
        out_specs=[rows(D_MODEL), rows(D_MODEL)],
        out_shape=[jax.ShapeDtypeStruct((m, D_MODEL), F32),
                   jax.ShapeDtypeStruct((m, D_MODEL), BF16)],
        compiler_params=_cparams(("parallel",)),
        name="mix_out",
    )(o_rw, o_att, w_branch_b, w_branch_b, z, w_out_b, x2, g2)


def _ffn_up_kernel(h_ref, wvb_ref, wgb_ref, cwv_ref, cwg_ref, cbv_ref, cbg_ref, wd_ref, a_ref, wdb_ref,
                   cv_ref, cg_ref, *, tm, sub, tiles_per_seq):
    halo = 8
    wdb_ref[...] = wd_ref[...].astype(BF16)

    @pl.when((pl.program_id(1) % tiles_per_seq) == 0)
    def _():
        cv_ref[...] = jnp.zeros_like(cv_ref)
        cg_ref[...] = jnp.zeros_like(cg_ref)

    row = lax.broadcasted_iota(jnp.int32, (halo, a_ref.shape[1]), 0)

    def conv(u, carry_ref, cw_ref, cb_ref):
        p1 = carry_ref[halo - 1:halo, :]
        p2 = carry_ref[halo - 2:halo - 1, :]
        r1 = pltpu.roll(u, 1, 0)
        r2 = pltpu.roll(u, 2, 0)
        prev1 = jnp.concatenate([jnp.where(row == 0, p1, r1[:halo, :]), r1[halo:, :]], axis=0)
        prev2 = jnp.concatenate(
            [jnp.where(row == 0, p2, jnp.where(row == 1, p1, r2[:halo, :])), r2[halo:, :]], axis=0)
        carry_ref[...] = u[sub - halo:sub, :]
        w = cw_ref[...].astype(BF16)
        return (prev2.astype(BF16) * w[0:1, :] + prev1.astype(BF16) * w[1:2, :] + u.astype(BF16) * w[2:3, :]
                + cb_ref[...].astype(BF16))

    for s in range(tm // sub):
        h = h_ref[s * sub:(s + 1) * sub, :]
        val = conv(jnp.dot(h, wvb_ref[...], preferred_element_type=F32), cv_ref, cwv_ref, cbv_ref)
        gate = conv(jnp.dot(h, wgb_ref[...], preferred_element_type=F32), cg_ref, cwg_ref, cbg_ref)
        a_ref[s * sub:(s + 1) * sub, :] = (gate * jax.nn.sigmoid(gate) * val).astype(BF16)


def _ffn_up(h2, w_up_b, conv_w, conv_b, w_down, seq, tm=4096, tn=512, sub=256):
    m = h2.shape[0]
    nj = D_FF // tn
    ni = m // tm
    wd_rows = w_down.shape[0] // (nj * ni)
    wd_spec = pl.BlockSpec((wd_rows, D_MODEL), lambda j, i: (j * ni + i, 0))
    return pl.pallas_call(
        functools.partial(_ffn_up_kernel, tm=tm, sub=sub, tiles_per_seq=seq // tm),
        grid=(nj, ni),
        in_specs=[pl.BlockSpec((tm, D_MODEL), lambda j, i: (i, 0)),
                  pl.BlockSpec((D_MODEL, tn), lambda j, i: (0, j)),
                  pl.BlockSpec((D_MODEL, tn), lambda j, i: (0, nj + j)),
                  pl.BlockSpec((3, tn), lambda j, i: (0, j)),
                  pl.BlockSpec((3, tn), lambda j, i: (0, nj + j)),
                  pl.BlockSpec((1, tn), lambda j, i: (0, j)),
                  pl.BlockSpec((1, tn), lambda j, i: (0, nj + j)),
                  wd_spec],
        out_specs=[pl.BlockSpec((tm, tn), lambda j, i: (i, j)), wd_spec],
        out_shape=[jax.ShapeDtypeStruct((m, D_FF), BF16), jax.ShapeDtypeStruct(w_down.shape, BF16)],
        scratch_shapes=[pltpu.VMEM((8, tn), F32), pltpu.VMEM((8, tn), F32)],
        compiler_params=_cparams(("parallel", "arbitrary")),
        name="ffn_up",
    )(h2, w_up_b, w_up_b, conv_w, conv_w, conv_b, conv_b, w_down)


def _ffn_down_kernel(a_ref, w_ref, x1_ref, o_ref):
    o_ref[...] = x1_ref[...] + jnp.dot(a_ref[...], w_ref[...], preferred_element_type=F32)


def _ffn_down(act, w_down_b, x1, tm=512, tn=1024):
    m = act.shape[0]
    return pl.pallas_call(
        _ffn_down_kernel,
        grid=(D_MODEL // tn, m // tm),
        in_specs=[pl.BlockSpec((tm, D_FF), lambda j, i: (i, 0)),
                  pl.BlockSpec((D_FF, tn), lambda j, i: (0, j)),
                  pl.BlockSpec((tm, tn), lambda j, i: (i, j))],
        out_specs=pl.BlockSpec((tm, tn), lambda j, i: (i, j)),
        out_shape=jax.ShapeDtypeStruct((m, D_MODEL), F32),
        compiler_params=_cparams(("parallel", "arbitrary")),
        name="ffn_down",
    )(act, w_down_b, x1)


def _pad_cols(w, n):
    return jnp.pad(w, ((0, 0), (0, n - w.shape[1])))


def _pad_rows(w, n):
    return jnp.pad(w, ((0, n - w.shape[0]), (0, 0)))


def _layer(x2, batch, seq, w_in_p, norm1_g, rw_mu, rw_w0, rw_w2, rw_a0, rw_a2, rw_g2, rw_k_k, rw_k_a,
           rw_r_k, rw_gn_w, rw_gn_b, q_norm_g, k_norm_g, attn_sinks, w_branch, w_out,
           norm2_g, w_up, conv_w, conv_b, w_down):
    row = lambda v: v.reshape(1, -1).astype(F32)
    o3 = 3 * RW_DIM
    o4 = o3 + DECAY_LORA
    o5 = o4 + ICLR_LORA
    o6 = o5 + GATE_LORA
    mu = rw_mu.reshape(1, -1)
    mu_l = jnp.concatenate([_pad_cols(mu[:, o3:o4], LORA_W), _pad_cols(mu[:, o4:o5], LORA_A),
                            mu[:, o5:o6]], axis=1)
    w2p = _pad_rows(rw_w2, LORA_W).astype(BF16)
    a2p = _pad_rows(rw_a2, LORA_A).astype(BF16)

    z = _in_proj(x2, row(norm1_g), w_in_p)
    o_rw, (w_branch_b, w_out_b, w_up_b) = _rwkv(
        z, batch, seq, mu[:, :RW_DIM], mu[:, RW_DIM:2 * RW_DIM], mu[:, 2 * RW_DIM:o3], mu_l,
        row(rw_w0), row(rw_a0), row(rw_k_k), row(rw_k_a), row(rw_r_k),
        w2p, a2p, rw_g2.astype(BF16), row(rw_gn_w), row(rw_gn_b), (w_branch, w_out, w_up))
    o_att = _swa(z, attn_sinks.astype(F32), jnp.tile(row(q_norm_g), (1, ATT_HEADS)),
                 jnp.tile(row(k_norm_g), (1, ATT_KV_HEADS)), seq)
    x1, h2 = _mix_out(o_rw, o_att, w_branch_b, z, w_out_b, x2, row(norm2_g))
    act, w_down_b = _ffn_up(h2, w_up_b, conv_w.astype(F32), row(conv_b), w_down, seq)
    return _ffn_down(act, w_down_b, x1)


def kernel(x, norm1_g, w_in, rw_mu, rw_w0, rw_w2, rw_a0, rw_a2, rw_g2, rw_k_k, rw_k_a, rw_r_k,
           rw_gn_w, rw_gn_b, q_norm_g, k_norm_g, attn_sinks, w_branch, w_out, norm2_g, w_up,
           conv_w, conv_b, w_down):
    batch, seq, d = x.shape
    x2 = x.reshape(batch * seq, d)
    params = (norm1_g, rw_mu, rw_w0, rw_w2, rw_a0, rw_a2, rw_g2, rw_k_k, rw_k_a, rw_r_k,
              rw_gn_w, rw_gn_b, q_norm_g, k_norm_g, attn_sinks, w_branch, w_out, norm2_g, w_up,
              conv_w, conv_b, w_down)
    for layer in range(norm1_g.shape[0]):
        x2 = _layer(x2, batch, seq, _w_in_layout(w_in, layer), *(p[layer] for p in params))
    return x2.reshape(batch, seq, d)
```
